```python
import math
import jax, jax.numpy as jnp
from jax import lax
import numpy as np

D_MODEL = 2048
BATCH = 8
SEQ = 4096
DEPTH = 1

GRID_W = 64
CTX_LEN = 256
D_FF = 5632
N_MOD = 9
EPS = 1e-6
HG_WIDTH = D_MODEL // 2
HG_HEADS = 8
HG_DK = HG_WIDTH // HG_HEADS
HG_DV = HG_WIDTH // HG_HEADS
CHUNK = 64
HY_WIDTH = D_MODEL - HG_WIDTH
FILT_EMB = 33
FILT_ORDER = 64
DECAY_TARGET = 1e-2
DECAY_FAST_PCT = 0.3
DECAY_SLOW_PCT = 1.5
COL_Q = 3 * HG_WIDTH
COL_G = 4 * HG_WIDTH
COL_HY = 5 * HG_WIDTH
IN_COLS = 5 * HG_WIDTH + 3 * HY_WIDTH

kernel_name = 'hybrid_hgrn2_hyena_macaron_dit'

F32 = jnp.float32


def _rmsnorm(x, g):
    xf = x.astype(F32)
    y = xf * lax.rsqrt(jnp.mean(xf * xf, axis=-1, keepdims=True) + EPS)
    return (y * g.astype(F32)).astype(x.dtype)


def _modulate(h, shift, scale):
    return h * (1 + scale) + shift


def _swiglu(h, w1, w3, w2):
    return (jax.nn.silu(h @ w1) * (h @ w3)) @ w2


def _flip(a):
    return jnp.flip(a, axis=1)


def _heads(a):
    return a.reshape(a.shape[:-1] + (HG_HEADS, a.shape[-1] // HG_HEADS))


def _hgrn_gates(a, lb):
    a = a.astype(F32)
    logf = jnp.log(lb + (1 - lb) * jax.nn.sigmoid(a))
    k = (1 - lb) * jax.nn.sigmoid(-a)
    return _heads(k), _heads(logf)


def _hgrn_inputs(p, lb):
    k_f, g_f = _hgrn_gates(p[..., :HG_WIDTH], lb[0])
    k_b, g_b = _hgrn_gates(p[..., HG_WIDTH:2 * HG_WIDTH], lb[1])
    v = _heads(p[..., 2 * HG_WIDTH:COL_Q].astype(F32))
    return k_f, g_f, k_b, g_b, v


def _query(p):
    return _heads(p[..., COL_Q:COL_G].astype(F32)) * (HG_DK ** -0.5)


def _hgrn_scan(q, k, v, logf, s0):
    bsz, n, h, _ = q.shape
    dv = v.shape[-1]
    nc = n // CHUNK

    def to_chunks(a):
        return a.reshape(bsz, nc, CHUNK, h, a.shape[-1]).transpose(1, 0, 3, 2, 4)

    lower_tri = jnp.tril(jnp.ones((CHUNK, CHUNK), dtype=bool))[:, :, None]

    def step(s, inp):
        qc, kc, vc, gc = inp
        b = jnp.cumsum(gc, axis=2)
        o = jnp.einsum('bhtk,bhkv->bhtv', qc * jnp.exp(b), s)
        rel = jnp.exp(jnp.where(lower_tri, b[:, :, :, None, :] - b[:, :, None, :, :], -jnp.inf))
        scores = jnp.einsum('bhtk,bhtsk,bhsk->bhts', qc, rel, kc)
        o = o + jnp.einsum('bhts,bhsv->bhtv', scores, vc)
        b_end = b[:, :, -1:, :]
        s = jnp.exp(b_end[:, :, 0, :, None]) * s + jnp.einsum('bhsk,bhsv->bhkv', kc * jnp.exp(b_end - b), vc)
        return s, o

    s_fin, o = lax.scan(step, s0, (to_chunks(q), to_chunks(k), to_chunks(v), to_chunks(logf)))
    o = o.transpose(1, 0, 3, 2, 4).reshape(bsz, n, h, dv)
    return o, s_fin


def _final_state(k, v, logf):
    cum = jnp.cumsum(logf, axis=1)
    return jnp.einsum('blhk,blhv->bhkv', k * jnp.exp(cum[:, -1:] - cum), v)


def _bidir_hgrn(q, k_f, g_f, k_b, g_b, v, s_f, s_b):
    o_f, s_f = _hgrn_scan(q, k_f, v, g_f, s_f)
    o_b, s_b = _hgrn_scan(_flip(q), _flip(k_b), _flip(v), _flip(g_b), s_b)
    return o_f + _flip(o_b), s_f, s_b


def _hgrn_readout(o, g, gain):
    o = o * lax.rsqrt(jnp.mean(o * o, axis=-1, keepdims=True) + EPS) * gain.astype(F32).reshape(HG_HEADS, HG_DV)
    o = o.reshape(o.shape[:2] + (HG_WIDTH,))
    return (o * jax.nn.silu(g.astype(F32))).astype(g.dtype)


def _short_conv(u, w, b, on_grid):
    shape = u.shape
    if on_grid:
        rows = shape[1] // GRID_W
        u = u.reshape(shape[0], rows, GRID_W, shape[2])
    pad = [(0, 0)] * (u.ndim - 2) + [(1, 1), (0, 0)]
    up = jnp.pad(u, pad)
    y = up[..., :-2, :] * w[0] + up[..., 1:-1, :] * w[1] + up[..., 2:, :] * w[2] + b
    return y.reshape(shape)


def _hyena_spectrum(n, w1, b1, w2, b2, w3, b3, freq, w_out):
    w1, b1, w2, b2, w3, b3, freq, w_out = [a.astype(F32) for a in (w1, b1, w2, b2, w3, b3, freq, w_out)]
    pos = jnp.arange(n, dtype=F32)
    t = pos / max(n - 1, 1)
    bands = (FILT_EMB - 1) // 2
    fb = jnp.linspace(1e-4, bands - 1, bands, dtype=F32)
    ang = (2 * math.pi / n) * pos[:, None] * fb[None, :]
    z = jnp.concatenate([t[:, None], jnp.cos(ang), -jnp.sin(ang)], axis=-1)
    hid = jnp.sin(freq * (z @ w1 + b1))
    hid = jnp.sin(freq * (hid @ w2 + b2))
    hid = jnp.sin(freq * (hid @ w3 + b3))
    h = (hid @ w_out).reshape(n, 2, 2, HY_WIDTH)
    deltas = jnp.abs(jnp.linspace(math.log(DECAY_TARGET) / DECAY_SLOW_PCT,
                                  math.log(DECAY_TARGET) / DECAY_FAST_PCT, HY_WIDTH, dtype=F32))
    h = h * jnp.exp(-t[:, None] * deltas)[:, None, None, :]
    h_fwd = h[:, :, 0]
    h_bwd = h[1:, :, 1]
    l1 = jnp.sum(jnp.abs(h_fwd), axis=0) + jnp.sum(jnp.abs(h_bwd), axis=0)
    filt = jnp.concatenate([h_fwd, jnp.zeros((1, 2, HY_WIDTH), F32), h_bwd[::-1]], axis=0) / l1
    return jnp.fft.rfft(filt, axis=0)


def _long_conv(u, spec, d):
    n = u.shape[1]
    uf = u.astype(F32)
    y = jnp.fft.irfft(jnp.fft.rfft(uf, n=2 * n, axis=1) * spec, n=2 * n, axis=1)[:, :n]
    return (y + d.astype(F32) * uf).astype(u.dtype)


def _hyena(p, spec, d, conv_w, conv_b, on_grid):
    u = _short_conv(p, conv_w, conv_b, on_grid)
    v, x1, x2 = jnp.split(u, 3, axis=-1)
    z = x1 * _long_conv(v, spec[:, 0], d[0])
    return x2 * _long_conv(z, spec[:, 1], d[1])


def setup_inputs(seed: int = 0) -> dict:
    key = jax.random.key(seed)
    ks = iter(jax.random.split(key, 40))

    def nrm(shape, scale):
        return jax.random.normal(next(ks), shape, F32) * scale

    def gain(shape):
        return 1.0 + nrm(shape, 0.02)

    return {
        'x': nrm((BATCH, SEQ, D_MODEL), 1.0),
        'c': nrm((BATCH, D_MODEL), 1.0),
        'ctx': nrm((BATCH, CTX_LEN, D_MODEL), 1.0),
        'c_ctx': nrm((D_MODEL,), 1.0),
        'ada_w': nrm((DEPTH, D_MODEL, N_MOD * D_MODEL), D_MODEL ** -0.5),
        'ada_b': nrm((DEPTH, N_MOD * D_MODEL), 0.02),
        'norm_ffn1': gain((DEPTH, D_MODEL)),
        'ffn1_w1': nrm((DEPTH, D_MODEL, D_FF), D_MODEL ** -0.5),
        'ffn1_w3': nrm((DEPTH, D_MODEL, D_FF), D_MODEL ** -0.5),
        'ffn1_w2': nrm((DEPTH, D_FF, D_MODEL), D_FF ** -0.5),
        'norm_mix': gain((DEPTH, D_MODEL)),
        'w_in': nrm((DEPTH, D_MODEL, IN_COLS), D_MODEL ** -0.5),
        'hg_lb_logits': nrm((2, DEPTH + 1, HG_WIDTH), 0.5),
        'hg_norm': gain((DEPTH, HG_WIDTH)),
        'hy_conv_w': nrm((DEPTH, 3, 3 * HY_WIDTH), 3 ** -0.5),
        'hy_conv_b': nrm((DEPTH, 3 * HY_WIDTH), 0.02),
        'filt_w1': nrm((DEPTH, FILT_EMB, FILT_ORDER), FILT_EMB ** -0.5),
        'filt_b1': nrm((DEPTH, FILT_ORDER), 0.1),
        'filt_w2': nrm((DEPTH, FILT_ORDER, FILT_ORDER), FILT_ORDER ** -0.5),
        'filt_b2': nrm((DEPTH, FILT_ORDER), 0.1),
        'filt_w3': nrm((DEPTH, FILT_ORDER, FILT_ORDER), FILT_ORDER ** -0.5),
        'filt_b3': nrm((DEPTH, FILT_ORDER), 0.1),
        'filt_freq': 1.0 + nrm((DEPTH, FILT_ORDER), 0.1),
        'filt_w_out': nrm((DEPTH, FILT_ORDER, 4 * HY_WIDTH), FILT_ORDER ** -0.5),
        'hy_d': nrm((DEPTH, 2, HY_WIDTH), 0.1),
        'w_out': nrm((DEPTH, D_MODEL, D_MODEL), D_MODEL ** -0.5),
        'norm_ffn2': gain((DEPTH, D_MODEL)),
        'ffn2_w1': nrm((DEPTH, D_MODEL, D_FF), D_MODEL ** -0.5),
        'ffn2_w3': nrm((DEPTH, D_MODEL, D_FF), D_MODEL ** -0.5),
        'ffn2_w2': nrm((DEPTH, D_FF, D_MODEL), D_FF ** -0.5),
        'final_norm': gain((D_MODEL,)),
    }


def reference(x, c, ctx, c_ctx, ada_w, ada_b, norm_ffn1, ffn1_w1, ffn1_w3, ffn1_w2, norm_mix, w_in,
              hg_lb_logits, hg_norm, hy_conv_w, hy_conv_b, filt_w1, filt_b1, filt_w2, filt_b2, filt_w3,
              filt_b3, filt_freq, filt_w_out, hy_d, w_out, norm_ffn2, ffn2_w1, ffn2_w3, ffn2_w2, final_norm):
    bsz, lat_len, _ = x.shape
    ctx_len = ctx.shape[1]
    lower = jnp.cumsum(jax.nn.softmax(hg_lb_logits.astype(F32), axis=1), axis=1)
    zero_state = jnp.zeros((bsz, HG_HEADS, HG_DK, HG_DV), F32)
    y = ctx
    for l in range(DEPTH):
        last = l == DEPTH - 1
        mx = [m[:, None, :] for m in jnp.split(jax.nn.silu(c) @ ada_w[l] + ada_b[l], N_MOD, axis=-1)]
        mc = jnp.split(jax.nn.silu(c_ctx) @ ada_w[l] + ada_b[l], N_MOD, axis=-1)
        filt = (filt_w1[l], filt_b1[l], filt_w2[l], filt_b2[l], filt_w3[l], filt_b3[l], filt_freq[l], filt_w_out[l])
        lb = lower[:, l]

        x = x + 0.5 * mx[2] * _swiglu(_modulate(_rmsnorm(x, norm_ffn1[l]), mx[0], mx[1]),
                                      ffn1_w1[l], ffn1_w3[l], ffn1_w2[l])
        y = y + 0.5 * mc[2] * _swiglu(_modulate(_rmsnorm(y, norm_ffn1[l]), mc[0], mc[1]),
                                      ffn1_w1[l], ffn1_w3[l], ffn1_w2[l])

        hc = _modulate(_rmsnorm(y, norm_mix[l]), mc[3], mc[4])
        if last:
            pc = hc @ w_in[l][:, :COL_Q]
            kcf, gcf, kcb, gcb, vc = _hgrn_inputs(pc, lb)
            s_f = _final_state(kcf, vc, gcf)
            s_b = _final_state(_flip(kcb), _flip(vc), _flip(gcb))
        else:
            pc = hc @ w_in[l]
            kcf, gcf, kcb, gcb, vc = _hgrn_inputs(pc, lb)
            o_c, s_f, s_b = _bidir_hgrn(_query(pc), kcf, gcf, kcb, gcb, vc, zero_state, zero_state)
            mix_c = jnp.concatenate([
                _hgrn_readout(o_c, pc[..., COL_G:COL_HY], hg_norm[l]),
                _hyena(pc[..., COL_HY:], _hyena_spectrum(ctx_len, *filt), hy_d[l], hy_conv_w[l], hy_conv_b[l], False),
            ], axis=-1)
            y = y + mc[5] * (mix_c @ w_out[l])
            y = y + 0.5 * mc[8] * _swiglu(_modulate(_rmsnorm(y, norm_ffn2[l]), mc[6], mc[7]),
                                          ffn2_w1[l], ffn2_w3[l], ffn2_w2[l])

        hx = _modulate(_rmsnorm(x, norm_mix[l]), mx[3], mx[4])
        px = hx @ w_in[l]
        kxf, gxf, kxb, gxb, vx = _hgrn_inputs(px, lb)
        o_x, _, _ = _bidir_hgrn(_query(px), kxf, gxf, kxb, gxb, vx, s_f, s_b)
        mix_x = jnp.concatenate([
            _hgrn_readout(o_x, px[..., COL_G:COL_HY], hg_norm[l]),
            _hyena(px[..., COL_HY:], _hyena_spectrum(lat_len, *filt), hy_d[l], hy_conv_w[l], hy_conv_b[l], True),
        ], axis=-1)
        x = x + mx[5] * (mix_x @ w_out[l])

        x = x + 0.5 * mx[8] * _swiglu(_modulate(_rmsnorm(x, norm_ffn2[l]), mx[6], mx[7]),
                                      ffn2_w1[l], ffn2_w3[l], ffn2_w2[l])
    return _rmsnorm(x, final_norm)
```

```python
import functools
import math

import numpy as np
import jax
import jax.numpy as jnp
from jax import lax
from jax.experimental import pallas as pl
from jax.experimental.pallas import tpu as pltpu

F32 = jnp.float32
BF16 = jnp.bfloat16
EPS = 1e-6
N_MOD = 9
HEAD_DIM = 128
GRID_W = 64
SCAN_CHUNK = 64
FILT_EMB = 33
DECAY_TARGET = 1e-2
DECAY_FAST_PCT = 0.3
DECAY_SLOW_PCT = 1.5
VMEM_LIMIT_BYTES = 56 * 1024 * 1024
HIGHEST = lax.Precision.HIGHEST


def _params(*semantics):
    return pltpu.CompilerParams(dimension_semantics=semantics, vmem_limit_bytes=VMEM_LIMIT_BYTES)


def _dot(a, b):
    return jnp.dot(a, b, preferred_element_type=F32)


def _silu(a):
    return a * jax.nn.sigmoid(a)


def _ada_kernel(c_ref, w_ref, b_ref, o_ref):
    h = _silu(c_ref[...]).astype(BF16)
    o_ref[...] = _dot(h, w_ref[...].astype(BF16)) + b_ref[...]


def _ada(cs, w, b):
    rows, d = cs.shape
    n = w.shape[1]
    tn = d // 2
    return pl.pallas_call(
        _ada_kernel,
        out_shape=jax.ShapeDtypeStruct((rows, n), F32),
        grid=(n // tn,),
        in_specs=[pl.BlockSpec((rows, d), lambda j: (0, 0)),
                  pl.BlockSpec((d, tn), lambda j: (0, j)),
                  pl.BlockSpec((1, tn), lambda j: (0, j))],
        out_specs=pl.BlockSpec((rows, tn), lambda j: (0, j)),
        compiler_params=_params("arbitrary"),
        name="ada_mod",
    )(cs, w, b.reshape(1, n))


def _norm_mod(x, gain, shift, scale):
    y = x * lax.rsqrt(jnp.mean(x * x, axis=-1, keepdims=True) + EPS) * gain
    return (y * (1.0 + scale) + shift).astype(BF16)


def _ffn_kernel(x_ref, sh_ref, sc_ref, gt_ref, g_ref, w1_ref, w3_ref, w2_ref, fin_ref, o_ref,
                h_scr, acc_scr, *, final_norm):
    j = pl.program_id(1)

    @pl.when(j == 0)
    def _():
        h_scr[...] = _norm_mod(x_ref[...], g_ref[...], sh_ref[0], sc_ref[0])
        acc_scr[...] = jnp.zeros_like(acc_scr)

    h = h_scr[...]
    a = _dot(h, w1_ref[...])
    b = _dot(h, w3_ref[...])
    act = (_silu(a) * b).astype(BF16)
    acc_scr[...] += _dot(act, w2_ref[...])

    @pl.when(j == pl.num_programs(1) - 1)
    def _():
        out = x_ref[...] + 0.5 * gt_ref[0] * acc_scr[...]
        if final_norm:
            out = out * lax.rsqrt(jnp.mean(out * out, axis=-1, keepdims=True) + EPS) * fin_ref[...]
        o_ref[...] = out


def _ffn(x, shift, scale, gate, gain, w1, w3, w2, fin, *, tokens_per_mod, final_norm, tm, tf):
    t, d = x.shape
    f = w1.shape[1]
    tm = min(tm, t)
    mod_spec = pl.BlockSpec((1, 1, d), lambda i, j: ((i * tm) // tokens_per_mod, 0, 0))
    vec_spec = pl.BlockSpec((1, d), lambda i, j: (0, 0))
    return pl.pallas_call(
        functools.partial(_ffn_kernel, final_norm=final_norm),
        out_shape=jax.ShapeDtypeStruct((t, d), F32),
        grid=(t // tm, f // tf),
        in_specs=[pl.BlockSpec((tm, d), lambda i, j: (i, 0)),
                  mod_spec, mod_spec, mod_spec, vec_spec,
                  pl.BlockSpec((d, tf), lambda i, j: (0, j)),
                  pl.BlockSpec((d, tf), lambda i, j: (0, j)),
                  pl.BlockSpec((tf, d), lambda i, j: (j, 0)),
                  vec_spec],
        out_specs=pl.BlockSpec((tm, d), lambda i, j: (i, 0)),
        scratch_shapes=[pltpu.VMEM((tm, d), BF16), pltpu.VMEM((tm, d), F32)],
        compiler_params=_params("parallel", "arbitrary"),
        name="swiglu_ffn",
    )(x, shift, scale, gate, gain.reshape(1, d), w1, w3, w2, fin.reshape(1, d))


def _proj_kernel(x_ref, sh_ref, sc_ref, g_ref, w_ref, cw_ref, cb_ref, o_ref, h_scr, *, conv_from):
    j = pl.program_id(1)

    @pl.when(j == 0)
    def _():
        h_scr[...] = _norm_mod(x_ref[...], g_ref[...], sh_ref[0], sc_ref[0])

    p = _dot(h_scr[...], w_ref[...])

    @pl.when(j < conv_from)
    def _():
        o_ref[...] = p

    @pl.when(j >= conv_from)
    def _():
        tm = p.shape[0]
        col = lax.broadcasted_iota(jnp.int32, p.shape, 0) % GRID_W
        prev = jnp.where(col == 0, 0.0, pltpu.roll(p, 1, axis=0))
        nxt = jnp.where(col == GRID_W - 1, 0.0, pltpu.roll(p, tm - 1, axis=0))
        o_ref[...] = prev * cw_ref[0:1, :] + p * cw_ref[1:2, :] + nxt * cw_ref[2:3, :] + cb_ref[...]


def _proj(x, shift, scale, gain, w, conv_w, conv_b, *, n_seg, seg_w, conv_from, tokens_per_mod, tm):
    t, d = x.shape
    tm = min(tm, t)
    mod_spec = pl.BlockSpec((1, 1, d), lambda i, j: ((i * tm) // tokens_per_mod, 0, 0))
    conv_idx = lambda i, j: (0, jnp.maximum(j - conv_from, 0))
    return pl.pallas_call(
        functools.partial(_proj_kernel, conv_from=conv_from),
        out_shape=jax.ShapeDtypeStruct((t, n_seg * seg_w), F32),
        grid=(t // tm, n_seg),
        in_specs=[pl.BlockSpec((tm, d), lambda i, j: (i, 0)),
                  mod_spec, mod_spec,
                  pl.BlockSpec((1, d), lambda i, j: (0, 0)),
                  pl.BlockSpec((d, seg_w), lambda i, j: (0, j)),
                  pl.BlockSpec((3, seg_w), conv_idx),
                  pl.BlockSpec((1, seg_w), conv_idx)],
        out_specs=pl.BlockSpec((tm, seg_w), lambda i, j: (i, j)),
        scratch_shapes=[pltpu.VMEM((tm, d), BF16)],
        compiler_params=_params("parallel", "arbitrary"),
        name="in_proj",
    )(x, shift, scale, gain.reshape(1, d), w, conv_w, conv_b.reshape(1, -1))


def _split_dot(tri, g):
    g1 = g.astype(BF16)
    r1 = g - g1.astype(F32)
    g2 = r1.astype(BF16)
    g3 = (r1 - g2.astype(F32)).astype(BF16)
    return _dot(tri, g1) + _dot(tri, g2) + _dot(tri, g3)


def _gates(a, lb):
    f = lb + (1.0 - lb) * jax.nn.sigmoid(a)
    return 1.0 - f, jnp.log(f)


def _tn_dot(a, b):
    return lax.dot_general(a, b, (((0,), (0,)), ((), ())), preferred_element_type=F32)


def _nt_dot(a, b):
    return lax.dot_general(a, b, (((1,), (1,)), ((), ())), preferred_element_type=F32)


def _hgrn_kernel(af_ref, ab_ref, v_ref, q_ref, gate_ref, caf_ref, cab_ref, cv_ref, lbl_ref, gain_ref,
                 o_ref, of_scr, ob_scr, sf_scr, sb_scr, *, n_chunks, n_ctx_chunks):
    c = SCAN_CHUNK
    mid = c // 2
    row = lax.broadcasted_iota(jnp.int32, (c, c), 0)
    col = lax.broadcasted_iota(jnp.int32, (c, c), 1)
    lower = row >= col
    tri_fwd = jnp.where(lower, 1.0, 0.0).astype(BF16)
    tri_bwd = jnp.where(col >= row, 1.0, 0.0).astype(BF16)
    upper = col >= row

    def lower_bound(direction):
        lg = lbl_ref[direction]
        ex = jnp.exp(lg - jnp.max(lg, axis=0, keepdims=True))
        return ex[0:1, :] / jnp.sum(ex, axis=0, keepdims=True)

    lb_f = lower_bound(0)
    lb_b = lower_bound(1)
    q_scale = HEAD_DIM ** -0.5

    def chunk(a, v, q, st_ref, lb, tri, mask, end_row, want_out):
        k, g = _gates(a, lb)
        b = _split_dot(tri, g)
        b_end = b[end_row:end_row + 1, :]
        st = st_ref[...]
        vb = v.astype(BF16)
        out = None
        if want_out:
            b_mid = b[mid:mid + 1, :]
            qs = q * q_scale
            q_in = (qs * jnp.exp(b)).astype(BF16)
            q_t = (qs * jnp.exp(b - b_mid)).astype(BF16)
            k_t = (k * jnp.exp(b_mid - b)).astype(BF16)
            scores = jnp.where(mask, _nt_dot(q_t, k_t), 0.0).astype(BF16)
            out = _nt_dot(q_in, st.astype(BF16)) + _dot(scores, vb)
        k_d = (k * jnp.exp(b_end - b)).astype(BF16)
        st_ref[...] = st * jnp.exp(b_end) + _tn_dot(vb, k_d)
        return out

    sf_scr[...] = jnp.zeros_like(sf_scr)
    sb_scr[...] = jnp.zeros_like(sb_scr)

    def ctx_body(i, carry):
        lo = pl.multiple_of(i * c, c)
        chunk(caf_ref[pl.ds(lo, c), :], cv_ref[pl.ds(lo, c), :], None, sf_scr, lb_f, tri_fwd, lower, c - 1, False)
        hi = pl.multiple_of((n_ctx_chunks - 1 - i) * c, c)
        chunk(cab_ref[pl.ds(hi, c), :], cv_ref[pl.ds(hi, c), :], None, sb_scr, lb_b, tri_bwd, upper, 0, False)
        return carry

    lax.fori_loop(0, n_ctx_chunks, ctx_body, 0)

    def body(i, carry):
        lo = pl.multiple_of(i * c, c)
        of_scr[pl.ds(lo, c), :] = chunk(af_ref[pl.ds(lo, c), :], v_ref[pl.ds(lo, c), :], q_ref[pl.ds(lo, c), :],
                                        sf_scr, lb_f, tri_fwd, lower, c - 1, True)
        hi = pl.multiple_of((n_chunks - 1 - i) * c, c)
        ob_scr[pl.ds(hi, c), :] = chunk(ab_ref[pl.ds(hi, c), :], v_ref[pl.ds(hi, c), :], q_ref[pl.ds(hi, c), :],
                                        sb_scr, lb_b, tri_bwd, upper, 0, True)
        return carry

    lax.fori_loop(0, n_chunks, body, 0)

    gain = gain_ref[...]
    rows = 4 * c

    def readout(i, carry):
        lo = pl.multiple_of(i * rows, rows)
        o = of_scr[pl.ds(lo, rows), :] + ob_scr[pl.ds(lo, rows), :]
        o = o * lax.rsqrt(jnp.mean(o * o, axis=-1, keepdims=True) + EPS) * gain
        o_ref[pl.ds(lo, rows), :] = o * _silu(gate_ref[pl.ds(lo, rows), :])
        return carry

    lax.fori_loop(0, n_chunks // 4, readout, 0)


def _hgrn(px, pc, lb_logits, gain, *, width):
    bsz, n, _ = px.shape
    n_ctx = pc.shape[1]
    heads = width // HEAD_DIM
    hd = HEAD_DIM

    def seg(s):
        return pl.BlockSpec((None, n, hd), lambda b, h: (b, 0, s * heads + h))

    def cseg(s):
        return pl.BlockSpec((None, n_ctx, hd), lambda b, h: (b, 0, s * heads + h))

    depth1 = lb_logits.shape[1]
    return pl.pallas_call(
        functools.partial(_hgrn_kernel, n_chunks=n // SCAN_CHUNK, n_ctx_chunks=n_ctx // SCAN_CHUNK),
        out_shape=jax.ShapeDtypeStruct((bsz, n, width), F32),
        grid=(bsz, heads),
        in_specs=[seg(0), seg(1), seg(2), seg(3), seg(4), cseg(0), cseg(1), cseg(2),
                  pl.BlockSpec((2, depth1, hd), lambda b, h: (0, 0, h)),
                  pl.BlockSpec((1, hd), lambda b, h: (0, h))],
        out_specs=pl.BlockSpec((None, n, hd), lambda b, h: (b, 0, h)),
        scratch_shapes=[pltpu.VMEM((n, hd), F32), pltpu.VMEM((n, hd), F32),
                        pltpu.VMEM((hd, hd), F32), pltpu.VMEM((hd, hd), F32)],
        compiler_params=_params("parallel", "parallel"),
        name="hgrn_scan",
    )(px, px, px, px, px, pc, pc, pc, lb_logits, gain.reshape(1, width))


def _dft_constants(n):
    n2 = GRID_W
    n1h = n // n2
    n1 = 2 * n1h
    big = 2 * n
    k1 = np.arange(n1h)[:, None]
    m1 = np.arange(n1h)[None, :]
    ang = -2.0 * np.pi * m1 * (k1 + 0.5) / n1
    g_fwd = np.concatenate([np.cos(ang), np.sin(ang)], axis=0)
    g_inv = np.concatenate([np.cos(ang).T, np.sin(ang).T], axis=1)
    m2 = np.arange(n2)[None, :]
    tw_ang = -2.0 * np.pi * m2 * (k1 + 0.5) / big
    tw = np.stack([np.cos(tw_ang), np.sin(tw_ang)], axis=1)
    tw = np.broadcast_to(tw[..., None], (n1h, 2, n2, 128))
    k2 = np.arange(n2)[:, None]
    f_ang = -2.0 * np.pi * k2 * m2 / n2
    fr, fi = np.cos(f_ang), np.sin(f_ang)
    f_fwd = np.block([[fr, -fi], [fi, fr]])
    f_inv = np.block([[fr, fi], [-fi, fr]])
    return g_fwd, g_inv, np.ascontiguousarray(tw), f_fwd, f_inv


def _filt_mlp_kernel(z_ref, w1_ref, b1_ref, w2_ref, b2_ref, w3_ref, b3_ref, fr_ref, o_ref):
    fr = fr_ref[...]
    hp = lambda a, b: jnp.dot(a, b, precision=HIGHEST, preferred_element_type=F32)
    hid = jnp.sin(fr * (hp(z_ref[...], w1_ref[...]) + b1_ref[...]))
    hid = jnp.sin(fr * (hp(hid, w2_ref[...]) + b2_ref[...]))
    o_ref[...] = jnp.sin(fr * (hp(hid, w3_ref[...]) + b3_ref[...]))


def _filt_taps_kernel(hid_ref, wo_ref, dl_ref, h_ref, s_ref, *, n):
    h = jnp.dot(hid_ref[...], wo_ref[...], precision=HIGHEST, preferred_element_type=F32)
    t = lax.broadcasted_iota(jnp.int32, h.shape, 0).astype(F32) * (1.0 / max(n - 1, 1))
    hw = h * jnp.exp(-t * dl_ref[...])
    h_ref[...] = hw
    s_ref[0:1, :] = jnp.sum(jnp.abs(hw), axis=0, keepdims=True)
    s_ref[1:2, :] = jnp.abs(hw[0:1, :])


def _spectrum_kernel(af_ref, ab_ref, tw_ref, f_ref, s_ref, d_ref, o_ref, *, n):
    n2 = GRID_W
    lanes = af_ref.shape[-1]
    reps = lanes // 128
    twr = jnp.tile(tw_ref[0], (1, reps))
    twi = jnp.tile(tw_ref[1], (1, reps))
    fm = f_ref[...]

    def inner(a_ref):
        ar, ai = a_ref[0], a_ref[1]
        stacked = jnp.concatenate([ar * twr - ai * twi, ar * twi + ai * twr], axis=0)
        z = jnp.dot(fm, stacked, precision=HIGHEST, preferred_element_type=F32)
        return z[:n2], z[n2:]

    zfr, zfi = inner(af_ref)
    zbr, zbi = inner(ab_ref)
    l1 = s_ref[0:1, :] + s_ref[2:3, :] - s_ref[3:4, :]
    hb0 = s_ref[4:5, :]
    norm = 1.0 / l1
    scale = 2.0 / (2 * n)
    o_ref[0] = scale * ((zfr + zbr - hb0) * norm + d_ref[...])
    o_ref[1] = scale * ((zfi - zbi) * norm)


def _outer_dft_kernel(g_ref, x_ref, o_ref, *, high):
    n1h = x_ref.shape[0]
    if high:
        r = jnp.dot(g_ref[...], x_ref[...], precision=HIGHEST, preferred_element_type=F32)
    else:
        r = _dot(g_ref[...], x_ref[...].astype(BF16))
    o_ref[0] = r[:n1h]
    o_ref[1] = r[n1h:]


def _outer_dft(g, x, *, high, tl):
    s, n1h, lanes = x.shape
    tl = min(tl, lanes)
    return pl.pallas_call(
        functools.partial(_outer_dft_kernel, high=high),
        out_shape=jax.ShapeDtypeStruct((s, 2, n1h, lanes), F32),
        grid=(s, lanes // tl),
        in_specs=[pl.BlockSpec((2 * n1h, n1h), lambda b, l: (0, 0)),
                  pl.BlockSpec((None, n1h, tl), lambda b, l: (b, 0, l))],
        out_specs=pl.BlockSpec((None, 2, n1h, tl), lambda b, l: (b, 0, 0, l)),
        compiler_params=_params("parallel", "parallel"),
        name="outer_dft",
    )(g, x)


def _hyena_spectra(n, width, filt, hy_d, consts):
    w1, b1, w2, b2, w3, b3, freq, w_out = filt
    g_fwd, _, tw, f_fwd, _ = consts
    n1h = n // GRID_W
    order = w1.shape[1]
    pos = np.arange(n, dtype=np.float64)
    t = pos / max(n - 1, 1)
    bands = (FILT_EMB - 1) // 2
    fb = np.linspace(1e-4, bands - 1, bands)
    ang = (2 * math.pi / n) * pos[:, None] * fb[None, :]
    z = np.concatenate([t[:, None], np.cos(ang), -np.sin(ang)], axis=-1)
    emb_pad = 128
    z = jnp.asarray(np.pad(z, ((0, 0), (0, emb_pad - FILT_EMB))), F32)
    w1p = jnp.pad(w1, ((0, emb_pad - FILT_EMB), (0, 0)))
    row = lambda a: a.reshape(1, -1)
    full = lambda shape: pl.BlockSpec(shape, lambda: tuple(0 for _ in shape))
    hid = pl.pallas_call(
        _filt_mlp_kernel,
        out_shape=jax.ShapeDtypeStruct((n, order), F32),
        in_specs=[full((n, emb_pad)), full((emb_pad, order)), full((1, order)), full((order, order)),
                  full((1, order)), full((order, order)), full((1, order)), full((1, order))],
        out_specs=full((n, order)),
        compiler_params=pltpu.CompilerParams(vmem_limit_bytes=VMEM_LIMIT_BYTES),
        name="filter_mlp",
    )(z, w1p, row(b1), w2, row(b2), w3, row(b3), row(freq))

    cols = 4 * width
    deltas = np.abs(np.linspace(math.log(DECAY_TARGET) / DECAY_SLOW_PCT,
                                math.log(DECAY_TARGET) / DECAY_FAST_PCT, width))
    deltas4 = jnp.asarray(np.tile(deltas, 4)[None, :], F32)
    tc = min(512, cols)
    taps, sums = pl.pallas_call(
        functools.partial(_filt_taps_kernel, n=n),
        out_shape=(jax.ShapeDtypeStruct((n, cols), F32), jax.ShapeDtypeStruct((2, cols), F32)),
        grid=(cols // tc,),
        in_specs=[pl.BlockSpec((n, order), lambda j: (0, 0)),
                  pl.BlockSpec((order, tc), lambda j: (0, j)),
                  pl.BlockSpec((1, tc), lambda j: (0, j))],
        out_specs=(pl.BlockSpec((n, tc), lambda j: (0, j)), pl.BlockSpec((2, tc), lambda j: (0, j))),
        compiler_params=_params("parallel"),
        name="filter_taps",
    )(hid, w_out, deltas4)

    a = _outer_dft(jnp.asarray(g_fwd, F32), taps.reshape(1, n1h, GRID_W * cols), high=True, tl=8192)
    a = a.reshape(2, n1h, GRID_W, cols)
    s4 = sums.reshape(2, 2, 2, width)
    bwd0 = taps[0].reshape(2, 2, width)[:, 1]
    stats = jnp.stack([s4[0, :, 0], s4[1, :, 0], s4[0, :, 1], s4[1, :, 1], bwd0], axis=1)

    def tap_spec(side):
        return pl.BlockSpec((2, None, GRID_W, width), lambda k1, f: (0, k1, 0, 2 * f + side))

    return pl.pallas_call(
        functools.partial(_spectrum_kernel, n=n),
        out_shape=jax.ShapeDtypeStruct((2, 2, n1h, GRID_W, width), F32),
        grid=(n1h, 2),
        in_specs=[tap_spec(0), tap_spec(1),
                  pl.BlockSpec((None, 2, GRID_W, 128), lambda k1, f: (k1, 0, 0, 0)),
                  pl.BlockSpec((2 * GRID_W, 2 * GRID_W), lambda k1, f: (0, 0)),
                  pl.BlockSpec((None, 5, width), lambda k1, f: (f, 0, 0)),
                  pl.BlockSpec((None, 1, width), lambda k1, f: (f, 0, 0))],
        out_specs=pl.BlockSpec((None, 2, None, GRID_W, width), lambda k1, f: (f, 0, k1, 0, 0)),
        compiler_params=_params("parallel", "parallel"),
        name="filter_spectrum",
    )(a, a, jnp.asarray(tw, F32), jnp.asarray(f_fwd, F32), stats, hy_d.reshape(2, 1, width))


def _inner_conv_kernel(a_ref, tw_ref, f_ref, fi_ref, h_ref, o_ref):
    n2 = GRID_W
    reps = a_ref.shape[-1] // 128
    twr = jnp.tile(tw_ref[0], (1, reps))
    twi = jnp.tile(tw_ref[1], (1, reps))
    ar, ai = a_ref[0], a_ref[1]
    stacked = jnp.concatenate([ar * twr - ai * twi, ar * twi + ai * twr], axis=0).astype(BF16)
    z = _dot(f_ref[...], stacked)
    zr, zi = z[:n2], z[n2:]
    hr, hi = h_ref[0], h_ref[1]
    stacked = jnp.concatenate([zr * hr - zi * hi, zr * hi + zi * hr], axis=0).astype(BF16)
    w = _dot(fi_ref[...], stacked)
    wr, wi = w[:n2], w[n2:]
    o_ref[0] = wr * twr + wi * twi
    o_ref[1] = wi * twr - wr * twi


def _inner_conv(a, tw, f_fwd, f_inv, spec):
    bsz, _, n1h, lanes = a.shape
    c = lanes // GRID_W
    a5 = a.reshape(bsz, 2, n1h, GRID_W, c)
    out = pl.pallas_call(
        _inner_conv_kernel,
        out_shape=jax.ShapeDtypeStruct(a5.shape, F32),
        grid=(n1h, bsz),
        in_specs=[pl.BlockSpec((None, 2, None, GRID_W, c), lambda k1, b: (b, 0, k1, 0, 0)),
                  pl.BlockSpec((None, 2, GRID_W, 128), lambda k1, b: (k1, 0, 0, 0)),
                  pl.BlockSpec((2 * GRID_W, 2 * GRID_W), lambda k1, b: (0, 0)),
                  pl.BlockSpec((2 * GRID_W, 2 * GRID_W), lambda k1, b: (0, 0)),
                  pl.BlockSpec((2, None, GRID_W, c), lambda k1, b: (0, k1, 0, 0))],
        out_specs=pl.BlockSpec((None, 2, None, GRID_W, c), lambda k1, b: (b, 0, k1, 0, 0)),
        compiler_params=_params("parallel", "arbitrary"),
        name="inner_conv",
    )(a5, tw, f_fwd, f_inv, spec)
    return out.reshape(a.shape)


def _outer_idft_gate_kernel(gi_ref, g_ref, b_ref, x_ref, o_ref, *, again):
    n1h = x_ref.shape[0]
    stacked = jnp.concatenate([b_ref[0], b_ref[1]], axis=0).astype(BF16)
    y = _dot(gi_ref[...], stacked) * x_ref[...]
    if again:
        r = _dot(g_ref[...], y.astype(BF16))
        o_ref[0] = r[:n1h]
        o_ref[1] = r[n1h:]
    else:
        o_ref[...] = y


def _outer_idft_gate(g_inv, g_fwd, b, px3, seg, *, again, tl):
    bsz, _, n1h, lanes = b.shape
    c = lanes // GRID_W
    tl = min(tl, c)
    per = c // tl
    n_seg = px3.shape[-1] // (GRID_W * c)
    gate_idx = lambda bb, l: (bb, 0, ((l // per) * n_seg + seg) * per + l % per)
    if again:
        out_shape = jax.ShapeDtypeStruct((bsz, 2, n1h, lanes), F32)
        out_spec = pl.BlockSpec((None, 2, n1h, tl), lambda bb, l: (bb, 0, 0, l))
    else:
        out_shape = jax.ShapeDtypeStruct((bsz, n1h, lanes), F32)
        out_spec = pl.BlockSpec((None, n1h, tl), lambda bb, l: (bb, 0, l))
    return pl.pallas_call(
        functools.partial(_outer_idft_gate_kernel, again=again),
        out_shape=out_shape,
        grid=(bsz, lanes // tl),
        in_specs=[pl.BlockSpec((n1h, 2 * n1h), lambda bb, l: (0, 0)),
                  pl.BlockSpec((2 * n1h, n1h), lambda bb, l: (0, 0)),
                  pl.BlockSpec((None, 2, n1h, tl), lambda bb, l: (bb, 0, 0, l)),
                  pl.BlockSpec((None, n1h, tl), gate_idx)],
        out_specs=out_spec,
        compiler_params=_params("parallel", "parallel"),
        name="outer_idft_gate",
    )(g_inv, g_fwd, b, px3)


def _outer_dft_seg_kernel(g_ref, x_ref, o_ref):
    n1h = x_ref.shape[0]
    r = _dot(g_ref[...], x_ref[...].astype(BF16))
    o_ref[0] = r[:n1h]
    o_ref[1] = r[n1h:]


def _outer_dft_seg(g_fwd, px3, seg, *, c, tl):
    bsz, n1h, total = px3.shape
    tl = min(tl, c)
    per = c // tl
    n_seg = total // (GRID_W * c)
    lanes = GRID_W * c
    return pl.pallas_call(
        _outer_dft_seg_kernel,
        out_shape=jax.ShapeDtypeStruct((bsz, 2, n1h, lanes), F32),
        grid=(bsz, lanes // tl),
        in_specs=[pl.BlockSpec((2 * n1h, n1h), lambda bb, l: (0, 0)),
                  pl.BlockSpec((None, n1h, tl), lambda bb, l: (bb, 0, ((l // per) * n_seg + seg) * per + l % per))],
        out_specs=pl.BlockSpec((None, 2, n1h, tl), lambda bb, l: (bb, 0, 0, l)),
        compiler_params=_params("parallel", "parallel"),
        name="outer_dft_seg",
    )(g_fwd, px3)


def _hyena_mix(px, spec, consts, *, bsz, n, width):
    g_fwd, g_inv, tw, f_fwd, f_inv = consts
    g_fwd_b, g_inv_b = jnp.asarray(g_fwd, BF16), jnp.asarray(g_inv, BF16)
    f_fwd_b, f_inv_b = jnp.asarray(f_fwd, BF16), jnp.asarray(f_inv, BF16)
    tw_f = jnp.asarray(tw, F32)
    n1h = n // GRID_W
    px3 = px.reshape(bsz, n1h, -1)
    tl = width
    a = _outer_dft_seg(g_fwd_b, px3, 5, c=width, tl=tl)
    a = _inner_conv(a, tw_f, f_fwd_b, f_inv_b, spec[0])
    a = _outer_idft_gate(g_inv_b, g_fwd_b, a, px3, 6, again=True, tl=tl)
    a = _inner_conv(a, tw_f, f_fwd_b, f_inv_b, spec[1])
    hy = _outer_idft_gate(g_inv_b, g_fwd_b, a, px3, 7, again=False, tl=tl)
    return hy.reshape(bsz * n, width)


def _out_proj_kernel(x_ref, hg_ref, hy_ref, wa_ref, wb_ref, gt_ref, o_ref):
    mix = _dot(hg_ref[...].astype(BF16), wa_ref[...]) + _dot(hy_ref[...].astype(BF16), wb_ref[...])
    o_ref[...] = x_ref[...] + gt_ref[0] * mix


def _out_proj(x, hg, hy, w, gate, *, tokens_per_mod, tm):
    t, d = x.shape
    wdt = hg.shape[1]
    tm = min(tm, t)
    return pl.pallas_call(
        _out_proj_kernel,
        out_shape=jax.ShapeDtypeStruct((t, d), F32),
        grid=(t // tm,),
        in_specs=[pl.BlockSpec((tm, d), lambda i: (i, 0)),
                  pl.BlockSpec((tm, wdt), lambda i: (i, 0)),
                  pl.BlockSpec((tm, wdt), lambda i: (i, 0)),
                  pl.BlockSpec((wdt, d), lambda i: (0, 0)),
                  pl.BlockSpec((wdt, d), lambda i: (1, 0)),
                  pl.BlockSpec((1, 1, d), lambda i: ((i * tm) // tokens_per_mod, 0, 0))],
        out_specs=pl.BlockSpec((tm, d), lambda i: (i, 0)),
        compiler_params=_params("parallel"),
        name="out_proj",
    )(x, hg, hy, w, w, gate)


def kernel(x, c, ctx, c_ctx, ada_w, ada_b, norm_ffn1, ffn1_w1, ffn1_w3, ffn1_w2, norm_mix, w_in, hg_lb_logits, hg_norm, hy_conv_w, hy_conv_b, filt_w1, filt_b1, filt_w2, filt_b2, filt_w3, filt_b3, filt_freq, filt_w_out, hy_d, w_out, norm_ffn2, ffn2_w1, ffn2_w3, ffn2_w2, final_norm):
    bsz, n, d = x.shape
    n_ctx = ctx.shape[1]
    depth = ada_w.shape[0]
    assert depth == 1, "single-layer block"
    width = d // 2
    n_seg = w_in.shape[2] // width
    assert n_seg == 8 and n % GRID_W == 0 and n_ctx % SCAN_CHUNK == 0 and width % HEAD_DIM == 0
    d_ff = ffn1_w1.shape[2]
    tf = 512 if d_ff % 512 == 0 else d_ff
    tm = 512
    l = 0

    rows = -(-(bsz + 1) // 8) * 8
    cs = jnp.concatenate([c, c_ctx[None, :], jnp.zeros((rows - bsz - 1, d), F32)], axis=0)
    mods = _ada(cs, ada_w[l], ada_b[l]).reshape(rows, N_MOD, d)
    mx = [mods[:bsz, i][:, None, :] for i in range(N_MOD)]
    mc = [mods[bsz:bsz + 1, i][:, None, :] for i in range(N_MOD)]

    bf = lambda a: a.astype(BF16)
    w1a, w3a, w2a = bf(ffn1_w1[l]), bf(ffn1_w3[l]), bf(ffn1_w2[l])
    xt = x.reshape(bsz * n, d)
    yt = ctx.reshape(bsz * n_ctx, d)

    xt = _ffn(xt, mx[0], mx[1], mx[2], norm_ffn1[l], w1a, w3a, w2a, final_norm,
              tokens_per_mod=n, final_norm=False, tm=tm, tf=tf)
    yt = _ffn(yt, mc[0], mc[1], mc[2], norm_ffn1[l], w1a, w3a, w2a, final_norm,
              tokens_per_mod=bsz * n_ctx, final_norm=False, tm=tm, tf=tf)

    w_in_b = bf(w_in[l])
    pc = _proj(yt, mc[3], mc[4], norm_mix[l], w_in_b, hy_conv_w[l], hy_conv_b[l], n_seg=3, seg_w=width,
               conv_from=n_seg, tokens_per_mod=bsz * n_ctx, tm=tm)
    px = _proj(xt, mx[3], mx[4], norm_mix[l], w_in_b, hy_conv_w[l], hy_conv_b[l], n_seg=n_seg, seg_w=width,
               conv_from=5, tokens_per_mod=n, tm=tm)
    cols = n_seg * width

    hg = _hgrn(px.reshape(bsz, n, cols), pc.reshape(bsz, n_ctx, 3 * width), hg_lb_logits, hg_norm[l], width=width)

    consts = _dft_constants(n)
    filt = (filt_w1[l], filt_b1[l], filt_w2[l], filt_b2[l], filt_w3[l], filt_b3[l], filt_freq[l], filt_w_out[l])
    spec = _hyena_spectra(n, width, filt, hy_d[l], consts)
    hy = _hyena_mix(px, spec, consts, bsz=bsz, n=n, width=width)

    xt = _out_proj(xt, hg.reshape(bsz * n, width), hy, bf(w_out[l]), mx[5],
                   tokens_per_mod=n, tm=tm)

    out = _ffn(xt, mx[6], mx[7], mx[8], norm_ffn2[l], bf(ffn2_w1[l]), bf(ffn2_w3[l]), bf(ffn2_w2[l]), final_norm,
               tokens_per_mod=n, final_norm=True, tm=tm, tf=tf)
    return out.reshape(bsz, n, d)
```

```python
import functools
import math

import numpy as np
import jax
import jax.numpy as jnp
from jax import lax
from jax.experimental import pallas as pl
from jax.experimental.pallas import tpu as pltpu

F32 = jnp.float32
BF16 = jnp.bfloat16
EPS = 1e-6
N_MOD = 9
HEAD_DIM = 128
GRID_W = 64
SCAN_CHUNK = 64
OUTER_ROWS = 8
INNER_BINS = 4
FILT_EMB = 33
DECAY_TARGET = 1e-2
DECAY_FAST_PCT = 0.3
DECAY_SLOW_PCT = 1.5
VMEM_LIMIT_BYTES = 56 * 1024 * 1024
HIGHEST = lax.Precision.HIGHEST


def _params(*semantics):
    return pltpu.CompilerParams(dimension_semantics=semantics, vmem_limit_bytes=VMEM_LIMIT_BYTES)


def _dot(a, b):
    return jnp.dot(a, b, preferred_element_type=F32)


def _silu(a):
    return a * jax.nn.sigmoid(a)


def _ada_kernel(c_ref, w_ref, b_ref, o_ref):
    h = _silu(c_ref[...]).astype(BF16)
    o_ref[...] = _dot(h, w_ref[...].astype(BF16)) + b_ref[...]


def _ada(cs, w, b):
    rows, d = cs.shape
    n = w.shape[1]
    tn = d // 2
    return pl.pallas_call(
        _ada_kernel,
        out_shape=jax.ShapeDtypeStruct((rows, n), F32),
        grid=(n // tn,),
        in_specs=[pl.BlockSpec((rows, d), lambda j: (0, 0)),
                  pl.BlockSpec((d, tn), lambda j: (0, j)),
                  pl.BlockSpec((1, tn), lambda j: (0, j))],
        out_specs=pl.BlockSpec((rows, tn), lambda j: (0, j)),
        compiler_params=_params("arbitrary"),
        name="ada_mod",
    )(cs, w, b.reshape(1, n))


def _norm_mod(x, gain, shift, scale):
    y = x * lax.rsqrt(jnp.mean(x * x, axis=-1, keepdims=True) + EPS) * gain
    return (y * (1.0 + scale) + shift).astype(BF16)


def _ffn_kernel(x_ref, sh_ref, sc_ref, gt_ref, g_ref, w1_ref, w3_ref, w2_ref, fin_ref, o_ref,
                h_scr, acc_scr, *, final_norm):
    j = pl.program_id(1)

    @pl.when(j == 0)
    def _():
        h_scr[...] = _norm_mod(x_ref[...], g_ref[...], sh_ref[0], sc_ref[0])
        acc_scr[...] = jnp.zeros_like(acc_scr)

    h = h_scr[...]
    a = _dot(h, w1_ref[...])
    b = _dot(h, w3_ref[...])
    act = (_silu(a) * b).astype(BF16)
    acc_scr[...] += _dot(act, w2_ref[...])

    @pl.when(j == pl.num_programs(1) - 1)
    def _():
        out = x_ref[...] + 0.5 * gt_ref[0] * acc_scr[...]
        if final_norm:
            out = out * lax.rsqrt(jnp.mean(out * out, axis=-1, keepdims=True) + EPS) * fin_ref[...]
        o_ref[...] = out


def _ffn(x, shift, scale, gate, gain, w1, w3, w2, fin, *, tokens_per_mod, final_norm, tm, tf):
    t, d = x.shape
    f = w1.shape[1]
    tm = min(tm, t)
    mod_spec = pl.BlockSpec((1, 1, d), lambda i, j: ((i * tm) // tokens_per_mod, 0, 0))
    vec_spec = pl.BlockSpec((1, d), lambda i, j: (0, 0))
    return pl.pallas_call(
        functools.partial(_ffn_kernel, final_norm=final_norm),
        out_shape=jax.ShapeDtypeStruct((t, d), F32),
        grid=(t // tm, f // tf),
        in_specs=[pl.BlockSpec((tm, d), lambda i, j: (i, 0)),
                  mod_spec, mod_spec, mod_spec, vec_spec,
                  pl.BlockSpec((d, tf), lambda i, j: (0, j)),
                  pl.BlockSpec((d, tf), lambda i, j: (0, j)),
                  pl.BlockSpec((tf, d), lambda i, j: (j, 0)),
                  vec_spec],
        out_specs=pl.BlockSpec((tm, d), lambda i, j: (i, 0)),
        scratch_shapes=[pltpu.VMEM((tm, d), BF16), pltpu.VMEM((tm, d), F32)],
        compiler_params=_params("parallel", "arbitrary"),
        name="swiglu_ffn",
    )(x, shift, scale, gate, gain.reshape(1, d), w1, w3, w2, fin.reshape(1, d))


def _proj_kernel(x_ref, sh_ref, sc_ref, g_ref, w_ref, cw_ref, cb_ref, o_ref, h_scr, *, conv_from):
    j = pl.program_id(1)

    @pl.when(j == 0)
    def _():
        h_scr[...] = _norm_mod(x_ref[...], g_ref[...], sh_ref[0], sc_ref[0])

    p = _dot(h_scr[...], w_ref[...])

    @pl.when(j < conv_from)
    def _():
        o_ref[...] = p

    @pl.when(j >= conv_from)
    def _():
        tm = p.shape[0]
        col = lax.broadcasted_iota(jnp.int32, p.shape, 0) % GRID_W
        prev = jnp.where(col == 0, 0.0, pltpu.roll(p, 1, axis=0))
        nxt = jnp.where(col == GRID_W - 1, 0.0, pltpu.roll(p, tm - 1, axis=0))
        o_ref[...] = prev * cw_ref[0:1, :] + p * cw_ref[1:2, :] + nxt * cw_ref[2:3, :] + cb_ref[...]


def _proj(x, shift, scale, gain, w, conv_w, conv_b, *, n_seg, seg_w, conv_from, tokens_per_mod, tm):
    t, d = x.shape
    tm = min(tm, t)
    mod_spec = pl.BlockSpec((1, 1, d), lambda i, j: ((i * tm) // tokens_per_mod, 0, 0))
    conv_idx = lambda i, j: (0, jnp.maximum(j - conv_from, 0))
    return pl.pallas_call(
        functools.partial(_proj_kernel, conv_from=conv_from),
        out_shape=jax.ShapeDtypeStruct((t, n_seg * seg_w), F32),
        grid=(t // tm, n_seg),
        in_specs=[pl.BlockSpec((tm, d), lambda i, j: (i, 0)),
                  mod_spec, mod_spec,
                  pl.BlockSpec((1, d), lambda i, j: (0, 0)),
                  pl.BlockSpec((d, seg_w), lambda i, j: (0, j)),
                  pl.BlockSpec((3, seg_w), conv_idx),
                  pl.BlockSpec((1, seg_w), conv_idx)],
        out_specs=pl.BlockSpec((tm, seg_w), lambda i, j: (i, j)),
        scratch_shapes=[pltpu.VMEM((tm, d), BF16)],
        compiler_params=_params("parallel", "arbitrary"),
        name="in_proj",
    )(x, shift, scale, gain.reshape(1, d), w, conv_w, conv_b.reshape(1, -1))


def _split_dot(tri, g):
    g1 = g.astype(BF16)
    r1 = g - g1.astype(F32)
    g2 = r1.astype(BF16)
    g3 = (r1 - g2.astype(F32)).astype(BF16)
    return _dot(tri, g1) + _dot(tri, g2) + _dot(tri, g3)


def _gates(a, lb):
    f = lb + (1.0 - lb) * jax.nn.sigmoid(a)
    return 1.0 - f, jnp.log(f)


def _tn_dot(a, b):
    return lax.dot_general(a, b, (((0,), (0,)), ((), ())), preferred_element_type=F32)


def _nt_dot(a, b):
    return lax.dot_general(a, b, (((1,), (1,)), ((), ())), preferred_element_type=F32)


def _hgrn_kernel(af_ref, ab_ref, v_ref, q_ref, gate_ref, caf_ref, cab_ref, cv_ref, lbl_ref, gain_ref,
                 o_ref, of_scr, ob_scr, sf_scr, sb_scr, *, n_chunks, n_ctx_chunks):
    c = SCAN_CHUNK
    mid = c // 2
    row = lax.broadcasted_iota(jnp.int32, (c, c), 0)
    col = lax.broadcasted_iota(jnp.int32, (c, c), 1)
    lower = row >= col
    tri_fwd = jnp.where(lower, 1.0, 0.0).astype(BF16)
    tri_bwd = jnp.where(col >= row, 1.0, 0.0).astype(BF16)
    upper = col >= row

    def lower_bound(direction):
        lg = lbl_ref[direction]
        ex = jnp.exp(lg - jnp.max(lg, axis=0, keepdims=True))
        return ex[0:1, :] / jnp.sum(ex, axis=0, keepdims=True)

    lb_f = lower_bound(0)
    lb_b = lower_bound(1)
    q_scale = HEAD_DIM ** -0.5

    def chunk(a, v, q, st_ref, lb, tri, mask, end_row, want_out):
        k, g = _gates(a, lb)
        b = _split_dot(tri, g)
        b_end = b[end_row:end_row + 1, :]
        st = st_ref[...]
        vb = v.astype(BF16)
        out = None
        if want_out:
            b_mid = b[mid:mid + 1, :]
            qs = q * q_scale
            q_in = (qs * jnp.exp(b)).astype(BF16)
            q_t = (qs * jnp.exp(b - b_mid)).astype(BF16)
            k_t = (k * jnp.exp(b_mid - b)).astype(BF16)
            scores = jnp.where(mask, _nt_dot(q_t, k_t), 0.0).astype(BF16)
            out = _nt_dot(q_in, st.astype(BF16)) + _dot(scores, vb)
        k_d = (k * jnp.exp(b_end - b)).astype(BF16)
        st_ref[...] = st * jnp.exp(b_end) + _tn_dot(vb, k_d)
        return out

    sf_scr[...] = jnp.zeros_like(sf_scr)
    sb_scr[...] = jnp.zeros_like(sb_scr)

    def ctx_body(i, carry):
        lo = pl.multiple_of(i * c, c)
        chunk(caf_ref[pl.ds(lo, c), :], cv_ref[pl.ds(lo, c), :], None, sf_scr, lb_f, tri_fwd, lower, c - 1, False)
        hi = pl.multiple_of((n_ctx_chunks - 1 - i) * c, c)
        chunk(cab_ref[pl.ds(hi, c), :], cv_ref[pl.ds(hi, c), :], None, sb_scr, lb_b, tri_bwd, upper, 0, False)
        return carry

    lax.fori_loop(0, n_ctx_chunks, ctx_body, 0)

    def body(i, carry):
        lo = pl.multiple_of(i * c, c)
        of_scr[pl.ds(lo, c), :] = chunk(af_ref[pl.ds(lo, c), :], v_ref[pl.ds(lo, c), :], q_ref[pl.ds(lo, c), :],
                                        sf_scr, lb_f, tri_fwd, lower, c - 1, True)
        hi = pl.multiple_of((n_chunks - 1 - i) * c, c)
        ob_scr[pl.ds(hi, c), :] = chunk(ab_ref[pl.ds(hi, c), :], v_ref[pl.ds(hi, c), :], q_ref[pl.ds(hi, c), :],
                                        sb_scr, lb_b, tri_bwd, upper, 0, True)
        return carry

    lax.fori_loop(0, n_chunks, body, 0, unroll=4)

    gain = gain_ref[...]
    rows = 4 * c

    def readout(i, carry):
        lo = pl.multiple_of(i * rows, rows)
        o = of_scr[pl.ds(lo, rows), :] + ob_scr[pl.ds(lo, rows), :]
        o = o * lax.rsqrt(jnp.mean(o * o, axis=-1, keepdims=True) + EPS) * gain
        o_ref[pl.ds(lo, rows), :] = o * _silu(gate_ref[pl.ds(lo, rows), :])
        return carry

    lax.fori_loop(0, n_chunks // 4, readout, 0)


def _hgrn(px, pc, lb_logits, gain, *, width):
    bsz, n, _ = px.shape
    n_ctx = pc.shape[1]
    heads = width // HEAD_DIM
    hd = HEAD_DIM

    def seg(s):
        return pl.BlockSpec((None, n, hd), lambda b, h: (b, 0, s * heads + h))

    def cseg(s):
        return pl.BlockSpec((None, n_ctx, hd), lambda b, h: (b, 0, s * heads + h))

    depth1 = lb_logits.shape[1]
    return pl.pallas_call(
        functools.partial(_hgrn_kernel, n_chunks=n // SCAN_CHUNK, n_ctx_chunks=n_ctx // SCAN_CHUNK),
        out_shape=jax.ShapeDtypeStruct((bsz, n, width), F32),
        grid=(bsz, heads),
        in_specs=[seg(0), seg(1), seg(2), seg(3), seg(4), cseg(0), cseg(1), cseg(2),
                  pl.BlockSpec((2, depth1, hd), lambda b, h: (0, 0, h)),
                  pl.BlockSpec((1, hd), lambda b, h: (0, h))],
        out_specs=pl.BlockSpec((None, n, hd), lambda b, h: (b, 0, h)),
        scratch_shapes=[pltpu.VMEM((n, hd), F32), pltpu.VMEM((n, hd), F32),
                        pltpu.VMEM((hd, hd), F32), pltpu.VMEM((hd, hd), F32)],
        compiler_params=_params("parallel", "parallel"),
        name="hgrn_scan",
    )(px, px, px, px, px, pc, pc, pc, lb_logits, gain.reshape(1, width))


def _dft_constants(n):
    n2 = GRID_W
    n1h = n // n2
    n1 = 2 * n1h
    big = 2 * n
    k1 = np.arange(n1h)[:, None]
    m1 = np.arange(n1h)[None, :]
    ang = -2.0 * np.pi * m1 * (k1 + 0.5) / n1
    eye = np.eye(OUTER_ROWS)
    g_fwd = np.kron(np.concatenate([np.cos(ang), np.sin(ang)], axis=0), eye)
    g_inv = np.kron(np.concatenate([np.cos(ang).T, np.sin(ang).T], axis=1), eye)
    m2 = np.arange(n2)[None, :]
    tw_ang = -2.0 * np.pi * m2 * (k1 + 0.5) / big
    tw = np.stack([np.cos(tw_ang), np.sin(tw_ang)], axis=1)
    tw = np.broadcast_to(tw[..., None], (n1h, 2, n2, 128))
    k2 = np.arange(n2)[:, None]
    f_ang = -2.0 * np.pi * k2 * m2 / n2
    fr, fi = np.cos(f_ang), np.sin(f_ang)
    f_fwd = np.block([[fr, -fi], [fi, fr]])
    f_inv = np.block([[fr, fi], [-fi, fr]])
    return g_fwd, g_inv, np.ascontiguousarray(tw), f_fwd, f_inv


def _filt_mlp_kernel(z_ref, w1_ref, b1_ref, w2_ref, b2_ref, w3_ref, b3_ref, fr_ref, o_ref):
    fr = fr_ref[...]
    hp = lambda a, b: jnp.dot(a, b, precision=HIGHEST, preferred_element_type=F32)
    hid = jnp.sin(fr * (hp(z_ref[...], w1_ref[...]) + b1_ref[...]))
    hid = jnp.sin(fr * (hp(hid, w2_ref[...]) + b2_ref[...]))
    o_ref[...] = jnp.sin(fr * (hp(hid, w3_ref[...]) + b3_ref[...]))


def _filt_taps_kernel(hid_ref, wo_ref, dl_ref, h_ref, s_ref, *, n):
    h = jnp.dot(hid_ref[...], wo_ref[...], precision=HIGHEST, preferred_element_type=F32)
    t = lax.broadcasted_iota(jnp.int32, h.shape, 0).astype(F32) * (1.0 / max(n - 1, 1))
    hw = h * jnp.exp(-t * dl_ref[...])
    h_ref[...] = hw
    s_ref[0:1, :] = jnp.sum(jnp.abs(hw), axis=0, keepdims=True)
    s_ref[1:2, :] = jnp.abs(hw[0:1, :])


def _spectrum_kernel(af_ref, ab_ref, tw_ref, f_ref, s_ref, d_ref, o_ref, *, n):
    n2 = GRID_W
    lanes = af_ref.shape[-1]
    reps = lanes // 128
    twr = jnp.tile(tw_ref[0], (1, reps))
    twi = jnp.tile(tw_ref[1], (1, reps))
    fm = f_ref[...]

    def inner(a_ref):
        ar, ai = a_ref[0], a_ref[1]
        stacked = jnp.concatenate([ar * twr - ai * twi, ar * twi + ai * twr], axis=0)
        z = jnp.dot(fm, stacked, precision=HIGHEST, preferred_element_type=F32)
        return z[:n2], z[n2:]

    zfr, zfi = inner(af_ref)
    zbr, zbi = inner(ab_ref)
    l1 = s_ref[0:1, :] + s_ref[2:3, :] - s_ref[3:4, :]
    hb0 = s_ref[4:5, :]
    norm = 1.0 / l1
    scale = 2.0 / (2 * n)
    o_ref[0] = scale * ((zfr + zbr - hb0) * norm + d_ref[...])
    o_ref[1] = scale * ((zfi - zbi) * norm)


def _outer_dft_kernel(g_ref, x_ref, o_ref, *, high):
    n1h, rows, tl = x_ref.shape
    x = x_ref[...].reshape(n1h * rows, tl)
    if high:
        r = jnp.dot(g_ref[...], x, precision=HIGHEST, preferred_element_type=F32)
    else:
        r = _dot(g_ref[...], x.astype(BF16))
    o_ref[...] = r.reshape(2, n1h, rows, tl)


def _outer_dft(g, x4, seg, *, c, tl, high):
    s, n1h, n2, _ = x4.shape
    tl = min(tl, c)
    per = c // tl
    rows = OUTER_ROWS
    return pl.pallas_call(
        functools.partial(_outer_dft_kernel, high=high),
        out_shape=jax.ShapeDtypeStruct((s, 2, n1h, n2, c), F32),
        grid=(s, n2 // rows, per),
        in_specs=[pl.BlockSpec((2 * n1h * rows, n1h * rows), lambda b, j, l: (0, 0)),
                  pl.BlockSpec((None, n1h, rows, tl), lambda b, j, l: (b, 0, j, seg * per + l))],
        out_specs=pl.BlockSpec((None, 2, n1h, rows, tl), lambda b, j, l: (b, 0, 0, j, l)),
        compiler_params=_params("parallel", "parallel", "parallel"),
        name="outer_dft",
    )(g, x4)


def _hyena_spectra(n, width, filt, hy_d, consts):
    w1, b1, w2, b2, w3, b3, freq, w_out = filt
    g_fwd, _, tw, f_fwd, _ = consts
    n1h = n // GRID_W
    order = w1.shape[1]
    pos = np.arange(n, dtype=np.float64)
    t = pos / max(n - 1, 1)
    bands = (FILT_EMB - 1) // 2
    fb = np.linspace(1e-4, bands - 1, bands)
    ang = (2 * math.pi / n) * pos[:, None] * fb[None, :]
    z = np.concatenate([t[:, None], np.cos(ang), -np.sin(ang)], axis=-1)
    emb_pad = 128
    z = jnp.asarray(np.pad(z, ((0, 0), (0, emb_pad - FILT_EMB))), F32)
    w1p = jnp.pad(w1, ((0, emb_pad - FILT_EMB), (0, 0)))
    row = lambda a: a.reshape(1, -1)
    full = lambda shape: pl.BlockSpec(shape, lambda: tuple(0 for _ in shape))
    hid = pl.pallas_call(
        _filt_mlp_kernel,
        out_shape=jax.ShapeDtypeStruct((n, order), F32),
        in_specs=[full((n, emb_pad)), full((emb_pad, order)), full((1, order)), full((order, order)),
                  full((1, order)), full((order, order)), full((1, order)), full((1, order))],
        out_specs=full((n, order)),
        compiler_params=pltpu.CompilerParams(vmem_limit_bytes=VMEM_LIMIT_BYTES),
        name="filter_mlp",
    )(z, w1p, row(b1), w2, row(b2), w3, row(b3), row(freq))

    cols = 4 * width
    deltas = np.abs(np.linspace(math.log(DECAY_TARGET) / DECAY_SLOW_PCT,
                                math.log(DECAY_TARGET) / DECAY_FAST_PCT, width))
    deltas4 = jnp.asarray(np.tile(deltas, 4)[None, :], F32)
    tc = min(512, cols)
    taps, sums = pl.pallas_call(
        functools.partial(_filt_taps_kernel, n=n),
        out_shape=(jax.ShapeDtypeStruct((n, cols), F32), jax.ShapeDtypeStruct((2, cols), F32)),
        grid=(cols // tc,),
        in_specs=[pl.BlockSpec((n, order), lambda j: (0, 0)),
                  pl.BlockSpec((order, tc), lambda j: (0, j)),
                  pl.BlockSpec((1, tc), lambda j: (0, j))],
        out_specs=(pl.BlockSpec((n, tc), lambda j: (0, j)), pl.BlockSpec((2, tc), lambda j: (0, j))),
        compiler_params=_params("parallel"),
        name="filter_taps",
    )(hid, w_out, deltas4)

    a = _outer_dft(jnp.asarray(g_fwd, F32), taps.reshape(1, n1h, GRID_W, cols), 0, c=cols, tl=1024, high=True)[0]
    s4 = sums.reshape(2, 2, 2, width)
    bwd0 = taps[0].reshape(2, 2, width)[:, 1]
    stats = jnp.stack([s4[0, :, 0], s4[1, :, 0], s4[0, :, 1], s4[1, :, 1], bwd0], axis=1)

    def tap_spec(side):
        return pl.BlockSpec((2, None, GRID_W, width), lambda k1, f: (0, k1, 0, 2 * f + side))

    return pl.pallas_call(
        functools.partial(_spectrum_kernel, n=n),
        out_shape=jax.ShapeDtypeStruct((2, 2, n1h, GRID_W, width), F32),
        grid=(n1h, 2),
        in_specs=[tap_spec(0), tap_spec(1),
                  pl.BlockSpec((None, 2, GRID_W, 128), lambda k1, f: (k1, 0, 0, 0)),
                  pl.BlockSpec((2 * GRID_W, 2 * GRID_W), lambda k1, f: (0, 0)),
                  pl.BlockSpec((None, 5, width), lambda k1, f: (f, 0, 0)),
                  pl.BlockSpec((None, 1, width), lambda k1, f: (f, 0, 0))],
        out_specs=pl.BlockSpec((None, 2, None, GRID_W, width), lambda k1, f: (f, 0, k1, 0, 0)),
        compiler_params=_params("parallel", "parallel"),
        name="filter_spectrum",
    )(a, a, jnp.asarray(tw, F32), jnp.asarray(f_fwd, F32), stats, hy_d.reshape(2, 1, width))


def _inner_conv_kernel(a_ref, tw_ref, f_ref, fi_ref, h_ref, o_ref):
    n2 = GRID_W
    reps = a_ref.shape[-1] // 128
    for i in range(a_ref.shape[1]):
        twr = jnp.tile(tw_ref[i, 0], (1, reps))
        twi = jnp.tile(tw_ref[i, 1], (1, reps))
        ar, ai = a_ref[0, i], a_ref[1, i]
        stacked = jnp.concatenate([ar * twr - ai * twi, ar * twi + ai * twr], axis=0).astype(BF16)
        z = _dot(f_ref[...], stacked)
        zr, zi = z[:n2], z[n2:]
        hr, hi = h_ref[0, i], h_ref[1, i]
        stacked = jnp.concatenate([zr * hr - zi * hi, zr * hi + zi * hr], axis=0).astype(BF16)
        w = _dot(fi_ref[...], stacked)
        wr, wi = w[:n2], w[n2:]
        o_ref[0, i] = wr * twr + wi * twi
        o_ref[1, i] = wi * twr - wr * twi


def _inner_conv(a, tw, f_fwd, f_inv, spec):
    bsz, _, n1h, n2, c = a.shape
    kb = min(INNER_BINS, n1h)
    return pl.pallas_call(
        _inner_conv_kernel,
        out_shape=jax.ShapeDtypeStruct(a.shape, F32),
        grid=(n1h // kb, bsz),
        in_specs=[pl.BlockSpec((None, 2, kb, n2, c), lambda k1, b: (b, 0, k1, 0, 0)),
                  pl.BlockSpec((kb, 2, n2, 128), lambda k1, b: (k1, 0, 0, 0)),
                  pl.BlockSpec((2 * n2, 2 * n2), lambda k1, b: (0, 0)),
                  pl.BlockSpec((2 * n2, 2 * n2), lambda k1, b: (0, 0)),
                  pl.BlockSpec((2, kb, n2, c), lambda k1, b: (0, k1, 0, 0))],
        out_specs=pl.BlockSpec((None, 2, kb, n2, c), lambda k1, b: (b, 0, k1, 0, 0)),
        compiler_params=_params("parallel", "arbitrary"),
        name="inner_conv",
    )(a, tw, f_fwd, f_inv, spec)


def _outer_idft_gate_kernel(gi_ref, g_ref, b_ref, x_ref, o_ref, *, again):
    n1h, rows, tl = x_ref.shape
    stacked = b_ref[...].reshape(2 * n1h * rows, tl).astype(BF16)
    y = _dot(gi_ref[...], stacked) * x_ref[...].reshape(n1h * rows, tl)
    if again:
        o_ref[...] = _dot(g_ref[...], y.astype(BF16)).reshape(2, n1h, rows, tl)
    else:
        o_ref[...] = y.reshape(n1h, rows, tl)


def _outer_idft_gate(g_inv, g_fwd, b, px4, seg, *, again, tl):
    bsz, _, n1h, n2, c = b.shape
    tl = min(tl, c)
    per = c // tl
    rows = OUTER_ROWS
    if again:
        out_shape = jax.ShapeDtypeStruct((bsz, 2, n1h, n2, c), F32)
        out_spec = pl.BlockSpec((None, 2, n1h, rows, tl), lambda bb, j, l: (bb, 0, 0, j, l))
    else:
        out_shape = jax.ShapeDtypeStruct((bsz, n1h, n2, c), F32)
        out_spec = pl.BlockSpec((None, n1h, rows, tl), lambda bb, j, l: (bb, 0, j, l))
    return pl.pallas_call(
        functools.partial(_outer_idft_gate_kernel, again=again),
        out_shape=out_shape,
        grid=(bsz, n2 // rows, per),
        in_specs=[pl.BlockSpec((n1h * rows, 2 * n1h * rows), lambda bb, j, l: (0, 0)),
                  pl.BlockSpec((2 * n1h * rows, n1h * rows), lambda bb, j, l: (0, 0)),
                  pl.BlockSpec((None, 2, n1h, rows, tl), lambda bb, j, l: (bb, 0, 0, j, l)),
                  pl.BlockSpec((None, n1h, rows, tl), lambda bb, j, l: (bb, 0, j, seg * per + l))],
        out_specs=out_spec,
        compiler_params=_params("parallel", "parallel", "parallel"),
        name="outer_idft_gate",
    )(g_inv, g_fwd, b, px4)


def _hyena_mix(px, spec, consts, *, bsz, n, width):
    g_fwd, g_inv, tw, f_fwd, f_inv = consts
    g_fwd_b, g_inv_b = jnp.asarray(g_fwd, BF16), jnp.asarray(g_inv, BF16)
    f_fwd_b, f_inv_b = jnp.asarray(f_fwd, BF16), jnp.asarray(f_inv, BF16)
    tw_f = jnp.asarray(tw, F32)
    n1h = n // GRID_W
    px4 = px.reshape(bsz, n1h, GRID_W, -1)
    tl = width
    a = _outer_dft(g_fwd_b, px4, 5, c=width, tl=tl, high=False)
    a = _inner_conv(a, tw_f, f_fwd_b, f_inv_b, spec[0])
    a = _outer_idft_gate(g_inv_b, g_fwd_b, a, px4, 6, again=True, tl=tl)
    a = _inner_conv(a, tw_f, f_fwd_b, f_inv_b, spec[1])
    hy = _outer_idft_gate(g_inv_b, g_fwd_b, a, px4, 7, again=False, tl=tl)
    return hy.reshape(bsz * n, width)


def _out_proj_kernel(x_ref, hg_ref, hy_ref, wa_ref, wb_ref, gt_ref, o_ref):
    mix = _dot(hg_ref[...].astype(BF16), wa_ref[...]) + _dot(hy_ref[...].astype(BF16), wb_ref[...])
    o_ref[...] = x_ref[...] + gt_ref[0] * mix


def _out_proj(x, hg, hy, w, gate, *, tokens_per_mod, tm):
    t, d = x.shape
    wdt = hg.shape[1]
    tm = min(tm, t)
    return pl.pallas_call(
        _out_proj_kernel,
        out_shape=jax.ShapeDtypeStruct((t, d), F32),
        grid=(t // tm,),
        in_specs=[pl.BlockSpec((tm, d), lambda i: (i, 0)),
                  pl.BlockSpec((tm, wdt), lambda i: (i, 0)),
                  pl.BlockSpec((tm, wdt), lambda i: (i, 0)),
                  pl.BlockSpec((wdt, d), lambda i: (0, 0)),
                  pl.BlockSpec((wdt, d), lambda i: (1, 0)),
                  pl.BlockSpec((1, 1, d), lambda i: ((i * tm) // tokens_per_mod, 0, 0))],
        out_specs=pl.BlockSpec((tm, d), lambda i: (i, 0)),
        compiler_params=_params("parallel"),
        name="out_proj",
    )(x, hg, hy, w, w, gate)


def kernel(x, c, ctx, c_ctx, ada_w, ada_b, norm_ffn1, ffn1_w1, ffn1_w3, ffn1_w2, norm_mix, w_in, hg_lb_logits, hg_norm, hy_conv_w, hy_conv_b, filt_w1, filt_b1, filt_w2, filt_b2, filt_w3, filt_b3, filt_freq, filt_w_out, hy_d, w_out, norm_ffn2, ffn2_w1, ffn2_w3, ffn2_w2, final_norm):
    bsz, n, d = x.shape
    n_ctx = ctx.shape[1]
    depth = ada_w.shape[0]
    assert depth == 1, "single-layer block"
    width = d // 2
    n_seg = w_in.shape[2] // width
    assert n_seg == 8 and n % GRID_W == 0 and n_ctx % SCAN_CHUNK == 0 and width % HEAD_DIM == 0
    d_ff = ffn1_w1.shape[2]
    tf = 512 if d_ff % 512 == 0 else d_ff
    tm = 512
    l = 0

    rows = -(-(bsz + 1) // 8) * 8
    cs = jnp.concatenate([c, c_ctx[None, :], jnp.zeros((rows - bsz - 1, d), F32)], axis=0)
    mods = _ada(cs, ada_w[l], ada_b[l]).reshape(rows, N_MOD, d)
    mx = [mods[:bsz, i][:, None, :] for i in range(N_MOD)]
    mc = [mods[bsz:bsz + 1, i][:, None, :] for i in range(N_MOD)]

    bf = lambda a: a.astype(BF16)
    w1a, w3a, w2a = bf(ffn1_w1[l]), bf(ffn1_w3[l]), bf(ffn1_w2[l])
    xt = x.reshape(bsz * n, d)
    yt = ctx.reshape(bsz * n_ctx, d)

    xt = _ffn(xt, mx[0], mx[1], mx[2], norm_ffn1[l], w1a, w3a, w2a, final_norm,
              tokens_per_mod=n, final_norm=False, tm=tm, tf=tf)
    yt = _ffn(yt, mc[0], mc[1], mc[2], norm_ffn1[l], w1a, w3a, w2a, final_norm,
              tokens_per_mod=bsz * n_ctx, final_norm=False, tm=tm, tf=tf)

    w_in_b = bf(w_in[l])
    pc = _proj(yt, mc[3], mc[4], norm_mix[l], w_in_b, hy_conv_w[l], hy_conv_b[l], n_seg=3, seg_w=width,
               conv_from=n_seg, tokens_per_mod=bsz * n_ctx, tm=tm)
    px = _proj(xt, mx[3], mx[4], norm_mix[l], w_in_b, hy_conv_w[l], hy_conv_b[l], n_seg=n_seg, seg_w=width,
               conv_from=5, tokens_per_mod=n, tm=tm)
    cols = n_seg * width

    hg = _hgrn(px.reshape(bsz, n, cols), pc.reshape(bsz, n_ctx, 3 * width), hg_lb_logits, hg_norm[l], width=width)

    consts = _dft_constants(n)
    filt = (filt_w1[l], filt_b1[l], filt_w2[l], filt_b2[l], filt_w3[l], filt_b3[l], filt_freq[l], filt_w_out[l])
    spec = _hyena_spectra(n, width, filt, hy_d[l], consts)
    hy = _hyena_mix(px, spec, consts, bsz=bsz, n=n, width=width)

    xt = _out_proj(xt, hg.reshape(bsz * n, width), hy, bf(w_out[l]), mx[5],
                   tokens_per_mod=n, tm=tm)

    out = _ffn(xt, mx[6], mx[7], mx[8], norm_ffn2[l], bf(ffn2_w1[l]), bf(ffn2_w3[l]), bf(ffn2_w2[l]), final_norm,
               tokens_per_mod=n, final_norm=True, tm=tm, tf=tf)
    return out.reshape(bsz, n, d)
```

```python
import functools
import math

import numpy as np
import jax
import jax.numpy as jnp
from jax import lax
from jax.experimental import pallas as pl
from jax.experimental.pallas import tpu as pltpu

F32 = jnp.float32
BF16 = jnp.bfloat16
EPS = 1e-6
N_MOD = 9
HEAD_DIM = 128
GRID_W = 64
SCAN_CHUNK = 64
PREP_CHUNKS = 4
OUTER_ROWS = 8
INNER_BINS = 4
FILT_EMB = 33
DECAY_TARGET = 1e-2
DECAY_FAST_PCT = 0.3
DECAY_SLOW_PCT = 1.5
VMEM_LIMIT_BYTES = 56 * 1024 * 1024
HIGHEST = lax.Precision.HIGHEST


def _params(*semantics):
    return pltpu.CompilerParams(dimension_semantics=semantics, vmem_limit_bytes=VMEM_LIMIT_BYTES)


def _dot(a, b):
    return jnp.dot(a, b, preferred_element_type=F32)


def _silu(a):
    return a * jax.nn.sigmoid(a)


def _ada_kernel(c_ref, w_ref, b_ref, o_ref):
    h = _silu(c_ref[...]).astype(BF16)
    o_ref[...] = _dot(h, w_ref[...].astype(BF16)) + b_ref[...]


def _ada(cs, w, b):
    rows, d = cs.shape
    n = w.shape[1]
    tn = d // 2
    return pl.pallas_call(
        _ada_kernel,
        out_shape=jax.ShapeDtypeStruct((rows, n), F32),
        grid=(n // tn,),
        in_specs=[pl.BlockSpec((rows, d), lambda j: (0, 0)),
                  pl.BlockSpec((d, tn), lambda j: (0, j)),
                  pl.BlockSpec((1, tn), lambda j: (0, j))],
        out_specs=pl.BlockSpec((rows, tn), lambda j: (0, j)),
        compiler_params=_params("arbitrary"),
        name="ada_mod",
    )(cs, w, b.reshape(1, n))


def _norm_mod(x, gain, shift, scale):
    y = x * lax.rsqrt(jnp.mean(x * x, axis=-1, keepdims=True) + EPS) * gain
    return (y * (1.0 + scale) + shift).astype(BF16)


def _ffn_kernel(x_ref, sh_ref, sc_ref, gt_ref, g_ref, w1_ref, w3_ref, w2_ref, fin_ref, o_ref,
                h_scr, acc_scr, *, final_norm):
    j = pl.program_id(1)

    @pl.when(j == 0)
    def _():
        h_scr[...] = _norm_mod(x_ref[...], g_ref[...], sh_ref[0], sc_ref[0])
        acc_scr[...] = jnp.zeros_like(acc_scr)

    h = h_scr[...]
    a = _dot(h, w1_ref[...])
    b = _dot(h, w3_ref[...])
    act = (_silu(a) * b).astype(BF16)
    acc_scr[...] += _dot(act, w2_ref[...])

    @pl.when(j == pl.num_programs(1) - 1)
    def _():
        out = x_ref[...] + 0.5 * gt_ref[0] * acc_scr[...]
        if final_norm:
            out = out * lax.rsqrt(jnp.mean(out * out, axis=-1, keepdims=True) + EPS) * fin_ref[...]
        o_ref[...] = out


def _ffn(x, shift, scale, gate, gain, w1, w3, w2, fin, *, tokens_per_mod, final_norm, tm, tf):
    t, d = x.shape
    f = w1.shape[1]
    tm = min(tm, tokens_per_mod)
    mod_spec = pl.BlockSpec((1, 1, d), lambda i, j: ((i * tm) // tokens_per_mod, 0, 0))
    vec_spec = pl.BlockSpec((1, d), lambda i, j: (0, 0))
    return pl.pallas_call(
        functools.partial(_ffn_kernel, final_norm=final_norm),
        out_shape=jax.ShapeDtypeStruct((t, d), F32),
        grid=(t // tm, f // tf),
        in_specs=[pl.BlockSpec((tm, d), lambda i, j: (i, 0)),
                  mod_spec, mod_spec, mod_spec, vec_spec,
                  pl.BlockSpec((d, tf), lambda i, j: (0, j)),
                  pl.BlockSpec((d, tf), lambda i, j: (0, j)),
                  pl.BlockSpec((tf, d), lambda i, j: (j, 0)),
                  vec_spec],
        out_specs=pl.BlockSpec((tm, d), lambda i, j: (i, 0)),
        scratch_shapes=[pltpu.VMEM((tm, d), BF16), pltpu.VMEM((tm, d), F32)],
        compiler_params=_params("parallel", "arbitrary"),
        name="swiglu_ffn",
    )(x, shift, scale, gate, gain.reshape(1, d), w1, w3, w2, fin.reshape(1, d))


def _proj_kernel(x_ref, sh_ref, sc_ref, g_ref, w_ref, cw_ref, cb_ref, o_ref, h_scr, *, conv_from):
    j = pl.program_id(1)

    @pl.when(j == 0)
    def _():
        h_scr[...] = _norm_mod(x_ref[...], g_ref[...], sh_ref[0], sc_ref[0])

    p = _dot(h_scr[...], w_ref[...])

    @pl.when(j < conv_from)
    def _():
        o_ref[...] = p

    @pl.when(j >= conv_from)
    def _():
        tm, w = p.shape
        rows3 = lambda t: t.reshape(tm // GRID_W, GRID_W, w)
        y = (rows3(pltpu.roll(p, 1, axis=0)) * cw_ref[0] + rows3(p) * cw_ref[1]
             + rows3(pltpu.roll(p, tm - 1, axis=0)) * cw_ref[2] + cb_ref[...])
        o_ref[...] = y.reshape(tm, w)


def _proj(x, shift, scale, gain, w, conv_w, conv_b, *, n_seg, seg_w, conv_from, tokens_per_mod, tm):
    t, d = x.shape
    tm = min(tm, tokens_per_mod)
    mod_spec = pl.BlockSpec((1, 1, d), lambda i, j: ((i * tm) // tokens_per_mod, 0, 0))
    conv_idx = lambda i, j: (0, jnp.maximum(j - conv_from, 0))
    pos = jnp.arange(GRID_W)[None, :, None]
    edge = jnp.stack([pos > 0, pos >= 0, pos < GRID_W - 1], axis=0)[:, 0]
    conv_w = jnp.where(edge, conv_w[:, None, :], 0.0)
    return pl.pallas_call(
        functools.partial(_proj_kernel, conv_from=conv_from),
        out_shape=jax.ShapeDtypeStruct((t, n_seg * seg_w), F32),
        grid=(t // tm, n_seg),
        in_specs=[pl.BlockSpec((tm, d), lambda i, j: (i, 0)),
                  mod_spec, mod_spec,
                  pl.BlockSpec((1, d), lambda i, j: (0, 0)),
                  pl.BlockSpec((d, seg_w), lambda i, j: (0, j)),
                  pl.BlockSpec((3, GRID_W, seg_w), lambda i, j: (0, 0, jnp.maximum(j - conv_from, 0))),
                  pl.BlockSpec((1, seg_w), conv_idx)],
        out_specs=pl.BlockSpec((tm, seg_w), lambda i, j: (i, j)),
        scratch_shapes=[pltpu.VMEM((tm, d), BF16)],
        compiler_params=_params("parallel", "arbitrary"),
        name="in_proj",
    )(x, shift, scale, gain.reshape(1, d), w, conv_w, conv_b.reshape(1, -1))


def _split_dot(tri, g):
    g1 = g.astype(BF16)
    r1 = g - g1.astype(F32)
    g2 = r1.astype(BF16)
    g3 = (r1 - g2.astype(F32)).astype(BF16)
    return _dot(tri, g1) + _dot(tri, g2) + _dot(tri, g3)


def _gates(a, lb):
    f = lb + (1.0 - lb) * jax.nn.sigmoid(a)
    return 1.0 - f, jnp.log(f)


def _tn_dot(a, b):
    return lax.dot_general(a, b, (((0,), (0,)), ((), ())), preferred_element_type=F32)


def _nt_dot(a, b):
    return lax.dot_general(a, b, (((1,), (1,)), ((), ())), preferred_element_type=F32)


def _hgrn_kernel(af_ref, ab_ref, v_ref, q_ref, gate_ref, caf_ref, cab_ref, cv_ref, lbl_ref, gain_ref,
                 o_ref, of_scr, ob_scr, sf_scr, sb_scr, qin_scr, qt_scr, kt_scr, kd_scr, dec_scr,
                 *, n_chunks, n_ctx_chunks):
    c = SCAN_CHUNK
    mid = c // 2
    row = lax.broadcasted_iota(jnp.int32, (c, c), 0)
    col = lax.broadcasted_iota(jnp.int32, (c, c), 1)
    lower = row >= col
    tri_fwd = jnp.where(lower, 1.0, 0.0).astype(BF16)
    tri_bwd = jnp.where(col >= row, 1.0, 0.0).astype(BF16)
    upper = col >= row
    rows_p = PREP_CHUNKS * c
    prow = lax.broadcasted_iota(jnp.int32, (rows_p, rows_p), 0)
    pcol = lax.broadcasted_iota(jnp.int32, (rows_p, rows_p), 1)
    same = (prow // c) == (pcol // c)
    tri_fwd_p = jnp.where(same & (prow >= pcol), 1.0, 0.0).astype(BF16)
    tri_bwd_p = jnp.where(same & (pcol >= prow), 1.0, 0.0).astype(BF16)

    def lower_bound(direction):
        lg = lbl_ref[direction]
        ex = jnp.exp(lg - jnp.max(lg, axis=0, keepdims=True))
        return ex[0:1, :] / jnp.sum(ex, axis=0, keepdims=True)

    lb_f = lower_bound(0)
    lb_b = lower_bound(1)
    q_scale = HEAD_DIM ** -0.5

    def ctx_chunk(a, v, st_ref, lb, tri, end_row):
        k, g = _gates(a, lb)
        b = _split_dot(tri, g)
        b_end = b[end_row:end_row + 1, :]
        k_d = (k * jnp.exp(b_end - b)).astype(BF16)
        st_ref[...] = st_ref[...] * jnp.exp(b_end) + _tn_dot(v.astype(BF16), k_d)

    sf_scr[...] = jnp.zeros_like(sf_scr)
    sb_scr[...] = jnp.zeros_like(sb_scr)

    def ctx_body(i, carry):
        lo = pl.multiple_of(i * c, c)
        ctx_chunk(caf_ref[pl.ds(lo, c), :], cv_ref[pl.ds(lo, c), :], sf_scr, lb_f, tri_fwd, c - 1)
        hi = pl.multiple_of((n_ctx_chunks - 1 - i) * c, c)
        ctx_chunk(cab_ref[pl.ds(hi, c), :], cv_ref[pl.ds(hi, c), :], sb_scr, lb_b, tri_bwd, 0)
        return carry

    lax.fori_loop(0, n_ctx_chunks, ctx_body, 0)

    def prep(a_ref, lb, tri_p, end_row, d, lo):
        k, g = _gates(a_ref[pl.ds(lo, rows_p), :], lb)
        b = _split_dot(tri_p, g).reshape(PREP_CHUNKS, c, HEAD_DIM)
        b_mid = b[:, mid:mid + 1, :]
        b_end = b[:, end_row:end_row + 1, :]
        q_t = (q_ref[pl.ds(lo, rows_p), :] * q_scale).reshape(PREP_CHUNKS, c, HEAD_DIM) * jnp.exp(b - b_mid)
        k_t = k.reshape(PREP_CHUNKS, c, HEAD_DIM) * jnp.exp(b_mid - b)
        flat = lambda t: t.reshape(rows_p, HEAD_DIM).astype(BF16)
        qt_scr[d, pl.ds(lo, rows_p), :] = flat(q_t)
        kt_scr[d, pl.ds(lo, rows_p), :] = flat(k_t)
        qin_scr[d, pl.ds(lo, rows_p), :] = flat(q_t * jnp.exp(b_mid))
        kd_scr[d, pl.ds(lo, rows_p), :] = flat(k_t * jnp.exp(b_end - b_mid))
        return jnp.broadcast_to(jnp.exp(b_end), (PREP_CHUNKS, 8, HEAD_DIM))

    def prep_body(i, carry):
        lo = pl.multiple_of(i * rows_p, rows_p)
        dec_scr[0, pl.ds(i * PREP_CHUNKS, PREP_CHUNKS)] = prep(af_ref, lb_f, tri_fwd_p, c - 1, 0, lo)
        dec_scr[1, pl.ds(i * PREP_CHUNKS, PREP_CHUNKS)] = prep(ab_ref, lb_b, tri_bwd_p, 0, 1, lo)
        return carry

    lax.fori_loop(0, n_chunks // PREP_CHUNKS, prep_body, 0)

    zeros_blk = jnp.zeros((c, HEAD_DIM), BF16)

    def block_diag(t):
        return jnp.concatenate(
            [jnp.concatenate([t[j * c:(j + 1) * c] if m == j else zeros_blk for m in range(PREP_CHUNKS)], axis=1)
             for j in range(PREP_CHUNKS)], axis=0)

    def scan_block(d, bi, st_ref, mask_p, out_ref, order):
        lo = pl.multiple_of(bi * rows_p, rows_p)
        vb = v_ref[pl.ds(lo, rows_p), :].astype(BF16)
        scores = _nt_dot(qt_scr[d, pl.ds(lo, rows_p), :], kt_scr[d, pl.ds(lo, rows_p), :])
        out = _dot(jnp.where(mask_p, scores, 0.0).astype(BF16), vb)
        upd = _tn_dot(vb, block_diag(kd_scr[d, pl.ds(lo, rows_p), :]))
        st = st_ref[...]
        seen = [None] * PREP_CHUNKS
        for j in order:
            seen[j] = st.astype(BF16)
            st = st * dec_scr[d, bi * PREP_CHUNKS + j][0:1, :] + upd[:, j * HEAD_DIM:(j + 1) * HEAD_DIM]
        st_ref[...] = st
        out_ref[pl.ds(lo, rows_p), :] = out + _nt_dot(block_diag(qin_scr[d, pl.ds(lo, rows_p), :]),
                                                      jnp.concatenate(seen, axis=1))

    n_blocks = n_chunks // PREP_CHUNKS
    mask_fwd_p = same & (prow >= pcol)
    mask_bwd_p = same & (pcol >= prow)

    def body(i, carry):
        scan_block(0, i, sf_scr, mask_fwd_p, of_scr, range(PREP_CHUNKS))
        scan_block(1, n_blocks - 1 - i, sb_scr, mask_bwd_p, ob_scr, range(PREP_CHUNKS - 1, -1, -1))
        return carry

    lax.fori_loop(0, n_blocks, body, 0, unroll=2)

    gain = gain_ref[...]
    rows = 4 * c

    def readout(i, carry):
        lo = pl.multiple_of(i * rows, rows)
        o = of_scr[pl.ds(lo, rows), :] + ob_scr[pl.ds(lo, rows), :]
        o = o * lax.rsqrt(jnp.mean(o * o, axis=-1, keepdims=True) + EPS) * gain
        o_ref[pl.ds(lo, rows), :] = o * _silu(gate_ref[pl.ds(lo, rows), :])
        return carry

    lax.fori_loop(0, n_chunks // 4, readout, 0)


def _hgrn(px, pc, lb_logits, gain, *, width):
    bsz, n, _ = px.shape
    n_ctx = pc.shape[1]
    heads = width // HEAD_DIM
    hd = HEAD_DIM

    def seg(s):
        return pl.BlockSpec((None, n, hd), lambda b, h: (b, 0, s * heads + h))

    def cseg(s):
        return pl.BlockSpec((None, n_ctx, hd), lambda b, h: (b, 0, s * heads + h))

    depth1 = lb_logits.shape[1]
    return pl.pallas_call(
        functools.partial(_hgrn_kernel, n_chunks=n // SCAN_CHUNK, n_ctx_chunks=n_ctx // SCAN_CHUNK),
        out_shape=jax.ShapeDtypeStruct((bsz, n, width), F32),
        grid=(bsz, heads),
        in_specs=[seg(0), seg(1), seg(2), seg(3), seg(4), cseg(0), cseg(1), cseg(2),
                  pl.BlockSpec((2, depth1, hd), lambda b, h: (0, 0, h)),
                  pl.BlockSpec((1, hd), lambda b, h: (0, h))],
        out_specs=pl.BlockSpec((None, n, hd), lambda b, h: (b, 0, h)),
        scratch_shapes=[pltpu.VMEM((n, hd), F32), pltpu.VMEM((n, hd), F32),
                        pltpu.VMEM((hd, hd), F32), pltpu.VMEM((hd, hd), F32),
                        pltpu.VMEM((2, n, hd), BF16), pltpu.VMEM((2, n, hd), BF16),
                        pltpu.VMEM((2, n, hd), BF16), pltpu.VMEM((2, n, hd), BF16),
                        pltpu.VMEM((2, n // SCAN_CHUNK, 8, hd), F32)],
        compiler_params=_params("parallel", "parallel"),
        name="hgrn_scan",
    )(px, px, px, px, px, pc, pc, pc, lb_logits, gain.reshape(1, width))


def _dft_constants(n):
    n2 = GRID_W
    n1h = n // n2
    n1 = 2 * n1h
    big = 2 * n
    k1 = np.arange(n1h)[:, None]
    m1 = np.arange(n1h)[None, :]
    ang = -2.0 * np.pi * m1 * (k1 + 0.5) / n1
    eye = np.eye(OUTER_ROWS)
    g_fwd = np.kron(np.concatenate([np.cos(ang), np.sin(ang)], axis=0), eye)
    g_inv = np.kron(np.concatenate([np.cos(ang).T, np.sin(ang).T], axis=1), eye)
    m2 = np.arange(n2)[None, :]
    tw_ang = -2.0 * np.pi * m2 * (k1 + 0.5) / big
    tw = np.stack([np.cos(tw_ang), np.sin(tw_ang)], axis=1)
    tw = np.broadcast_to(tw[..., None], (n1h, 2, n2, 128))
    k2 = np.arange(n2)[:, None]
    f_ang = -2.0 * np.pi * k2 * m2 / n2
    fr, fi = np.cos(f_ang), np.sin(f_ang)
    f_fwd = np.block([[fr, -fi], [fi, fr]])
    f_inv = np.block([[fr, fi], [-fi, fr]])
    return g_fwd, g_inv, np.ascontiguousarray(tw), f_fwd, f_inv


def _filt_mlp_kernel(z_ref, w1_ref, b1_ref, w2_ref, b2_ref, w3_ref, b3_ref, fr_ref, o_ref):
    fr = fr_ref[...]
    hp = lambda a, b: jnp.dot(a, b, precision=HIGHEST, preferred_element_type=F32)
    hid = jnp.sin(fr * (hp(z_ref[...], w1_ref[...]) + b1_ref[...]))
    hid = jnp.sin(fr * (hp(hid, w2_ref[...]) + b2_ref[...]))
    o_ref[...] = jnp.sin(fr * (hp(hid, w3_ref[...]) + b3_ref[...]))


def _filt_taps_kernel(hid_ref, wo_ref, dl_ref, h_ref, s_ref, *, n):
    h = jnp.dot(hid_ref[...], wo_ref[...], precision=HIGHEST, preferred_element_type=F32)
    t = lax.broadcasted_iota(jnp.int32, h.shape, 0).astype(F32) * (1.0 / max(n - 1, 1))
    hw = h * jnp.exp(-t * dl_ref[...])
    h_ref[...] = hw
    s_ref[0:1, :] = jnp.sum(jnp.abs(hw), axis=0, keepdims=True)
    s_ref[1:2, :] = jnp.abs(hw[0:1, :])


def _spectrum_kernel(af_ref, ab_ref, tw_ref, f_ref, s_ref, d_ref, o_ref, *, n):
    n2 = GRID_W
    lanes = af_ref.shape[-1]
    reps = lanes // 128
    twr = jnp.tile(tw_ref[0], (1, reps))
    twi = jnp.tile(tw_ref[1], (1, reps))
    fm = f_ref[...]

    def inner(a_ref):
        ar, ai = a_ref[0], a_ref[1]
        stacked = jnp.concatenate([ar * twr - ai * twi, ar * twi + ai * twr], axis=0)
        z = jnp.dot(fm, stacked, precision=HIGHEST, preferred_element_type=F32)
        return z[:n2], z[n2:]

    zfr, zfi = inner(af_ref)
    zbr, zbi = inner(ab_ref)
    l1 = s_ref[0:1, :] + s_ref[2:3, :] - s_ref[3:4, :]
    hb0 = s_ref[4:5, :]
    norm = 1.0 / l1
    scale = 2.0 / (2 * n)
    o_ref[0] = scale * ((zfr + zbr - hb0) * norm + d_ref[...])
    o_ref[1] = scale * ((zfi - zbi) * norm)


def _outer_dft_kernel(g_ref, x_ref, o_ref, *, high):
    n1h, rows, tl = x_ref.shape
    x = x_ref[...].reshape(n1h * rows, tl)
    if high:
        r = jnp.dot(g_ref[...], x, precision=HIGHEST, preferred_element_type=F32)
    else:
        r = _dot(g_ref[...], x.astype(BF16))
    o_ref[...] = r.reshape(2, n1h, rows, tl)


def _outer_dft(g, x4, seg, *, c, tl, high):
    s, n1h, n2, _ = x4.shape
    tl = min(tl, c)
    per = c // tl
    rows = OUTER_ROWS
    return pl.pallas_call(
        functools.partial(_outer_dft_kernel, high=high),
        out_shape=jax.ShapeDtypeStruct((s, 2, n1h, n2, c), F32),
        grid=(s, n2 // rows, per),
        in_specs=[pl.BlockSpec((2 * n1h * rows, n1h * rows), lambda b, j, l: (0, 0)),
                  pl.BlockSpec((None, n1h, rows, tl), lambda b, j, l: (b, 0, j, seg * per + l))],
        out_specs=pl.BlockSpec((None, 2, n1h, rows, tl), lambda b, j, l: (b, 0, 0, j, l)),
        compiler_params=_params("parallel", "parallel", "parallel"),
        name="outer_dft",
    )(g, x4)


def _hyena_spectra(n, width, filt, hy_d, consts):
    w1, b1, w2, b2, w3, b3, freq, w_out = filt
    g_fwd, _, tw, f_fwd, _ = consts
    n1h = n // GRID_W
    order = w1.shape[1]
    pos = np.arange(n, dtype=np.float64)
    t = pos / max(n - 1, 1)
    bands = (FILT_EMB - 1) // 2
    fb = np.linspace(1e-4, bands - 1, bands)
    ang = (2 * math.pi / n) * pos[:, None] * fb[None, :]
    z = np.concatenate([t[:, None], np.cos(ang), -np.sin(ang)], axis=-1)
    emb_pad = 128
    z = jnp.asarray(np.pad(z, ((0, 0), (0, emb_pad - FILT_EMB))), F32)
    w1p = jnp.pad(w1, ((0, emb_pad - FILT_EMB), (0, 0)))
    row = lambda a: a.reshape(1, -1)
    full = lambda shape: pl.BlockSpec(shape, lambda: tuple(0 for _ in shape))
    hid = pl.pallas_call(
        _filt_mlp_kernel,
        out_shape=jax.ShapeDtypeStruct((n, order), F32),
        in_specs=[full((n, emb_pad)), full((emb_pad, order)), full((1, order)), full((order, order)),
                  full((1, order)), full((order, order)), full((1, order)), full((1, order))],
        out_specs=full((n, order)),
        compiler_params=pltpu.CompilerParams(vmem_limit_bytes=VMEM_LIMIT_BYTES),
        name="filter_mlp",
    )(z, w1p, row(b1), w2, row(b2), w3, row(b3), row(freq))

    cols = 4 * width
    deltas = np.abs(np.linspace(math.log(DECAY_TARGET) / DECAY_SLOW_PCT,
                                math.log(DECAY_TARGET) / DECAY_FAST_PCT, width))
    deltas4 = jnp.asarray(np.tile(deltas, 4)[None, :], F32)
    tc = min(512, cols)
    taps, sums = pl.pallas_call(
        functools.partial(_filt_taps_kernel, n=n),
        out_shape=(jax.ShapeDtypeStruct((n, cols), F32), jax.ShapeDtypeStruct((2, cols), F32)),
        grid=(cols // tc,),
        in_specs=[pl.BlockSpec((n, order), lambda j: (0, 0)),
                  pl.BlockSpec((order, tc), lambda j: (0, j)),
                  pl.BlockSpec((1, tc), lambda j: (0, j))],
        out_specs=(pl.BlockSpec((n, tc), lambda j: (0, j)), pl.BlockSpec((2, tc), lambda j: (0, j))),
        compiler_params=_params("parallel"),
        name="filter_taps",
    )(hid, w_out, deltas4)

    a = _outer_dft(jnp.asarray(g_fwd, F32), taps.reshape(1, n1h, GRID_W, cols), 0, c=cols, tl=1024, high=True)[0]
    s4 = sums.reshape(2, 2, 2, width)
    bwd0 = taps[0].reshape(2, 2, width)[:, 1]
    stats = jnp.stack([s4[0, :, 0], s4[1, :, 0], s4[0, :, 1], s4[1, :, 1], bwd0], axis=1)

    def tap_spec(side):
        return pl.BlockSpec((2, None, GRID_W, width), lambda k1, f: (0, k1, 0, 2 * f + side))

    return pl.pallas_call(
        functools.partial(_spectrum_kernel, n=n),
        out_shape=jax.ShapeDtypeStruct((2, 2, n1h, GRID_W, width), F32),
        grid=(n1h, 2),
        in_specs=[tap_spec(0), tap_spec(1),
                  pl.BlockSpec((None, 2, GRID_W, 128), lambda k1, f: (k1, 0, 0, 0)),
                  pl.BlockSpec((2 * GRID_W, 2 * GRID_W), lambda k1, f: (0, 0)),
                  pl.BlockSpec((None, 5, width), lambda k1, f: (f, 0, 0)),
                  pl.BlockSpec((None, 1, width), lambda k1, f: (f, 0, 0))],
        out_specs=pl.BlockSpec((None, 2, None, GRID_W, width), lambda k1, f: (f, 0, k1, 0, 0)),
        compiler_params=_params("parallel", "parallel"),
        name="filter_spectrum",
    )(a, a, jnp.asarray(tw, F32), jnp.asarray(f_fwd, F32), stats, hy_d.reshape(2, 1, width))


def _inner_conv_kernel(a_ref, tw_ref, f_ref, fi_ref, h_ref, o_ref):
    n2 = GRID_W
    reps = a_ref.shape[-1] // 128
    for i in range(a_ref.shape[1]):
        twr = jnp.tile(tw_ref[i, 0], (1, reps))
        twi = jnp.tile(tw_ref[i, 1], (1, reps))
        ar, ai = a_ref[0, i], a_ref[1, i]
        stacked = jnp.concatenate([ar * twr - ai * twi, ar * twi + ai * twr], axis=0).astype(BF16)
        z = _dot(f_ref[...], stacked)
        zr, zi = z[:n2], z[n2:]
        hr, hi = h_ref[0, i], h_ref[1, i]
        stacked = jnp.concatenate([zr * hr - zi * hi, zr * hi + zi * hr], axis=0).astype(BF16)
        w = _dot(fi_ref[...], stacked)
        wr, wi = w[:n2], w[n2:]
        o_ref[0, i] = wr * twr + wi * twi
        o_ref[1, i] = wi * twr - wr * twi


def _inner_conv(a, tw, f_fwd, f_inv, spec):
    bsz, _, n1h, n2, c = a.shape
    kb = min(INNER_BINS, n1h)
    return pl.pallas_call(
        _inner_conv_kernel,
        out_shape=jax.ShapeDtypeStruct(a.shape, F32),
        grid=(n1h // kb, bsz),
        in_specs=[pl.BlockSpec((None, 2, kb, n2, c), lambda k1, b: (b, 0, k1, 0, 0)),
                  pl.BlockSpec((kb, 2, n2, 128), lambda k1, b: (k1, 0, 0, 0)),
                  pl.BlockSpec((2 * n2, 2 * n2), lambda k1, b: (0, 0)),
                  pl.BlockSpec((2 * n2, 2 * n2), lambda k1, b: (0, 0)),
                  pl.BlockSpec((2, kb, n2, c), lambda k1, b: (0, k1, 0, 0))],
        out_specs=pl.BlockSpec((None, 2, kb, n2, c), lambda k1, b: (b, 0, k1, 0, 0)),
        compiler_params=_params("parallel", "arbitrary"),
        name="inner_conv",
    )(a, tw, f_fwd, f_inv, spec)


def _outer_idft_gate_kernel(gi_ref, g_ref, b_ref, x_ref, o_ref, *, again):
    n1h, rows, tl = x_ref.shape
    stacked = b_ref[...].reshape(2 * n1h * rows, tl).astype(BF16)
    y = _dot(gi_ref[...], stacked) * x_ref[...].reshape(n1h * rows, tl)
    if again:
        o_ref[...] = _dot(g_ref[...], y.astype(BF16)).reshape(2, n1h, rows, tl)
    else:
        o_ref[...] = y.reshape(n1h, rows, tl)


def _outer_idft_gate(g_inv, g_fwd, b, px4, seg, *, again, tl):
    bsz, _, n1h, n2, c = b.shape
    tl = min(tl, c)
    per = c // tl
    rows = OUTER_ROWS
    if again:
        out_shape = jax.ShapeDtypeStruct((bsz, 2, n1h, n2, c), F32)
        out_spec = pl.BlockSpec((None, 2, n1h, rows, tl), lambda bb, j, l: (bb, 0, 0, j, l))
    else:
        out_shape = jax.ShapeDtypeStruct((bsz, n1h, n2, c), F32)
        out_spec = pl.BlockSpec((None, n1h, rows, tl), lambda bb, j, l: (bb, 0, j, l))
    return pl.pallas_call(
        functools.partial(_outer_idft_gate_kernel, again=again),
        out_shape=out_shape,
        grid=(bsz, n2 // rows, per),
        in_specs=[pl.BlockSpec((n1h * rows, 2 * n1h * rows), lambda bb, j, l: (0, 0)),
                  pl.BlockSpec((2 * n1h * rows, n1h * rows), lambda bb, j, l: (0, 0)),
                  pl.BlockSpec((None, 2, n1h, rows, tl), lambda bb, j, l: (bb, 0, 0, j, l)),
                  pl.BlockSpec((None, n1h, rows, tl), lambda bb, j, l: (bb, 0, j, seg * per + l))],
        out_specs=out_spec,
        compiler_params=_params("parallel", "parallel", "parallel"),
        name="outer_idft_gate",
    )(g_inv, g_fwd, b, px4)


def _hyena_mix(px, spec, consts, *, bsz, n, width):
    g_fwd, g_inv, tw, f_fwd, f_inv = consts
    g_fwd_b, g_inv_b = jnp.asarray(g_fwd, BF16), jnp.asarray(g_inv, BF16)
    f_fwd_b, f_inv_b = jnp.asarray(f_fwd, BF16), jnp.asarray(f_inv, BF16)
    tw_f = jnp.asarray(tw, F32)
    n1h = n // GRID_W
    px4 = px.reshape(bsz, n1h, GRID_W, -1)
    tl = width
    a = _outer_dft(g_fwd_b, px4, 5, c=width, tl=tl, high=False)
    a = _inner_conv(a, tw_f, f_fwd_b, f_inv_b, spec[0])
    a = _outer_idft_gate(g_inv_b, g_fwd_b, a, px4, 6, again=True, tl=tl)
    a = _inner_conv(a, tw_f, f_fwd_b, f_inv_b, spec[1])
    hy = _outer_idft_gate(g_inv_b, g_fwd_b, a, px4, 7, again=False, tl=tl)
    return hy.reshape(bsz * n, width)


def _out_proj_kernel(x_ref, hg_ref, hy_ref, wa_ref, wb_ref, gt_ref, o_ref):
    mix = _dot(hg_ref[...].astype(BF16), wa_ref[...]) + _dot(hy_ref[...].astype(BF16), wb_ref[...])
    o_ref[...] = x_ref[...] + gt_ref[0] * mix


def _out_proj(x, hg, hy, w, gate, *, tokens_per_mod, tm):
    t, d = x.shape
    wdt = hg.shape[1]
    tm = min(tm, tokens_per_mod)
    return pl.pallas_call(
        _out_proj_kernel,
        out_shape=jax.ShapeDtypeStruct((t, d), F32),
        grid=(t // tm,),
        in_specs=[pl.BlockSpec((tm, d), lambda i: (i, 0)),
                  pl.BlockSpec((tm, wdt), lambda i: (i, 0)),
                  pl.BlockSpec((tm, wdt), lambda i: (i, 0)),
                  pl.BlockSpec((wdt, d), lambda i: (0, 0)),
                  pl.BlockSpec((wdt, d), lambda i: (1, 0)),
                  pl.BlockSpec((1, 1, d), lambda i: ((i * tm) // tokens_per_mod, 0, 0))],
        out_specs=pl.BlockSpec((tm, d), lambda i: (i, 0)),
        compiler_params=_params("parallel"),
        name="out_proj",
    )(x, hg, hy, w, w, gate)


def kernel(x, c, ctx, c_ctx, ada_w, ada_b, norm_ffn1, ffn1_w1, ffn1_w3, ffn1_w2, norm_mix, w_in, hg_lb_logits, hg_norm, hy_conv_w, hy_conv_b, filt_w1, filt_b1, filt_w2, filt_b2, filt_w3, filt_b3, filt_freq, filt_w_out, hy_d, w_out, norm_ffn2, ffn2_w1, ffn2_w3, ffn2_w2, final_norm):
    bsz, n, d = x.shape
    n_ctx = ctx.shape[1]
    depth = ada_w.shape[0]
    assert depth == 1, "single-layer block"
    width = d // 2
    n_seg = w_in.shape[2] // width
    assert n_seg == 8 and n % GRID_W == 0 and n_ctx % SCAN_CHUNK == 0 and width % HEAD_DIM == 0
    d_ff = ffn1_w1.shape[2]
    tf = 512 if d_ff % 512 == 0 else d_ff
    tm = 512
    l = 0

    rows = -(-(bsz + 1) // 8) * 8
    cs = jnp.concatenate([c, c_ctx[None, :], jnp.zeros((rows - bsz - 1, d), F32)], axis=0)
    mods = _ada(cs, ada_w[l], ada_b[l]).reshape(rows, N_MOD, d)
    mx = [mods[:bsz, i][:, None, :] for i in range(N_MOD)]
    mc = [mods[bsz:bsz + 1, i][:, None, :] for i in range(N_MOD)]

    bf = lambda a: a.astype(BF16)
    w1a, w3a, w2a = bf(ffn1_w1[l]), bf(ffn1_w3[l]), bf(ffn1_w2[l])
    xt = x.reshape(bsz * n, d)
    yt = ctx.reshape(bsz * n_ctx, d)

    xt = _ffn(xt, mx[0], mx[1], mx[2], norm_ffn1[l], w1a, w3a, w2a, final_norm,
              tokens_per_mod=n, final_norm=False, tm=tm, tf=tf)
    yt = _ffn(yt, mc[0], mc[1], mc[2], norm_ffn1[l], w1a, w3a, w2a, final_norm,
              tokens_per_mod=bsz * n_ctx, final_norm=False, tm=tm, tf=tf)

    w_in_b = bf(w_in[l])
    pc = _proj(yt, mc[3], mc[4], norm_mix[l], w_in_b, hy_conv_w[l], hy_conv_b[l], n_seg=3, seg_w=width,
               conv_from=n_seg, tokens_per_mod=bsz * n_ctx, tm=tm)
    px = _proj(xt, mx[3], mx[4], norm_mix[l], w_in_b, hy_conv_w[l], hy_conv_b[l], n_seg=n_seg, seg_w=width,
               conv_from=5, tokens_per_mod=n, tm=2 * tm)
    cols = n_seg * width

    hg = _hgrn(px.reshape(bsz, n, cols), pc.reshape(bsz, n_ctx, 3 * width), hg_lb_logits, hg_norm[l], width=width)

    consts = _dft_constants(n)
    filt = (filt_w1[l], filt_b1[l], filt_w2[l], filt_b2[l], filt_w3[l], filt_b3[l], filt_freq[l], filt_w_out[l])
    spec = _hyena_spectra(n, width, filt, hy_d[l], consts)
    hy = _hyena_mix(px, spec, consts, bsz=bsz, n=n, width=width)

    xt = _out_proj(xt, hg.reshape(bsz * n, width), hy, bf(w_out[l]), mx[5],
                   tokens_per_mod=n, tm=tm)

    out = _ffn(xt, mx[6], mx[7], mx[8], norm_ffn2[l], bf(ffn2_w1[l]), bf(ffn2_w3[l]), bf(ffn2_w2[l]), final_norm,
               tokens_per_mod=n, final_norm=True, tm=tm, tf=tf)
    return out.reshape(bsz, n, d)
```

```python
import functools
import math

import numpy as np
import jax
import jax.numpy as jnp
from jax import lax
from jax.experimental import pallas as pl
from jax.experimental.pallas import tpu as pltpu

F32 = jnp.float32
BF16 = jnp.bfloat16
EPS = 1e-6
N_MOD = 9
HEAD_DIM = 128
GRID_W = 64
SCAN_CHUNK = 64
PREP_CHUNKS = 4
SCAN_BLOCKS = 2
OUTER_ROWS = 8
BF16_ROWS = 16
INNER_BINS = 4
FILT_EMB = 33
DECAY_TARGET = 1e-2
DECAY_FAST_PCT = 0.3
DECAY_SLOW_PCT = 1.5
VMEM_LIMIT_BYTES = 56 * 1024 * 1024
HIGHEST = lax.Precision.HIGHEST


def _params(*semantics):
    return pltpu.CompilerParams(dimension_semantics=semantics, vmem_limit_bytes=VMEM_LIMIT_BYTES)


def _dot(a, b):
    return jnp.dot(a, b, preferred_element_type=F32)


def _silu(a):
    return a * jax.nn.sigmoid(a)


def _ada_kernel(c_ref, w_ref, b_ref, o_ref):
    h = _silu(c_ref[...]).astype(BF16)
    o_ref[...] = _dot(h, w_ref[...].astype(BF16)) + b_ref[...]


def _ada(cs, w, b):
    rows, d = cs.shape
    n = w.shape[1]
    tn = d // 2
    return pl.pallas_call(
        _ada_kernel,
        out_shape=jax.ShapeDtypeStruct((rows, n), F32),
        grid=(n // tn,),
        in_specs=[pl.BlockSpec((rows, d), lambda j: (0, 0)),
                  pl.BlockSpec((d, tn), lambda j: (0, j)),
                  pl.BlockSpec((1, tn), lambda j: (0, j))],
        out_specs=pl.BlockSpec((rows, tn), lambda j: (0, j)),
        compiler_params=_params("arbitrary"),
        name="ada_mod",
    )(cs, w, b.reshape(1, n))


def _norm_mod(x, gain, shift, scale):
    y = x * lax.rsqrt(jnp.mean(x * x, axis=-1, keepdims=True) + EPS) * gain
    return (y * (1.0 + scale) + shift).astype(BF16)


def _ffn_kernel(x_ref, sh_ref, sc_ref, gt_ref, g_ref, w1_ref, w3_ref, w2_ref, fin_ref, o_ref,
                h_scr, acc_scr, *, final_norm):
    j = pl.program_id(1)

    @pl.when(j == 0)
    def _():
        h_scr[...] = _norm_mod(x_ref[...], g_ref[...], sh_ref[0], sc_ref[0])
        acc_scr[...] = jnp.zeros_like(acc_scr)

    h = h_scr[...]
    a = _dot(h, w1_ref[...])
    b = _dot(h, w3_ref[...])
    act = (_silu(a) * b).astype(BF16)
    acc_scr[...] += _dot(act, w2_ref[...])

    @pl.when(j == pl.num_programs(1) - 1)
    def _():
        out = x_ref[...] + 0.5 * gt_ref[0] * acc_scr[...]
        if final_norm:
            out = out * lax.rsqrt(jnp.mean(out * out, axis=-1, keepdims=True) + EPS) * fin_ref[...]
        o_ref[...] = out


def _ffn(x, shift, scale, gate, gain, w1, w3, w2, fin, *, tokens_per_mod, final_norm, tm, tf):
    t, d = x.shape
    f = w1.shape[1]
    tm = min(tm, tokens_per_mod)
    mod_spec = pl.BlockSpec((1, 1, d), lambda i, j: ((i * tm) // tokens_per_mod, 0, 0))
    vec_spec = pl.BlockSpec((1, d), lambda i, j: (0, 0))
    return pl.pallas_call(
        functools.partial(_ffn_kernel, final_norm=final_norm),
        out_shape=jax.ShapeDtypeStruct((t, d), F32),
        grid=(t // tm, f // tf),
        in_specs=[pl.BlockSpec((tm, d), lambda i, j: (i, 0)),
                  mod_spec, mod_spec, mod_spec, vec_spec,
                  pl.BlockSpec((d, tf), lambda i, j: (0, j)),
                  pl.BlockSpec((d, tf), lambda i, j: (0, j)),
                  pl.BlockSpec((tf, d), lambda i, j: (j, 0)),
                  vec_spec],
        out_specs=pl.BlockSpec((tm, d), lambda i, j: (i, 0)),
        scratch_shapes=[pltpu.VMEM((tm, d), BF16), pltpu.VMEM((tm, d), F32)],
        compiler_params=_params("parallel", "arbitrary"),
        name="swiglu_ffn",
    )(x, shift, scale, gate, gain.reshape(1, d), w1, w3, w2, fin.reshape(1, d))


def _proj_kernel(x_ref, sh_ref, sc_ref, g_ref, w_ref, cw_ref, cb_ref, o_ref, h_scr, *, conv_from):
    j = pl.program_id(1)

    @pl.when(j == 0)
    def _():
        h_scr[...] = _norm_mod(x_ref[...], g_ref[...], sh_ref[0], sc_ref[0])

    p = _dot(h_scr[...], w_ref[...])

    @pl.when(j < conv_from)
    def _():
        o_ref[...] = p

    @pl.when(j >= conv_from)
    def _():
        tm, w = p.shape
        rows3 = lambda t: t.reshape(tm // GRID_W, GRID_W, w)
        y = (rows3(pltpu.roll(p, 1, axis=0)) * cw_ref[0] + rows3(p) * cw_ref[1]
             + rows3(pltpu.roll(p, tm - 1, axis=0)) * cw_ref[2] + cb_ref[...])
        o_ref[...] = y.reshape(tm, w)


def _proj(x, shift, scale, gain, w, conv_w, conv_b, *, n_seg, seg_w, conv_from, tokens_per_mod, tm):
    t, d = x.shape
    tm = min(tm, tokens_per_mod)
    mod_spec = pl.BlockSpec((1, 1, d), lambda i, j: ((i * tm) // tokens_per_mod, 0, 0))
    conv_idx = lambda i, j: (0, jnp.maximum(j - conv_from, 0))
    pos = jnp.arange(GRID_W)[None, :, None]
    edge = jnp.stack([pos > 0, pos >= 0, pos < GRID_W - 1], axis=0)[:, 0]
    conv_w = jnp.where(edge, conv_w[:, None, :], 0.0)
    return pl.pallas_call(
        functools.partial(_proj_kernel, conv_from=conv_from),
        out_shape=jax.ShapeDtypeStruct((t, n_seg * seg_w), F32),
        grid=(t // tm, n_seg),
        in_specs=[pl.BlockSpec((tm, d), lambda i, j: (i, 0)),
                  mod_spec, mod_spec,
                  pl.BlockSpec((1, d), lambda i, j: (0, 0)),
                  pl.BlockSpec((d, seg_w), lambda i, j: (0, j)),
                  pl.BlockSpec((3, GRID_W, seg_w), lambda i, j: (0, 0, jnp.maximum(j - conv_from, 0))),
                  pl.BlockSpec((1, seg_w), conv_idx)],
        out_specs=pl.BlockSpec((tm, seg_w), lambda i, j: (i, j)),
        scratch_shapes=[pltpu.VMEM((tm, d), BF16)],
        compiler_params=_params("parallel", "arbitrary"),
        name="in_proj",
    )(x, shift, scale, gain.reshape(1, d), w, conv_w, conv_b.reshape(1, -1))


def _split_dot(tri, g):
    g1 = g.astype(BF16)
    r1 = g - g1.astype(F32)
    g2 = r1.astype(BF16)
    g3 = (r1 - g2.astype(F32)).astype(BF16)
    return _dot(tri, g1) + _dot(tri, g2) + _dot(tri, g3)


def _gates(a, lb):
    f = lb + (1.0 - lb) * jax.nn.sigmoid(a)
    return 1.0 - f, jnp.log(f)


def _tn_dot(a, b):
    return lax.dot_general(a, b, (((0,), (0,)), ((), ())), preferred_element_type=F32)


def _nt_dot(a, b):
    return lax.dot_general(a, b, (((1,), (1,)), ((), ())), preferred_element_type=F32)


def _hgrn_kernel(af_ref, ab_ref, v_ref, q_ref, gate_ref, caf_ref, cab_ref, cv_ref, lbl_ref, gain_ref,
                 o_ref, of_scr, ob_scr, sf_scr, sb_scr, qin_scr, qt_scr, kt_scr, kd_scr, dec_scr,
                 *, n_chunks, n_ctx_chunks):
    c = SCAN_CHUNK
    mid = c // 2
    row = lax.broadcasted_iota(jnp.int32, (c, c), 0)
    col = lax.broadcasted_iota(jnp.int32, (c, c), 1)
    lower = row >= col
    tri_fwd = jnp.where(lower, 1.0, 0.0).astype(BF16)
    tri_bwd = jnp.where(col >= row, 1.0, 0.0).astype(BF16)
    upper = col >= row
    rows_p = PREP_CHUNKS * c
    prow = lax.broadcasted_iota(jnp.int32, (rows_p, rows_p), 0)
    pcol = lax.broadcasted_iota(jnp.int32, (rows_p, rows_p), 1)
    same = (prow // c) == (pcol // c)
    tri_fwd_p = jnp.where(same & (prow >= pcol), 1.0, 0.0).astype(BF16)
    tri_bwd_p = jnp.where(same & (pcol >= prow), 1.0, 0.0).astype(BF16)

    def lower_bound(direction):
        lg = lbl_ref[direction]
        ex = jnp.exp(lg - jnp.max(lg, axis=0, keepdims=True))
        return ex[0:1, :] / jnp.sum(ex, axis=0, keepdims=True)

    lb_f = lower_bound(0)
    lb_b = lower_bound(1)
    q_scale = HEAD_DIM ** -0.5

    def ctx_chunk(a, v, st_ref, lb, tri, end_row):
        k, g = _gates(a, lb)
        b = _split_dot(tri, g)
        b_end = b[end_row:end_row + 1, :]
        k_d = (k * jnp.exp(b_end - b)).astype(BF16)
        st_ref[...] = st_ref[...] * jnp.exp(b_end) + _tn_dot(v.astype(BF16), k_d)

    sf_scr[...] = jnp.zeros_like(sf_scr)
    sb_scr[...] = jnp.zeros_like(sb_scr)

    def ctx_body(i, carry):
        lo = pl.multiple_of(i * c, c)
        ctx_chunk(caf_ref[pl.ds(lo, c), :], cv_ref[pl.ds(lo, c), :], sf_scr, lb_f, tri_fwd, c - 1)
        hi = pl.multiple_of((n_ctx_chunks - 1 - i) * c, c)
        ctx_chunk(cab_ref[pl.ds(hi, c), :], cv_ref[pl.ds(hi, c), :], sb_scr, lb_b, tri_bwd, 0)
        return carry

    lax.fori_loop(0, n_ctx_chunks, ctx_body, 0, unroll=True)

    def prep(a_ref, lb, tri_p, end_row, d, lo):
        k, g = _gates(a_ref[pl.ds(lo, rows_p), :], lb)
        b = _split_dot(tri_p, g).reshape(PREP_CHUNKS, c, HEAD_DIM)
        b_mid = b[:, mid:mid + 1, :]
        b_end = b[:, end_row:end_row + 1, :]
        q_t = (q_ref[pl.ds(lo, rows_p), :] * q_scale).reshape(PREP_CHUNKS, c, HEAD_DIM) * jnp.exp(b - b_mid)
        k_t = k.reshape(PREP_CHUNKS, c, HEAD_DIM) * jnp.exp(b_mid - b)
        flat = lambda t: t.reshape(rows_p, HEAD_DIM).astype(BF16)
        qt_scr[d, pl.ds(lo, rows_p), :] = flat(q_t)
        kt_scr[d, pl.ds(lo, rows_p), :] = flat(k_t)
        qin_scr[d, pl.ds(lo, rows_p), :] = flat(q_t * jnp.exp(b_mid))
        kd_scr[d, pl.ds(lo, rows_p), :] = flat(k_t * jnp.exp(b_end - b_mid))
        return jnp.broadcast_to(jnp.exp(b_end), (PREP_CHUNKS, 8, HEAD_DIM))

    def prep_body(i, carry):
        lo = pl.multiple_of(i * rows_p, rows_p)
        dec_scr[0, pl.ds(i * PREP_CHUNKS, PREP_CHUNKS)] = prep(af_ref, lb_f, tri_fwd_p, c - 1, 0, lo)
        dec_scr[1, pl.ds(i * PREP_CHUNKS, PREP_CHUNKS)] = prep(ab_ref, lb_b, tri_bwd_p, 0, 1, lo)
        return carry

    lax.fori_loop(0, n_chunks // PREP_CHUNKS, prep_body, 0)

    zeros_blk = jnp.zeros((c, HEAD_DIM), BF16)

    def block_diag(t):
        return jnp.concatenate(
            [jnp.concatenate([t[j * c:(j + 1) * c] if m == j else zeros_blk for m in range(PREP_CHUNKS)], axis=1)
             for j in range(PREP_CHUNKS)], axis=0)

    n_blocks = n_chunks // PREP_CHUNKS
    masks = (same & (prow >= pcol), same & (pcol >= prow))
    st_refs = (sf_scr, sb_scr)
    out_refs = (of_scr, ob_scr)
    orders = (tuple(range(PREP_CHUNKS)), tuple(range(PREP_CHUNKS - 1, -1, -1)))

    def body(i, carry):
        streams = [(0, i * SCAN_BLOCKS + u) for u in range(SCAN_BLOCKS)]
        streams += [(1, n_blocks - 1 - (i * SCAN_BLOCKS + u)) for u in range(SCAN_BLOCKS)]
        los = [pl.multiple_of(bi * rows_p, rows_p) for _, bi in streams]
        vbs = [v_ref[pl.ds(lo, rows_p), :].astype(BF16) for lo in los]
        scores = [_nt_dot(qt_scr[d, pl.ds(lo, rows_p), :], kt_scr[d, pl.ds(lo, rows_p), :])
                  for (d, _), lo in zip(streams, los)]
        upds = [_tn_dot(vb, block_diag(kd_scr[d, pl.ds(lo, rows_p), :]))
                for (d, _), lo, vb in zip(streams, los, vbs)]
        outs = [_dot(jnp.where(masks[d], s, 0.0).astype(BF16), vb) for (d, _), s, vb in zip(streams, scores, vbs)]
        seen = [[None] * PREP_CHUNKS for _ in streams]
        for d in (0, 1):
            st = st_refs[d][...]
            for n_s, (ds_, bi) in enumerate(streams):
                if ds_ != d:
                    continue
                for j in orders[d]:
                    seen[n_s][j] = st.astype(BF16)
                    st = (st * dec_scr[d, bi * PREP_CHUNKS + j][0:1, :]
                          + upds[n_s][:, j * HEAD_DIM:(j + 1) * HEAD_DIM])
            st_refs[d][...] = st
        for n_s, ((d, _), lo) in enumerate(zip(streams, los)):
            q_in = qin_scr[d, pl.ds(lo, rows_p), :]
            carried = jnp.concatenate([_nt_dot(q_in[j * c:(j + 1) * c], seen[n_s][j]) for j in range(PREP_CHUNKS)],
                                      axis=0)
            out_refs[d][pl.ds(lo, rows_p), :] = outs[n_s] + carried
        return carry

    lax.fori_loop(0, n_blocks // SCAN_BLOCKS, body, 0)

    gain = gain_ref[...]
    rows = 4 * c

    def readout(i, carry):
        lo = pl.multiple_of(i * rows, rows)
        o = of_scr[pl.ds(lo, rows), :] + ob_scr[pl.ds(lo, rows), :]
        o = o * lax.rsqrt(jnp.mean(o * o, axis=-1, keepdims=True) + EPS) * gain
        o_ref[pl.ds(lo, rows), :] = o * _silu(gate_ref[pl.ds(lo, rows), :])
        return carry

    lax.fori_loop(0, n_chunks // 4, readout, 0)


def _hgrn(px, pc, lb_logits, gain, *, width):
    bsz, n, _ = px.shape
    n_ctx = pc.shape[1]
    heads = width // HEAD_DIM
    hd = HEAD_DIM

    def seg(s):
        return pl.BlockSpec((None, n, hd), lambda b, h: (b, 0, s * heads + h))

    def cseg(s):
        return pl.BlockSpec((None, n_ctx, hd), lambda b, h: (b, 0, s * heads + h))

    depth1 = lb_logits.shape[1]
    return pl.pallas_call(
        functools.partial(_hgrn_kernel, n_chunks=n // SCAN_CHUNK, n_ctx_chunks=n_ctx // SCAN_CHUNK),
        out_shape=jax.ShapeDtypeStruct((bsz, n, width), F32),
        grid=(bsz, heads),
        in_specs=[seg(0), seg(1), seg(2), seg(3), seg(4), cseg(0), cseg(1), cseg(2),
                  pl.BlockSpec((2, depth1, hd), lambda b, h: (0, 0, h)),
                  pl.BlockSpec((1, hd), lambda b, h: (0, h))],
        out_specs=pl.BlockSpec((None, n, hd), lambda b, h: (b, 0, h)),
        scratch_shapes=[pltpu.VMEM((n, hd), F32), pltpu.VMEM((n, hd), F32),
                        pltpu.VMEM((hd, hd), F32), pltpu.VMEM((hd, hd), F32),
                        pltpu.VMEM((2, n, hd), BF16), pltpu.VMEM((2, n, hd), BF16),
                        pltpu.VMEM((2, n, hd), BF16), pltpu.VMEM((2, n, hd), BF16),
                        pltpu.VMEM((2, n // SCAN_CHUNK, 8, hd), F32)],
        compiler_params=_params("parallel", "parallel"),
        name="hgrn_scan",
    )(px, px, px, px, px, pc, pc, pc, lb_logits, gain.reshape(1, width))


def _dft_constants(n):
    n2 = GRID_W
    n1h = n // n2
    n1 = 2 * n1h
    big = 2 * n
    k1 = np.arange(n1h)[:, None]
    m1 = np.arange(n1h)[None, :]
    ang = -2.0 * np.pi * m1 * (k1 + 0.5) / n1
    eye = np.eye(OUTER_ROWS)
    g_fwd = np.kron(np.concatenate([np.cos(ang), np.sin(ang)], axis=0), eye)
    g_inv = np.kron(np.concatenate([np.cos(ang).T, np.sin(ang).T], axis=1), eye)
    m2 = np.arange(n2)[None, :]
    tw_ang = -2.0 * np.pi * m2 * (k1 + 0.5) / big
    tw = np.stack([np.cos(tw_ang), np.sin(tw_ang)], axis=1)
    tw = np.broadcast_to(tw[..., None], (n1h, 2, n2, 128))
    k2 = np.arange(n2)[:, None]
    f_ang = -2.0 * np.pi * k2 * m2 / n2
    fr, fi = np.cos(f_ang), np.sin(f_ang)
    f_fwd = np.block([[fr, -fi], [fi, fr]])
    f_inv = np.block([[fr, fi], [-fi, fr]])
    return g_fwd, g_inv, np.ascontiguousarray(tw), f_fwd, f_inv


def _filt_mlp_kernel(z_ref, w1_ref, b1_ref, w2_ref, b2_ref, w3_ref, b3_ref, fr_ref, o_ref):
    fr = fr_ref[...]
    hp = lambda a, b: jnp.dot(a, b, precision=HIGHEST, preferred_element_type=F32)
    hid = jnp.sin(fr * (hp(z_ref[...], w1_ref[...]) + b1_ref[...]))
    hid = jnp.sin(fr * (hp(hid, w2_ref[...]) + b2_ref[...]))
    o_ref[...] = jnp.sin(fr * (hp(hid, w3_ref[...]) + b3_ref[...]))


def _filt_taps_kernel(hid_ref, wo_ref, dl_ref, h_ref, s_ref, *, n):
    h = jnp.dot(hid_ref[...], wo_ref[...], precision=HIGHEST, preferred_element_type=F32)
    t = lax.broadcasted_iota(jnp.int32, h.shape, 0).astype(F32) * (1.0 / max(n - 1, 1))
    hw = h * jnp.exp(-t * dl_ref[...])
    h_ref[...] = hw
    s_ref[0:1, :] = jnp.sum(jnp.abs(hw), axis=0, keepdims=True)
    s_ref[1:2, :] = jnp.abs(hw[0:1, :])


def _spectrum_kernel(af_ref, ab_ref, tw_ref, f_ref, fl_ref, s_ref, d_ref, o_ref, *, n):
    n2 = GRID_W
    reps = af_ref.shape[-1] // 128
    l1 = s_ref[0:1, :] + s_ref[2:3, :] - s_ref[3:4, :]
    hb0 = s_ref[4:5, :]
    norm = 1.0 / l1
    scale = 2.0 / (2 * n)
    for i in range(af_ref.shape[1]):
        twr = jnp.tile(tw_ref[i, 0], (1, reps))
        twi = jnp.tile(tw_ref[i, 1], (1, reps))

        def inner(a_ref):
            ar, ai = a_ref[0, i], a_ref[1, i]
            stacked = jnp.concatenate([ar * twr - ai * twi, ar * twi + ai * twr], axis=0)
            z = _dot3(f_ref[...], fl_ref[...], stacked)
            return z[:n2], z[n2:]

        zfr, zfi = inner(af_ref)
        zbr, zbi = inner(ab_ref)
        o_ref[0, i] = scale * ((zfr + zbr - hb0) * norm + d_ref[...])
        o_ref[1, i] = scale * ((zfi - zbi) * norm)


def _dot3(a_hi, a_lo, x):
    x_hi = x.astype(BF16)
    x_lo = (x - x_hi.astype(F32)).astype(BF16)
    return _dot(a_hi, x_hi) + _dot(a_hi, x_lo) + _dot(a_lo, x_hi)


def _hi_lo(a):
    hi = np.asarray(a, np.float64).astype(BF16)
    lo = (a - hi.astype(np.float64)).astype(BF16)
    return jnp.asarray(hi), jnp.asarray(lo)


def _slabs(x):
    return [x[..., h * OUTER_ROWS:(h + 1) * OUTER_ROWS, :] for h in range(x.shape[-2] // OUTER_ROWS)]


def _outer_dft_kernel(g_ref, gl_ref, x_ref, o_ref, *, high):
    n1h, _, tl = x_ref.shape
    parts = []
    for xs in _slabs(x_ref[...]):
        xs = xs.reshape(n1h * OUTER_ROWS, tl)
        r = _dot3(g_ref[...], gl_ref[...], xs) if high else _dot(g_ref[...], xs.astype(BF16))
        parts.append(r.reshape(2, n1h, OUTER_ROWS, tl))
    o_ref[...] = jnp.concatenate(parts, axis=2).astype(o_ref.dtype)


def _outer_dft(g, g_lo, x4, seg, *, c, tl, high):
    s, n1h, n2, _ = x4.shape
    tl = min(tl, c)
    per = c // tl
    rows = OUTER_ROWS if high else BF16_ROWS
    g_spec = pl.BlockSpec((2 * n1h * OUTER_ROWS, n1h * OUTER_ROWS), lambda b, j, l: (0, 0))
    return pl.pallas_call(
        functools.partial(_outer_dft_kernel, high=high),
        out_shape=jax.ShapeDtypeStruct((s, 2, n1h, n2, c), F32 if high else BF16),
        grid=(s, n2 // rows, per),
        in_specs=[g_spec, g_spec,
                  pl.BlockSpec((None, n1h, rows, tl), lambda b, j, l: (b, 0, j, seg * per + l))],
        out_specs=pl.BlockSpec((None, 2, n1h, rows, tl), lambda b, j, l: (b, 0, 0, j, l)),
        compiler_params=_params("parallel", "parallel", "parallel"),
        name="outer_dft",
    )(g, g_lo, x4)


def _hyena_spectra(n, width, filt, hy_d, consts):
    w1, b1, w2, b2, w3, b3, freq, w_out = filt
    g_fwd, _, tw, f_fwd, _ = consts
    n1h = n // GRID_W
    order = w1.shape[1]
    pos = np.arange(n, dtype=np.float64)
    t = pos / max(n - 1, 1)
    bands = (FILT_EMB - 1) // 2
    fb = np.linspace(1e-4, bands - 1, bands)
    ang = (2 * math.pi / n) * pos[:, None] * fb[None, :]
    z = np.concatenate([t[:, None], np.cos(ang), -np.sin(ang)], axis=-1)
    emb_pad = 128
    z = jnp.asarray(np.pad(z, ((0, 0), (0, emb_pad - FILT_EMB))), F32)
    w1p = jnp.pad(w1, ((0, emb_pad - FILT_EMB), (0, 0)))
    row = lambda a: a.reshape(1, -1)
    full = lambda shape: pl.BlockSpec(shape, lambda: tuple(0 for _ in shape))
    hid = pl.pallas_call(
        _filt_mlp_kernel,
        out_shape=jax.ShapeDtypeStruct((n, order), F32),
        in_specs=[full((n, emb_pad)), full((emb_pad, order)), full((1, order)), full((order, order)),
                  full((1, order)), full((order, order)), full((1, order)), full((1, order))],
        out_specs=full((n, order)),
        compiler_params=pltpu.CompilerParams(vmem_limit_bytes=VMEM_LIMIT_BYTES),
        name="filter_mlp",
    )(z, w1p, row(b1), w2, row(b2), w3, row(b3), row(freq))

    cols = 4 * width
    deltas = np.abs(np.linspace(math.log(DECAY_TARGET) / DECAY_SLOW_PCT,
                                math.log(DECAY_TARGET) / DECAY_FAST_PCT, width))
    deltas4 = jnp.asarray(np.tile(deltas, 4)[None, :], F32)
    tc = min(512, cols)
    taps, sums = pl.pallas_call(
        functools.partial(_filt_taps_kernel, n=n),
        out_shape=(jax.ShapeDtypeStruct((n, cols), F32), jax.ShapeDtypeStruct((2, cols), F32)),
        grid=(cols // tc,),
        in_specs=[pl.BlockSpec((n, order), lambda j: (0, 0)),
                  pl.BlockSpec((order, tc), lambda j: (0, j)),
                  pl.BlockSpec((1, tc), lambda j: (0, j))],
        out_specs=(pl.BlockSpec((n, tc), lambda j: (0, j)), pl.BlockSpec((2, tc), lambda j: (0, j))),
        compiler_params=_params("parallel"),
        name="filter_taps",
    )(hid, w_out, deltas4)

    g_hi, g_lo = _hi_lo(g_fwd)
    a = _outer_dft(g_hi, g_lo, taps.reshape(1, n1h, GRID_W, cols), 0, c=cols, tl=1024, high=True)[0]
    s4 = sums.reshape(2, 2, 2, width)
    bwd0 = taps[0].reshape(2, 2, width)[:, 1]
    stats = jnp.stack([s4[0, :, 0], s4[1, :, 0], s4[0, :, 1], s4[1, :, 1], bwd0], axis=1)

    kb = min(INNER_BINS, n1h)

    def tap_spec(side):
        return pl.BlockSpec((2, kb, GRID_W, width), lambda k1, f: (0, k1, 0, 2 * f + side))

    f_hi, f_lo = _hi_lo(f_fwd)
    f_spec = pl.BlockSpec((2 * GRID_W, 2 * GRID_W), lambda k1, f: (0, 0))
    return pl.pallas_call(
        functools.partial(_spectrum_kernel, n=n),
        out_shape=jax.ShapeDtypeStruct((2, 2, n1h, GRID_W, width), F32),
        grid=(n1h // kb, 2),
        in_specs=[tap_spec(0), tap_spec(1),
                  pl.BlockSpec((kb, 2, GRID_W, 128), lambda k1, f: (k1, 0, 0, 0)),
                  f_spec, f_spec,
                  pl.BlockSpec((None, 5, width), lambda k1, f: (f, 0, 0)),
                  pl.BlockSpec((None, 1, width), lambda k1, f: (f, 0, 0))],
        out_specs=pl.BlockSpec((None, 2, kb, GRID_W, width), lambda k1, f: (f, 0, k1, 0, 0)),
        compiler_params=_params("parallel", "parallel"),
        name="filter_spectrum",
    )(a, a, jnp.asarray(tw, F32), f_hi, f_lo, stats, hy_d.reshape(2, 1, width))


def _inner_conv_kernel(a_ref, tw_ref, f_ref, fi_ref, h_ref, o_ref):
    n2 = GRID_W
    reps = a_ref.shape[-1] // 128
    for i in range(a_ref.shape[1]):
        twr = jnp.tile(tw_ref[i, 0], (1, reps))
        twi = jnp.tile(tw_ref[i, 1], (1, reps))
        ar, ai = a_ref[0, i].astype(F32), a_ref[1, i].astype(F32)
        stacked = jnp.concatenate([ar * twr - ai * twi, ar * twi + ai * twr], axis=0).astype(BF16)
        z = _dot(f_ref[...], stacked)
        zr, zi = z[:n2], z[n2:]
        hr, hi = h_ref[0, i], h_ref[1, i]
        stacked = jnp.concatenate([zr * hr - zi * hi, zr * hi + zi * hr], axis=0).astype(BF16)
        w = _dot(fi_ref[...], stacked)
        wr, wi = w[:n2], w[n2:]
        o_ref[0, i] = (wr * twr + wi * twi).astype(BF16)
        o_ref[1, i] = (wi * twr - wr * twi).astype(BF16)


def _inner_conv(a, tw, f_fwd, f_inv, spec):
    bsz, _, n1h, n2, c = a.shape
    kb = min(INNER_BINS, n1h)
    return pl.pallas_call(
        _inner_conv_kernel,
        out_shape=jax.ShapeDtypeStruct(a.shape, BF16),
        grid=(n1h // kb, bsz),
        in_specs=[pl.BlockSpec((None, 2, kb, n2, c), lambda k1, b: (b, 0, k1, 0, 0)),
                  pl.BlockSpec((kb, 2, n2, 128), lambda k1, b: (k1, 0, 0, 0)),
                  pl.BlockSpec((2 * n2, 2 * n2), lambda k1, b: (0, 0)),
                  pl.BlockSpec((2 * n2, 2 * n2), lambda k1, b: (0, 0)),
                  pl.BlockSpec((2, kb, n2, c), lambda k1, b: (0, k1, 0, 0))],
        out_specs=pl.BlockSpec((None, 2, kb, n2, c), lambda k1, b: (b, 0, k1, 0, 0)),
        compiler_params=_params("parallel", "arbitrary"),
        name="inner_conv",
    )(a, tw, f_fwd, f_inv, spec)


def _outer_idft_gate_kernel(gi_ref, g_ref, b_ref, x_ref, o_ref, *, again):
    n1h, _, tl = x_ref.shape
    parts = []
    for bs, xs in zip(_slabs(b_ref[...].astype(F32)), _slabs(x_ref[...])):
        stacked = bs.reshape(2 * n1h * OUTER_ROWS, tl).astype(BF16)
        y = _dot(gi_ref[...], stacked) * xs.reshape(n1h * OUTER_ROWS, tl)
        if again:
            parts.append(_dot(g_ref[...], y.astype(BF16)).reshape(2, n1h, OUTER_ROWS, tl))
        else:
            parts.append(y.reshape(n1h, OUTER_ROWS, tl))
    o_ref[...] = jnp.concatenate(parts, axis=-2).astype(o_ref.dtype)


def _outer_idft_gate(g_inv, g_fwd, b, px4, seg, *, again, tl):
    bsz, _, n1h, n2, c = b.shape
    tl = min(tl, c)
    per = c // tl
    rows = BF16_ROWS
    if again:
        out_shape = jax.ShapeDtypeStruct((bsz, 2, n1h, n2, c), BF16)
        out_spec = pl.BlockSpec((None, 2, n1h, rows, tl), lambda bb, j, l: (bb, 0, 0, j, l))
    else:
        out_shape = jax.ShapeDtypeStruct((bsz, n1h, n2, c), F32)
        out_spec = pl.BlockSpec((None, n1h, rows, tl), lambda bb, j, l: (bb, 0, j, l))
    return pl.pallas_call(
        functools.partial(_outer_idft_gate_kernel, again=again),
        out_shape=out_shape,
        grid=(bsz, n2 // rows, per),
        in_specs=[pl.BlockSpec((n1h * OUTER_ROWS, 2 * n1h * OUTER_ROWS), lambda bb, j, l: (0, 0)),
                  pl.BlockSpec((2 * n1h * OUTER_ROWS, n1h * OUTER_ROWS), lambda bb, j, l: (0, 0)),
                  pl.BlockSpec((None, 2, n1h, rows, tl), lambda bb, j, l: (bb, 0, 0, j, l)),
                  pl.BlockSpec((None, n1h, rows, tl), lambda bb, j, l: (bb, 0, j, seg * per + l))],
        out_specs=out_spec,
        compiler_params=_params("parallel", "parallel", "parallel"),
        name="outer_idft_gate",
    )(g_inv, g_fwd, b, px4)


def _hyena_mix(px, spec, consts, *, bsz, n, width):
    g_fwd, g_inv, tw, f_fwd, f_inv = consts
    g_fwd_b, g_inv_b = jnp.asarray(g_fwd, BF16), jnp.asarray(g_inv, BF16)
    f_fwd_b, f_inv_b = jnp.asarray(f_fwd, BF16), jnp.asarray(f_inv, BF16)
    tw_f = jnp.asarray(tw, F32)
    n1h = n // GRID_W
    px4 = px.reshape(bsz, n1h, GRID_W, -1)
    tl = width
    a = _outer_dft(g_fwd_b, g_fwd_b, px4, 5, c=width, tl=tl, high=False)
    a = _inner_conv(a, tw_f, f_fwd_b, f_inv_b, spec[0])
    a = _outer_idft_gate(g_inv_b, g_fwd_b, a, px4, 6, again=True, tl=tl)
    a = _inner_conv(a, tw_f, f_fwd_b, f_inv_b, spec[1])
    hy = _outer_idft_gate(g_inv_b, g_fwd_b, a, px4, 7, again=False, tl=tl)
    return hy.reshape(bsz * n, width)


def _out_proj_kernel(x_ref, hg_ref, hy_ref, wa_ref, wb_ref, gt_ref, o_ref):
    mix = _dot(hg_ref[...].astype(BF16), wa_ref[...]) + _dot(hy_ref[...].astype(BF16), wb_ref[...])
    o_ref[...] = x_ref[...] + gt_ref[0] * mix


def _out_proj(x, hg, hy, w, gate, *, tokens_per_mod, tm):
    t, d = x.shape
    wdt = hg.shape[1]
    tm = min(tm, tokens_per_mod)
    return pl.pallas_call(
        _out_proj_kernel,
        out_shape=jax.ShapeDtypeStruct((t, d), F32),
        grid=(t // tm,),
        in_specs=[pl.BlockSpec((tm, d), lambda i: (i, 0)),
                  pl.BlockSpec((tm, wdt), lambda i: (i, 0)),
                  pl.BlockSpec((tm, wdt), lambda i: (i, 0)),
                  pl.BlockSpec((wdt, d), lambda i: (0, 0)),
                  pl.BlockSpec((wdt, d), lambda i: (1, 0)),
                  pl.BlockSpec((1, 1, d), lambda i: ((i * tm) // tokens_per_mod, 0, 0))],
        out_specs=pl.BlockSpec((tm, d), lambda i: (i, 0)),
        compiler_params=_params("parallel"),
        name="out_proj",
    )(x, hg, hy, w, w, gate)


def kernel(x, c, ctx, c_ctx, ada_w, ada_b, norm_ffn1, ffn1_w1, ffn1_w3, ffn1_w2, norm_mix, w_in, hg_lb_logits, hg_norm, hy_conv_w, hy_conv_b, filt_w1, filt_b1, filt_w2, filt_b2, filt_w3, filt_b3, filt_freq, filt_w_out, hy_d, w_out, norm_ffn2, ffn2_w1, ffn2_w3, ffn2_w2, final_norm):
    bsz, n, d = x.shape
    n_ctx = ctx.shape[1]
    depth = ada_w.shape[0]
    assert depth == 1, "single-layer block"
    width = d // 2
    n_seg = w_in.shape[2] // width
    assert n_seg == 8 and n % GRID_W == 0 and n_ctx % SCAN_CHUNK == 0 and width % HEAD_DIM == 0
    d_ff = ffn1_w1.shape[2]
    tf = 512 if d_ff % 512 == 0 else d_ff
    tm = 512
    l = 0

    rows = -(-(bsz + 1) // 8) * 8
    cs = jnp.concatenate([c, c_ctx[None, :], jnp.zeros((rows - bsz - 1, d), F32)], axis=0)
    mods = _ada(cs, ada_w[l], ada_b[l]).reshape(rows, N_MOD, d)
    mx = [mods[:bsz, i][:, None, :] for i in range(N_MOD)]
    mc = [mods[bsz:bsz + 1, i][:, None, :] for i in range(N_MOD)]

    bf = lambda a: a.astype(BF16)
    w1a, w3a, w2a = bf(ffn1_w1[l]), bf(ffn1_w3[l]), bf(ffn1_w2[l])
    xt = x.reshape(bsz * n, d)
    yt = ctx.reshape(bsz * n_ctx, d)

    xt = _ffn(xt, mx[0], mx[1], mx[2], norm_ffn1[l], w1a, w3a, w2a, final_norm,
              tokens_per_mod=n, final_norm=False, tm=tm, tf=tf)
    yt = _ffn(yt, mc[0], mc[1], mc[2], norm_ffn1[l], w1a, w3a, w2a, final_norm,
              tokens_per_mod=bsz * n_ctx, final_norm=False, tm=tm, tf=tf)

    w_in_b = bf(w_in[l])
    pc = _proj(yt, mc[3], mc[4], norm_mix[l], w_in_b, hy_conv_w[l], hy_conv_b[l], n_seg=3, seg_w=width,
               conv_from=n_seg, tokens_per_mod=bsz * n_ctx, tm=tm)
    px = _proj(xt, mx[3], mx[4], norm_mix[l], w_in_b, hy_conv_w[l], hy_conv_b[l], n_seg=n_seg, seg_w=width,
               conv_from=5, tokens_per_mod=n, tm=2 * tm)
    cols = n_seg * width

    hg = _hgrn(px.reshape(bsz, n, cols), pc.reshape(bsz, n_ctx, 3 * width), hg_lb_logits, hg_norm[l], width=width)

    consts = _dft_constants(n)
    filt = (filt_w1[l], filt_b1[l], filt_w2[l], filt_b2[l], filt_w3[l], filt_b3[l], filt_freq[l], filt_w_out[l])
    spec = _hyena_spectra(n, width, filt, hy_d[l], consts)
    hy = _hyena_mix(px, spec, consts, bsz=bsz, n=n, width=width)

    xt = _out_proj(xt, hg.reshape(bsz * n, width), hy, bf(w_out[l]), mx[5],
                   tokens_per_mod=n, tm=tm)

    out = _ffn(xt, mx[6], mx[7], mx[8], norm_ffn2[l], bf(ffn2_w1[l]), bf(ffn2_w3[l]), bf(ffn2_w2[l]), final_norm,
               tokens_per_mod=n, final_norm=True, tm=tm, tf=tf)
    return out.reshape(bsz, n, d)
```

```python
import functools
import math

import numpy as np
import jax
import jax.numpy as jnp
from jax import lax
from jax.experimental import pallas as pl
from jax.experimental.pallas import tpu as pltpu

F32 = jnp.float32
BF16 = jnp.bfloat16
EPS = 1e-6
N_MOD = 9
HEAD_DIM = 128
GRID_W = 64
SCAN_CHUNK = 64
PREP_CHUNKS = 4
SCAN_BLOCKS = 2
OUTER_ROWS = 8
BF16_ROWS = 16
INNER_BINS = 8
FILT_EMB = 33
DECAY_TARGET = 1e-2
DECAY_FAST_PCT = 0.3
DECAY_SLOW_PCT = 1.5
VMEM_LIMIT_BYTES = 56 * 1024 * 1024
HIGHEST = lax.Precision.HIGHEST


def _params(*semantics):
    return pltpu.CompilerParams(dimension_semantics=semantics, vmem_limit_bytes=VMEM_LIMIT_BYTES)


def _dot(a, b):
    return jnp.dot(a, b, preferred_element_type=F32)


def _silu(a):
    return a * jax.nn.sigmoid(a)


def _ada_kernel(c_ref, w_ref, b_ref, o_ref):
    h = _silu(c_ref[...]).astype(BF16)
    o_ref[...] = _dot(h, w_ref[...].astype(BF16)) + b_ref[...]


def _ada(cs, w, b):
    rows, d = cs.shape
    n = w.shape[1]
    tn = d // 2
    return pl.pallas_call(
        _ada_kernel,
        out_shape=jax.ShapeDtypeStruct((rows, n), F32),
        grid=(n // tn,),
        in_specs=[pl.BlockSpec((rows, d), lambda j: (0, 0)),
                  pl.BlockSpec((d, tn), lambda j: (0, j)),
                  pl.BlockSpec((1, tn), lambda j: (0, j))],
        out_specs=pl.BlockSpec((rows, tn), lambda j: (0, j)),
        compiler_params=_params("arbitrary"),
        name="ada_mod",
    )(cs, w, b.reshape(1, n))


def _norm_mod(x, gain, shift, scale):
    y = x * lax.rsqrt(jnp.mean(x * x, axis=-1, keepdims=True) + EPS) * gain
    return (y * (1.0 + scale) + shift).astype(BF16)


def _ffn_kernel(x_ref, sh_ref, sc_ref, gt_ref, g_ref, w1_ref, w3_ref, w2_ref, fin_ref, o_ref, h_scr, *, final_norm):
    j = pl.program_id(1)

    @pl.when(j == 0)
    def _():
        h_scr[...] = _norm_mod(x_ref[...], g_ref[...], sh_ref[0], sc_ref[0])
        o_ref[...] = jnp.zeros_like(o_ref)

    h = h_scr[...]
    a = _dot(h, w1_ref[...])
    b = _dot(h, w3_ref[...])
    act = (_silu(a) * b).astype(BF16)
    o_ref[...] += _dot(act, w2_ref[...])

    @pl.when(j == pl.num_programs(1) - 1)
    def _():
        out = x_ref[...] + 0.5 * gt_ref[0] * o_ref[...]
        if final_norm:
            out = out * lax.rsqrt(jnp.mean(out * out, axis=-1, keepdims=True) + EPS) * fin_ref[...]
        o_ref[...] = out


def _ffn(x, shift, scale, gate, gain, w1, w3, w2, fin, *, tokens_per_mod, final_norm, tm, tf):
    t, d = x.shape
    f = w1.shape[1]
    tm = min(tm, tokens_per_mod)
    mod_spec = pl.BlockSpec((1, 1, d), lambda i, j: ((i * tm) // tokens_per_mod, 0, 0))
    vec_spec = pl.BlockSpec((1, d), lambda i, j: (0, 0))
    return pl.pallas_call(
        functools.partial(_ffn_kernel, final_norm=final_norm),
        out_shape=jax.ShapeDtypeStruct((t, d), F32),
        grid=(t // tm, f // tf),
        in_specs=[pl.BlockSpec((tm, d), lambda i, j: (i, 0)),
                  mod_spec, mod_spec, mod_spec, vec_spec,
                  pl.BlockSpec((d, tf), lambda i, j: (0, j)),
                  pl.BlockSpec((d, tf), lambda i, j: (0, j)),
                  pl.BlockSpec((tf, d), lambda i, j: (j, 0)),
                  vec_spec],
        out_specs=pl.BlockSpec((tm, d), lambda i, j: (i, 0)),
        scratch_shapes=[pltpu.VMEM((tm, d), BF16)],
        compiler_params=_params("parallel", "arbitrary"),
        name="swiglu_ffn",
    )(x, shift, scale, gate, gain.reshape(1, d), w1, w3, w2, fin.reshape(1, d))


def _proj_kernel(x_ref, sh_ref, sc_ref, g_ref, w_ref, cw_ref, cb_ref, o_ref, h_scr, *, conv_from):
    j = pl.program_id(1)

    @pl.when(j == 0)
    def _():
        h_scr[...] = _norm_mod(x_ref[...], g_ref[...], sh_ref[0], sc_ref[0])

    p = _dot(h_scr[...], w_ref[...])

    @pl.when(j < conv_from)
    def _():
        o_ref[...] = p

    @pl.when(j >= conv_from)
    def _():
        tm, w = p.shape
        rows3 = lambda t: t.reshape(tm // GRID_W, GRID_W, w)
        y = (rows3(pltpu.roll(p, 1, axis=0)) * cw_ref[0] + rows3(p) * cw_ref[1]
             + rows3(pltpu.roll(p, tm - 1, axis=0)) * cw_ref[2] + cb_ref[...])
        o_ref[...] = y.reshape(tm, w)


def _proj(x, shift, scale, gain, w, conv_w, conv_b, *, n_seg, seg_w, conv_from, tokens_per_mod, tm):
    t, d = x.shape
    tm = min(tm, tokens_per_mod)
    mod_spec = pl.BlockSpec((1, 1, d), lambda i, j: ((i * tm) // tokens_per_mod, 0, 0))
    conv_idx = lambda i, j: (0, jnp.maximum(j - conv_from, 0))
    pos = jnp.arange(GRID_W)[None, :, None]
    edge = jnp.stack([pos > 0, pos >= 0, pos < GRID_W - 1], axis=0)[:, 0]
    conv_w = jnp.where(edge, conv_w[:, None, :], 0.0)
    return pl.pallas_call(
        functools.partial(_proj_kernel, conv_from=conv_from),
        out_shape=jax.ShapeDtypeStruct((t, n_seg * seg_w), F32),
        grid=(t // tm, n_seg),
        in_specs=[pl.BlockSpec((tm, d), lambda i, j: (i, 0)),
                  mod_spec, mod_spec,
                  pl.BlockSpec((1, d), lambda i, j: (0, 0)),
                  pl.BlockSpec((d, seg_w), lambda i, j: (0, j)),
                  pl.BlockSpec((3, GRID_W, seg_w), lambda i, j: (0, 0, jnp.maximum(j - conv_from, 0))),
                  pl.BlockSpec((1, seg_w), conv_idx)],
        out_specs=pl.BlockSpec((tm, seg_w), lambda i, j: (i, j)),
        scratch_shapes=[pltpu.VMEM((tm, d), BF16)],
        compiler_params=_params("parallel", "arbitrary"),
        name="in_proj",
    )(x, shift, scale, gain.reshape(1, d), w, conv_w, conv_b.reshape(1, -1))


def _split3(g):
    g1 = g.astype(BF16)
    r1 = g - g1.astype(F32)
    g2 = r1.astype(BF16)
    return g1, g2, (r1 - g2.astype(F32)).astype(BF16)


def _split_dot(tri, g):
    g1, g2, g3 = _split3(g)
    return _dot(tri, g1) + _dot(tri, g2) + _dot(tri, g3)


def _gates(a, lb):
    f = lb + (1.0 - lb) * jax.nn.sigmoid(a)
    return 1.0 - f, jnp.log(f)


def _tn_dot(a, b):
    return lax.dot_general(a, b, (((0,), (0,)), ((), ())), preferred_element_type=F32)


def _nt_dot(a, b):
    return lax.dot_general(a, b, (((1,), (1,)), ((), ())), preferred_element_type=F32)


def _hgrn_kernel(af_ref, ab_ref, v_ref, q_ref, gate_ref, caf_ref, cab_ref, cv_ref, lbl_ref, gain_ref,
                 o_ref, of_scr, ob_scr, sf_scr, sb_scr, qin_scr, qt_scr, kt_scr, kd_scr, dec_scr,
                 *, n_chunks, n_ctx_chunks):
    c = SCAN_CHUNK
    mid = c // 2
    rows_p = PREP_CHUNKS * c
    prow = lax.broadcasted_iota(jnp.int32, (rows_p, rows_p), 0)
    pcol = lax.broadcasted_iota(jnp.int32, (rows_p, rows_p), 1)
    same = (prow // c) == (pcol // c)
    tri_fwd_p = jnp.where(same & (prow >= pcol), 1.0, 0.0).astype(BF16)
    tri_bwd_p = jnp.where(same & (pcol >= prow), 1.0, 0.0).astype(BF16)

    def lower_bound(direction):
        lg = lbl_ref[direction]
        ex = jnp.exp(lg - jnp.max(lg, axis=0, keepdims=True))
        return ex[0:1, :] / jnp.sum(ex, axis=0, keepdims=True)

    lb_f = lower_bound(0)
    lb_b = lower_bound(1)
    q_scale = HEAD_DIM ** -0.5

    n_blocks = n_chunks // PREP_CHUNKS
    n_steps = n_blocks // SCAN_BLOCKS

    def step_blocks(it):
        fwd = [(0, it * SCAN_BLOCKS + u) for u in range(SCAN_BLOCKS)]
        return fwd + [(1, n_blocks - 1 - bi) for _, bi in fwd]

    def prepare(it, s):
        it = jnp.minimum(it, n_steps - 1)
        blocks = step_blocks(it)
        los = [pl.multiple_of(bi * rows_p, rows_p) for _, bi in blocks]
        gates = [_gates((af_ref, ab_ref)[d][pl.ds(lo, rows_p), :], (lb_f, lb_b)[d]) for (d, _), lo in zip(blocks, los)]
        sums = {}
        for d, tri_p in ((0, tri_fwd_p), (1, tri_bwd_p)):
            slots_d = [slot for slot, (dd, _) in enumerate(blocks) if dd == d]
            wide = _dot(tri_p, jnp.concatenate([p for slot in slots_d for p in _split3(gates[slot][1])], axis=1))
            for n_s, slot in enumerate(slots_d):
                parts = [wide[:, (3 * n_s + m) * HEAD_DIM:(3 * n_s + m + 1) * HEAD_DIM] for m in range(3)]
                sums[slot] = parts[0] + parts[1] + parts[2]
        for slot, ((d, _), lo) in enumerate(zip(blocks, los)):
            end_row = (c - 1, 0)[d]
            k = gates[slot][0]
            b = sums[slot].reshape(PREP_CHUNKS, c, HEAD_DIM)
            b_mid = b[:, mid:mid + 1, :]
            b_end = b[:, end_row:end_row + 1, :]
            q_t = (q_ref[pl.ds(lo, rows_p), :] * q_scale).reshape(PREP_CHUNKS, c, HEAD_DIM) * jnp.exp(b - b_mid)
            k_t = k.reshape(PREP_CHUNKS, c, HEAD_DIM) * jnp.exp(b_mid - b)
            flat = lambda t: t.reshape(rows_p, HEAD_DIM).astype(BF16)
            qt_scr[s, slot] = flat(q_t)
            kt_scr[s, slot] = flat(k_t)
            qin_scr[s, slot] = flat(q_t * jnp.exp(b_mid))
            kd_scr[s, slot] = flat(k_t * jnp.exp(b_end - b_mid))
            dec_scr[s, slot] = jnp.broadcast_to(jnp.exp(b_end), (PREP_CHUNKS, 8, HEAD_DIM))

    zeros_blk = jnp.zeros((c, HEAD_DIM), BF16)

    def block_diag(t):
        return jnp.concatenate(
            [jnp.concatenate([t[j * c:(j + 1) * c] if m == j else zeros_blk for m in range(PREP_CHUNKS)], axis=1)
             for j in range(PREP_CHUNKS)], axis=0)

    masks = (same & (prow >= pcol), same & (pcol >= prow))
    st_refs = (sf_scr, sb_scr)
    out_refs = (of_scr, ob_scr)
    orders = (tuple(range(PREP_CHUNKS)), tuple(range(PREP_CHUNKS - 1, -1, -1)))

    def scan(it, s):
        streams = step_blocks(it)
        los = [pl.multiple_of(bi * rows_p, rows_p) for _, bi in streams]
        vbs = [v_ref[pl.ds(lo, rows_p), :].astype(BF16) for lo in los]
        scores = [_nt_dot(qt_scr[s, slot], kt_scr[s, slot]) for slot in range(len(streams))]
        upds = [_tn_dot(vb, block_diag(kd_scr[s, slot])) for slot, vb in enumerate(vbs)]
        outs = [_dot(jnp.where(masks[d], sc, 0.0).astype(BF16), vb) for (d, _), sc, vb in zip(streams, scores, vbs)]
        seen = [[None] * PREP_CHUNKS for _ in streams]
        for d in (0, 1):
            st = st_refs[d][...]
            for slot, (ds_, _) in enumerate(streams):
                if ds_ != d:
                    continue
                for j in orders[d]:
                    seen[slot][j] = st.astype(BF16)
                    st = st * dec_scr[s, slot, j][0:1, :] + upds[slot][:, j * HEAD_DIM:(j + 1) * HEAD_DIM]
            st_refs[d][...] = st
        for slot, ((d, _), lo) in enumerate(zip(streams, los)):
            q_in = qin_scr[s, slot]
            carried = jnp.concatenate([_nt_dot(q_in[j * c:(j + 1) * c], seen[slot][j]) for j in range(PREP_CHUNKS)],
                                      axis=0)
            out_refs[d][pl.ds(lo, rows_p), :] = outs[slot] + carried

    n_ctx_blocks = n_ctx_chunks // PREP_CHUNKS
    ctx_state = [jnp.zeros((HEAD_DIM, HEAD_DIM), F32), jnp.zeros((HEAD_DIM, HEAD_DIM), F32)]
    for step in range(n_ctx_blocks):
        upds, decs = [], []
        for d, bi in ((0, step), (1, n_ctx_blocks - 1 - step)):
            a_ref, lb, tri_p, end_row = ((caf_ref, lb_f, tri_fwd_p, c - 1), (cab_ref, lb_b, tri_bwd_p, 0))[d]
            rows_c = pl.ds(bi * rows_p, rows_p)
            k, g = _gates(a_ref[rows_c, :], lb)
            b = _split_dot(tri_p, g).reshape(PREP_CHUNKS, c, HEAD_DIM)
            b_end = b[:, end_row:end_row + 1, :]
            k_d = (k.reshape(PREP_CHUNKS, c, HEAD_DIM) * jnp.exp(b_end - b)).reshape(rows_p, HEAD_DIM).astype(BF16)
            upds.append(_tn_dot(cv_ref[rows_c, :].astype(BF16), block_diag(k_d)))
            decs.append(jnp.exp(b_end))
        for d in (0, 1):
            for j in orders[d]:
                ctx_state[d] = ctx_state[d] * decs[d][j] + upds[d][:, j * HEAD_DIM:(j + 1) * HEAD_DIM]
    sf_scr[...] = ctx_state[0]
    sb_scr[...] = ctx_state[1]

    prepare(0, 0)

    def body(i, carry):
        scan(2 * i, 0)
        prepare(2 * i + 1, 1)
        scan(2 * i + 1, 1)
        prepare(2 * i + 2, 0)
        return carry

    lax.fori_loop(0, n_steps // 2, body, 0)

    gain = gain_ref[...]
    rows = 4 * c

    def readout(i, carry):
        lo = pl.multiple_of(i * rows, rows)
        o = of_scr[pl.ds(lo, rows), :] + ob_scr[pl.ds(lo, rows), :]
        o = o * lax.rsqrt(jnp.mean(o * o, axis=-1, keepdims=True) + EPS) * gain
        o_ref[pl.ds(lo, rows), :] = o * _silu(gate_ref[pl.ds(lo, rows), :])
        return carry

    lax.fori_loop(0, n_chunks // 4, readout, 0)


def _hgrn(px, pc, lb_logits, gain, *, width):
    bsz, n, _ = px.shape
    n_ctx = pc.shape[1]
    heads = width // HEAD_DIM
    hd = HEAD_DIM

    def seg(s):
        return pl.BlockSpec((None, n, hd), lambda b, h: (b, 0, s * heads + h))

    def cseg(s):
        return pl.BlockSpec((None, n_ctx, hd), lambda b, h: (b, 0, s * heads + h))

    depth1 = lb_logits.shape[1]
    slots = 2 * SCAN_BLOCKS
    assert (n // SCAN_CHUNK) % (2 * SCAN_BLOCKS * PREP_CHUNKS) == 0, "scan steps come in pairs"
    assert (n_ctx // SCAN_CHUNK) % PREP_CHUNKS == 0, "the context prefix is scanned in whole blocks"
    return pl.pallas_call(
        functools.partial(_hgrn_kernel, n_chunks=n // SCAN_CHUNK, n_ctx_chunks=n_ctx // SCAN_CHUNK),
        out_shape=jax.ShapeDtypeStruct((bsz, n, width), F32),
        grid=(bsz, heads),
        in_specs=[seg(0), seg(1), seg(2), seg(3), seg(4), cseg(0), cseg(1), cseg(2),
                  pl.BlockSpec((2, depth1, hd), lambda b, h: (0, 0, h)),
                  pl.BlockSpec((1, hd), lambda b, h: (0, h))],
        out_specs=pl.BlockSpec((None, n, hd), lambda b, h: (b, 0, h)),
        scratch_shapes=[pltpu.VMEM((n, hd), F32), pltpu.VMEM((n, hd), F32),
                        pltpu.VMEM((hd, hd), F32), pltpu.VMEM((hd, hd), F32),
                        *[pltpu.VMEM((2, slots, PREP_CHUNKS * SCAN_CHUNK, hd), BF16) for _ in range(4)],
                        pltpu.VMEM((2, slots, PREP_CHUNKS, 8, hd), F32)],
        compiler_params=_params("parallel", "parallel"),
        name="hgrn_scan",
    )(px, px, px, px, px, pc, pc, pc, lb_logits, gain.reshape(1, width))


def _dft_constants(n):
    n2 = GRID_W
    n1h = n // n2
    n1 = 2 * n1h
    big = 2 * n
    k1 = np.arange(n1h)[:, None]
    m1 = np.arange(n1h)[None, :]
    ang = -2.0 * np.pi * m1 * (k1 + 0.5) / n1
    eye = np.eye(OUTER_ROWS)
    g_fwd = np.kron(np.concatenate([np.cos(ang), np.sin(ang)], axis=0), eye)
    g_inv = np.kron(np.concatenate([np.cos(ang).T, np.sin(ang).T], axis=1), eye)
    m2 = np.arange(n2)[None, :]
    tw_ang = -2.0 * np.pi * m2 * (k1 + 0.5) / big
    tw = np.stack([np.cos(tw_ang), np.sin(tw_ang)], axis=1)
    tw = np.broadcast_to(tw[..., None], (n1h, 2, n2, 128))
    k2 = np.arange(n2)[:, None]
    f_ang = -2.0 * np.pi * k2 * m2 / n2
    fr, fi = np.cos(f_ang), np.sin(f_ang)
    f_fwd = np.block([[fr, -fi], [fi, fr]])
    f_inv = np.block([[fr, fi], [-fi, fr]])
    return g_fwd, g_inv, np.ascontiguousarray(tw), f_fwd, f_inv


def _filt_mlp_kernel(z_ref, w1_ref, b1_ref, w2_ref, b2_ref, w3_ref, b3_ref, fr_ref, o_ref):
    fr = fr_ref[...]
    hp = lambda a, b: jnp.dot(a, b, precision=HIGHEST, preferred_element_type=F32)
    hid = jnp.sin(fr * (hp(z_ref[...], w1_ref[...]) + b1_ref[...]))
    hid = jnp.sin(fr * (hp(hid, w2_ref[...]) + b2_ref[...]))
    o_ref[...] = jnp.sin(fr * (hp(hid, w3_ref[...]) + b3_ref[...]))


def _filt_taps_kernel(hid_ref, wo_ref, dl_ref, h_ref, s_ref, *, n):
    h = jnp.dot(hid_ref[...], wo_ref[...], precision=HIGHEST, preferred_element_type=F32)
    t = lax.broadcasted_iota(jnp.int32, h.shape, 0).astype(F32) * (1.0 / max(n - 1, 1))
    hw = h * jnp.exp(-t * dl_ref[...])
    h_ref[...] = hw
    s_ref[0:1, :] = jnp.sum(jnp.abs(hw), axis=0, keepdims=True)
    s_ref[1:2, :] = jnp.abs(hw[0:1, :])


def _spectrum_kernel(af_ref, ab_ref, tw_ref, f_ref, fl_ref, s_ref, d_ref, o_ref, *, n):
    n2 = GRID_W
    reps = af_ref.shape[-1] // 128
    l1 = s_ref[0:1, :] + s_ref[2:3, :] - s_ref[3:4, :]
    hb0 = s_ref[4:5, :]
    norm = 1.0 / l1
    scale = 2.0 / (2 * n)
    for i in range(af_ref.shape[1]):
        twr = jnp.tile(tw_ref[i, 0], (1, reps))
        twi = jnp.tile(tw_ref[i, 1], (1, reps))

        def inner(a_ref):
            ar, ai = a_ref[0, i], a_ref[1, i]
            stacked = jnp.concatenate([ar * twr - ai * twi, ar * twi + ai * twr], axis=0)
            z = _dot3(f_ref[...], fl_ref[...], stacked)
            return z[:n2], z[n2:]

        zfr, zfi = inner(af_ref)
        zbr, zbi = inner(ab_ref)
        o_ref[0, i] = scale * ((zfr + zbr - hb0) * norm + d_ref[...])
        o_ref[1, i] = scale * ((zfi - zbi) * norm)


def _dot3(a_hi, a_lo, x):
    x_hi = x.astype(BF16)
    x_lo = (x - x_hi.astype(F32)).astype(BF16)
    return _dot(a_hi, x_hi) + _dot(a_hi, x_lo) + _dot(a_lo, x_hi)


def _hi_lo(a):
    hi = np.asarray(a, np.float64).astype(BF16)
    lo = (a - hi.astype(np.float64)).astype(BF16)
    return jnp.asarray(hi), jnp.asarray(lo)


def _slabs(x):
    return [x[..., h * OUTER_ROWS:(h + 1) * OUTER_ROWS, :] for h in range(x.shape[-2] // OUTER_ROWS)]


def _outer_dft_kernel(g_ref, gl_ref, x_ref, o_ref, *, high):
    n1h, _, tl = x_ref.shape
    parts = []
    for xs in _slabs(x_ref[...]):
        xs = xs.reshape(n1h * OUTER_ROWS, tl)
        r = _dot3(g_ref[...], gl_ref[...], xs) if high else _dot(g_ref[...], xs.astype(BF16))
        parts.append(r.reshape(2, n1h, OUTER_ROWS, tl))
    o_ref[...] = jnp.concatenate(parts, axis=2).astype(o_ref.dtype)


def _outer_dft(g, g_lo, x4, seg, *, c, tl, high):
    s, n1h, n2, _ = x4.shape
    tl = min(tl, c)
    per = c // tl
    rows = OUTER_ROWS if high else BF16_ROWS
    g_spec = pl.BlockSpec((2 * n1h * OUTER_ROWS, n1h * OUTER_ROWS), lambda b, j, l: (0, 0))
    return pl.pallas_call(
        functools.partial(_outer_dft_kernel, high=high),
        out_shape=jax.ShapeDtypeStruct((s, 2, n1h, n2, c), F32 if high else BF16),
        grid=(s, n2 // rows, per),
        in_specs=[g_spec, g_spec,
                  pl.BlockSpec((None, n1h, rows, tl), lambda b, j, l: (b, 0, j, seg * per + l))],
        out_specs=pl.BlockSpec((None, 2, n1h, rows, tl), lambda b, j, l: (b, 0, 0, j, l)),
        compiler_params=_params("parallel", "parallel", "parallel"),
        name="outer_dft",
    )(g, g_lo, x4)


def _hyena_spectra(n, width, filt, hy_d, consts):
    w1, b1, w2, b2, w3, b3, freq, w_out = filt
    g_fwd, _, tw, f_fwd, _ = consts
    n1h = n // GRID_W
    order = w1.shape[1]
    pos = np.arange(n, dtype=np.float64)
    t = pos / max(n - 1, 1)
    bands = (FILT_EMB - 1) // 2
    fb = np.linspace(1e-4, bands - 1, bands)
    ang = (2 * math.pi / n) * pos[:, None] * fb[None, :]
    z = np.concatenate([t[:, None], np.cos(ang), -np.sin(ang)], axis=-1)
    emb_pad = 128
    z = jnp.asarray(np.pad(z, ((0, 0), (0, emb_pad - FILT_EMB))), F32)
    w1p = jnp.pad(w1, ((0, emb_pad - FILT_EMB), (0, 0)))
    row = lambda a: a.reshape(1, -1)
    full = lambda shape: pl.BlockSpec(shape, lambda: tuple(0 for _ in shape))
    hid = pl.pallas_call(
        _filt_mlp_kernel,
        out_shape=jax.ShapeDtypeStruct((n, order), F32),
        in_specs=[full((n, emb_pad)), full((emb_pad, order)), full((1, order)), full((order, order)),
                  full((1, order)), full((order, order)), full((1, order)), full((1, order))],
        out_specs=full((n, order)),
        compiler_params=pltpu.CompilerParams(vmem_limit_bytes=VMEM_LIMIT_BYTES),
        name="filter_mlp",
    )(z, w1p, row(b1), w2, row(b2), w3, row(b3), row(freq))

    cols = 4 * width
    deltas = np.abs(np.linspace(math.log(DECAY_TARGET) / DECAY_SLOW_PCT,
                                math.log(DECAY_TARGET) / DECAY_FAST_PCT, width))
    deltas4 = jnp.asarray(np.tile(deltas, 4)[None, :], F32)
    tc = min(512, cols)
    taps, sums = pl.pallas_call(
        functools.partial(_filt_taps_kernel, n=n),
        out_shape=(jax.ShapeDtypeStruct((n, cols), F32), jax.ShapeDtypeStruct((2, cols), F32)),
        grid=(cols // tc,),
        in_specs=[pl.BlockSpec((n, order), lambda j: (0, 0)),
                  pl.BlockSpec((order, tc), lambda j: (0, j)),
                  pl.BlockSpec((1, tc), lambda j: (0, j))],
        out_specs=(pl.BlockSpec((n, tc), lambda j: (0, j)), pl.BlockSpec((2, tc), lambda j: (0, j))),
        compiler_params=_params("parallel"),
        name="filter_taps",
    )(hid, w_out, deltas4)

    g_hi, g_lo = _hi_lo(g_fwd)
    a = _outer_dft(g_hi, g_lo, taps.reshape(1, n1h, GRID_W, cols), 0, c=cols, tl=1024, high=True)[0]
    s4 = sums.reshape(2, 2, 2, width)
    bwd0 = taps[0].reshape(2, 2, width)[:, 1]
    stats = jnp.stack([s4[0, :, 0], s4[1, :, 0], s4[0, :, 1], s4[1, :, 1], bwd0], axis=1)

    kb = min(INNER_BINS, n1h)

    def tap_spec(side):
        return pl.BlockSpec((2, kb, GRID_W, width), lambda k1, f: (0, k1, 0, 2 * f + side))

    f_hi, f_lo = _hi_lo(f_fwd)
    f_spec = pl.BlockSpec((2 * GRID_W, 2 * GRID_W), lambda k1, f: (0, 0))
    return pl.pallas_call(
        functools.partial(_spectrum_kernel, n=n),
        out_shape=jax.ShapeDtypeStruct((2, 2, n1h, GRID_W, width), F32),
        grid=(n1h // kb, 2),
        in_specs=[tap_spec(0), tap_spec(1),
                  pl.BlockSpec((kb, 2, GRID_W, 128), lambda k1, f: (k1, 0, 0, 0)),
                  f_spec, f_spec,
                  pl.BlockSpec((None, 5, width), lambda k1, f: (f, 0, 0)),
                  pl.BlockSpec((None, 1, width), lambda k1, f: (f, 0, 0))],
        out_specs=pl.BlockSpec((None, 2, kb, GRID_W, width), lambda k1, f: (f, 0, k1, 0, 0)),
        compiler_params=_params("parallel", "parallel"),
        name="filter_spectrum",
    )(a, a, jnp.asarray(tw, F32), f_hi, f_lo, stats, hy_d.reshape(2, 1, width))


def _inner_conv_kernel(a_ref, tw_ref, f_ref, fi_ref, h_ref, o_ref):
    n2 = GRID_W
    reps = a_ref.shape[-1] // 128
    for i in range(a_ref.shape[1]):
        twr = jnp.tile(tw_ref[i, 0], (1, reps))
        twi = jnp.tile(tw_ref[i, 1], (1, reps))
        ar, ai = a_ref[0, i].astype(F32), a_ref[1, i].astype(F32)
        stacked = jnp.concatenate([ar * twr - ai * twi, ar * twi + ai * twr], axis=0).astype(BF16)
        z = _dot(f_ref[...], stacked)
        zr, zi = z[:n2], z[n2:]
        hr, hi = h_ref[0, i], h_ref[1, i]
        stacked = jnp.concatenate([zr * hr - zi * hi, zr * hi + zi * hr], axis=0).astype(BF16)
        w = _dot(fi_ref[...], stacked)
        wr, wi = w[:n2], w[n2:]
        o_ref[0, i] = (wr * twr + wi * twi).astype(BF16)
        o_ref[1, i] = (wi * twr - wr * twi).astype(BF16)


def _inner_conv(a, tw, f_fwd, f_inv, spec):
    bsz, _, n1h, n2, c = a.shape
    kb = min(INNER_BINS, n1h)
    return pl.pallas_call(
        _inner_conv_kernel,
        out_shape=jax.ShapeDtypeStruct(a.shape, BF16),
        grid=(n1h // kb, bsz),
        in_specs=[pl.BlockSpec((None, 2, kb, n2, c), lambda k1, b: (b, 0, k1, 0, 0)),
                  pl.BlockSpec((kb, 2, n2, 128), lambda k1, b: (k1, 0, 0, 0)),
                  pl.BlockSpec((2 * n2, 2 * n2), lambda k1, b: (0, 0)),
                  pl.BlockSpec((2 * n2, 2 * n2), lambda k1, b: (0, 0)),
                  pl.BlockSpec((2, kb, n2, c), lambda k1, b: (0, k1, 0, 0))],
        out_specs=pl.BlockSpec((None, 2, kb, n2, c), lambda k1, b: (b, 0, k1, 0, 0)),
        compiler_params=_params("parallel", "arbitrary"),
        name="inner_conv",
    )(a, tw, f_fwd, f_inv, spec)


def _outer_idft_gate_kernel(gi_ref, g_ref, b_ref, x_ref, o_ref, *, again):
    n1h, _, tl = x_ref.shape
    parts = []
    for bs, xs in zip(_slabs(b_ref[...].astype(F32)), _slabs(x_ref[...])):
        stacked = bs.reshape(2 * n1h * OUTER_ROWS, tl).astype(BF16)
        y = _dot(gi_ref[...], stacked) * xs.reshape(n1h * OUTER_ROWS, tl)
        if again:
            parts.append(_dot(g_ref[...], y.astype(BF16)).reshape(2, n1h, OUTER_ROWS, tl))
        else:
            parts.append(y.reshape(n1h, OUTER_ROWS, tl))
    o_ref[...] = jnp.concatenate(parts, axis=-2).astype(o_ref.dtype)


def _outer_idft_gate(g_inv, g_fwd, b, px4, seg, *, again, tl):
    bsz, _, n1h, n2, c = b.shape
    tl = min(tl, c)
    per = c // tl
    rows = BF16_ROWS
    if again:
        out_shape = jax.ShapeDtypeStruct((bsz, 2, n1h, n2, c), BF16)
        out_spec = pl.BlockSpec((None, 2, n1h, rows, tl), lambda bb, j, l: (bb, 0, 0, j, l))
    else:
        out_shape = jax.ShapeDtypeStruct((bsz, n1h, n2, c), F32)
        out_spec = pl.BlockSpec((None, n1h, rows, tl), lambda bb, j, l: (bb, 0, j, l))
    return pl.pallas_call(
        functools.partial(_outer_idft_gate_kernel, again=again),
        out_shape=out_shape,
        grid=(bsz, n2 // rows, per),
        in_specs=[pl.BlockSpec((n1h * OUTER_ROWS, 2 * n1h * OUTER_ROWS), lambda bb, j, l: (0, 0)),
                  pl.BlockSpec((2 * n1h * OUTER_ROWS, n1h * OUTER_ROWS), lambda bb, j, l: (0, 0)),
                  pl.BlockSpec((None, 2, n1h, rows, tl), lambda bb, j, l: (bb, 0, 0, j, l)),
                  pl.BlockSpec((None, n1h, rows, tl), lambda bb, j, l: (bb, 0, j, seg * per + l))],
        out_specs=out_spec,
        compiler_params=_params("parallel", "parallel", "parallel"),
        name="outer_idft_gate",
    )(g_inv, g_fwd, b, px4)


def _hyena_mix(px, spec, consts, *, bsz, n, width):
    g_fwd, g_inv, tw, f_fwd, f_inv = consts
    g_fwd_b, g_inv_b = jnp.asarray(g_fwd, BF16), jnp.asarray(g_inv, BF16)
    f_fwd_b, f_inv_b = jnp.asarray(f_fwd, BF16), jnp.asarray(f_inv, BF16)
    tw_f = jnp.asarray(tw, F32)
    n1h = n // GRID_W
    px4 = px.reshape(bsz, n1h, GRID_W, -1)
    tl = width
    a = _outer_dft(g_fwd_b, g_fwd_b, px4, 5, c=width, tl=tl, high=False)
    a = _inner_conv(a, tw_f, f_fwd_b, f_inv_b, spec[0])
    a = _outer_idft_gate(g_inv_b, g_fwd_b, a, px4, 6, again=True, tl=tl)
    a = _inner_conv(a, tw_f, f_fwd_b, f_inv_b, spec[1])
    hy = _outer_idft_gate(g_inv_b, g_fwd_b, a, px4, 7, again=False, tl=tl)
    return hy.reshape(bsz * n, width)


def _out_proj_kernel(x_ref, hg_ref, hy_ref, wa_ref, wb_ref, gt_ref, o_ref):
    mix = _dot(hg_ref[...].astype(BF16), wa_ref[...]) + _dot(hy_ref[...].astype(BF16), wb_ref[...])
    o_ref[...] = x_ref[...] + gt_ref[0] * mix


def _out_proj(x, hg, hy, w, gate, *, tokens_per_mod, tm):
    t, d = x.shape
    wdt = hg.shape[1]
    tm = min(tm, tokens_per_mod)
    return pl.pallas_call(
        _out_proj_kernel,
        out_shape=jax.ShapeDtypeStruct((t, d), F32),
        grid=(t // tm,),
        in_specs=[pl.BlockSpec((tm, d), lambda i: (i, 0)),
                  pl.BlockSpec((tm, wdt), lambda i: (i, 0)),
                  pl.BlockSpec((tm, wdt), lambda i: (i, 0)),
                  pl.BlockSpec((wdt, d), lambda i: (0, 0)),
                  pl.BlockSpec((wdt, d), lambda i: (1, 0)),
                  pl.BlockSpec((1, 1, d), lambda i: ((i * tm) // tokens_per_mod, 0, 0))],
        out_specs=pl.BlockSpec((tm, d), lambda i: (i, 0)),
        compiler_params=_params("parallel"),
        name="out_proj",
    )(x, hg, hy, w, w, gate)


def kernel(x, c, ctx, c_ctx, ada_w, ada_b, norm_ffn1, ffn1_w1, ffn1_w3, ffn1_w2, norm_mix, w_in, hg_lb_logits, hg_norm, hy_conv_w, hy_conv_b, filt_w1, filt_b1, filt_w2, filt_b2, filt_w3, filt_b3, filt_freq, filt_w_out, hy_d, w_out, norm_ffn2, ffn2_w1, ffn2_w3, ffn2_w2, final_norm):
    bsz, n, d = x.shape
    n_ctx = ctx.shape[1]
    depth = ada_w.shape[0]
    assert depth == 1, "single-layer block"
    width = d // 2
    n_seg = w_in.shape[2] // width
    assert n_seg == 8 and n % GRID_W == 0 and n_ctx % SCAN_CHUNK == 0 and width % HEAD_DIM == 0
    d_ff = ffn1_w1.shape[2]
    tf = 512 if d_ff % 512 == 0 else d_ff
    tm = 512
    tm_wide = 1024
    l = 0

    rows = -(-(bsz + 1) // 8) * 8
    cs = jnp.concatenate([c, c_ctx[None, :], jnp.zeros((rows - bsz - 1, d), F32)], axis=0)
    mods = _ada(cs, ada_w[l], ada_b[l]).reshape(rows, N_MOD, d)
    mx = [mods[:bsz, i][:, None, :] for i in range(N_MOD)]
    mc = [mods[bsz:bsz + 1, i][:, None, :] for i in range(N_MOD)]

    bf = lambda a: a.astype(BF16)
    w1a, w3a, w2a = bf(ffn1_w1[l]), bf(ffn1_w3[l]), bf(ffn1_w2[l])
    xt = x.reshape(bsz * n, d)
    yt = ctx.reshape(bsz * n_ctx, d)

    xt = _ffn(xt, mx[0], mx[1], mx[2], norm_ffn1[l], w1a, w3a, w2a, final_norm,
              tokens_per_mod=n, final_norm=False, tm=tm, tf=tf)
    yt = _ffn(yt, mc[0], mc[1], mc[2], norm_ffn1[l], w1a, w3a, w2a, final_norm,
              tokens_per_mod=bsz * n_ctx, final_norm=False, tm=tm, tf=tf)

    w_in_b = bf(w_in[l])
    pc = _proj(yt, mc[3], mc[4], norm_mix[l], w_in_b, hy_conv_w[l], hy_conv_b[l], n_seg=3, seg_w=width,
               conv_from=n_seg, tokens_per_mod=bsz * n_ctx, tm=tm)
    px = _proj(xt, mx[3], mx[4], norm_mix[l], w_in_b, hy_conv_w[l], hy_conv_b[l], n_seg=n_seg, seg_w=width,
               conv_from=5, tokens_per_mod=n, tm=tm_wide)
    cols = n_seg * width

    hg = _hgrn(px.reshape(bsz, n, cols), pc.reshape(bsz, n_ctx, 3 * width), hg_lb_logits, hg_norm[l], width=width)

    consts = _dft_constants(n)
    filt = (filt_w1[l], filt_b1[l], filt_w2[l], filt_b2[l], filt_w3[l], filt_b3[l], filt_freq[l], filt_w_out[l])
    spec = _hyena_spectra(n, width, filt, hy_d[l], consts)
    hy = _hyena_mix(px, spec, consts, bsz=bsz, n=n, width=width)

    xt = _out_proj(xt, hg.reshape(bsz * n, width), hy, bf(w_out[l]), mx[5],
                   tokens_per_mod=n, tm=tm)

    out = _ffn(xt, mx[6], mx[7], mx[8], norm_ffn2[l], bf(ffn2_w1[l]), bf(ffn2_w3[l]), bf(ffn2_w2[l]), final_norm,
               tokens_per_mod=n, final_norm=True, tm=tm, tf=tf)
    return out.reshape(bsz, n, d)
```

```python
import functools
import math

import numpy as np
import jax
import jax.numpy as jnp
from jax import lax
from jax.experimental import pallas as pl
from jax.experimental.pallas import tpu as pltpu

F32 = jnp.float32
BF16 = jnp.bfloat16
EPS = 1e-6
N_MOD = 9
HEAD_DIM = 128
GRID_W = 64
SCAN_CHUNK = 64
PREP_CHUNKS = 4
SCAN_BLOCKS = 2
OUTER_ROWS = 8
BF16_ROWS = 16
INNER_BINS = 8
FILT_EMB = 33
DECAY_TARGET = 1e-2
DECAY_FAST_PCT = 0.3
DECAY_SLOW_PCT = 1.5
VMEM_LIMIT_BYTES = 56 * 1024 * 1024
HIGHEST = lax.Precision.HIGHEST


def _params(*semantics):
    return pltpu.CompilerParams(dimension_semantics=semantics, vmem_limit_bytes=VMEM_LIMIT_BYTES)


def _dot(a, b):
    return jnp.dot(a, b, preferred_element_type=F32)


def _silu(a):
    return a * jax.nn.sigmoid(a)


def _ada_kernel(c_ref, w_ref, b_ref, o_ref):
    h = _silu(c_ref[...]).astype(BF16)
    o_ref[...] = _dot(h, w_ref[...].astype(BF16)) + b_ref[...]


def _ada(cs, w, b):
    rows, d = cs.shape
    n = w.shape[1]
    tn = d // 2
    return pl.pallas_call(
        _ada_kernel,
        out_shape=jax.ShapeDtypeStruct((rows, n), F32),
        grid=(n // tn,),
        in_specs=[pl.BlockSpec((rows, d), lambda j: (0, 0)),
                  pl.BlockSpec((d, tn), lambda j: (0, j)),
                  pl.BlockSpec((1, tn), lambda j: (0, j))],
        out_specs=pl.BlockSpec((rows, tn), lambda j: (0, j)),
        compiler_params=_params("arbitrary"),
        name="ada_mod",
    )(cs, w, b.reshape(1, n))


def _norm_mod(x, gain, shift, scale):
    y = x * lax.rsqrt(jnp.mean(x * x, axis=-1, keepdims=True) + EPS) * gain
    return (y * (1.0 + scale) + shift).astype(BF16)


def _ffn_kernel(x_ref, sh_ref, sc_ref, gt_ref, g_ref, w1_ref, w3_ref, w2_ref, fin_ref, o_ref, h_scr, *, final_norm):
    j = pl.program_id(1)

    @pl.when(j == 0)
    def _():
        h_scr[...] = _norm_mod(x_ref[...], g_ref[...], sh_ref[0], sc_ref[0])
        o_ref[...] = jnp.zeros_like(o_ref)

    h = h_scr[...]
    a = _dot(h, w1_ref[...])
    b = _dot(h, w3_ref[...])
    act = (_silu(a) * b).astype(BF16)
    o_ref[...] += _dot(act, w2_ref[...])

    @pl.when(j == pl.num_programs(1) - 1)
    def _():
        out = x_ref[...] + 0.5 * gt_ref[0] * o_ref[...]
        if final_norm:
            out = out * lax.rsqrt(jnp.mean(out * out, axis=-1, keepdims=True) + EPS) * fin_ref[...]
        o_ref[...] = out


def _ffn(x, shift, scale, gate, gain, w1, w3, w2, fin, *, tokens_per_mod, final_norm, tm, tf):
    t, d = x.shape
    f = w1.shape[1]
    tm = min(tm, tokens_per_mod)
    mod_spec = pl.BlockSpec((1, 1, d), lambda i, j: ((i * tm) // tokens_per_mod, 0, 0))
    vec_spec = pl.BlockSpec((1, d), lambda i, j: (0, 0))
    return pl.pallas_call(
        functools.partial(_ffn_kernel, final_norm=final_norm),
        out_shape=jax.ShapeDtypeStruct((t, d), F32),
        grid=(t // tm, f // tf),
        in_specs=[pl.BlockSpec((tm, d), lambda i, j: (i, 0)),
                  mod_spec, mod_spec, mod_spec, vec_spec,
                  pl.BlockSpec((d, tf), lambda i, j: (0, j)),
                  pl.BlockSpec((d, tf), lambda i, j: (0, j)),
                  pl.BlockSpec((tf, d), lambda i, j: (j, 0)),
                  vec_spec],
        out_specs=pl.BlockSpec((tm, d), lambda i, j: (i, 0)),
        scratch_shapes=[pltpu.VMEM((tm, d), BF16)],
        compiler_params=_params("parallel", "arbitrary"),
        name="swiglu_ffn",
    )(x, shift, scale, gate, gain.reshape(1, d), w1, w3, w2, fin.reshape(1, d))


def _proj_kernel(x_ref, sh_ref, sc_ref, g_ref, w_ref, cw_ref, cb_ref, o_ref, h_scr, *, conv_from):
    j = pl.program_id(1)

    @pl.when(j == 0)
    def _():
        h_scr[...] = _norm_mod(x_ref[...], g_ref[...], sh_ref[0], sc_ref[0])

    p = _dot(h_scr[...], w_ref[...])

    @pl.when(j < conv_from)
    def _():
        o_ref[...] = p

    @pl.when(j >= conv_from)
    def _():
        tm, w = p.shape
        rows3 = lambda t: t.reshape(tm // GRID_W, GRID_W, w)
        y = (rows3(pltpu.roll(p, 1, axis=0)) * cw_ref[0] + rows3(p) * cw_ref[1]
             + rows3(pltpu.roll(p, tm - 1, axis=0)) * cw_ref[2] + cb_ref[...])
        o_ref[...] = y.reshape(tm, w)


def _proj(x, shift, scale, gain, w, conv_w, conv_b, *, n_seg, seg_w, conv_from, tokens_per_mod, tm):
    t, d = x.shape
    tm = min(tm, tokens_per_mod)
    mod_spec = pl.BlockSpec((1, 1, d), lambda i, j: ((i * tm) // tokens_per_mod, 0, 0))
    conv_idx = lambda i, j: (0, jnp.maximum(j - conv_from, 0))
    pos = jnp.arange(GRID_W)[None, :, None]
    edge = jnp.stack([pos > 0, pos >= 0, pos < GRID_W - 1], axis=0)[:, 0]
    conv_w = jnp.where(edge, conv_w[:, None, :], 0.0)
    return pl.pallas_call(
        functools.partial(_proj_kernel, conv_from=conv_from),
        out_shape=jax.ShapeDtypeStruct((t, n_seg * seg_w), F32),
        grid=(t // tm, n_seg),
        in_specs=[pl.BlockSpec((tm, d), lambda i, j: (i, 0)),
                  mod_spec, mod_spec,
                  pl.BlockSpec((1, d), lambda i, j: (0, 0)),
                  pl.BlockSpec((d, seg_w), lambda i, j: (0, j)),
                  pl.BlockSpec((3, GRID_W, seg_w), lambda i, j: (0, 0, jnp.maximum(j - conv_from, 0))),
                  pl.BlockSpec((1, seg_w), conv_idx)],
        out_specs=pl.BlockSpec((tm, seg_w), lambda i, j: (i, j)),
        scratch_shapes=[pltpu.VMEM((tm, d), BF16)],
        compiler_params=_params("parallel", "arbitrary"),
        name="in_proj",
    )(x, shift, scale, gain.reshape(1, d), w, conv_w, conv_b.reshape(1, -1))


def _split3(g):
    g1 = g.astype(BF16)
    r1 = g - g1.astype(F32)
    g2 = r1.astype(BF16)
    return g1, g2, (r1 - g2.astype(F32)).astype(BF16)


def _split_dot(tri, g):
    g1, g2, g3 = _split3(g)
    return _dot(tri, g1) + _dot(tri, g2) + _dot(tri, g3)


def _gates(a, lb):
    f = lb + (1.0 - lb) * jax.nn.sigmoid(a)
    return 1.0 - f, jnp.log(f)


def _tn_dot(a, b):
    return lax.dot_general(a, b, (((0,), (0,)), ((), ())), preferred_element_type=F32)


def _nt_dot(a, b):
    return lax.dot_general(a, b, (((1,), (1,)), ((), ())), preferred_element_type=F32)


def _hgrn_kernel(af_ref, ab_ref, v_ref, q_ref, gate_ref, caf_ref, cab_ref, cv_ref, lbl_ref, gain_ref,
                 o_ref, of_scr, ob_scr, sf_scr, sb_scr, qin_scr, qt_scr, kt_scr, kd_scr, dec_scr,
                 *, n_chunks, n_ctx_chunks):
    c = SCAN_CHUNK
    mid = c // 2
    rows_p = PREP_CHUNKS * c
    prow = lax.broadcasted_iota(jnp.int32, (rows_p, rows_p), 0)
    pcol = lax.broadcasted_iota(jnp.int32, (rows_p, rows_p), 1)
    same = (prow // c) == (pcol // c)
    tri_fwd_p = jnp.where(same & (prow >= pcol), 1.0, 0.0).astype(BF16)
    tri_bwd_p = jnp.where(same & (pcol >= prow), 1.0, 0.0).astype(BF16)

    def lower_bound(direction):
        lg = lbl_ref[direction]
        ex = jnp.exp(lg - jnp.max(lg, axis=0, keepdims=True))
        return ex[0:1, :] / jnp.sum(ex, axis=0, keepdims=True)

    lb_f = lower_bound(0)
    lb_b = lower_bound(1)
    q_scale = HEAD_DIM ** -0.5

    n_blocks = n_chunks // PREP_CHUNKS
    n_steps = n_blocks // SCAN_BLOCKS

    def step_blocks(it):
        fwd = [(0, it * SCAN_BLOCKS + u) for u in range(SCAN_BLOCKS)]
        return fwd + [(1, n_blocks - 1 - bi) for _, bi in fwd]

    def prepare(it, s):
        it = jnp.minimum(it, n_steps - 1)
        blocks = step_blocks(it)
        los = [pl.multiple_of(bi * rows_p, rows_p) for _, bi in blocks]
        gates = [_gates((af_ref, ab_ref)[d][pl.ds(lo, rows_p), :], (lb_f, lb_b)[d]) for (d, _), lo in zip(blocks, los)]
        sums = {}
        for d, tri_p in ((0, tri_fwd_p), (1, tri_bwd_p)):
            slots_d = [slot for slot, (dd, _) in enumerate(blocks) if dd == d]
            wide = _dot(tri_p, jnp.concatenate([p for slot in slots_d for p in _split3(gates[slot][1])], axis=1))
            for n_s, slot in enumerate(slots_d):
                parts = [wide[:, (3 * n_s + m) * HEAD_DIM:(3 * n_s + m + 1) * HEAD_DIM] for m in range(3)]
                sums[slot] = parts[0] + parts[1] + parts[2]
        for slot, ((d, _), lo) in enumerate(zip(blocks, los)):
            end_row = (c - 1, 0)[d]
            k = gates[slot][0]
            b = sums[slot].reshape(PREP_CHUNKS, c, HEAD_DIM)
            b_mid = b[:, mid:mid + 1, :]
            b_end = b[:, end_row:end_row + 1, :]
            q_t = (q_ref[pl.ds(lo, rows_p), :] * q_scale).reshape(PREP_CHUNKS, c, HEAD_DIM) * jnp.exp(b - b_mid)
            k_t = k.reshape(PREP_CHUNKS, c, HEAD_DIM) * jnp.exp(b_mid - b)
            flat = lambda t: t.reshape(rows_p, HEAD_DIM).astype(BF16)
            qt_scr[s, slot] = flat(q_t)
            kt_scr[s, slot] = flat(k_t)
            qin_scr[s, slot] = flat(q_t * jnp.exp(b_mid))
            kd_scr[s, slot] = flat(k_t * jnp.exp(b_end - b_mid))
            dec_scr[s, slot] = jnp.broadcast_to(jnp.exp(b_end), (PREP_CHUNKS, 8, HEAD_DIM))

    zeros_blk = jnp.zeros((c, HEAD_DIM), BF16)

    def block_diag(t):
        return jnp.concatenate(
            [jnp.concatenate([t[j * c:(j + 1) * c] if m == j else zeros_blk for m in range(PREP_CHUNKS)], axis=1)
             for j in range(PREP_CHUNKS)], axis=0)

    masks = (same & (prow >= pcol), same & (pcol >= prow))
    st_refs = (sf_scr, sb_scr)
    out_refs = (of_scr, ob_scr)
    orders = (tuple(range(PREP_CHUNKS)), tuple(range(PREP_CHUNKS - 1, -1, -1)))

    def scan(it, s):
        streams = step_blocks(it)
        los = [pl.multiple_of(bi * rows_p, rows_p) for _, bi in streams]
        vbs = [v_ref[pl.ds(lo, rows_p), :].astype(BF16) for lo in los]
        scores = [_nt_dot(qt_scr[s, slot], kt_scr[s, slot]) for slot in range(len(streams))]
        upds = [_tn_dot(vb, block_diag(kd_scr[s, slot])) for slot, vb in enumerate(vbs)]
        outs = [_dot(jnp.where(masks[d], sc, 0.0).astype(BF16), vb) for (d, _), sc, vb in zip(streams, scores, vbs)]
        seen = [[None] * PREP_CHUNKS for _ in streams]
        for d in (0, 1):
            st = st_refs[d][...]
            for slot, (ds_, _) in enumerate(streams):
                if ds_ != d:
                    continue
                for j in orders[d]:
                    seen[slot][j] = st.astype(BF16)
                    st = st * dec_scr[s, slot, j][0:1, :] + upds[slot][:, j * HEAD_DIM:(j + 1) * HEAD_DIM]
            st_refs[d][...] = st
        for slot, ((d, _), lo) in enumerate(zip(streams, los)):
            q_in = qin_scr[s, slot]
            carried = jnp.concatenate([_nt_dot(q_in[j * c:(j + 1) * c], seen[slot][j]) for j in range(PREP_CHUNKS)],
                                      axis=0)
            out_refs[d][pl.ds(lo, rows_p), :] = outs[slot] + carried

    n_ctx_blocks = n_ctx_chunks // PREP_CHUNKS
    ctx_state = [jnp.zeros((HEAD_DIM, HEAD_DIM), F32), jnp.zeros((HEAD_DIM, HEAD_DIM), F32)]
    for step in range(n_ctx_blocks):
        upds, decs = [], []
        for d, bi in ((0, step), (1, n_ctx_blocks - 1 - step)):
            a_ref, lb, tri_p, end_row = ((caf_ref, lb_f, tri_fwd_p, c - 1), (cab_ref, lb_b, tri_bwd_p, 0))[d]
            rows_c = pl.ds(bi * rows_p, rows_p)
            k, g = _gates(a_ref[rows_c, :], lb)
            b = _split_dot(tri_p, g).reshape(PREP_CHUNKS, c, HEAD_DIM)
            b_end = b[:, end_row:end_row + 1, :]
            k_d = (k.reshape(PREP_CHUNKS, c, HEAD_DIM) * jnp.exp(b_end - b)).reshape(rows_p, HEAD_DIM).astype(BF16)
            upds.append(_tn_dot(cv_ref[rows_c, :].astype(BF16), block_diag(k_d)))
            decs.append(jnp.exp(b_end))
        for d in (0, 1):
            for j in orders[d]:
                ctx_state[d] = ctx_state[d] * decs[d][j] + upds[d][:, j * HEAD_DIM:(j + 1) * HEAD_DIM]
    sf_scr[...] = ctx_state[0]
    sb_scr[...] = ctx_state[1]

    prepare(0, 0)

    def body(i, carry):
        scan(2 * i, 0)
        prepare(2 * i + 1, 1)
        scan(2 * i + 1, 1)
        prepare(2 * i + 2, 0)
        return carry

    lax.fori_loop(0, n_steps // 2, body, 0)

    gain = gain_ref[...]
    rows = 4 * c

    def readout(i, carry):
        lo = pl.multiple_of(i * rows, rows)
        o = of_scr[pl.ds(lo, rows), :] + ob_scr[pl.ds(lo, rows), :]
        o = o * lax.rsqrt(jnp.mean(o * o, axis=-1, keepdims=True) + EPS) * gain
        o_ref[pl.ds(lo, rows), :] = o * _silu(gate_ref[pl.ds(lo, rows), :])
        return carry

    lax.fori_loop(0, n_chunks // 4, readout, 0)


def _hgrn(px, pc, lb_logits, gain, *, width):
    bsz, n, _ = px.shape
    n_ctx = pc.shape[1]
    heads = width // HEAD_DIM
    hd = HEAD_DIM

    def seg(s):
        return pl.BlockSpec((None, n, hd), lambda b, h: (b, 0, s * heads + h))

    def cseg(s):
        return pl.BlockSpec((None, n_ctx, hd), lambda b, h: (b, 0, s * heads + h))

    depth1 = lb_logits.shape[1]
    slots = 2 * SCAN_BLOCKS
    assert (n // SCAN_CHUNK) % (2 * SCAN_BLOCKS * PREP_CHUNKS) == 0, "scan steps come in pairs"
    assert (n_ctx // SCAN_CHUNK) % PREP_CHUNKS == 0, "the context prefix is scanned in whole blocks"
    return pl.pallas_call(
        functools.partial(_hgrn_kernel, n_chunks=n // SCAN_CHUNK, n_ctx_chunks=n_ctx // SCAN_CHUNK),
        out_shape=jax.ShapeDtypeStruct((bsz, n, width), F32),
        grid=(bsz, heads),
        in_specs=[seg(0), seg(1), seg(2), seg(3), seg(4), cseg(0), cseg(1), cseg(2),
                  pl.BlockSpec((2, depth1, hd), lambda b, h: (0, 0, h)),
                  pl.BlockSpec((1, hd), lambda b, h: (0, h))],
        out_specs=pl.BlockSpec((None, n, hd), lambda b, h: (b, 0, h)),
        scratch_shapes=[pltpu.VMEM((n, hd), F32), pltpu.VMEM((n, hd), F32),
                        pltpu.VMEM((hd, hd), F32), pltpu.VMEM((hd, hd), F32),
                        *[pltpu.VMEM((2, slots, PREP_CHUNKS * SCAN_CHUNK, hd), BF16) for _ in range(4)],
                        pltpu.VMEM((2, slots, PREP_CHUNKS, 8, hd), F32)],
        compiler_params=_params("parallel", "parallel"),
        name="hgrn_scan",
    )(px, px, px, px, px, pc, pc, pc, lb_logits, gain.reshape(1, width))


def _dft_constants(n):
    n2 = GRID_W
    n1h = n // n2
    m2r = np.arange(n2)[None, :]
    k2 = np.arange(n2)[:, None]
    f_ang = -2.0 * np.pi * k2 * m2r / n2
    fr, fi = np.cos(f_ang), np.sin(f_ang)
    f_fwd = np.block([[fr, -fi], [fi, fr]])
    f_inv = np.block([[fr, fi], [-fi, fr]])
    k1 = np.arange(n1h)[:, None, None]
    m1 = np.arange(n1h)[None, :, None]
    m2 = np.arange(n2)[None, None, :]
    theta = -2.0 * np.pi * (k1 + 0.5) * (m1 / (2 * n1h) + m2 / (2 * n))
    parts = np.stack([np.cos(theta), np.sin(theta)], axis=0)
    parts = parts.reshape(2, n1h, n1h, n2 // OUTER_ROWS, OUTER_ROWS)
    eye = np.eye(OUTER_ROWS)
    rows = n1h * OUTER_ROWS
    fwd = np.einsum("pknsj,jm->spkjnm", parts, eye).reshape(n2 // OUTER_ROWS, 2 * rows, rows)
    inv = np.einsum("pknsj,jm->snjpkm", parts, eye).reshape(n2 // OUTER_ROWS, rows, 2 * rows)
    return fwd, inv, f_fwd, f_inv


def _filt_mlp_kernel(z_ref, w1_ref, b1_ref, w2_ref, b2_ref, w3_ref, b3_ref, fr_ref, o_ref):
    fr = fr_ref[...]
    hp = lambda a, b: jnp.dot(a, b, precision=HIGHEST, preferred_element_type=F32)
    hid = jnp.sin(fr * (hp(z_ref[...], w1_ref[...]) + b1_ref[...]))
    hid = jnp.sin(fr * (hp(hid, w2_ref[...]) + b2_ref[...]))
    o_ref[...] = jnp.sin(fr * (hp(hid, w3_ref[...]) + b3_ref[...]))


def _filt_taps_kernel(hid_ref, wo_ref, dl_ref, h_ref, s_ref, *, n):
    h = jnp.dot(hid_ref[...], wo_ref[...], precision=HIGHEST, preferred_element_type=F32)
    t = lax.broadcasted_iota(jnp.int32, h.shape, 0).astype(F32) * (1.0 / max(n - 1, 1))
    hw = h * jnp.exp(-t * dl_ref[...])
    h_ref[...] = hw
    s_ref[0:1, :] = jnp.sum(jnp.abs(hw), axis=0, keepdims=True)
    s_ref[1:2, :] = jnp.abs(hw[0:1, :])


def _spectrum_kernel(af_ref, ab_ref, f_ref, fl_ref, s_ref, d_ref, o_ref, *, n):
    n2 = GRID_W
    l1 = s_ref[0:1, :] + s_ref[2:3, :] - s_ref[3:4, :]
    hb0 = s_ref[4:5, :]
    norm = 1.0 / l1
    scale = 2.0 / (2 * n)
    for i in range(af_ref.shape[1]):
        def inner(a_ref):
            z = _dot3(f_ref[...], fl_ref[...], jnp.concatenate([a_ref[0, i], a_ref[1, i]], axis=0))
            return z[:n2], z[n2:]

        zfr, zfi = inner(af_ref)
        zbr, zbi = inner(ab_ref)
        o_ref[0, i] = scale * ((zfr + zbr - hb0) * norm + d_ref[...])
        o_ref[1, i] = scale * ((zfi - zbi) * norm)


def _dot3(a_hi, a_lo, x):
    x_hi = x.astype(BF16)
    x_lo = (x - x_hi.astype(F32)).astype(BF16)
    return _dot(a_hi, x_hi) + _dot(a_hi, x_lo) + _dot(a_lo, x_hi)


def _hi_lo(a):
    hi = np.asarray(a, np.float64).astype(BF16)
    lo = (a - hi.astype(np.float64)).astype(BF16)
    return jnp.asarray(hi), jnp.asarray(lo)


def _slabs(x):
    return [x[..., h * OUTER_ROWS:(h + 1) * OUTER_ROWS, :] for h in range(x.shape[-2] // OUTER_ROWS)]


def _outer_dft_kernel(g_ref, gl_ref, x_ref, o_ref, *, high):
    n1h, _, tl = x_ref.shape
    parts = []
    for h, xs in enumerate(_slabs(x_ref[...])):
        xs = xs.reshape(n1h * OUTER_ROWS, tl)
        r = _dot3(g_ref[h], gl_ref[h], xs) if high else _dot(g_ref[h], xs.astype(BF16))
        parts.append(r.reshape(2, n1h, OUTER_ROWS, tl))
    o_ref[...] = jnp.concatenate(parts, axis=2).astype(o_ref.dtype)


def _outer_dft(g, g_lo, x4, seg, *, c, tl, high):
    s, n1h, n2, _ = x4.shape
    tl = min(tl, c)
    per = c // tl
    rows = OUTER_ROWS if high else BF16_ROWS
    g_spec = pl.BlockSpec((rows // OUTER_ROWS,) + g.shape[1:], lambda b, j, l: (j, 0, 0))
    return pl.pallas_call(
        functools.partial(_outer_dft_kernel, high=high),
        out_shape=jax.ShapeDtypeStruct((s, 2, n1h, n2, c), F32 if high else BF16),
        grid=(s, n2 // rows, per),
        in_specs=[g_spec, g_spec,
                  pl.BlockSpec((None, n1h, rows, tl), lambda b, j, l: (b, 0, j, seg * per + l))],
        out_specs=pl.BlockSpec((None, 2, n1h, rows, tl), lambda b, j, l: (b, 0, 0, j, l)),
        compiler_params=_params("parallel", "parallel", "parallel"),
        name="outer_dft",
    )(g, g_lo, x4)


def _hyena_spectra(n, width, filt, hy_d, consts):
    w1, b1, w2, b2, w3, b3, freq, w_out = filt
    gt_fwd, _, f_fwd, _ = consts
    n1h = n // GRID_W
    order = w1.shape[1]
    pos = np.arange(n, dtype=np.float64)
    t = pos / max(n - 1, 1)
    bands = (FILT_EMB - 1) // 2
    fb = np.linspace(1e-4, bands - 1, bands)
    ang = (2 * math.pi / n) * pos[:, None] * fb[None, :]
    z = np.concatenate([t[:, None], np.cos(ang), -np.sin(ang)], axis=-1)
    emb_pad = 128
    z = jnp.asarray(np.pad(z, ((0, 0), (0, emb_pad - FILT_EMB))), F32)
    w1p = jnp.pad(w1, ((0, emb_pad - FILT_EMB), (0, 0)))
    row = lambda a: a.reshape(1, -1)
    full = lambda shape: pl.BlockSpec(shape, lambda: tuple(0 for _ in shape))
    hid = pl.pallas_call(
        _filt_mlp_kernel,
        out_shape=jax.ShapeDtypeStruct((n, order), F32),
        in_specs=[full((n, emb_pad)), full((emb_pad, order)), full((1, order)), full((order, order)),
                  full((1, order)), full((order, order)), full((1, order)), full((1, order))],
        out_specs=full((n, order)),
        compiler_params=pltpu.CompilerParams(vmem_limit_bytes=VMEM_LIMIT_BYTES),
        name="filter_mlp",
    )(z, w1p, row(b1), w2, row(b2), w3, row(b3), row(freq))

    cols = 4 * width
    deltas = np.abs(np.linspace(math.log(DECAY_TARGET) / DECAY_SLOW_PCT,
                                math.log(DECAY_TARGET) / DECAY_FAST_PCT, width))
    deltas4 = jnp.asarray(np.tile(deltas, 4)[None, :], F32)
    tc = min(512, cols)
    taps, sums = pl.pallas_call(
        functools.partial(_filt_taps_kernel, n=n),
        out_shape=(jax.ShapeDtypeStruct((n, cols), F32), jax.ShapeDtypeStruct((2, cols), F32)),
        grid=(cols // tc,),
        in_specs=[pl.BlockSpec((n, order), lambda j: (0, 0)),
                  pl.BlockSpec((order, tc), lambda j: (0, j)),
                  pl.BlockSpec((1, tc), lambda j: (0, j))],
        out_specs=(pl.BlockSpec((n, tc), lambda j: (0, j)), pl.BlockSpec((2, tc), lambda j: (0, j))),
        compiler_params=_params("parallel"),
        name="filter_taps",
    )(hid, w_out, deltas4)

    g_hi, g_lo = _hi_lo(gt_fwd)
    a = _outer_dft(g_hi, g_lo, taps.reshape(1, n1h, GRID_W, cols), 0, c=cols, tl=1024, high=True)[0]
    s4 = sums.reshape(2, 2, 2, width)
    bwd0 = taps[0].reshape(2, 2, width)[:, 1]
    stats = jnp.stack([s4[0, :, 0], s4[1, :, 0], s4[0, :, 1], s4[1, :, 1], bwd0], axis=1)

    kb = min(INNER_BINS, n1h)

    def tap_spec(side):
        return pl.BlockSpec((2, kb, GRID_W, width), lambda k1, f: (0, k1, 0, 2 * f + side))

    f_hi, f_lo = _hi_lo(f_fwd)
    f_spec = pl.BlockSpec((2 * GRID_W, 2 * GRID_W), lambda k1, f: (0, 0))
    return pl.pallas_call(
        functools.partial(_spectrum_kernel, n=n),
        out_shape=jax.ShapeDtypeStruct((2, 2, n1h, GRID_W, width), F32),
        grid=(n1h // kb, 2),
        in_specs=[tap_spec(0), tap_spec(1), f_spec, f_spec,
                  pl.BlockSpec((None, 5, width), lambda k1, f: (f, 0, 0)),
                  pl.BlockSpec((None, 1, width), lambda k1, f: (f, 0, 0))],
        out_specs=pl.BlockSpec((None, 2, kb, GRID_W, width), lambda k1, f: (f, 0, k1, 0, 0)),
        compiler_params=_params("parallel", "parallel"),
        name="filter_spectrum",
    )(a, a, f_hi, f_lo, stats, hy_d.reshape(2, 1, width))


def _inner_conv_kernel(a_ref, f_ref, fi_ref, h_ref, o_ref):
    n2 = GRID_W
    for i in range(a_ref.shape[1]):
        z = _dot(f_ref[...], jnp.concatenate([a_ref[0, i], a_ref[1, i]], axis=0))
        zr, zi = z[:n2], z[n2:]
        hr, hi = h_ref[0, i], h_ref[1, i]
        stacked = jnp.concatenate([zr * hr - zi * hi, zr * hi + zi * hr], axis=0).astype(BF16)
        w = _dot(fi_ref[...], stacked).astype(BF16)
        o_ref[0, i] = w[:n2]
        o_ref[1, i] = w[n2:]


def _inner_conv(a, f_fwd, f_inv, spec):
    bsz, _, n1h, n2, c = a.shape
    kb = min(INNER_BINS, n1h)
    return pl.pallas_call(
        _inner_conv_kernel,
        out_shape=jax.ShapeDtypeStruct(a.shape, BF16),
        grid=(n1h // kb, bsz),
        in_specs=[pl.BlockSpec((None, 2, kb, n2, c), lambda k1, b: (b, 0, k1, 0, 0)),
                  pl.BlockSpec((2 * n2, 2 * n2), lambda k1, b: (0, 0)),
                  pl.BlockSpec((2 * n2, 2 * n2), lambda k1, b: (0, 0)),
                  pl.BlockSpec((2, kb, n2, c), lambda k1, b: (0, k1, 0, 0))],
        out_specs=pl.BlockSpec((None, 2, kb, n2, c), lambda k1, b: (b, 0, k1, 0, 0)),
        compiler_params=_params("parallel", "arbitrary"),
        name="inner_conv",
    )(a, f_fwd, f_inv, spec)


def _outer_idft_gate_kernel(gi_ref, g_ref, b_ref, x_ref, o_ref, *, again):
    n1h, _, tl = x_ref.shape
    parts = []
    for h, (bs, xs) in enumerate(zip(_slabs(b_ref[...].astype(F32)), _slabs(x_ref[...]))):
        stacked = bs.reshape(2 * n1h * OUTER_ROWS, tl).astype(BF16)
        y = _dot(gi_ref[h], stacked) * xs.reshape(n1h * OUTER_ROWS, tl)
        if again:
            parts.append(_dot(g_ref[h], y.astype(BF16)).reshape(2, n1h, OUTER_ROWS, tl))
        else:
            parts.append(y.reshape(n1h, OUTER_ROWS, tl))
    o_ref[...] = jnp.concatenate(parts, axis=-2).astype(o_ref.dtype)


def _outer_idft_gate(g_inv, g_fwd, b, px4, seg, *, again, tl):
    bsz, _, n1h, n2, c = b.shape
    tl = min(tl, c)
    per = c // tl
    rows = BF16_ROWS
    if again:
        out_shape = jax.ShapeDtypeStruct((bsz, 2, n1h, n2, c), BF16)
        out_spec = pl.BlockSpec((None, 2, n1h, rows, tl), lambda bb, j, l: (bb, 0, 0, j, l))
    else:
        out_shape = jax.ShapeDtypeStruct((bsz, n1h, n2, c), F32)
        out_spec = pl.BlockSpec((None, n1h, rows, tl), lambda bb, j, l: (bb, 0, j, l))
    return pl.pallas_call(
        functools.partial(_outer_idft_gate_kernel, again=again),
        out_shape=out_shape,
        grid=(bsz, n2 // rows, per),
        in_specs=[pl.BlockSpec((rows // OUTER_ROWS,) + g_inv.shape[1:], lambda bb, j, l: (j, 0, 0)),
                  pl.BlockSpec((rows // OUTER_ROWS,) + g_fwd.shape[1:], lambda bb, j, l: (j, 0, 0)),
                  pl.BlockSpec((None, 2, n1h, rows, tl), lambda bb, j, l: (bb, 0, 0, j, l)),
                  pl.BlockSpec((None, n1h, rows, tl), lambda bb, j, l: (bb, 0, j, seg * per + l))],
        out_specs=out_spec,
        compiler_params=_params("parallel", "parallel", "parallel"),
        name="outer_idft_gate",
    )(g_inv, g_fwd, b, px4)


def _hyena_mix(px, spec, consts, *, bsz, n, width):
    g_fwd, g_inv, f_fwd, f_inv = consts
    g_fwd_b, g_inv_b = jnp.asarray(g_fwd, BF16), jnp.asarray(g_inv, BF16)
    f_fwd_b, f_inv_b = jnp.asarray(f_fwd, BF16), jnp.asarray(f_inv, BF16)
    n1h = n // GRID_W
    px4 = px.reshape(bsz, n1h, GRID_W, -1)
    tl = width
    a = _outer_dft(g_fwd_b, g_fwd_b, px4, 5, c=width, tl=tl, high=False)
    a = _inner_conv(a, f_fwd_b, f_inv_b, spec[0])
    a = _outer_idft_gate(g_inv_b, g_fwd_b, a, px4, 6, again=True, tl=tl)
    a = _inner_conv(a, f_fwd_b, f_inv_b, spec[1])
    hy = _outer_idft_gate(g_inv_b, g_fwd_b, a, px4, 7, again=False, tl=tl)
    return hy.reshape(bsz * n, width)


def _out_proj_kernel(x_ref, hg_ref, hy_ref, wa_ref, wb_ref, gt_ref, o_ref):
    mix = _dot(hg_ref[...].astype(BF16), wa_ref[...]) + _dot(hy_ref[...].astype(BF16), wb_ref[...])
    o_ref[...] = x_ref[...] + gt_ref[0] * mix


def _out_proj(x, hg, hy, w, gate, *, tokens_per_mod, tm):
    t, d = x.shape
    wdt = hg.shape[1]
    tm = min(tm, tokens_per_mod)
    return pl.pallas_call(
        _out_proj_kernel,
        out_shape=jax.ShapeDtypeStruct((t, d), F32),
        grid=(t // tm,),
        in_specs=[pl.BlockSpec((tm, d), lambda i: (i, 0)),
                  pl.BlockSpec((tm, wdt), lambda i: (i, 0)),
                  pl.BlockSpec((tm, wdt), lambda i: (i, 0)),
                  pl.BlockSpec((wdt, d), lambda i: (0, 0)),
                  pl.BlockSpec((wdt, d), lambda i: (1, 0)),
                  pl.BlockSpec((1, 1, d), lambda i: ((i * tm) // tokens_per_mod, 0, 0))],
        out_specs=pl.BlockSpec((tm, d), lambda i: (i, 0)),
        compiler_params=_params("parallel"),
        name="out_proj",
    )(x, hg, hy, w, w, gate)


def kernel(x, c, ctx, c_ctx, ada_w, ada_b, norm_ffn1, ffn1_w1, ffn1_w3, ffn1_w2, norm_mix, w_in, hg_lb_logits, hg_norm, hy_conv_w, hy_conv_b, filt_w1, filt_b1, filt_w2, filt_b2, filt_w3, filt_b3, filt_freq, filt_w_out, hy_d, w_out, norm_ffn2, ffn2_w1, ffn2_w3, ffn2_w2, final_norm):
    bsz, n, d = x.shape
    n_ctx = ctx.shape[1]
    depth = ada_w.shape[0]
    assert depth == 1, "single-layer block"
    width = d // 2
    n_seg = w_in.shape[2] // width
    assert n_seg == 8 and n % GRID_W == 0 and n_ctx % SCAN_CHUNK == 0 and width % HEAD_DIM == 0
    d_ff = ffn1_w1.shape[2]
    tf = 512 if d_ff % 512 == 0 else d_ff
    tm = 512
    tm_wide = 1024
    l = 0

    rows = -(-(bsz + 1) // 8) * 8
    cs = jnp.concatenate([c, c_ctx[None, :], jnp.zeros((rows - bsz - 1, d), F32)], axis=0)
    mods = _ada(cs, ada_w[l], ada_b[l]).reshape(rows, N_MOD, d)
    mx = [mods[:bsz, i][:, None, :] for i in range(N_MOD)]
    mc = [mods[bsz:bsz + 1, i][:, None, :] for i in range(N_MOD)]

    bf = lambda a: a.astype(BF16)
    w1a, w3a, w2a = bf(ffn1_w1[l]), bf(ffn1_w3[l]), bf(ffn1_w2[l])
    xt = x.reshape(bsz * n, d)
    yt = ctx.reshape(bsz * n_ctx, d)

    xt = _ffn(xt, mx[0], mx[1], mx[2], norm_ffn1[l], w1a, w3a, w2a, final_norm,
              tokens_per_mod=n, final_norm=False, tm=tm, tf=tf)
    yt = _ffn(yt, mc[0], mc[1], mc[2], norm_ffn1[l], w1a, w3a, w2a, final_norm,
              tokens_per_mod=bsz * n_ctx, final_norm=False, tm=tm, tf=tf)

    w_in_b = bf(w_in[l])
    pc = _proj(yt, mc[3], mc[4], norm_mix[l], w_in_b, hy_conv_w[l], hy_conv_b[l], n_seg=3, seg_w=width,
               conv_from=n_seg, tokens_per_mod=bsz * n_ctx, tm=tm)
    px = _proj(xt, mx[3], mx[4], norm_mix[l], w_in_b, hy_conv_w[l], hy_conv_b[l], n_seg=n_seg, seg_w=width,
               conv_from=5, tokens_per_mod=n, tm=tm_wide)
    cols = n_seg * width

    hg = _hgrn(px.reshape(bsz, n, cols), pc.reshape(bsz, n_ctx, 3 * width), hg_lb_logits, hg_norm[l], width=width)

    consts = _dft_constants(n)
    filt = (filt_w1[l], filt_b1[l], filt_w2[l], filt_b2[l], filt_w3[l], filt_b3[l], filt_freq[l], filt_w_out[l])
    spec = _hyena_spectra(n, width, filt, hy_d[l], consts)
    hy = _hyena_mix(px, spec, consts, bsz=bsz, n=n, width=width)

    xt = _out_proj(xt, hg.reshape(bsz * n, width), hy, bf(w_out[l]), mx[5],
                   tokens_per_mod=n, tm=tm)

    out = _ffn(xt, mx[6], mx[7], mx[8], norm_ffn2[l], bf(ffn2_w1[l]), bf(ffn2_w3[l]), bf(ffn2_w2[l]), final_norm,
               tokens_per_mod=n, final_norm=True, tm=tm, tf=tf)
    return out.reshape(bsz, n, d)
```

```python
import functools
import math

import numpy as np
import jax
import jax.numpy as jnp
from jax import lax
from jax.experimental import pallas as pl
from jax.experimental.pallas import tpu as pltpu

F32 = jnp.float32
BF16 = jnp.bfloat16
EPS = 1e-6
N_MOD = 9
HEAD_DIM = 128
GRID_W = 64
SCAN_CHUNK = 64
PREP_CHUNKS = 4
SCAN_BLOCKS = 2
OUTER_ROWS = 8
BF16_ROWS = 16
INNER_BINS = 8
FILT_EMB = 33
DECAY_TARGET = 1e-2
DECAY_FAST_PCT = 0.3
DECAY_SLOW_PCT = 1.5
VMEM_LIMIT_BYTES = 56 * 1024 * 1024
HIGHEST = lax.Precision.HIGHEST


def _params(*semantics):
    return pltpu.CompilerParams(dimension_semantics=semantics, vmem_limit_bytes=VMEM_LIMIT_BYTES)


def _dot(a, b):
    return jnp.dot(a, b, preferred_element_type=F32)


def _silu(a):
    return a * jax.nn.sigmoid(a)


def _ada_kernel(c_ref, w_ref, b_ref, o_ref):
    h = _silu(c_ref[...]).astype(BF16)
    o_ref[...] = _dot(h, w_ref[...].astype(BF16)) + b_ref[...]


def _ada(cs, w, b):
    rows, d = cs.shape
    n = w.shape[1]
    tn = d // 2
    return pl.pallas_call(
        _ada_kernel,
        out_shape=jax.ShapeDtypeStruct((rows, n), F32),
        grid=(n // tn,),
        in_specs=[pl.BlockSpec((rows, d), lambda j: (0, 0)),
                  pl.BlockSpec((d, tn), lambda j: (0, j)),
                  pl.BlockSpec((1, tn), lambda j: (0, j))],
        out_specs=pl.BlockSpec((rows, tn), lambda j: (0, j)),
        compiler_params=_params("arbitrary"),
        name="ada_mod",
    )(cs, w, b.reshape(1, n))


def _norm_mod(x, gain, shift, scale):
    y = x * lax.rsqrt(jnp.mean(x * x, axis=-1, keepdims=True) + EPS) * gain
    return (y * (1.0 + scale) + shift).astype(BF16)


def _ffn_kernel(x_ref, sh_ref, sc_ref, gt_ref, g_ref, w1_ref, w3_ref, w2_ref, fin_ref, o_ref, h_scr, *, final_norm):
    j = pl.program_id(1)

    @pl.when(j == 0)
    def _():
        h_scr[...] = _norm_mod(x_ref[...], g_ref[...], sh_ref[0], sc_ref[0])
        o_ref[...] = jnp.zeros_like(o_ref)

    h = h_scr[...]
    a = _dot(h, w1_ref[...])
    b = _dot(h, w3_ref[...])
    act = (_silu(a) * b).astype(BF16)
    o_ref[...] += _dot(act, w2_ref[...])

    @pl.when(j == pl.num_programs(1) - 1)
    def _():
        out = x_ref[...] + 0.5 * gt_ref[0] * o_ref[...]
        if final_norm:
            out = out * lax.rsqrt(jnp.mean(out * out, axis=-1, keepdims=True) + EPS) * fin_ref[...]
        o_ref[...] = out


def _ffn(x, shift, scale, gate, gain, w1, w3, w2, fin, *, tokens_per_mod, final_norm, tm, tf):
    t, d = x.shape
    f = w1.shape[1]
    tm = min(tm, tokens_per_mod)
    mod_spec = pl.BlockSpec((1, 1, d), lambda i, j: ((i * tm) // tokens_per_mod, 0, 0))
    vec_spec = pl.BlockSpec((1, d), lambda i, j: (0, 0))
    return pl.pallas_call(
        functools.partial(_ffn_kernel, final_norm=final_norm),
        out_shape=jax.ShapeDtypeStruct((t, d), F32),
        grid=(t // tm, f // tf),
        in_specs=[pl.BlockSpec((tm, d), lambda i, j: (i, 0)),
                  mod_spec, mod_spec, mod_spec, vec_spec,
                  pl.BlockSpec((d, tf), lambda i, j: (0, j)),
                  pl.BlockSpec((d, tf), lambda i, j: (0, j)),
                  pl.BlockSpec((tf, d), lambda i, j: (j, 0)),
                  vec_spec],
        out_specs=pl.BlockSpec((tm, d), lambda i, j: (i, 0)),
        scratch_shapes=[pltpu.VMEM((tm, d), BF16)],
        compiler_params=_params("parallel", "arbitrary"),
        name="swiglu_ffn",
    )(x, shift, scale, gate, gain.reshape(1, d), w1, w3, w2, fin.reshape(1, d))


def _proj_kernel(x_ref, sh_ref, sc_ref, g_ref, w_ref, cw_ref, cb_ref, o_ref, h_scr, *, conv_from):
    j = pl.program_id(1)

    @pl.when(j == 0)
    def _():
        h_scr[...] = _norm_mod(x_ref[...], g_ref[...], sh_ref[0], sc_ref[0])

    p = _dot(h_scr[...], w_ref[...])

    @pl.when(j < conv_from)
    def _():
        o_ref[...] = p

    @pl.when(j >= conv_from)
    def _():
        tm, w = p.shape
        rows3 = lambda t: t.reshape(tm // GRID_W, GRID_W, w)
        y = (rows3(pltpu.roll(p, 1, axis=0)) * cw_ref[0] + rows3(p) * cw_ref[1]
             + rows3(pltpu.roll(p, tm - 1, axis=0)) * cw_ref[2] + cb_ref[...])
        o_ref[...] = y.reshape(tm, w)


def _proj(x, shift, scale, gain, w, conv_w, conv_b, *, n_seg, seg_w, conv_from, tokens_per_mod, tm):
    t, d = x.shape
    tm = min(tm, tokens_per_mod)
    mod_spec = pl.BlockSpec((1, 1, d), lambda i, j: ((i * tm) // tokens_per_mod, 0, 0))
    conv_idx = lambda i, j: (0, jnp.maximum(j - conv_from, 0))
    pos = jnp.arange(GRID_W)[None, :, None]
    edge = jnp.stack([pos > 0, pos >= 0, pos < GRID_W - 1], axis=0)[:, 0]
    conv_w = jnp.where(edge, conv_w[:, None, :], 0.0)
    return pl.pallas_call(
        functools.partial(_proj_kernel, conv_from=conv_from),
        out_shape=jax.ShapeDtypeStruct((t, n_seg * seg_w), F32),
        grid=(t // tm, n_seg),
        in_specs=[pl.BlockSpec((tm, d), lambda i, j: (i, 0)),
                  mod_spec, mod_spec,
                  pl.BlockSpec((1, d), lambda i, j: (0, 0)),
                  pl.BlockSpec((d, seg_w), lambda i, j: (0, j)),
                  pl.BlockSpec((3, GRID_W, seg_w), lambda i, j: (0, 0, jnp.maximum(j - conv_from, 0))),
                  pl.BlockSpec((1, seg_w), conv_idx)],
        out_specs=pl.BlockSpec((tm, seg_w), lambda i, j: (i, j)),
        scratch_shapes=[pltpu.VMEM((tm, d), BF16)],
        compiler_params=_params("parallel", "arbitrary"),
        name="in_proj",
    )(x, shift, scale, gain.reshape(1, d), w, conv_w, conv_b.reshape(1, -1))


def _split3(g):
    g1 = g.astype(BF16)
    r1 = g - g1.astype(F32)
    g2 = r1.astype(BF16)
    return g1, g2, (r1 - g2.astype(F32)).astype(BF16)


def _split_dot(tri, g):
    g1, g2, g3 = _split3(g)
    return _dot(tri, g1) + _dot(tri, g2) + _dot(tri, g3)


def _gates(a, lb):
    f = lb + (1.0 - lb) * jax.nn.sigmoid(a)
    return 1.0 - f, jnp.log(f)


def _tn_dot(a, b):
    return lax.dot_general(a, b, (((0,), (0,)), ((), ())), preferred_element_type=F32)


def _nt_dot(a, b):
    return lax.dot_general(a, b, (((1,), (1,)), ((), ())), preferred_element_type=F32)


def _hgrn_kernel(af_ref, ab_ref, v_ref, q_ref, gate_ref, caf_ref, cab_ref, cv_ref, lbl_ref, gain_ref,
                 o_ref, of_scr, ob_scr, sf_scr, sb_scr, qin_scr, qt_scr, kt_scr, kd_scr, dec_scr,
                 *, n_chunks, n_ctx_chunks):
    c = SCAN_CHUNK
    mid = c // 2
    rows_p = PREP_CHUNKS * c
    prow = lax.broadcasted_iota(jnp.int32, (rows_p, rows_p), 0)
    pcol = lax.broadcasted_iota(jnp.int32, (rows_p, rows_p), 1)
    same = (prow // c) == (pcol // c)
    tri_fwd_p = jnp.where(same & (prow >= pcol), 1.0, 0.0).astype(BF16)
    tri_bwd_p = jnp.where(same & (pcol >= prow), 1.0, 0.0).astype(BF16)

    def lower_bound(direction):
        lg = lbl_ref[direction]
        ex = jnp.exp(lg - jnp.max(lg, axis=0, keepdims=True))
        return ex[0:1, :] / jnp.sum(ex, axis=0, keepdims=True)

    lb_f = lower_bound(0)
    lb_b = lower_bound(1)
    q_scale = HEAD_DIM ** -0.5

    n_blocks = n_chunks // PREP_CHUNKS
    n_steps = n_blocks // SCAN_BLOCKS

    def step_blocks(it):
        fwd = [(0, it * SCAN_BLOCKS + u) for u in range(SCAN_BLOCKS)]
        return fwd + [(1, n_blocks - 1 - bi) for _, bi in fwd]

    def prepare(it, s):
        it = jnp.minimum(it, n_steps - 1)
        blocks = step_blocks(it)
        los = [pl.multiple_of(bi * rows_p, rows_p) for _, bi in blocks]
        gates = [_gates((af_ref, ab_ref)[d][pl.ds(lo, rows_p), :], (lb_f, lb_b)[d]) for (d, _), lo in zip(blocks, los)]
        sums = {}
        for d, tri_p in ((0, tri_fwd_p), (1, tri_bwd_p)):
            slots_d = [slot for slot, (dd, _) in enumerate(blocks) if dd == d]
            wide = _dot(tri_p, jnp.concatenate([p for slot in slots_d for p in _split3(gates[slot][1])], axis=1))
            for n_s, slot in enumerate(slots_d):
                parts = [wide[:, (3 * n_s + m) * HEAD_DIM:(3 * n_s + m + 1) * HEAD_DIM] for m in range(3)]
                sums[slot] = parts[0] + parts[1] + parts[2]
        for slot, ((d, _), lo) in enumerate(zip(blocks, los)):
            end_row = (c - 1, 0)[d]
            k = gates[slot][0]
            b = sums[slot].reshape(PREP_CHUNKS, c, HEAD_DIM)
            b_mid = b[:, mid:mid + 1, :]
            b_end = b[:, end_row:end_row + 1, :]
            q_t = (q_ref[pl.ds(lo, rows_p), :] * q_scale).reshape(PREP_CHUNKS, c, HEAD_DIM) * jnp.exp(b - b_mid)
            k_t = k.reshape(PREP_CHUNKS, c, HEAD_DIM) * jnp.exp(b_mid - b)
            flat = lambda t: t.reshape(rows_p, HEAD_DIM).astype(BF16)
            qt_scr[s, slot] = flat(q_t)
            kt_scr[s, slot] = flat(k_t)
            qin_scr[s, slot] = flat(q_t * jnp.exp(b_mid))
            kd_scr[s, slot] = flat(k_t * jnp.exp(b_end - b_mid))
            dec_scr[s, slot] = jnp.broadcast_to(jnp.exp(b_end), (PREP_CHUNKS, 8, HEAD_DIM))

    zeros_blk = jnp.zeros((c, HEAD_DIM), BF16)

    def block_diag(t):
        return jnp.concatenate(
            [jnp.concatenate([t[j * c:(j + 1) * c] if m == j else zeros_blk for m in range(PREP_CHUNKS)], axis=1)
             for j in range(PREP_CHUNKS)], axis=0)

    masks = (same & (prow >= pcol), same & (pcol >= prow))
    st_refs = (sf_scr, sb_scr)
    out_refs = (of_scr, ob_scr)
    orders = (tuple(range(PREP_CHUNKS)), tuple(range(PREP_CHUNKS - 1, -1, -1)))

    def scan(it, s):
        streams = step_blocks(it)
        los = [pl.multiple_of(bi * rows_p, rows_p) for _, bi in streams]
        vbs = [v_ref[pl.ds(lo, rows_p), :].astype(BF16) for lo in los]
        scores = [_nt_dot(qt_scr[s, slot], kt_scr[s, slot]) for slot in range(len(streams))]
        upds = [_tn_dot(vb, block_diag(kd_scr[s, slot])) for slot, vb in enumerate(vbs)]
        outs = [_dot(jnp.where(masks[d], sc, 0.0).astype(BF16), vb) for (d, _), sc, vb in zip(streams, scores, vbs)]
        seen = [[None] * PREP_CHUNKS for _ in streams]
        for d in (0, 1):
            st = st_refs[d][...]
            for slot, (ds_, _) in enumerate(streams):
                if ds_ != d:
                    continue
                for j in orders[d]:
                    seen[slot][j] = st.astype(BF16)
                    st = st * dec_scr[s, slot, j][0:1, :] + upds[slot][:, j * HEAD_DIM:(j + 1) * HEAD_DIM]
            st_refs[d][...] = st
        for slot, ((d, _), lo) in enumerate(zip(streams, los)):
            q_in = qin_scr[s, slot]
            carried = jnp.concatenate([_nt_dot(q_in[j * c:(j + 1) * c], seen[slot][j]) for j in range(PREP_CHUNKS)],
                                      axis=0)
            out_refs[d][pl.ds(lo, rows_p), :] = outs[slot] + carried

    n_ctx_blocks = n_ctx_chunks // PREP_CHUNKS
    ctx_state = [jnp.zeros((HEAD_DIM, HEAD_DIM), F32), jnp.zeros((HEAD_DIM, HEAD_DIM), F32)]
    for step in range(n_ctx_blocks):
        upds, decs = [], []
        for d, bi in ((0, step), (1, n_ctx_blocks - 1 - step)):
            a_ref, lb, tri_p, end_row = ((caf_ref, lb_f, tri_fwd_p, c - 1), (cab_ref, lb_b, tri_bwd_p, 0))[d]
            rows_c = pl.ds(bi * rows_p, rows_p)
            k, g = _gates(a_ref[rows_c, :], lb)
            b = _split_dot(tri_p, g).reshape(PREP_CHUNKS, c, HEAD_DIM)
            b_end = b[:, end_row:end_row + 1, :]
            k_d = (k.reshape(PREP_CHUNKS, c, HEAD_DIM) * jnp.exp(b_end - b)).reshape(rows_p, HEAD_DIM).astype(BF16)
            upds.append(_tn_dot(cv_ref[rows_c, :].astype(BF16), block_diag(k_d)))
            decs.append(jnp.exp(b_end))
        for d in (0, 1):
            for j in orders[d]:
                ctx_state[d] = ctx_state[d] * decs[d][j] + upds[d][:, j * HEAD_DIM:(j + 1) * HEAD_DIM]
    sf_scr[...] = ctx_state[0]
    sb_scr[...] = ctx_state[1]

    prepare(0, 0)

    def body(i, carry):
        scan(2 * i, 0)
        prepare(2 * i + 1, 1)
        scan(2 * i + 1, 1)
        prepare(2 * i + 2, 0)
        return carry

    lax.fori_loop(0, n_steps // 2, body, 0)

    gain = gain_ref[...]
    rows = 4 * c

    def readout(i, carry):
        lo = pl.multiple_of(i * rows, rows)
        o = of_scr[pl.ds(lo, rows), :] + ob_scr[pl.ds(lo, rows), :]
        o = o * lax.rsqrt(jnp.mean(o * o, axis=-1, keepdims=True) + EPS) * gain
        o_ref[pl.ds(lo, rows), :] = o * _silu(gate_ref[pl.ds(lo, rows), :])
        return carry

    lax.fori_loop(0, n_chunks // 4, readout, 0)


def _hgrn(px, pc, lb_logits, gain, *, width):
    bsz, n, _ = px.shape
    n_ctx = pc.shape[1]
    heads = width // HEAD_DIM
    hd = HEAD_DIM

    def seg(s):
        return pl.BlockSpec((None, n, hd), lambda b, h: (b, 0, s * heads + h))

    def cseg(s):
        return pl.BlockSpec((None, n_ctx, hd), lambda b, h: (b, 0, s * heads + h))

    depth1 = lb_logits.shape[1]
    slots = 2 * SCAN_BLOCKS
    assert (n // SCAN_CHUNK) % (2 * SCAN_BLOCKS * PREP_CHUNKS) == 0, "scan steps come in pairs"
    assert (n_ctx // SCAN_CHUNK) % PREP_CHUNKS == 0, "the context prefix is scanned in whole blocks"
    return pl.pallas_call(
        functools.partial(_hgrn_kernel, n_chunks=n // SCAN_CHUNK, n_ctx_chunks=n_ctx // SCAN_CHUNK),
        out_shape=jax.ShapeDtypeStruct((bsz, n, width), F32),
        grid=(bsz, heads),
        in_specs=[seg(0), seg(1), seg(2), seg(3), seg(4), cseg(0), cseg(1), cseg(2),
                  pl.BlockSpec((2, depth1, hd), lambda b, h: (0, 0, h)),
                  pl.BlockSpec((1, hd), lambda b, h: (0, h))],
        out_specs=pl.BlockSpec((None, n, hd), lambda b, h: (b, 0, h)),
        scratch_shapes=[pltpu.VMEM((n, hd), F32), pltpu.VMEM((n, hd), F32),
                        pltpu.VMEM((hd, hd), F32), pltpu.VMEM((hd, hd), F32),
                        *[pltpu.VMEM((2, slots, PREP_CHUNKS * SCAN_CHUNK, hd), BF16) for _ in range(4)],
                        pltpu.VMEM((2, slots, PREP_CHUNKS, 8, hd), F32)],
        compiler_params=_params("parallel", "parallel"),
        name="hgrn_scan",
    )(px, px, px, px, px, pc, pc, pc, lb_logits, gain.reshape(1, width))


def _dft_constants(n):
    n2 = GRID_W
    n1h = n // n2
    m2r = np.arange(n2)[None, :]
    k2 = np.arange(n2)[:, None]
    f_ang = -2.0 * np.pi * k2 * m2r / n2
    fr, fi = np.cos(f_ang), np.sin(f_ang)
    f_fwd = np.block([[fr, -fi], [fi, fr]])
    f_inv = np.block([[fr, fi], [-fi, fr]])
    k1 = np.arange(n1h)[:, None, None]
    m1 = np.arange(n1h)[None, :, None]
    m2 = np.arange(n2)[None, None, :]
    theta = -2.0 * np.pi * (k1 + 0.5) * (m1 / (2 * n1h) + m2 / (2 * n))
    parts = np.stack([np.cos(theta), np.sin(theta)], axis=0)
    parts = parts.reshape(2, n1h, n1h, n2 // OUTER_ROWS, OUTER_ROWS)
    eye = np.eye(OUTER_ROWS)
    rows = n1h * OUTER_ROWS
    fwd = np.einsum("pknsj,jm->spkjnm", parts, eye).reshape(n2 // OUTER_ROWS, 2 * rows, rows)
    inv = np.einsum("pknsj,jm->snjpkm", parts, eye).reshape(n2 // OUTER_ROWS, rows, 2 * rows)
    return fwd, inv, f_fwd, f_inv


def _filt_mlp_kernel(z_ref, w1_ref, b1_ref, w2_ref, b2_ref, w3_ref, b3_ref, fr_ref, o_ref):
    fr = fr_ref[...]
    hp = lambda a, b: jnp.dot(a, b, precision=HIGHEST, preferred_element_type=F32)
    hid = jnp.sin(fr * (hp(z_ref[...], w1_ref[...]) + b1_ref[...]))
    hid = jnp.sin(fr * (hp(hid, w2_ref[...]) + b2_ref[...]))
    o_ref[...] = jnp.sin(fr * (hp(hid, w3_ref[...]) + b3_ref[...]))


def _filt_taps_kernel(hid_ref, wo_ref, dl_ref, h_ref, s_ref, *, n):
    h = jnp.dot(hid_ref[...], wo_ref[...], precision=HIGHEST, preferred_element_type=F32)
    t = lax.broadcasted_iota(jnp.int32, h.shape, 0).astype(F32) * (1.0 / max(n - 1, 1))
    hw = h * jnp.exp(-t * dl_ref[...])
    h_ref[...] = hw
    s_ref[0:1, :] = jnp.sum(jnp.abs(hw), axis=0, keepdims=True)
    s_ref[1:2, :] = jnp.abs(hw[0:1, :])


def _spectrum_kernel(af_ref, ab_ref, f_ref, fl_ref, s_ref, d_ref, o_ref, *, n):
    n2 = GRID_W
    l1 = s_ref[0:1, :] + s_ref[2:3, :] - s_ref[3:4, :]
    hb0 = s_ref[4:5, :]
    norm = 1.0 / l1
    scale = 2.0 / (2 * n)
    for i in range(af_ref.shape[1]):
        def inner(a_ref):
            z = _dot3(f_ref[...], fl_ref[...], jnp.concatenate([a_ref[0, i], a_ref[1, i]], axis=0))
            return z[:n2], z[n2:]

        zfr, zfi = inner(af_ref)
        zbr, zbi = inner(ab_ref)
        o_ref[0, i] = scale * ((zfr + zbr - hb0) * norm + d_ref[...])
        o_ref[1, i] = scale * ((zfi - zbi) * norm)


def _dot3(a_hi, a_lo, x):
    x_hi = x.astype(BF16)
    x_lo = (x - x_hi.astype(F32)).astype(BF16)
    return _dot(a_hi, x_hi) + _dot(a_hi, x_lo) + _dot(a_lo, x_hi)


def _hi_lo(a):
    hi = np.asarray(a, np.float64).astype(BF16)
    lo = (a - hi.astype(np.float64)).astype(BF16)
    return jnp.asarray(hi), jnp.asarray(lo)


def _slabs(x):
    return [x[..., h * OUTER_ROWS:(h + 1) * OUTER_ROWS, :] for h in range(x.shape[-2] // OUTER_ROWS)]


def _outer_dft_kernel(g_ref, gl_ref, x_ref, o_ref, *, high):
    n1h, _, tl = x_ref.shape
    parts = []
    for h, xs in enumerate(_slabs(x_ref[...])):
        xs = xs.reshape(n1h * OUTER_ROWS, tl)
        r = _dot3(g_ref[h], gl_ref[h], xs) if high else _dot(g_ref[h], xs.astype(BF16))
        parts.append(r.reshape(2, n1h, OUTER_ROWS, tl))
    o_ref[...] = jnp.concatenate(parts, axis=2).astype(o_ref.dtype)


def _outer_dft(g, g_lo, x4, seg, *, c, tl, high):
    s, n1h, n2, _ = x4.shape
    tl = min(tl, c)
    per = c // tl
    rows = OUTER_ROWS if high else BF16_ROWS
    g_spec = pl.BlockSpec((rows // OUTER_ROWS,) + g.shape[1:], lambda j, b, l: (j, 0, 0))
    return pl.pallas_call(
        functools.partial(_outer_dft_kernel, high=high),
        out_shape=jax.ShapeDtypeStruct((s, 2, n1h, n2, c), F32 if high else BF16),
        grid=(n2 // rows, s, per),
        in_specs=[g_spec, g_spec,
                  pl.BlockSpec((None, n1h, rows, tl), lambda j, b, l: (b, 0, j, seg * per + l))],
        out_specs=pl.BlockSpec((None, 2, n1h, rows, tl), lambda j, b, l: (b, 0, 0, j, l)),
        compiler_params=_params("parallel", "parallel", "parallel"),
        name="outer_dft",
    )(g, g_lo, x4)


def _hyena_spectra(n, width, filt, hy_d, consts):
    w1, b1, w2, b2, w3, b3, freq, w_out = filt
    gt_fwd, _, f_fwd, _ = consts
    n1h = n // GRID_W
    order = w1.shape[1]
    pos = np.arange(n, dtype=np.float64)
    t = pos / max(n - 1, 1)
    bands = (FILT_EMB - 1) // 2
    fb = np.linspace(1e-4, bands - 1, bands)
    ang = (2 * math.pi / n) * pos[:, None] * fb[None, :]
    z = np.concatenate([t[:, None], np.cos(ang), -np.sin(ang)], axis=-1)
    emb_pad = 128
    z = jnp.asarray(np.pad(z, ((0, 0), (0, emb_pad - FILT_EMB))), F32)
    w1p = jnp.pad(w1, ((0, emb_pad - FILT_EMB), (0, 0)))
    row = lambda a: a.reshape(1, -1)
    full = lambda shape: pl.BlockSpec(shape, lambda: tuple(0 for _ in shape))
    hid = pl.pallas_call(
        _filt_mlp_kernel,
        out_shape=jax.ShapeDtypeStruct((n, order), F32),
        in_specs=[full((n, emb_pad)), full((emb_pad, order)), full((1, order)), full((order, order)),
                  full((1, order)), full((order, order)), full((1, order)), full((1, order))],
        out_specs=full((n, order)),
        compiler_params=pltpu.CompilerParams(vmem_limit_bytes=VMEM_LIMIT_BYTES),
        name="filter_mlp",
    )(z, w1p, row(b1), w2, row(b2), w3, row(b3), row(freq))

    cols = 4 * width
    deltas = np.abs(np.linspace(math.log(DECAY_TARGET) / DECAY_SLOW_PCT,
                                math.log(DECAY_TARGET) / DECAY_FAST_PCT, width))
    deltas4 = jnp.asarray(np.tile(deltas, 4)[None, :], F32)
    tc = min(512, cols)
    taps, sums = pl.pallas_call(
        functools.partial(_filt_taps_kernel, n=n),
        out_shape=(jax.ShapeDtypeStruct((n, cols), F32), jax.ShapeDtypeStruct((2, cols), F32)),
        grid=(cols // tc,),
        in_specs=[pl.BlockSpec((n, order), lambda j: (0, 0)),
                  pl.BlockSpec((order, tc), lambda j: (0, j)),
                  pl.BlockSpec((1, tc), lambda j: (0, j))],
        out_specs=(pl.BlockSpec((n, tc), lambda j: (0, j)), pl.BlockSpec((2, tc), lambda j: (0, j))),
        compiler_params=_params("parallel"),
        name="filter_taps",
    )(hid, w_out, deltas4)

    g_hi, g_lo = _hi_lo(gt_fwd)
    a = _outer_dft(g_hi, g_lo, taps.reshape(1, n1h, GRID_W, cols), 0, c=cols, tl=1024, high=True)[0]
    s4 = sums.reshape(2, 2, 2, width)
    bwd0 = taps[0].reshape(2, 2, width)[:, 1]
    stats = jnp.stack([s4[0, :, 0], s4[1, :, 0], s4[0, :, 1], s4[1, :, 1], bwd0], axis=1)

    kb = min(INNER_BINS, n1h)

    def tap_spec(side):
        return pl.BlockSpec((2, kb, GRID_W, width), lambda k1, f: (0, k1, 0, 2 * f + side))

    f_hi, f_lo = _hi_lo(f_fwd)
    f_spec = pl.BlockSpec((2 * GRID_W, 2 * GRID_W), lambda k1, f: (0, 0))
    return pl.pallas_call(
        functools.partial(_spectrum_kernel, n=n),
        out_shape=jax.ShapeDtypeStruct((2, 2, n1h, GRID_W, width), F32),
        grid=(n1h // kb, 2),
        in_specs=[tap_spec(0), tap_spec(1), f_spec, f_spec,
                  pl.BlockSpec((None, 5, width), lambda k1, f: (f, 0, 0)),
                  pl.BlockSpec((None, 1, width), lambda k1, f: (f, 0, 0))],
        out_specs=pl.BlockSpec((None, 2, kb, GRID_W, width), lambda k1, f: (f, 0, k1, 0, 0)),
        compiler_params=_params("parallel", "parallel"),
        name="filter_spectrum",
    )(a, a, f_hi, f_lo, stats, hy_d.reshape(2, 1, width))


def _inner_conv_kernel(a_ref, f_ref, fi_ref, h_ref, o_ref):
    n2 = GRID_W
    for i in range(a_ref.shape[1]):
        z = _dot(f_ref[...], jnp.concatenate([a_ref[0, i], a_ref[1, i]], axis=0))
        zr, zi = z[:n2], z[n2:]
        hr, hi = h_ref[0, i], h_ref[1, i]
        stacked = jnp.concatenate([zr * hr - zi * hi, zr * hi + zi * hr], axis=0).astype(BF16)
        w = _dot(fi_ref[...], stacked).astype(BF16)
        o_ref[0, i] = w[:n2]
        o_ref[1, i] = w[n2:]


def _inner_conv(a, f_fwd, f_inv, spec):
    bsz, _, n1h, n2, c = a.shape
    kb = min(INNER_BINS, n1h)
    return pl.pallas_call(
        _inner_conv_kernel,
        out_shape=jax.ShapeDtypeStruct(a.shape, BF16),
        grid=(n1h // kb, bsz),
        in_specs=[pl.BlockSpec((None, 2, kb, n2, c), lambda k1, b: (b, 0, k1, 0, 0)),
                  pl.BlockSpec((2 * n2, 2 * n2), lambda k1, b: (0, 0)),
                  pl.BlockSpec((2 * n2, 2 * n2), lambda k1, b: (0, 0)),
                  pl.BlockSpec((2, kb, n2, c), lambda k1, b: (0, k1, 0, 0))],
        out_specs=pl.BlockSpec((None, 2, kb, n2, c), lambda k1, b: (b, 0, k1, 0, 0)),
        compiler_params=_params("parallel", "arbitrary"),
        name="inner_conv",
    )(a, f_fwd, f_inv, spec)


def _outer_idft_gate_kernel(gi_ref, g_ref, b_ref, x_ref, o_ref, *, again):
    n1h, _, tl = x_ref.shape
    parts = []
    for h, (bs, xs) in enumerate(zip(_slabs(b_ref[...].astype(F32)), _slabs(x_ref[...]))):
        stacked = bs.reshape(2 * n1h * OUTER_ROWS, tl).astype(BF16)
        y = _dot(gi_ref[h], stacked) * xs.reshape(n1h * OUTER_ROWS, tl)
        if again:
            parts.append(_dot(g_ref[h], y.astype(BF16)).reshape(2, n1h, OUTER_ROWS, tl))
        else:
            parts.append(y.reshape(n1h, OUTER_ROWS, tl))
    o_ref[...] = jnp.concatenate(parts, axis=-2).astype(o_ref.dtype)


def _outer_idft_gate(g_inv, g_fwd, b, px4, seg, *, again, tl):
    bsz, _, n1h, n2, c = b.shape
    tl = min(tl, c)
    per = c // tl
    rows = BF16_ROWS
    if again:
        out_shape = jax.ShapeDtypeStruct((bsz, 2, n1h, n2, c), BF16)
        out_spec = pl.BlockSpec((None, 2, n1h, rows, tl), lambda j, bb, l: (bb, 0, 0, j, l))
    else:
        out_shape = jax.ShapeDtypeStruct((bsz, n1h, n2, c), F32)
        out_spec = pl.BlockSpec((None, n1h, rows, tl), lambda j, bb, l: (bb, 0, j, l))
    return pl.pallas_call(
        functools.partial(_outer_idft_gate_kernel, again=again),
        out_shape=out_shape,
        grid=(n2 // rows, bsz, per),
        in_specs=[pl.BlockSpec((rows // OUTER_ROWS,) + g_inv.shape[1:], lambda j, bb, l: (j, 0, 0)),
                  pl.BlockSpec((rows // OUTER_ROWS,) + g_fwd.shape[1:], lambda j, bb, l: (j, 0, 0)),
                  pl.BlockSpec((None, 2, n1h, rows, tl), lambda j, bb, l: (bb, 0, 0, j, l)),
                  pl.BlockSpec((None, n1h, rows, tl), lambda j, bb, l: (bb, 0, j, seg * per + l))],
        out_specs=out_spec,
        compiler_params=_params("parallel", "parallel", "parallel"),
        name="outer_idft_gate",
    )(g_inv, g_fwd, b, px4)


def _hyena_mix(px, spec, consts, *, bsz, n, width):
    g_fwd, g_inv, f_fwd, f_inv = consts
    g_fwd_b, g_inv_b = jnp.asarray(g_fwd, BF16), jnp.asarray(g_inv, BF16)
    f_fwd_b, f_inv_b = jnp.asarray(f_fwd, BF16), jnp.asarray(f_inv, BF16)
    n1h = n // GRID_W
    px4 = px.reshape(bsz, n1h, GRID_W, -1)
    tl = width
    a = _outer_dft(g_fwd_b, g_fwd_b, px4, 5, c=width, tl=tl, high=False)
    a = _inner_conv(a, f_fwd_b, f_inv_b, spec[0])
    a = _outer_idft_gate(g_inv_b, g_fwd_b, a, px4, 6, again=True, tl=tl)
    a = _inner_conv(a, f_fwd_b, f_inv_b, spec[1])
    hy = _outer_idft_gate(g_inv_b, g_fwd_b, a, px4, 7, again=False, tl=tl)
    return hy.reshape(bsz * n, width)


def _out_proj_kernel(x_ref, hg_ref, hy_ref, wa_ref, wb_ref, gt_ref, o_ref):
    mix = _dot(hg_ref[...].astype(BF16), wa_ref[...]) + _dot(hy_ref[...].astype(BF16), wb_ref[...])
    o_ref[...] = x_ref[...] + gt_ref[0] * mix


def _out_proj(x, hg, hy, w, gate, *, tokens_per_mod, tm):
    t, d = x.shape
    wdt = hg.shape[1]
    tm = min(tm, tokens_per_mod)
    return pl.pallas_call(
        _out_proj_kernel,
        out_shape=jax.ShapeDtypeStruct((t, d), F32),
        grid=(t // tm,),
        in_specs=[pl.BlockSpec((tm, d), lambda i: (i, 0)),
                  pl.BlockSpec((tm, wdt), lambda i: (i, 0)),
                  pl.BlockSpec((tm, wdt), lambda i: (i, 0)),
                  pl.BlockSpec((wdt, d), lambda i: (0, 0)),
                  pl.BlockSpec((wdt, d), lambda i: (1, 0)),
                  pl.BlockSpec((1, 1, d), lambda i: ((i * tm) // tokens_per_mod, 0, 0))],
        out_specs=pl.BlockSpec((tm, d), lambda i: (i, 0)),
        compiler_params=_params("parallel"),
        name="out_proj",
    )(x, hg, hy, w, w, gate)


def kernel(x, c, ctx, c_ctx, ada_w, ada_b, norm_ffn1, ffn1_w1, ffn1_w3, ffn1_w2, norm_mix, w_in, hg_lb_logits, hg_norm, hy_conv_w, hy_conv_b, filt_w1, filt_b1, filt_w2, filt_b2, filt_w3, filt_b3, filt_freq, filt_w_out, hy_d, w_out, norm_ffn2, ffn2_w1, ffn2_w3, ffn2_w2, final_norm):
    bsz, n, d = x.shape
    n_ctx = ctx.shape[1]
    depth = ada_w.shape[0]
    assert depth == 1, "single-layer block"
    width = d // 2
    n_seg = w_in.shape[2] // width
    assert n_seg == 8 and n % GRID_W == 0 and n_ctx % SCAN_CHUNK == 0 and width % HEAD_DIM == 0
    d_ff = ffn1_w1.shape[2]
    tf = 512 if d_ff % 512 == 0 else d_ff
    tm = 512
    tm_wide = 1024
    l = 0

    rows = -(-(bsz + 1) // 8) * 8
    cs = jnp.concatenate([c, c_ctx[None, :], jnp.zeros((rows - bsz - 1, d), F32)], axis=0)
    mods = _ada(cs, ada_w[l], ada_b[l]).reshape(rows, N_MOD, d)
    mx = [mods[:bsz, i][:, None, :] for i in range(N_MOD)]
    mc = [mods[bsz:bsz + 1, i][:, None, :] for i in range(N_MOD)]

    bf = lambda a: a.astype(BF16)
    w1a, w3a, w2a = bf(ffn1_w1[l]), bf(ffn1_w3[l]), bf(ffn1_w2[l])
    xt = x.reshape(bsz * n, d)
    yt = ctx.reshape(bsz * n_ctx, d)

    xt = _ffn(xt, mx[0], mx[1], mx[2], norm_ffn1[l], w1a, w3a, w2a, final_norm,
              tokens_per_mod=n, final_norm=False, tm=tm, tf=tf)
    yt = _ffn(yt, mc[0], mc[1], mc[2], norm_ffn1[l], w1a, w3a, w2a, final_norm,
              tokens_per_mod=bsz * n_ctx, final_norm=False, tm=tm, tf=tf)

    w_in_b = bf(w_in[l])
    pc = _proj(yt, mc[3], mc[4], norm_mix[l], w_in_b, hy_conv_w[l], hy_conv_b[l], n_seg=3, seg_w=width,
               conv_from=n_seg, tokens_per_mod=bsz * n_ctx, tm=tm)
    px = _proj(xt, mx[3], mx[4], norm_mix[l], w_in_b, hy_conv_w[l], hy_conv_b[l], n_seg=n_seg, seg_w=width,
               conv_from=5, tokens_per_mod=n, tm=tm_wide)
    cols = n_seg * width

    hg = _hgrn(px.reshape(bsz, n, cols), pc.reshape(bsz, n_ctx, 3 * width), hg_lb_logits, hg_norm[l], width=width)

    consts = _dft_constants(n)
    filt = (filt_w1[l], filt_b1[l], filt_w2[l], filt_b2[l], filt_w3[l], filt_b3[l], filt_freq[l], filt_w_out[l])
    spec = _hyena_spectra(n, width, filt, hy_d[l], consts)
    hy = _hyena_mix(px, spec, consts, bsz=bsz, n=n, width=width)

    xt = _out_proj(xt, hg.reshape(bsz * n, width), hy, bf(w_out[l]), mx[5],
                   tokens_per_mod=n, tm=tm)

    out = _ffn(xt, mx[6], mx[7], mx[8], norm_ffn2[l], bf(ffn2_w1[l]), bf(ffn2_w3[l]), bf(ffn2_w2[l]), final_norm,
               tokens_per_mod=n, final_norm=True, tm=tm, tf=tf)
    return out.reshape(bsz, n, d)
```

```python
import functools
import math

import numpy as np
import jax
import jax.numpy as jnp
from jax import lax
from jax.experimental import pallas as pl
from jax.experimental.pallas import tpu as pltpu

F32 = jnp.float32
BF16 = jnp.bfloat16
EPS = 1e-6
N_MOD = 9
HEAD_DIM = 128
GRID_W = 64
TOKEN_TILE = 512
HIDDEN_TILE = 512
SCAN_CHUNK = 64
PREP_CHUNKS = 4
SCAN_BLOCKS = 2
OUTER_ROWS = 8
BF16_ROWS = 16
INNER_BINS = 8
FILT_EMB = 33
DECAY_TARGET = 1e-2
DECAY_FAST_PCT = 0.3
DECAY_SLOW_PCT = 1.5
VMEM_LIMIT_BYTES = 56 * 1024 * 1024
HIGHEST = lax.Precision.HIGHEST


def _params(*semantics):
    return pltpu.CompilerParams(dimension_semantics=semantics, vmem_limit_bytes=VMEM_LIMIT_BYTES)


def _dot(a, b):
    return jnp.dot(a, b, preferred_element_type=F32)


def _silu(a):
    return a * jax.nn.sigmoid(a)


def _ada_kernel(c_ref, w_ref, b_ref, o_ref):
    h = _silu(c_ref[...]).astype(BF16)
    o_ref[...] = _dot(h, w_ref[...].astype(BF16)) + b_ref[...]


def _ada(cs, w, b):
    rows, d = cs.shape
    n = w.shape[1]
    tn = d // 2
    return pl.pallas_call(
        _ada_kernel,
        out_shape=jax.ShapeDtypeStruct((rows, n), F32),
        grid=(n // tn,),
        in_specs=[pl.BlockSpec((rows, d), lambda j: (0, 0)),
                  pl.BlockSpec((d, tn), lambda j: (0, j)),
                  pl.BlockSpec((1, tn), lambda j: (0, j))],
        out_specs=pl.BlockSpec((rows, tn), lambda j: (0, j)),
        compiler_params=_params("arbitrary"),
        name="ada_mod",
    )(cs, w, b.reshape(1, n))


def _norm_mod(x, gain, shift, scale):
    y = x * lax.rsqrt(jnp.mean(x * x, axis=-1, keepdims=True) + EPS) * gain
    return (y * (1.0 + scale) + shift).astype(BF16)


def _ffn_kernel(x_ref, sh_ref, sc_ref, gt_ref, g_ref, w1_ref, w3_ref, w2_ref, fin_ref, o_ref, h_scr, *, final_norm):
    j = pl.program_id(1)

    @pl.when(j == 0)
    def _():
        h_scr[...] = _norm_mod(x_ref[...], g_ref[...], sh_ref[0], sc_ref[0])
        o_ref[...] = jnp.zeros_like(o_ref)

    h = h_scr[...]
    a = _dot(h, w1_ref[...])
    b = _dot(h, w3_ref[...])
    act = (_silu(a) * b).astype(BF16)
    o_ref[...] += _dot(act, w2_ref[...])

    @pl.when(j == pl.num_programs(1) - 1)
    def _():
        out = x_ref[...] + 0.5 * gt_ref[0] * o_ref[...]
        if final_norm:
            out = out * lax.rsqrt(jnp.mean(out * out, axis=-1, keepdims=True) + EPS) * fin_ref[...]
        o_ref[...] = out


def _ffn(x, shift, scale, gate, gain, w1, w3, w2, fin, *, tokens_per_mod, final_norm, tm, tf):
    t, d = x.shape
    f = w1.shape[1]
    tm = min(tm, tokens_per_mod)
    mod_spec = pl.BlockSpec((1, 1, d), lambda i, j: ((i * tm) // tokens_per_mod, 0, 0))
    vec_spec = pl.BlockSpec((1, d), lambda i, j: (0, 0))
    return pl.pallas_call(
        functools.partial(_ffn_kernel, final_norm=final_norm),
        out_shape=jax.ShapeDtypeStruct((t, d), F32),
        grid=(t // tm, f // tf),
        in_specs=[pl.BlockSpec((tm, d), lambda i, j: (i, 0)),
                  mod_spec, mod_spec, mod_spec, vec_spec,
                  pl.BlockSpec((d, tf), lambda i, j: (0, j)),
                  pl.BlockSpec((d, tf), lambda i, j: (0, j)),
                  pl.BlockSpec((tf, d), lambda i, j: (j, 0)),
                  vec_spec],
        out_specs=pl.BlockSpec((tm, d), lambda i, j: (i, 0)),
        scratch_shapes=[pltpu.VMEM((tm, d), BF16)],
        compiler_params=_params("parallel", "arbitrary"),
        name="swiglu_ffn",
    )(x, shift, scale, gate, gain.reshape(1, d), w1, w3, w2, fin.reshape(1, d))


def _proj_kernel(x_ref, sh_ref, sc_ref, g_ref, w_ref, cw_ref, cb_ref, o_ref, h_scr, *, conv_from):
    j = pl.program_id(1)

    @pl.when(j == 0)
    def _():
        h_scr[...] = _norm_mod(x_ref[...], g_ref[...], sh_ref[0], sc_ref[0])

    p = _dot(h_scr[...], w_ref[...])

    @pl.when(j < conv_from)
    def _():
        o_ref[...] = p

    @pl.when(j >= conv_from)
    def _():
        tm, w = p.shape
        rows3 = lambda t: t.reshape(tm // GRID_W, GRID_W, w)
        y = (rows3(pltpu.roll(p, 1, axis=0)) * cw_ref[0] + rows3(p) * cw_ref[1]
             + rows3(pltpu.roll(p, tm - 1, axis=0)) * cw_ref[2] + cb_ref[...])
        o_ref[...] = y.reshape(tm, w)


def _proj(x, shift, scale, gain, w, conv_w, conv_b, *, n_seg, seg_w, conv_from, tokens_per_mod, tm):
    t, d = x.shape
    tm = min(tm, tokens_per_mod)
    mod_spec = pl.BlockSpec((1, 1, d), lambda i, j: ((i * tm) // tokens_per_mod, 0, 0))
    conv_idx = lambda i, j: (0, jnp.maximum(j - conv_from, 0))
    pos = jnp.arange(GRID_W)[None, :, None]
    edge = jnp.stack([pos > 0, pos >= 0, pos < GRID_W - 1], axis=0)[:, 0]
    conv_w = jnp.where(edge, conv_w[:, None, :], 0.0)
    return pl.pallas_call(
        functools.partial(_proj_kernel, conv_from=conv_from),
        out_shape=jax.ShapeDtypeStruct((t, n_seg * seg_w), F32),
        grid=(t // tm, n_seg),
        in_specs=[pl.BlockSpec((tm, d), lambda i, j: (i, 0)),
                  mod_spec, mod_spec,
                  pl.BlockSpec((1, d), lambda i, j: (0, 0)),
                  pl.BlockSpec((d, seg_w), lambda i, j: (0, j)),
                  pl.BlockSpec((3, GRID_W, seg_w), lambda i, j: (0, 0, jnp.maximum(j - conv_from, 0))),
                  pl.BlockSpec((1, seg_w), conv_idx)],
        out_specs=pl.BlockSpec((tm, seg_w), lambda i, j: (i, j)),
        scratch_shapes=[pltpu.VMEM((tm, d), BF16)],
        compiler_params=_params("parallel", "arbitrary"),
        name="in_proj",
    )(x, shift, scale, gain.reshape(1, d), w, conv_w, conv_b.reshape(1, -1))


def _split3(g):
    g1 = g.astype(BF16)
    r1 = g - g1.astype(F32)
    g2 = r1.astype(BF16)
    return g1, g2, (r1 - g2.astype(F32)).astype(BF16)


def _split_dot(tri, g):
    g1, g2, g3 = _split3(g)
    return _dot(tri, g1) + _dot(tri, g2) + _dot(tri, g3)


def _gates(a, lb):
    f = lb + (1.0 - lb) * jax.nn.sigmoid(a)
    return 1.0 - f, jnp.log(f)


def _tn_dot(a, b):
    return lax.dot_general(a, b, (((0,), (0,)), ((), ())), preferred_element_type=F32)


def _nt_dot(a, b):
    return lax.dot_general(a, b, (((1,), (1,)), ((), ())), preferred_element_type=F32)


def _hgrn_kernel(af_ref, ab_ref, v_ref, q_ref, gate_ref, caf_ref, cab_ref, cv_ref, lbl_ref, gain_ref,
                 o_ref, of_scr, ob_scr, sf_scr, sb_scr, qin_scr, qt_scr, kt_scr, kd_scr, dec_scr,
                 *, n_chunks, n_ctx_chunks):
    c = SCAN_CHUNK
    mid = c // 2
    rows_p = PREP_CHUNKS * c
    prow = lax.broadcasted_iota(jnp.int32, (rows_p, rows_p), 0)
    pcol = lax.broadcasted_iota(jnp.int32, (rows_p, rows_p), 1)
    same = (prow // c) == (pcol // c)
    tri_fwd_p = jnp.where(same & (prow >= pcol), 1.0, 0.0).astype(BF16)
    tri_bwd_p = jnp.where(same & (pcol >= prow), 1.0, 0.0).astype(BF16)

    def lower_bound(direction):
        lg = lbl_ref[direction]
        ex = jnp.exp(lg - jnp.max(lg, axis=0, keepdims=True))
        return ex[0:1, :] / jnp.sum(ex, axis=0, keepdims=True)

    lb_f = lower_bound(0)
    lb_b = lower_bound(1)
    q_scale = HEAD_DIM ** -0.5

    n_blocks = n_chunks // PREP_CHUNKS
    n_steps = n_blocks // SCAN_BLOCKS

    def step_blocks(it):
        fwd = [(0, it * SCAN_BLOCKS + u) for u in range(SCAN_BLOCKS)]
        return fwd + [(1, n_blocks - 1 - bi) for _, bi in fwd]

    def prepare(it, s):
        it = jnp.minimum(it, n_steps - 1)
        blocks = step_blocks(it)
        los = [pl.multiple_of(bi * rows_p, rows_p) for _, bi in blocks]
        gates = [_gates((af_ref, ab_ref)[d][pl.ds(lo, rows_p), :], (lb_f, lb_b)[d]) for (d, _), lo in zip(blocks, los)]
        sums = {}
        for d, tri_p in ((0, tri_fwd_p), (1, tri_bwd_p)):
            slots_d = [slot for slot, (dd, _) in enumerate(blocks) if dd == d]
            wide = _dot(tri_p, jnp.concatenate([p for slot in slots_d for p in _split3(gates[slot][1])], axis=1))
            for n_s, slot in enumerate(slots_d):
                parts = [wide[:, (3 * n_s + m) * HEAD_DIM:(3 * n_s + m + 1) * HEAD_DIM] for m in range(3)]
                sums[slot] = parts[0] + parts[1] + parts[2]
        for slot, ((d, _), lo) in enumerate(zip(blocks, los)):
            end_row = (c - 1, 0)[d]
            k = gates[slot][0]
            b = sums[slot].reshape(PREP_CHUNKS, c, HEAD_DIM)
            b_mid = b[:, mid:mid + 1, :]
            b_end = b[:, end_row:end_row + 1, :]
            q_t = (q_ref[pl.ds(lo, rows_p), :] * q_scale).reshape(PREP_CHUNKS, c, HEAD_DIM) * jnp.exp(b - b_mid)
            k_t = k.reshape(PREP_CHUNKS, c, HEAD_DIM) * jnp.exp(b_mid - b)
            flat = lambda t: t.reshape(rows_p, HEAD_DIM).astype(BF16)
            qt_scr[s, slot] = flat(q_t)
            kt_scr[s, slot] = flat(k_t)
            qin_scr[s, slot] = flat(q_t * jnp.exp(b_mid))
            kd_scr[s, slot] = flat(k_t * jnp.exp(b_end - b_mid))
            dec_scr[s, slot] = jnp.broadcast_to(jnp.exp(b_end), (PREP_CHUNKS, 8, HEAD_DIM))

    zeros_blk = jnp.zeros((c, HEAD_DIM), BF16)

    def block_diag(t):
        return jnp.concatenate(
            [jnp.concatenate([t[j * c:(j + 1) * c] if m == j else zeros_blk for m in range(PREP_CHUNKS)], axis=1)
             for j in range(PREP_CHUNKS)], axis=0)

    masks = (same & (prow >= pcol), same & (pcol >= prow))
    st_refs = (sf_scr, sb_scr)
    out_refs = (of_scr, ob_scr)
    orders = (tuple(range(PREP_CHUNKS)), tuple(range(PREP_CHUNKS - 1, -1, -1)))

    gain = gain_ref[...]

    def scan(it, s, finish):
        streams = step_blocks(it)
        los = [pl.multiple_of(bi * rows_p, rows_p) for _, bi in streams]
        vbs = [v_ref[pl.ds(lo, rows_p), :].astype(BF16) for lo in los]
        scores = [_nt_dot(qt_scr[s, slot], kt_scr[s, slot]) for slot in range(len(streams))]
        upds = [_tn_dot(vb, block_diag(kd_scr[s, slot])) for slot, vb in enumerate(vbs)]
        outs = [_dot(jnp.where(masks[d], sc, 0.0).astype(BF16), vb) for (d, _), sc, vb in zip(streams, scores, vbs)]
        seen = [[None] * PREP_CHUNKS for _ in streams]
        for d in (0, 1):
            st = st_refs[d][...]
            for slot, (ds_, _) in enumerate(streams):
                if ds_ != d:
                    continue
                for j in orders[d]:
                    seen[slot][j] = st.astype(BF16)
                    st = st * dec_scr[s, slot, j][0:1, :] + upds[slot][:, j * HEAD_DIM:(j + 1) * HEAD_DIM]
            st_refs[d][...] = st
        for slot, ((d, _), lo) in enumerate(zip(streams, los)):
            q_in = qin_scr[s, slot]
            carried = jnp.concatenate([_nt_dot(q_in[j * c:(j + 1) * c], seen[slot][j]) for j in range(PREP_CHUNKS)],
                                      axis=0)
            rows = pl.ds(lo, rows_p)
            if finish:
                o = outs[slot] + carried + out_refs[1 - d][rows, :]
                o = o * lax.rsqrt(jnp.mean(o * o, axis=-1, keepdims=True) + EPS) * gain
                o_ref[rows, :] = (o * _silu(gate_ref[rows, :])).astype(o_ref.dtype)
            else:
                out_refs[d][rows, :] = outs[slot] + carried

    n_ctx_blocks = n_ctx_chunks // PREP_CHUNKS
    ctx_state = [jnp.zeros((HEAD_DIM, HEAD_DIM), F32), jnp.zeros((HEAD_DIM, HEAD_DIM), F32)]
    for step in range(n_ctx_blocks):
        upds, decs = [], []
        for d, bi in ((0, step), (1, n_ctx_blocks - 1 - step)):
            a_ref, lb, tri_p, end_row = ((caf_ref, lb_f, tri_fwd_p, c - 1), (cab_ref, lb_b, tri_bwd_p, 0))[d]
            rows_c = pl.ds(bi * rows_p, rows_p)
            k, g = _gates(a_ref[rows_c, :], lb)
            b = _split_dot(tri_p, g).reshape(PREP_CHUNKS, c, HEAD_DIM)
            b_end = b[:, end_row:end_row + 1, :]
            k_d = (k.reshape(PREP_CHUNKS, c, HEAD_DIM) * jnp.exp(b_end - b)).reshape(rows_p, HEAD_DIM).astype(BF16)
            upds.append(_tn_dot(cv_ref[rows_c, :].astype(BF16), block_diag(k_d)))
            decs.append(jnp.exp(b_end))
        for d in (0, 1):
            for j in orders[d]:
                ctx_state[d] = ctx_state[d] * decs[d][j] + upds[d][:, j * HEAD_DIM:(j + 1) * HEAD_DIM]
    sf_scr[...] = ctx_state[0]
    sb_scr[...] = ctx_state[1]

    prepare(0, 0)

    def body(finish):
        def pair(i, carry):
            scan(2 * i, 0, finish)
            prepare(2 * i + 1, 1)
            scan(2 * i + 1, 1, finish)
            prepare(2 * i + 2, 0)
            return carry
        return pair

    lax.fori_loop(0, n_steps // 4, body(False), 0)
    lax.fori_loop(n_steps // 4, n_steps // 2, body(True), 0)


def _hgrn(px, pc, lb_logits, gain, *, width):
    bsz, n, _ = px.shape
    n_ctx = pc.shape[1]
    heads = width // HEAD_DIM
    hd = HEAD_DIM

    def seg(s):
        return pl.BlockSpec((None, n, hd), lambda b, h: (b, 0, s * heads + h))

    def cseg(s):
        return pl.BlockSpec((None, n_ctx, hd), lambda b, h: (b, 0, s * heads + h))

    depth1 = lb_logits.shape[1]
    slots = 2 * SCAN_BLOCKS
    assert (n // SCAN_CHUNK) % (4 * SCAN_BLOCKS * PREP_CHUNKS) == 0, "scan steps come in pairs, per half"
    assert (n_ctx // SCAN_CHUNK) % PREP_CHUNKS == 0, "the context prefix is scanned in whole blocks"
    return pl.pallas_call(
        functools.partial(_hgrn_kernel, n_chunks=n // SCAN_CHUNK, n_ctx_chunks=n_ctx // SCAN_CHUNK),
        out_shape=jax.ShapeDtypeStruct((bsz, n, width), BF16),
        grid=(bsz, heads),
        in_specs=[seg(0), seg(1), seg(2), seg(3), seg(4), cseg(0), cseg(1), cseg(2),
                  pl.BlockSpec((2, depth1, hd), lambda b, h: (0, 0, h)),
                  pl.BlockSpec((1, hd), lambda b, h: (0, h))],
        out_specs=pl.BlockSpec((None, n, hd), lambda b, h: (b, 0, h)),
        scratch_shapes=[pltpu.VMEM((n, hd), F32), pltpu.VMEM((n, hd), F32),
                        pltpu.VMEM((hd, hd), F32), pltpu.VMEM((hd, hd), F32),
                        *[pltpu.VMEM((2, slots, PREP_CHUNKS * SCAN_CHUNK, hd), BF16) for _ in range(4)],
                        pltpu.VMEM((2, slots, PREP_CHUNKS, 8, hd), F32)],
        compiler_params=_params("parallel", "parallel"),
        name="hgrn_scan",
    )(px, px, px, px, px, pc, pc, pc, lb_logits, gain.reshape(1, width))


def _dft_constants(n):
    n2 = GRID_W
    n1h = n // n2
    m2r = np.arange(n2)[None, :]
    k2 = np.arange(n2)[:, None]
    f_ang = -2.0 * np.pi * k2 * m2r / n2
    fr, fi = np.cos(f_ang), np.sin(f_ang)
    f_fwd = np.block([[fr, -fi], [fi, fr]])
    f_inv = np.block([[fr, fi], [-fi, fr]])
    k1 = np.arange(n1h)[:, None, None]
    m1 = np.arange(n1h)[None, :, None]
    m2 = np.arange(n2)[None, None, :]
    theta = -2.0 * np.pi * (k1 + 0.5) * (m1 / (2 * n1h) + m2 / (2 * n))
    parts = np.stack([np.cos(theta), np.sin(theta)], axis=0)
    parts = parts.reshape(2, n1h, n1h, n2 // OUTER_ROWS, OUTER_ROWS)
    eye = np.eye(OUTER_ROWS)
    rows = n1h * OUTER_ROWS
    fwd = np.einsum("pknsj,jm->spkjnm", parts, eye).reshape(n2 // OUTER_ROWS, 2 * rows, rows)
    inv = np.einsum("pknsj,jm->snjpkm", parts, eye).reshape(n2 // OUTER_ROWS, rows, 2 * rows)
    return fwd, inv, f_fwd, f_inv


def _filt_mlp_kernel(z_ref, w1_ref, b1_ref, w2_ref, b2_ref, w3_ref, b3_ref, fr_ref, o_ref):
    fr = fr_ref[...]
    hp = lambda a, b: jnp.dot(a, b, precision=HIGHEST, preferred_element_type=F32)
    hid = jnp.sin(fr * (hp(z_ref[...], w1_ref[...]) + b1_ref[...]))
    hid = jnp.sin(fr * (hp(hid, w2_ref[...]) + b2_ref[...]))
    o_ref[...] = jnp.sin(fr * (hp(hid, w3_ref[...]) + b3_ref[...]))


def _filt_taps_kernel(hid_ref, wo_ref, dl_ref, h_ref, s_ref, *, n):
    h = jnp.dot(hid_ref[...], wo_ref[...], precision=HIGHEST, preferred_element_type=F32)
    t = lax.broadcasted_iota(jnp.int32, h.shape, 0).astype(F32) * (1.0 / max(n - 1, 1))
    hw = h * jnp.exp(-t * dl_ref[...])
    h_ref[...] = hw
    s_ref[0:1, :] = jnp.sum(jnp.abs(hw), axis=0, keepdims=True)
    s_ref[1:2, :] = jnp.abs(hw[0:1, :])


def _spectrum_kernel(af_ref, ab_ref, f_ref, fl_ref, s_ref, d_ref, o_ref, *, n):
    n2 = GRID_W
    l1 = s_ref[0:1, :] + s_ref[2:3, :] - s_ref[3:4, :]
    hb0 = s_ref[4:5, :]
    norm = 1.0 / l1
    scale = 2.0 / (2 * n)
    for i in range(af_ref.shape[1]):
        def inner(a_ref):
            z = _dot3(f_ref[...], fl_ref[...], jnp.concatenate([a_ref[0, i], a_ref[1, i]], axis=0))
            return z[:n2], z[n2:]

        zfr, zfi = inner(af_ref)
        zbr, zbi = inner(ab_ref)
        o_ref[0, i] = scale * ((zfr + zbr - hb0) * norm + d_ref[...])
        o_ref[1, i] = scale * ((zfi - zbi) * norm)


def _dot3(a_hi, a_lo, x):
    x_hi = x.astype(BF16)
    x_lo = (x - x_hi.astype(F32)).astype(BF16)
    return _dot(a_hi, x_hi) + _dot(a_hi, x_lo) + _dot(a_lo, x_hi)


def _hi_lo(a):
    hi = np.asarray(a, np.float64).astype(BF16)
    lo = (a - hi.astype(np.float64)).astype(BF16)
    return jnp.asarray(hi), jnp.asarray(lo)


def _slabs(x):
    return [x[..., h * OUTER_ROWS:(h + 1) * OUTER_ROWS, :] for h in range(x.shape[-2] // OUTER_ROWS)]


def _outer_dft_kernel(*refs, high):
    g_refs, x_ref, o_ref = refs[:-2], refs[-2], refs[-1]
    n1h, _, tl = x_ref.shape
    parts = []
    for h, xs in enumerate(_slabs(x_ref[...])):
        xs = xs.reshape(n1h * OUTER_ROWS, tl)
        r = _dot3(g_refs[0][h], g_refs[1][h], xs) if high else _dot(g_refs[0][h], xs.astype(BF16))
        parts.append(r.reshape(2, n1h, OUTER_ROWS, tl))
    o_ref[...] = jnp.concatenate(parts, axis=2).astype(o_ref.dtype)


def _outer_dft(gs, x4, seg, *, c, tl):
    high = len(gs) == 2
    s, n1h, n2, _ = x4.shape
    tl = min(tl, c)
    per = c // tl
    rows = OUTER_ROWS if high else BF16_ROWS
    g_spec = pl.BlockSpec((rows // OUTER_ROWS,) + gs[0].shape[1:], lambda j, b, l: (j, 0, 0))
    return pl.pallas_call(
        functools.partial(_outer_dft_kernel, high=high),
        out_shape=jax.ShapeDtypeStruct((s, 2, n1h, n2, c), F32 if high else BF16),
        grid=(n2 // rows, s, per),
        in_specs=[g_spec] * len(gs) + [pl.BlockSpec((None, n1h, rows, tl),
                                                    lambda j, b, l: (b, 0, j, seg * per + l))],
        out_specs=pl.BlockSpec((None, 2, n1h, rows, tl), lambda j, b, l: (b, 0, 0, j, l)),
        compiler_params=_params("parallel", "parallel", "parallel"),
        name="outer_dft",
    )(*gs, x4)


def _hyena_spectra(n, width, filt, hy_d, consts):
    w1, b1, w2, b2, w3, b3, freq, w_out = filt
    gt_fwd, _, f_fwd, _ = consts
    n1h = n // GRID_W
    order = w1.shape[1]
    pos = np.arange(n, dtype=np.float64)
    t = pos / max(n - 1, 1)
    bands = (FILT_EMB - 1) // 2
    fb = np.linspace(1e-4, bands - 1, bands)
    ang = (2 * math.pi / n) * pos[:, None] * fb[None, :]
    z = np.concatenate([t[:, None], np.cos(ang), -np.sin(ang)], axis=-1)
    emb_pad = 128
    z = jnp.asarray(np.pad(z, ((0, 0), (0, emb_pad - FILT_EMB))), F32)
    w1p = jnp.pad(w1, ((0, emb_pad - FILT_EMB), (0, 0)))
    row = lambda a: a.reshape(1, -1)
    full = lambda shape: pl.BlockSpec(shape, lambda: tuple(0 for _ in shape))
    hid = pl.pallas_call(
        _filt_mlp_kernel,
        out_shape=jax.ShapeDtypeStruct((n, order), F32),
        in_specs=[full((n, emb_pad)), full((emb_pad, order)), full((1, order)), full((order, order)),
                  full((1, order)), full((order, order)), full((1, order)), full((1, order))],
        out_specs=full((n, order)),
        compiler_params=pltpu.CompilerParams(vmem_limit_bytes=VMEM_LIMIT_BYTES),
        name="filter_mlp",
    )(z, w1p, row(b1), w2, row(b2), w3, row(b3), row(freq))

    cols = 4 * width
    deltas = np.abs(np.linspace(math.log(DECAY_TARGET) / DECAY_SLOW_PCT,
                                math.log(DECAY_TARGET) / DECAY_FAST_PCT, width))
    deltas4 = jnp.asarray(np.tile(deltas, 4)[None, :], F32)
    tc = min(512, cols)
    taps, sums = pl.pallas_call(
        functools.partial(_filt_taps_kernel, n=n),
        out_shape=(jax.ShapeDtypeStruct((n, cols), F32), jax.ShapeDtypeStruct((2, cols), F32)),
        grid=(cols // tc,),
        in_specs=[pl.BlockSpec((n, order), lambda j: (0, 0)),
                  pl.BlockSpec((order, tc), lambda j: (0, j)),
                  pl.BlockSpec((1, tc), lambda j: (0, j))],
        out_specs=(pl.BlockSpec((n, tc), lambda j: (0, j)), pl.BlockSpec((2, tc), lambda j: (0, j))),
        compiler_params=_params("parallel"),
        name="filter_taps",
    )(hid, w_out, deltas4)

    a = _outer_dft(_hi_lo(gt_fwd), taps.reshape(1, n1h, GRID_W, cols), 0, c=cols, tl=1024)[0]
    s4 = sums.reshape(2, 2, 2, width)
    bwd0 = taps[0].reshape(2, 2, width)[:, 1]
    stats = jnp.stack([s4[0, :, 0], s4[1, :, 0], s4[0, :, 1], s4[1, :, 1], bwd0], axis=1)

    kb = min(INNER_BINS, n1h)

    def tap_spec(side):
        return pl.BlockSpec((2, kb, GRID_W, width), lambda k1, f: (0, k1, 0, 2 * f + side))

    f_hi, f_lo = _hi_lo(f_fwd)
    f_spec = pl.BlockSpec((2 * GRID_W, 2 * GRID_W), lambda k1, f: (0, 0))
    return pl.pallas_call(
        functools.partial(_spectrum_kernel, n=n),
        out_shape=jax.ShapeDtypeStruct((2, 2, n1h, GRID_W, width), F32),
        grid=(n1h // kb, 2),
        in_specs=[tap_spec(0), tap_spec(1), f_spec, f_spec,
                  pl.BlockSpec((None, 5, width), lambda k1, f: (f, 0, 0)),
                  pl.BlockSpec((None, 1, width), lambda k1, f: (f, 0, 0))],
        out_specs=pl.BlockSpec((None, 2, kb, GRID_W, width), lambda k1, f: (f, 0, k1, 0, 0)),
        compiler_params=_params("parallel", "parallel"),
        name="filter_spectrum",
    )(a, a, f_hi, f_lo, stats, hy_d.reshape(2, 1, width))


def _inner_conv_kernel(a_ref, f_ref, fi_ref, h_ref, o_ref):
    n2 = GRID_W
    for i in range(a_ref.shape[1]):
        z = _dot(f_ref[...], jnp.concatenate([a_ref[0, i], a_ref[1, i]], axis=0))
        zr, zi = z[:n2], z[n2:]
        hr, hi = h_ref[0, i], h_ref[1, i]
        stacked = jnp.concatenate([zr * hr - zi * hi, zr * hi + zi * hr], axis=0).astype(BF16)
        w = _dot(fi_ref[...], stacked).astype(BF16)
        o_ref[0, i] = w[:n2]
        o_ref[1, i] = w[n2:]


def _inner_conv(a, f_fwd, f_inv, spec):
    bsz, _, n1h, n2, c = a.shape
    kb = min(INNER_BINS, n1h)
    return pl.pallas_call(
        _inner_conv_kernel,
        out_shape=jax.ShapeDtypeStruct(a.shape, BF16),
        grid=(n1h // kb, bsz),
        in_specs=[pl.BlockSpec((None, 2, kb, n2, c), lambda k1, b: (b, 0, k1, 0, 0)),
                  pl.BlockSpec((2 * n2, 2 * n2), lambda k1, b: (0, 0)),
                  pl.BlockSpec((2 * n2, 2 * n2), lambda k1, b: (0, 0)),
                  pl.BlockSpec((2, kb, n2, c), lambda k1, b: (0, k1, 0, 0))],
        out_specs=pl.BlockSpec((None, 2, kb, n2, c), lambda k1, b: (b, 0, k1, 0, 0)),
        compiler_params=_params("parallel", "arbitrary"),
        name="inner_conv",
    )(a, f_fwd, f_inv, spec)


def _outer_idft_gate_kernel(gi_ref, g_ref, b_ref, x_ref, o_ref, *, again):
    n1h, _, tl = x_ref.shape
    parts = []
    for h, (bs, xs) in enumerate(zip(_slabs(b_ref[...].astype(F32)), _slabs(x_ref[...]))):
        stacked = bs.reshape(2 * n1h * OUTER_ROWS, tl).astype(BF16)
        y = _dot(gi_ref[h], stacked) * xs.reshape(n1h * OUTER_ROWS, tl)
        if again:
            parts.append(_dot(g_ref[h], y.astype(BF16)).reshape(2, n1h, OUTER_ROWS, tl))
        else:
            parts.append(y.reshape(n1h, OUTER_ROWS, tl))
    o_ref[...] = jnp.concatenate(parts, axis=-2).astype(o_ref.dtype)


def _outer_idft_gate(g_inv, g_fwd, b, px4, seg, *, again, tl):
    bsz, _, n1h, n2, c = b.shape
    tl = min(tl, c)
    per = c // tl
    rows = BF16_ROWS
    if again:
        out_shape = jax.ShapeDtypeStruct((bsz, 2, n1h, n2, c), BF16)
        out_spec = pl.BlockSpec((None, 2, n1h, rows, tl), lambda j, bb, l: (bb, 0, 0, j, l))
    else:
        out_shape = jax.ShapeDtypeStruct((bsz, n1h, n2, c), BF16)
        out_spec = pl.BlockSpec((None, n1h, rows, tl), lambda j, bb, l: (bb, 0, j, l))
    return pl.pallas_call(
        functools.partial(_outer_idft_gate_kernel, again=again),
        out_shape=out_shape,
        grid=(n2 // rows, bsz, per),
        in_specs=[pl.BlockSpec((rows // OUTER_ROWS,) + g_inv.shape[1:], lambda j, bb, l: (j, 0, 0)),
                  pl.BlockSpec((rows // OUTER_ROWS,) + g_fwd.shape[1:], lambda j, bb, l: (j, 0, 0)),
                  pl.BlockSpec((None, 2, n1h, rows, tl), lambda j, bb, l: (bb, 0, 0, j, l)),
                  pl.BlockSpec((None, n1h, rows, tl), lambda j, bb, l: (bb, 0, j, seg * per + l))],
        out_specs=out_spec,
        compiler_params=_params("parallel", "parallel", "parallel"),
        name="outer_idft_gate",
    )(g_inv, g_fwd, b, px4)


def _hyena_mix(px, spec, consts, *, bsz, n, width):
    g_fwd, g_inv, f_fwd, f_inv = consts
    g_fwd_b, g_inv_b = jnp.asarray(g_fwd, BF16), jnp.asarray(g_inv, BF16)
    f_fwd_b, f_inv_b = jnp.asarray(f_fwd, BF16), jnp.asarray(f_inv, BF16)
    n1h = n // GRID_W
    px4 = px.reshape(bsz, n1h, GRID_W, -1)
    tl = width
    a = _outer_dft((g_fwd_b,), px4, 5, c=width, tl=tl)
    a = _inner_conv(a, f_fwd_b, f_inv_b, spec[0])
    a = _outer_idft_gate(g_inv_b, g_fwd_b, a, px4, 6, again=True, tl=tl)
    a = _inner_conv(a, f_fwd_b, f_inv_b, spec[1])
    hy = _outer_idft_gate(g_inv_b, g_fwd_b, a, px4, 7, again=False, tl=tl)
    return hy.reshape(bsz * n, width)


def _out_proj_kernel(x_ref, hg_ref, hy_ref, wa_ref, wb_ref, gt_ref, o_ref):
    mix = _dot(hg_ref[...], wa_ref[...]) + _dot(hy_ref[...], wb_ref[...])
    o_ref[...] = x_ref[...] + gt_ref[0] * mix


def _out_proj(x, hg, hy, w, gate, *, tokens_per_mod, tm):
    t, d = x.shape
    wdt = hg.shape[1]
    tm = min(tm, tokens_per_mod)
    return pl.pallas_call(
        _out_proj_kernel,
        out_shape=jax.ShapeDtypeStruct((t, d), F32),
        grid=(t // tm,),
        in_specs=[pl.BlockSpec((tm, d), lambda i: (i, 0)),
                  pl.BlockSpec((tm, wdt), lambda i: (i, 0)),
                  pl.BlockSpec((tm, wdt), lambda i: (i, 0)),
                  pl.BlockSpec((wdt, d), lambda i: (0, 0)),
                  pl.BlockSpec((wdt, d), lambda i: (1, 0)),
                  pl.BlockSpec((1, 1, d), lambda i: ((i * tm) // tokens_per_mod, 0, 0))],
        out_specs=pl.BlockSpec((tm, d), lambda i: (i, 0)),
        compiler_params=_params("parallel"),
        name="out_proj",
    )(x, hg, hy, w, w, gate)


def _tiles(d_ff):
    return TOKEN_TILE, 2 * TOKEN_TILE, HIDDEN_TILE if d_ff % HIDDEN_TILE == 0 else d_ff


def kernel(x, c, ctx, c_ctx, ada_w, ada_b, norm_ffn1, ffn1_w1, ffn1_w3, ffn1_w2, norm_mix, w_in, hg_lb_logits, hg_norm, hy_conv_w, hy_conv_b, filt_w1, filt_b1, filt_w2, filt_b2, filt_w3, filt_b3, filt_freq, filt_w_out, hy_d, w_out, norm_ffn2, ffn2_w1, ffn2_w3, ffn2_w2, final_norm):
    bsz, n, d = x.shape
    n_ctx = ctx.shape[1]
    depth = ada_w.shape[0]
    assert depth == 1, "single-layer block"
    width = d // 2
    n_seg = w_in.shape[2] // width
    assert n_seg == 8 and n % GRID_W == 0 and n_ctx % SCAN_CHUNK == 0 and width % HEAD_DIM == 0
    tm, tm_wide, tf = _tiles(ffn1_w1.shape[2])
    l = 0

    rows = -(-(bsz + 1) // 8) * 8
    cs = jnp.concatenate([c, c_ctx[None, :], jnp.zeros((rows - bsz - 1, d), F32)], axis=0)
    mods = _ada(cs, ada_w[l], ada_b[l]).reshape(rows, N_MOD, d)
    mx = [mods[:bsz, i][:, None, :] for i in range(N_MOD)]
    mc = [mods[bsz:bsz + 1, i][:, None, :] for i in range(N_MOD)]

    bf = lambda a: a.astype(BF16)
    w1a, w3a, w2a = bf(ffn1_w1[l]), bf(ffn1_w3[l]), bf(ffn1_w2[l])
    xt = x.reshape(bsz * n, d)
    yt = ctx.reshape(bsz * n_ctx, d)

    xt = _ffn(xt, mx[0], mx[1], mx[2], norm_ffn1[l], w1a, w3a, w2a, final_norm,
              tokens_per_mod=n, final_norm=False, tm=tm, tf=tf)
    yt = _ffn(yt, mc[0], mc[1], mc[2], norm_ffn1[l], w1a, w3a, w2a, final_norm,
              tokens_per_mod=bsz * n_ctx, final_norm=False, tm=tm, tf=tf)

    w_in_b = bf(w_in[l])
    pc = _proj(yt, mc[3], mc[4], norm_mix[l], w_in_b, hy_conv_w[l], hy_conv_b[l], n_seg=3, seg_w=width,
               conv_from=n_seg, tokens_per_mod=bsz * n_ctx, tm=tm)
    px = _proj(xt, mx[3], mx[4], norm_mix[l], w_in_b, hy_conv_w[l], hy_conv_b[l], n_seg=n_seg, seg_w=width,
               conv_from=5, tokens_per_mod=n, tm=tm_wide)
    cols = n_seg * width

    hg = _hgrn(px.reshape(bsz, n, cols), pc.reshape(bsz, n_ctx, 3 * width), hg_lb_logits, hg_norm[l], width=width)

    consts = _dft_constants(n)
    filt = (filt_w1[l], filt_b1[l], filt_w2[l], filt_b2[l], filt_w3[l], filt_b3[l], filt_freq[l], filt_w_out[l])
    spec = _hyena_spectra(n, width, filt, hy_d[l], consts)
    hy = _hyena_mix(px, spec, consts, bsz=bsz, n=n, width=width)

    xt = _out_proj(xt, hg.reshape(bsz * n, width), hy, bf(w_out[l]), mx[5],
                   tokens_per_mod=n, tm=tm)

    out = _ffn(xt, mx[6], mx[7], mx[8], norm_ffn2[l], bf(ffn2_w1[l]), bf(ffn2_w3[l]), bf(ffn2_w2[l]), final_norm,
               tokens_per_mod=n, final_norm=True, tm=tm, tf=tf)
    return out.reshape(bsz, n, d)
```

```python
import functools
import math

import numpy as np
import jax
import jax.numpy as jnp
from jax import lax
from jax.experimental import pallas as pl
from jax.experimental.pallas import tpu as pltpu

F32 = jnp.float32
BF16 = jnp.bfloat16
EPS = 1e-6
N_MOD = 9
HEAD_DIM = 128
GRID_W = 64
CAST_ROWS = 256
TOKEN_TILE = 512
HIDDEN_TILE = 512
SCAN_CHUNK = 64
PREP_CHUNKS = 4
SCAN_BLOCKS = 2
OUTER_ROWS = 8
BF16_ROWS = 16
INNER_BINS = 8
FILT_EMB = 33
DECAY_TARGET = 1e-2
DECAY_FAST_PCT = 0.3
DECAY_SLOW_PCT = 1.5
VMEM_LIMIT_BYTES = 56 * 1024 * 1024
HIGHEST = lax.Precision.HIGHEST


def _params(*semantics):
    return pltpu.CompilerParams(dimension_semantics=semantics, vmem_limit_bytes=VMEM_LIMIT_BYTES)


def _dot(a, b):
    return jnp.dot(a, b, preferred_element_type=F32)


def _silu(a):
    return a * jax.nn.sigmoid(a)


def _cast_kernel(x_ref, o_ref):
    o_ref[...] = x_ref[...].astype(BF16)


def _to_bf16(w):
    r, c = w.shape
    rb = CAST_ROWS if r % CAST_ROWS == 0 else r
    return pl.pallas_call(
        _cast_kernel,
        out_shape=jax.ShapeDtypeStruct((r, c), BF16),
        grid=(r // rb,),
        in_specs=[pl.BlockSpec((rb, c), lambda i: (i, 0))],
        out_specs=pl.BlockSpec((rb, c), lambda i: (i, 0)),
        compiler_params=_params("parallel"),
        name="to_bf16",
    )(w)


def _ada_kernel(c_ref, w_ref, b_ref, o_ref):
    h = _silu(c_ref[...]).astype(BF16)
    o_ref[...] = _dot(h, w_ref[...].astype(BF16)) + b_ref[...]


def _ada(cs, w, b):
    rows, d = cs.shape
    n = w.shape[1]
    tn = d // 2
    return pl.pallas_call(
        _ada_kernel,
        out_shape=jax.ShapeDtypeStruct((rows, n), F32),
        grid=(n // tn,),
        in_specs=[pl.BlockSpec((rows, d), lambda j: (0, 0)),
                  pl.BlockSpec((d, tn), lambda j: (0, j)),
                  pl.BlockSpec((1, tn), lambda j: (0, j))],
        out_specs=pl.BlockSpec((rows, tn), lambda j: (0, j)),
        compiler_params=_params("arbitrary"),
        name="ada_mod",
    )(cs, w, b.reshape(1, n))


def _norm_mod(x, gain, shift, scale):
    y = x * lax.rsqrt(jnp.mean(x * x, axis=-1, keepdims=True) + EPS) * gain
    return (y * (1.0 + scale) + shift).astype(BF16)


def _ffn_kernel(x_ref, sh_ref, sc_ref, gt_ref, g_ref, w1_ref, w3_ref, w2_ref, fin_ref, o_ref, h_scr, *, final_norm):
    j = pl.program_id(1)

    @pl.when(j == 0)
    def _():
        h_scr[...] = _norm_mod(x_ref[...], g_ref[...], sh_ref[0], sc_ref[0])
        o_ref[...] = jnp.zeros_like(o_ref)

    h = h_scr[...]
    a = _dot(h, w1_ref[...])
    b = _dot(h, w3_ref[...])
    act = (_silu(a) * b).astype(BF16)
    o_ref[...] += _dot(act, w2_ref[...])

    @pl.when(j == pl.num_programs(1) - 1)
    def _():
        out = x_ref[...] + 0.5 * gt_ref[0] * o_ref[...]
        if final_norm:
            out = out * lax.rsqrt(jnp.mean(out * out, axis=-1, keepdims=True) + EPS) * fin_ref[...]
        o_ref[...] = out


def _ffn(x, shift, scale, gate, gain, w1, w3, w2, fin, *, tokens_per_mod, final_norm, tm, tf):
    t, d = x.shape
    f = w1.shape[1]
    tm = min(tm, tokens_per_mod)
    mod_spec = pl.BlockSpec((1, 1, d), lambda i, j: ((i * tm) // tokens_per_mod, 0, 0))
    vec_spec = pl.BlockSpec((1, d), lambda i, j: (0, 0))
    return pl.pallas_call(
        functools.partial(_ffn_kernel, final_norm=final_norm),
        out_shape=jax.ShapeDtypeStruct((t, d), F32),
        grid=(t // tm, f // tf),
        in_specs=[pl.BlockSpec((tm, d), lambda i, j: (i, 0)),
                  mod_spec, mod_spec, mod_spec, vec_spec,
                  pl.BlockSpec((d, tf), lambda i, j: (0, j)),
                  pl.BlockSpec((d, tf), lambda i, j: (0, j)),
                  pl.BlockSpec((tf, d), lambda i, j: (j, 0)),
                  vec_spec],
        out_specs=pl.BlockSpec((tm, d), lambda i, j: (i, 0)),
        scratch_shapes=[pltpu.VMEM((tm, d), BF16)],
        compiler_params=_params("parallel", "arbitrary"),
        name="swiglu_ffn",
    )(x, shift, scale, gate, gain.reshape(1, d), w1, w3, w2, fin.reshape(1, d))


def _proj_kernel(x_ref, sh_ref, sc_ref, g_ref, w_ref, cw_ref, cb_ref, o_ref, h_scr, *, conv_from):
    j = pl.program_id(1)

    @pl.when(j == 0)
    def _():
        h_scr[...] = _norm_mod(x_ref[...], g_ref[...], sh_ref[0], sc_ref[0])

    p = _dot(h_scr[...], w_ref[...])

    @pl.when(j < conv_from)
    def _():
        o_ref[...] = p

    @pl.when(j >= conv_from)
    def _():
        tm, w = p.shape
        rows3 = lambda t: t.reshape(tm // GRID_W, GRID_W, w)
        y = (rows3(pltpu.roll(p, 1, axis=0)) * cw_ref[0] + rows3(p) * cw_ref[1]
             + rows3(pltpu.roll(p, tm - 1, axis=0)) * cw_ref[2] + cb_ref[...])
        o_ref[...] = y.reshape(tm, w)


def _proj(x, shift, scale, gain, w, conv_w, conv_b, *, n_seg, seg_w, conv_from, tokens_per_mod, tm):
    t, d = x.shape
    tm = min(tm, tokens_per_mod)
    mod_spec = pl.BlockSpec((1, 1, d), lambda i, j: ((i * tm) // tokens_per_mod, 0, 0))
    conv_idx = lambda i, j: (0, jnp.maximum(j - conv_from, 0))
    pos = jnp.arange(GRID_W)[None, :, None]
    edge = jnp.stack([pos > 0, pos >= 0, pos < GRID_W - 1], axis=0)[:, 0]
    conv_w = jnp.where(edge, conv_w[:, None, :], 0.0)
    return pl.pallas_call(
        functools.partial(_proj_kernel, conv_from=conv_from),
        out_shape=jax.ShapeDtypeStruct((t, n_seg * seg_w), F32),
        grid=(t // tm, n_seg),
        in_specs=[pl.BlockSpec((tm, d), lambda i, j: (i, 0)),
                  mod_spec, mod_spec,
                  pl.BlockSpec((1, d), lambda i, j: (0, 0)),
                  pl.BlockSpec((d, seg_w), lambda i, j: (0, j)),
                  pl.BlockSpec((3, GRID_W, seg_w), lambda i, j: (0, 0, jnp.maximum(j - conv_from, 0))),
                  pl.BlockSpec((1, seg_w), conv_idx)],
        out_specs=pl.BlockSpec((tm, seg_w), lambda i, j: (i, j)),
        scratch_shapes=[pltpu.VMEM((tm, d), BF16)],
        compiler_params=_params("parallel", "arbitrary"),
        name="in_proj",
    )(x, shift, scale, gain.reshape(1, d), w, conv_w, conv_b.reshape(1, -1))


def _split3(g):
    g1 = g.astype(BF16)
    r1 = g - g1.astype(F32)
    g2 = r1.astype(BF16)
    return g1, g2, (r1 - g2.astype(F32)).astype(BF16)


def _split_dot(tri, g):
    g1, g2, g3 = _split3(g)
    return _dot(tri, g1) + _dot(tri, g2) + _dot(tri, g3)


def _gates(a, lb):
    f = lb + (1.0 - lb) * jax.nn.sigmoid(a)
    return 1.0 - f, jnp.log(f)


def _tn_dot(a, b):
    return lax.dot_general(a, b, (((0,), (0,)), ((), ())), preferred_element_type=F32)


def _nt_dot(a, b):
    return lax.dot_general(a, b, (((1,), (1,)), ((), ())), preferred_element_type=F32)


def _hgrn_kernel(af_ref, ab_ref, v_ref, q_ref, gate_ref, caf_ref, cab_ref, cv_ref, lbl_ref, gain_ref,
                 o_ref, of_scr, ob_scr, sf_scr, sb_scr, qin_scr, qt_scr, kt_scr, kd_scr, dec_scr,
                 *, n_chunks, n_ctx_chunks):
    c = SCAN_CHUNK
    mid = c // 2
    rows_p = PREP_CHUNKS * c
    prow = lax.broadcasted_iota(jnp.int32, (rows_p, rows_p), 0)
    pcol = lax.broadcasted_iota(jnp.int32, (rows_p, rows_p), 1)
    same = (prow // c) == (pcol // c)
    tri_fwd_p = jnp.where(same & (prow >= pcol), 1.0, 0.0).astype(BF16)
    tri_bwd_p = jnp.where(same & (pcol >= prow), 1.0, 0.0).astype(BF16)

    def lower_bound(direction):
        lg = lbl_ref[direction]
        ex = jnp.exp(lg - jnp.max(lg, axis=0, keepdims=True))
        return ex[0:1, :] / jnp.sum(ex, axis=0, keepdims=True)

    lb_f = lower_bound(0)
    lb_b = lower_bound(1)
    q_scale = HEAD_DIM ** -0.5

    n_blocks = n_chunks // PREP_CHUNKS
    n_steps = n_blocks // SCAN_BLOCKS

    def step_blocks(it):
        fwd = [(0, it * SCAN_BLOCKS + u) for u in range(SCAN_BLOCKS)]
        return fwd + [(1, n_blocks - 1 - bi) for _, bi in fwd]

    def prepare(it, s):
        it = jnp.minimum(it, n_steps - 1)
        blocks = step_blocks(it)
        los = [pl.multiple_of(bi * rows_p, rows_p) for _, bi in blocks]
        gates = [_gates((af_ref, ab_ref)[d][pl.ds(lo, rows_p), :], (lb_f, lb_b)[d]) for (d, _), lo in zip(blocks, los)]
        sums = {}
        for d, tri_p in ((0, tri_fwd_p), (1, tri_bwd_p)):
            slots_d = [slot for slot, (dd, _) in enumerate(blocks) if dd == d]
            wide = _dot(tri_p, jnp.concatenate([p for slot in slots_d for p in _split3(gates[slot][1])], axis=1))
            for n_s, slot in enumerate(slots_d):
                parts = [wide[:, (3 * n_s + m) * HEAD_DIM:(3 * n_s + m + 1) * HEAD_DIM] for m in range(3)]
                sums[slot] = parts[0] + parts[1] + parts[2]
        for slot, ((d, _), lo) in enumerate(zip(blocks, los)):
            end_row = (c - 1, 0)[d]
            k = gates[slot][0]
            b = sums[slot].reshape(PREP_CHUNKS, c, HEAD_DIM)
            b_mid = b[:, mid:mid + 1, :]
            b_end = b[:, end_row:end_row + 1, :]
            q_t = (q_ref[pl.ds(lo, rows_p), :] * q_scale).reshape(PREP_CHUNKS, c, HEAD_DIM) * jnp.exp(b - b_mid)
            k_t = k.reshape(PREP_CHUNKS, c, HEAD_DIM) * jnp.exp(b_mid - b)
            flat = lambda t: t.reshape(rows_p, HEAD_DIM).astype(BF16)
            qt_scr[s, slot] = flat(q_t)
            kt_scr[s, slot] = flat(k_t)
            qin_scr[s, slot] = flat(q_t * jnp.exp(b_mid))
            kd_scr[s, slot] = flat(k_t * jnp.exp(b_end - b_mid))
            dec_scr[s, slot] = jnp.broadcast_to(jnp.exp(b_end), (PREP_CHUNKS, 8, HEAD_DIM))

    zeros_blk = jnp.zeros((c, HEAD_DIM), BF16)

    def block_diag(t):
        return jnp.concatenate(
            [jnp.concatenate([t[j * c:(j + 1) * c] if m == j else zeros_blk for m in range(PREP_CHUNKS)], axis=1)
             for j in range(PREP_CHUNKS)], axis=0)

    masks = (same & (prow >= pcol), same & (pcol >= prow))
    st_refs = (sf_scr, sb_scr)
    out_refs = (of_scr, ob_scr)
    orders = (tuple(range(PREP_CHUNKS)), tuple(range(PREP_CHUNKS - 1, -1, -1)))

    gain = gain_ref[...]

    def scan(it, s, finish):
        streams = step_blocks(it)
        los = [pl.multiple_of(bi * rows_p, rows_p) for _, bi in streams]
        vbs = [v_ref[pl.ds(lo, rows_p), :].astype(BF16) for lo in los]
        scores = [_nt_dot(qt_scr[s, slot], kt_scr[s, slot]) for slot in range(len(streams))]
        upds = [_tn_dot(vb, block_diag(kd_scr[s, slot])) for slot, vb in enumerate(vbs)]
        outs = [_dot(jnp.where(masks[d], sc, 0.0).astype(BF16), vb) for (d, _), sc, vb in zip(streams, scores, vbs)]
        seen = [[None] * PREP_CHUNKS for _ in streams]
        for d in (0, 1):
            st = st_refs[d][...]
            for slot, (ds_, _) in enumerate(streams):
                if ds_ != d:
                    continue
                for j in orders[d]:
                    seen[slot][j] = st.astype(BF16)
                    st = st * dec_scr[s, slot, j][0:1, :] + upds[slot][:, j * HEAD_DIM:(j + 1) * HEAD_DIM]
            st_refs[d][...] = st
        for slot, ((d, _), lo) in enumerate(zip(streams, los)):
            q_in = qin_scr[s, slot]
            carried = jnp.concatenate([_nt_dot(q_in[j * c:(j + 1) * c], seen[slot][j]) for j in range(PREP_CHUNKS)],
                                      axis=0)
            rows = pl.ds(lo, rows_p)
            if finish:
                o = outs[slot] + carried + out_refs[1 - d][rows, :]
                o = o * lax.rsqrt(jnp.mean(o * o, axis=-1, keepdims=True) + EPS) * gain
                o_ref[rows, :] = (o * _silu(gate_ref[rows, :])).astype(o_ref.dtype)
            else:
                out_refs[d][rows, :] = outs[slot] + carried

    n_ctx_blocks = n_ctx_chunks // PREP_CHUNKS
    ctx_state = [jnp.zeros((HEAD_DIM, HEAD_DIM), F32), jnp.zeros((HEAD_DIM, HEAD_DIM), F32)]
    for step in range(n_ctx_blocks):
        upds, decs = [], []
        for d, bi in ((0, step), (1, n_ctx_blocks - 1 - step)):
            a_ref, lb, tri_p, end_row = ((caf_ref, lb_f, tri_fwd_p, c - 1), (cab_ref, lb_b, tri_bwd_p, 0))[d]
            rows_c = pl.ds(bi * rows_p, rows_p)
            k, g = _gates(a_ref[rows_c, :], lb)
            b = _split_dot(tri_p, g).reshape(PREP_CHUNKS, c, HEAD_DIM)
            b_end = b[:, end_row:end_row + 1, :]
            k_d = (k.reshape(PREP_CHUNKS, c, HEAD_DIM) * jnp.exp(b_end - b)).reshape(rows_p, HEAD_DIM).astype(BF16)
            upds.append(_tn_dot(cv_ref[rows_c, :].astype(BF16), block_diag(k_d)))
            decs.append(jnp.exp(b_end))
        for d in (0, 1):
            for j in orders[d]:
                ctx_state[d] = ctx_state[d] * decs[d][j] + upds[d][:, j * HEAD_DIM:(j + 1) * HEAD_DIM]
    sf_scr[...] = ctx_state[0]
    sb_scr[...] = ctx_state[1]

    prepare(0, 0)

    def body(finish):
        def pair(i, carry):
            scan(2 * i, 0, finish)
            prepare(2 * i + 1, 1)
            scan(2 * i + 1, 1, finish)
            prepare(2 * i + 2, 0)
            return carry
        return pair

    lax.fori_loop(0, n_steps // 4, body(False), 0)
    lax.fori_loop(n_steps // 4, n_steps // 2, body(True), 0)


def _hgrn(px, pc, lb_logits, gain, *, width):
    bsz, n, _ = px.shape
    n_ctx = pc.shape[1]
    heads = width // HEAD_DIM
    hd = HEAD_DIM

    def seg(s):
        return pl.BlockSpec((None, n, hd), lambda b, h: (b, 0, s * heads + h))

    def cseg(s):
        return pl.BlockSpec((None, n_ctx, hd), lambda b, h: (b, 0, s * heads + h))

    depth1 = lb_logits.shape[1]
    slots = 2 * SCAN_BLOCKS
    assert (n // SCAN_CHUNK) % (4 * SCAN_BLOCKS * PREP_CHUNKS) == 0, "scan steps come in pairs, per half"
    assert (n_ctx // SCAN_CHUNK) % PREP_CHUNKS == 0, "the context prefix is scanned in whole blocks"
    return pl.pallas_call(
        functools.partial(_hgrn_kernel, n_chunks=n // SCAN_CHUNK, n_ctx_chunks=n_ctx // SCAN_CHUNK),
        out_shape=jax.ShapeDtypeStruct((bsz, n, width), BF16),
        grid=(bsz, heads),
        in_specs=[seg(0), seg(1), seg(2), seg(3), seg(4), cseg(0), cseg(1), cseg(2),
                  pl.BlockSpec((2, depth1, hd), lambda b, h: (0, 0, h)),
                  pl.BlockSpec((1, hd), lambda b, h: (0, h))],
        out_specs=pl.BlockSpec((None, n, hd), lambda b, h: (b, 0, h)),
        scratch_shapes=[pltpu.VMEM((n, hd), F32), pltpu.VMEM((n, hd), F32),
                        pltpu.VMEM((hd, hd), F32), pltpu.VMEM((hd, hd), F32),
                        *[pltpu.VMEM((2, slots, PREP_CHUNKS * SCAN_CHUNK, hd), BF16) for _ in range(4)],
                        pltpu.VMEM((2, slots, PREP_CHUNKS, 8, hd), F32)],
        compiler_params=_params("parallel", "parallel"),
        name="hgrn_scan",
    )(px, px, px, px, px, pc, pc, pc, lb_logits, gain.reshape(1, width))


def _dft_constants(n):
    n2 = GRID_W
    n1h = n // n2
    m2r = np.arange(n2)[None, :]
    k2 = np.arange(n2)[:, None]
    f_ang = -2.0 * np.pi * k2 * m2r / n2
    fr, fi = np.cos(f_ang), np.sin(f_ang)
    f_fwd = np.block([[fr, -fi], [fi, fr]])
    f_inv = np.block([[fr, fi], [-fi, fr]])
    k1 = np.arange(n1h)[:, None, None]
    m1 = np.arange(n1h)[None, :, None]
    m2 = np.arange(n2)[None, None, :]
    theta = -2.0 * np.pi * (k1 + 0.5) * (m1 / (2 * n1h) + m2 / (2 * n))
    parts = np.stack([np.cos(theta), np.sin(theta)], axis=0)
    parts = parts.reshape(2, n1h, n1h, n2 // OUTER_ROWS, OUTER_ROWS)
    eye = np.eye(OUTER_ROWS)
    rows = n1h * OUTER_ROWS
    fwd = np.einsum("pknsj,jm->spkjnm", parts, eye).reshape(n2 // OUTER_ROWS, 2 * rows, rows)
    inv = np.einsum("pknsj,jm->snjpkm", parts, eye).reshape(n2 // OUTER_ROWS, rows, 2 * rows)
    return fwd, inv, f_fwd, f_inv


def _filt_mlp_kernel(z_ref, w1_ref, b1_ref, w2_ref, b2_ref, w3_ref, b3_ref, fr_ref, o_ref):
    fr = fr_ref[...]
    hp = lambda a, b: jnp.dot(a, b, precision=HIGHEST, preferred_element_type=F32)
    hid = jnp.sin(fr * (hp(z_ref[...], w1_ref[...]) + b1_ref[...]))
    hid = jnp.sin(fr * (hp(hid, w2_ref[...]) + b2_ref[...]))
    o_ref[...] = jnp.sin(fr * (hp(hid, w3_ref[...]) + b3_ref[...]))


def _filt_taps_kernel(hid_ref, wo_ref, dl_ref, h_ref, s_ref, *, n):
    hid = hid_ref[...]
    hid_hi = hid.astype(BF16)
    h = _dot3(hid_hi, (hid - hid_hi.astype(F32)).astype(BF16), wo_ref[...])
    t = lax.broadcasted_iota(jnp.int32, h.shape, 0).astype(F32) * (1.0 / max(n - 1, 1))
    hw = h * jnp.exp(-t * dl_ref[...])
    h_ref[...] = hw
    s_ref[0:1, :] = jnp.sum(jnp.abs(hw), axis=0, keepdims=True)
    s_ref[1:2, :] = jnp.abs(hw[0:1, :])


def _spectrum_kernel(af_ref, ab_ref, f_ref, fl_ref, s_ref, d_ref, o_ref, *, n):
    n2 = GRID_W
    l1 = s_ref[0:1, :] + s_ref[2:3, :] - s_ref[3:4, :]
    hb0 = s_ref[4:5, :]
    norm = 1.0 / l1
    scale = 2.0 / (2 * n)
    for i in range(af_ref.shape[1]):
        def inner(a_ref):
            z = _dot3(f_ref[...], fl_ref[...], jnp.concatenate([a_ref[0, i], a_ref[1, i]], axis=0))
            return z[:n2], z[n2:]

        zfr, zfi = inner(af_ref)
        zbr, zbi = inner(ab_ref)
        o_ref[0, i] = scale * ((zfr + zbr - hb0) * norm + d_ref[...])
        o_ref[1, i] = scale * ((zfi - zbi) * norm)


def _dot3(a_hi, a_lo, x):
    x_hi = x.astype(BF16)
    x_lo = (x - x_hi.astype(F32)).astype(BF16)
    return _dot(a_hi, x_hi) + _dot(a_hi, x_lo) + _dot(a_lo, x_hi)


def _hi_lo(a):
    hi = np.asarray(a, np.float64).astype(BF16)
    lo = (a - hi.astype(np.float64)).astype(BF16)
    return jnp.asarray(hi), jnp.asarray(lo)


def _slabs(x):
    return [x[..., h * OUTER_ROWS:(h + 1) * OUTER_ROWS, :] for h in range(x.shape[-2] // OUTER_ROWS)]


def _outer_dft_kernel(*refs, high):
    g_refs, x_ref, o_ref = refs[:-2], refs[-2], refs[-1]
    n1h, _, tl = x_ref.shape
    parts = []
    for h, xs in enumerate(_slabs(x_ref[...])):
        xs = xs.reshape(n1h * OUTER_ROWS, tl)
        r = _dot3(g_refs[0][h], g_refs[1][h], xs) if high else _dot(g_refs[0][h], xs.astype(BF16))
        parts.append(r.reshape(2, n1h, OUTER_ROWS, tl))
    o_ref[...] = jnp.concatenate(parts, axis=2).astype(o_ref.dtype)


def _outer_dft(gs, x4, seg, *, c, tl):
    high = len(gs) == 2
    s, n1h, n2, _ = x4.shape
    tl = min(tl, c)
    per = c // tl
    rows = OUTER_ROWS if high else BF16_ROWS
    g_spec = pl.BlockSpec((rows // OUTER_ROWS,) + gs[0].shape[1:], lambda j, b, l: (j, 0, 0))
    return pl.pallas_call(
        functools.partial(_outer_dft_kernel, high=high),
        out_shape=jax.ShapeDtypeStruct((s, 2, n1h, n2, c), F32 if high else BF16),
        grid=(n2 // rows, s, per),
        in_specs=[g_spec] * len(gs) + [pl.BlockSpec((None, n1h, rows, tl),
                                                    lambda j, b, l: (b, 0, j, seg * per + l))],
        out_specs=pl.BlockSpec((None, 2, n1h, rows, tl), lambda j, b, l: (b, 0, 0, j, l)),
        compiler_params=_params("parallel", "parallel", "parallel"),
        name="outer_dft",
    )(*gs, x4)


def _hyena_spectra(n, width, filt, hy_d, consts):
    w1, b1, w2, b2, w3, b3, freq, w_out = filt
    gt_fwd, _, f_fwd, _ = consts
    n1h = n // GRID_W
    order = w1.shape[1]
    pos = np.arange(n, dtype=np.float64)
    t = pos / max(n - 1, 1)
    bands = (FILT_EMB - 1) // 2
    fb = np.linspace(1e-4, bands - 1, bands)
    ang = (2 * math.pi / n) * pos[:, None] * fb[None, :]
    z = np.concatenate([t[:, None], np.cos(ang), -np.sin(ang)], axis=-1)
    emb_pad = 128
    z = jnp.asarray(np.pad(z, ((0, 0), (0, emb_pad - FILT_EMB))), F32)
    w1p = jnp.pad(w1, ((0, emb_pad - FILT_EMB), (0, 0)))
    row = lambda a: a.reshape(1, -1)
    full = lambda shape: pl.BlockSpec(shape, lambda: tuple(0 for _ in shape))
    hid = pl.pallas_call(
        _filt_mlp_kernel,
        out_shape=jax.ShapeDtypeStruct((n, order), F32),
        in_specs=[full((n, emb_pad)), full((emb_pad, order)), full((1, order)), full((order, order)),
                  full((1, order)), full((order, order)), full((1, order)), full((1, order))],
        out_specs=full((n, order)),
        compiler_params=pltpu.CompilerParams(vmem_limit_bytes=VMEM_LIMIT_BYTES),
        name="filter_mlp",
    )(z, w1p, row(b1), w2, row(b2), w3, row(b3), row(freq))

    cols = 4 * width
    deltas = np.abs(np.linspace(math.log(DECAY_TARGET) / DECAY_SLOW_PCT,
                                math.log(DECAY_TARGET) / DECAY_FAST_PCT, width))
    deltas4 = jnp.asarray(np.tile(deltas, 4)[None, :], F32)
    tc = min(512, cols)
    taps, sums = pl.pallas_call(
        functools.partial(_filt_taps_kernel, n=n),
        out_shape=(jax.ShapeDtypeStruct((n, cols), F32), jax.ShapeDtypeStruct((2, cols), F32)),
        grid=(cols // tc,),
        in_specs=[pl.BlockSpec((n, order), lambda j: (0, 0)),
                  pl.BlockSpec((order, tc), lambda j: (0, j)),
                  pl.BlockSpec((1, tc), lambda j: (0, j))],
        out_specs=(pl.BlockSpec((n, tc), lambda j: (0, j)), pl.BlockSpec((2, tc), lambda j: (0, j))),
        compiler_params=_params("parallel"),
        name="filter_taps",
    )(hid, w_out, deltas4)

    a = _outer_dft(_hi_lo(gt_fwd), taps.reshape(1, n1h, GRID_W, cols), 0, c=cols, tl=1024)[0]
    s4 = sums.reshape(2, 2, 2, width)
    bwd0 = taps[0].reshape(2, 2, width)[:, 1]
    stats = jnp.stack([s4[0, :, 0], s4[1, :, 0], s4[0, :, 1], s4[1, :, 1], bwd0], axis=1)

    kb = min(INNER_BINS, n1h)

    def tap_spec(side):
        return pl.BlockSpec((2, kb, GRID_W, width), lambda k1, f: (0, k1, 0, 2 * f + side))

    f_hi, f_lo = _hi_lo(f_fwd)
    f_spec = pl.BlockSpec((2 * GRID_W, 2 * GRID_W), lambda k1, f: (0, 0))
    return pl.pallas_call(
        functools.partial(_spectrum_kernel, n=n),
        out_shape=jax.ShapeDtypeStruct((2, 2, n1h, GRID_W, width), F32),
        grid=(n1h // kb, 2),
        in_specs=[tap_spec(0), tap_spec(1), f_spec, f_spec,
                  pl.BlockSpec((None, 5, width), lambda k1, f: (f, 0, 0)),
                  pl.BlockSpec((None, 1, width), lambda k1, f: (f, 0, 0))],
        out_specs=pl.BlockSpec((None, 2, kb, GRID_W, width), lambda k1, f: (f, 0, k1, 0, 0)),
        compiler_params=_params("parallel", "parallel"),
        name="filter_spectrum",
    )(a, a, f_hi, f_lo, stats, hy_d.reshape(2, 1, width))


def _inner_conv_kernel(a_ref, f_ref, fi_ref, h_ref, o_ref):
    n2 = GRID_W
    for i in range(a_ref.shape[1]):
        z = _dot(f_ref[...], jnp.concatenate([a_ref[0, i], a_ref[1, i]], axis=0))
        zr, zi = z[:n2], z[n2:]
        hr, hi = h_ref[0, i], h_ref[1, i]
        stacked = jnp.concatenate([zr * hr - zi * hi, zr * hi + zi * hr], axis=0).astype(BF16)
        w = _dot(fi_ref[...], stacked).astype(BF16)
        o_ref[0, i] = w[:n2]
        o_ref[1, i] = w[n2:]


def _inner_conv(a, f_fwd, f_inv, spec):
    bsz, _, n1h, n2, c = a.shape
    kb = min(INNER_BINS, n1h)
    return pl.pallas_call(
        _inner_conv_kernel,
        out_shape=jax.ShapeDtypeStruct(a.shape, BF16),
        grid=(n1h // kb, bsz),
        in_specs=[pl.BlockSpec((None, 2, kb, n2, c), lambda k1, b: (b, 0, k1, 0, 0)),
                  pl.BlockSpec((2 * n2, 2 * n2), lambda k1, b: (0, 0)),
                  pl.BlockSpec((2 * n2, 2 * n2), lambda k1, b: (0, 0)),
                  pl.BlockSpec((2, kb, n2, c), lambda k1, b: (0, k1, 0, 0))],
        out_specs=pl.BlockSpec((None, 2, kb, n2, c), lambda k1, b: (b, 0, k1, 0, 0)),
        compiler_params=_params("parallel", "arbitrary"),
        name="inner_conv",
    )(a, f_fwd, f_inv, spec)


def _outer_idft_gate_kernel(gi_ref, g_ref, b_ref, x_ref, o_ref, *, again):
    n1h, _, tl = x_ref.shape
    parts = []
    for h, (bs, xs) in enumerate(zip(_slabs(b_ref[...].astype(F32)), _slabs(x_ref[...]))):
        stacked = bs.reshape(2 * n1h * OUTER_ROWS, tl).astype(BF16)
        y = _dot(gi_ref[h], stacked) * xs.reshape(n1h * OUTER_ROWS, tl)
        if again:
            parts.append(_dot(g_ref[h], y.astype(BF16)).reshape(2, n1h, OUTER_ROWS, tl))
        else:
            parts.append(y.reshape(n1h, OUTER_ROWS, tl))
    o_ref[...] = jnp.concatenate(parts, axis=-2).astype(o_ref.dtype)


def _outer_idft_gate(g_inv, g_fwd, b, px4, seg, *, again, tl):
    bsz, _, n1h, n2, c = b.shape
    tl = min(tl, c)
    per = c // tl
    rows = BF16_ROWS
    if again:
        out_shape = jax.ShapeDtypeStruct((bsz, 2, n1h, n2, c), BF16)
        out_spec = pl.BlockSpec((None, 2, n1h, rows, tl), lambda j, bb, l: (bb, 0, 0, j, l))
    else:
        out_shape = jax.ShapeDtypeStruct((bsz, n1h, n2, c), BF16)
        out_spec = pl.BlockSpec((None, n1h, rows, tl), lambda j, bb, l: (bb, 0, j, l))
    return pl.pallas_call(
        functools.partial(_outer_idft_gate_kernel, again=again),
        out_shape=out_shape,
        grid=(n2 // rows, bsz, per),
        in_specs=[pl.BlockSpec((rows // OUTER_ROWS,) + g_inv.shape[1:], lambda j, bb, l: (j, 0, 0)),
                  pl.BlockSpec((rows // OUTER_ROWS,) + g_fwd.shape[1:], lambda j, bb, l: (j, 0, 0)),
                  pl.BlockSpec((None, 2, n1h, rows, tl), lambda j, bb, l: (bb, 0, 0, j, l)),
                  pl.BlockSpec((None, n1h, rows, tl), lambda j, bb, l: (bb, 0, j, seg * per + l))],
        out_specs=out_spec,
        compiler_params=_params("parallel", "parallel", "parallel"),
        name="outer_idft_gate",
    )(g_inv, g_fwd, b, px4)


def _hyena_mix(px, spec, consts, *, bsz, n, width):
    g_fwd, g_inv, f_fwd, f_inv = consts
    g_fwd_b, g_inv_b = jnp.asarray(g_fwd, BF16), jnp.asarray(g_inv, BF16)
    f_fwd_b, f_inv_b = jnp.asarray(f_fwd, BF16), jnp.asarray(f_inv, BF16)
    n1h = n // GRID_W
    px4 = px.reshape(bsz, n1h, GRID_W, -1)
    tl = width
    a = _outer_dft((g_fwd_b,), px4, 5, c=width, tl=tl)
    a = _inner_conv(a, f_fwd_b, f_inv_b, spec[0])
    a = _outer_idft_gate(g_inv_b, g_fwd_b, a, px4, 6, again=True, tl=tl)
    a = _inner_conv(a, f_fwd_b, f_inv_b, spec[1])
    hy = _outer_idft_gate(g_inv_b, g_fwd_b, a, px4, 7, again=False, tl=tl)
    return hy.reshape(bsz * n, width)


def _out_proj_kernel(x_ref, hg_ref, hy_ref, wa_ref, wb_ref, gt_ref, o_ref):
    mix = _dot(hg_ref[...], wa_ref[...]) + _dot(hy_ref[...], wb_ref[...])
    o_ref[...] = x_ref[...] + gt_ref[0] * mix


def _out_proj(x, hg, hy, w, gate, *, tokens_per_mod, tm):
    t, d = x.shape
    wdt = hg.shape[1]
    tm = min(tm, tokens_per_mod)
    return pl.pallas_call(
        _out_proj_kernel,
        out_shape=jax.ShapeDtypeStruct((t, d), F32),
        grid=(t // tm,),
        in_specs=[pl.BlockSpec((tm, d), lambda i: (i, 0)),
                  pl.BlockSpec((tm, wdt), lambda i: (i, 0)),
                  pl.BlockSpec((tm, wdt), lambda i: (i, 0)),
                  pl.BlockSpec((wdt, d), lambda i: (0, 0)),
                  pl.BlockSpec((wdt, d), lambda i: (1, 0)),
                  pl.BlockSpec((1, 1, d), lambda i: ((i * tm) // tokens_per_mod, 0, 0))],
        out_specs=pl.BlockSpec((tm, d), lambda i: (i, 0)),
        compiler_params=_params("parallel"),
        name="out_proj",
    )(x, hg, hy, w, w, gate)


def _tiles(d_ff):
    return TOKEN_TILE, 2 * TOKEN_TILE, HIDDEN_TILE if d_ff % HIDDEN_TILE == 0 else d_ff


def kernel(x, c, ctx, c_ctx, ada_w, ada_b, norm_ffn1, ffn1_w1, ffn1_w3, ffn1_w2, norm_mix, w_in, hg_lb_logits, hg_norm, hy_conv_w, hy_conv_b, filt_w1, filt_b1, filt_w2, filt_b2, filt_w3, filt_b3, filt_freq, filt_w_out, hy_d, w_out, norm_ffn2, ffn2_w1, ffn2_w3, ffn2_w2, final_norm):
    bsz, n, d = x.shape
    n_ctx = ctx.shape[1]
    depth = ada_w.shape[0]
    assert depth == 1, "single-layer block"
    width = d // 2
    n_seg = w_in.shape[2] // width
    assert n_seg == 8 and n % GRID_W == 0 and n_ctx % SCAN_CHUNK == 0 and width % HEAD_DIM == 0
    tm, tm_wide, tf = _tiles(ffn1_w1.shape[2])
    l = 0

    rows = -(-(bsz + 1) // 8) * 8
    cs = jnp.concatenate([c, c_ctx[None, :], jnp.zeros((rows - bsz - 1, d), F32)], axis=0)
    mods = _ada(cs, ada_w[l], ada_b[l]).reshape(rows, N_MOD, d)
    mx = [mods[:bsz, i][:, None, :] for i in range(N_MOD)]
    mc = [mods[bsz:bsz + 1, i][:, None, :] for i in range(N_MOD)]

    bf = _to_bf16
    w1a, w3a, w2a = bf(ffn1_w1[l]), bf(ffn1_w3[l]), bf(ffn1_w2[l])
    xt = x.reshape(bsz * n, d)
    yt = ctx.reshape(bsz * n_ctx, d)

    xt = _ffn(xt, mx[0], mx[1], mx[2], norm_ffn1[l], w1a, w3a, w2a, final_norm,
              tokens_per_mod=n, final_norm=False, tm=tm, tf=tf)
    yt = _ffn(yt, mc[0], mc[1], mc[2], norm_ffn1[l], w1a, w3a, w2a, final_norm,
              tokens_per_mod=bsz * n_ctx, final_norm=False, tm=tm, tf=tf)

    w_in_b = bf(w_in[l])
    pc = _proj(yt, mc[3], mc[4], norm_mix[l], w_in_b, hy_conv_w[l], hy_conv_b[l], n_seg=3, seg_w=width,
               conv_from=n_seg, tokens_per_mod=bsz * n_ctx, tm=tm)
    px = _proj(xt, mx[3], mx[4], norm_mix[l], w_in_b, hy_conv_w[l], hy_conv_b[l], n_seg=n_seg, seg_w=width,
               conv_from=5, tokens_per_mod=n, tm=tm_wide)
    cols = n_seg * width

    hg = _hgrn(px.reshape(bsz, n, cols), pc.reshape(bsz, n_ctx, 3 * width), hg_lb_logits, hg_norm[l], width=width)

    consts = _dft_constants(n)
    filt = (filt_w1[l], filt_b1[l], filt_w2[l], filt_b2[l], filt_w3[l], filt_b3[l], filt_freq[l], filt_w_out[l])
    spec = _hyena_spectra(n, width, filt, hy_d[l], consts)
    hy = _hyena_mix(px, spec, consts, bsz=bsz, n=n, width=width)

    xt = _out_proj(xt, hg.reshape(bsz * n, width), hy, bf(w_out[l]), mx[5],
                   tokens_per_mod=n, tm=tm)

    out = _ffn(xt, mx[6], mx[7], mx[8], norm_ffn2[l], bf(ffn2_w1[l]), bf(ffn2_w3[l]), bf(ffn2_w2[l]), final_norm,
               tokens_per_mod=n, final_norm=True, tm=tm, tf=tf)
    return out.reshape(bsz, n, d)
```

```python
import functools
import math

import numpy as np
import jax
import jax.numpy as jnp
from jax import lax
from jax.experimental import pallas as pl
from jax.experimental.pallas import tpu as pltpu

F32 = jnp.float32
BF16 = jnp.bfloat16
EPS = 1e-6
N_MOD = 9
HEAD_DIM = 128
GRID_W = 64
CAST_ROWS = 256
TOKEN_TILE = 512
HIDDEN_TILE = 512
SCAN_CHUNK = 64
PREP_CHUNKS = 4
SCAN_BLOCKS = 2
OUTER_ROWS = 8
BF16_ROWS = 16
INNER_BINS = 8
FILT_EMB = 33
DECAY_TARGET = 1e-2
DECAY_FAST_PCT = 0.3
DECAY_SLOW_PCT = 1.5
VMEM_LIMIT_BYTES = 56 * 1024 * 1024
HIGHEST = lax.Precision.HIGHEST


def _params(*semantics):
    return pltpu.CompilerParams(dimension_semantics=semantics, vmem_limit_bytes=VMEM_LIMIT_BYTES)


def _dot(a, b):
    return jnp.dot(a, b, preferred_element_type=F32)


def _silu(a):
    return a * jax.nn.sigmoid(a)


def _cast_kernel(x_ref, o_ref):
    o_ref[...] = x_ref[...].astype(BF16)


def _to_bf16(w):
    r, c = w.shape
    rb = CAST_ROWS if r % CAST_ROWS == 0 else r
    return pl.pallas_call(
        _cast_kernel,
        out_shape=jax.ShapeDtypeStruct((r, c), BF16),
        grid=(r // rb,),
        in_specs=[pl.BlockSpec((rb, c), lambda i: (i, 0))],
        out_specs=pl.BlockSpec((rb, c), lambda i: (i, 0)),
        compiler_params=_params("parallel"),
        name="to_bf16",
    )(w)


def _ada_kernel(c_ref, w_ref, b_ref, o_ref):
    h = _silu(c_ref[...]).astype(BF16)
    o_ref[...] = _dot(h, w_ref[...].astype(BF16)) + b_ref[...]


def _ada(cs, w, b):
    rows, d = cs.shape
    n = w.shape[1]
    tn = d // 2
    return pl.pallas_call(
        _ada_kernel,
        out_shape=jax.ShapeDtypeStruct((rows, n), F32),
        grid=(n // tn,),
        in_specs=[pl.BlockSpec((rows, d), lambda j: (0, 0)),
                  pl.BlockSpec((d, tn), lambda j: (0, j)),
                  pl.BlockSpec((1, tn), lambda j: (0, j))],
        out_specs=pl.BlockSpec((rows, tn), lambda j: (0, j)),
        compiler_params=_params("arbitrary"),
        name="ada_mod",
    )(cs, w, b.reshape(1, n))


def _norm_mod(x, gain, shift, scale):
    y = x * lax.rsqrt(jnp.mean(x * x, axis=-1, keepdims=True) + EPS) * gain
    return (y * (1.0 + scale) + shift).astype(BF16)


def _ffn_kernel(x_ref, sh_ref, sc_ref, gt_ref, g_ref, w1_ref, w3_ref, w2_ref, fin_ref, o_ref, h_scr, *, final_norm):
    j = pl.program_id(1)

    @pl.when(j == 0)
    def _():
        h_scr[...] = _norm_mod(x_ref[...], g_ref[...], sh_ref[0], sc_ref[0])
        o_ref[...] = jnp.zeros_like(o_ref)

    h = h_scr[...]
    a = _dot(h, w1_ref[...])
    b = _dot(h, w3_ref[...])
    act = (_silu(a) * b).astype(BF16)
    o_ref[...] += _dot(act, w2_ref[...])

    @pl.when(j == pl.num_programs(1) - 1)
    def _():
        out = x_ref[...] + 0.5 * gt_ref[0] * o_ref[...]
        if final_norm:
            out = out * lax.rsqrt(jnp.mean(out * out, axis=-1, keepdims=True) + EPS) * fin_ref[...]
        o_ref[...] = out


def _ffn(x, shift, scale, gate, gain, w1, w3, w2, fin, *, tokens_per_mod, final_norm, tm, tf):
    t, d = x.shape
    f = w1.shape[1]
    tm = min(tm, tokens_per_mod)
    mod_spec = pl.BlockSpec((1, 1, d), lambda i, j: ((i * tm) // tokens_per_mod, 0, 0))
    vec_spec = pl.BlockSpec((1, d), lambda i, j: (0, 0))
    return pl.pallas_call(
        functools.partial(_ffn_kernel, final_norm=final_norm),
        out_shape=jax.ShapeDtypeStruct((t, d), F32),
        grid=(t // tm, f // tf),
        in_specs=[pl.BlockSpec((tm, d), lambda i, j: (i, 0)),
                  mod_spec, mod_spec, mod_spec, vec_spec,
                  pl.BlockSpec((d, tf), lambda i, j: (0, j)),
                  pl.BlockSpec((d, tf), lambda i, j: (0, j)),
                  pl.BlockSpec((tf, d), lambda i, j: (j, 0)),
                  vec_spec],
        out_specs=pl.BlockSpec((tm, d), lambda i, j: (i, 0)),
        scratch_shapes=[pltpu.VMEM((tm, d), BF16)],
        compiler_params=_params("parallel", "arbitrary"),
        name="swiglu_ffn",
    )(x, shift, scale, gate, gain.reshape(1, d), w1, w3, w2, fin.reshape(1, d))


def _proj_kernel(x_ref, sh_ref, sc_ref, g_ref, w_ref, cw_ref, cb_ref, o_ref, h_scr, *, conv_from):
    j = pl.program_id(1)

    @pl.when(j == 0)
    def _():
        h_scr[...] = _norm_mod(x_ref[...], g_ref[...], sh_ref[0], sc_ref[0])

    p = _dot(h_scr[...], w_ref[...])

    @pl.when(j < conv_from)
    def _():
        o_ref[...] = p

    @pl.when(j >= conv_from)
    def _():
        tm, w = p.shape
        rows3 = lambda t: t.reshape(tm // GRID_W, GRID_W, w)
        y = (rows3(pltpu.roll(p, 1, axis=0)) * cw_ref[0] + rows3(p) * cw_ref[1]
             + rows3(pltpu.roll(p, tm - 1, axis=0)) * cw_ref[2] + cb_ref[...])
        o_ref[...] = y.reshape(tm, w)


def _proj(x, shift, scale, gain, w, conv_w, conv_b, *, n_seg, seg_w, conv_from, tokens_per_mod, tm):
    t, d = x.shape
    tm = min(tm, tokens_per_mod)
    mod_spec = pl.BlockSpec((1, 1, d), lambda i, j: ((i * tm) // tokens_per_mod, 0, 0))
    conv_idx = lambda i, j: (0, jnp.maximum(j - conv_from, 0))
    pos = jnp.arange(GRID_W)[None, :, None]
    edge = jnp.stack([pos > 0, pos >= 0, pos < GRID_W - 1], axis=0)[:, 0]
    conv_w = jnp.where(edge, conv_w[:, None, :], 0.0)
    return pl.pallas_call(
        functools.partial(_proj_kernel, conv_from=conv_from),
        out_shape=jax.ShapeDtypeStruct((t, n_seg * seg_w), F32),
        grid=(t // tm, n_seg),
        in_specs=[pl.BlockSpec((tm, d), lambda i, j: (i, 0)),
                  mod_spec, mod_spec,
                  pl.BlockSpec((1, d), lambda i, j: (0, 0)),
                  pl.BlockSpec((d, seg_w), lambda i, j: (0, j)),
                  pl.BlockSpec((3, GRID_W, seg_w), lambda i, j: (0, 0, jnp.maximum(j - conv_from, 0))),
                  pl.BlockSpec((1, seg_w), conv_idx)],
        out_specs=pl.BlockSpec((tm, seg_w), lambda i, j: (i, j)),
        scratch_shapes=[pltpu.VMEM((tm, d), BF16)],
        compiler_params=_params("parallel", "arbitrary"),
        name="in_proj",
    )(x, shift, scale, gain.reshape(1, d), w, conv_w, conv_b.reshape(1, -1))


def _split3(g):
    g1 = g.astype(BF16)
    r1 = g - g1.astype(F32)
    g2 = r1.astype(BF16)
    return g1, g2, (r1 - g2.astype(F32)).astype(BF16)


def _split_dot(tri, g):
    g1, g2, g3 = _split3(g)
    return _dot(tri, g1) + _dot(tri, g2) + _dot(tri, g3)


def _gates(a, lb):
    f = lb + (1.0 - lb) * jax.nn.sigmoid(a)
    return 1.0 - f, jnp.log(f)


def _tn_dot(a, b):
    return lax.dot_general(a, b, (((0,), (0,)), ((), ())), preferred_element_type=F32)


def _nt_dot(a, b):
    return lax.dot_general(a, b, (((1,), (1,)), ((), ())), preferred_element_type=F32)


def _hgrn_kernel(af_ref, ab_ref, v_ref, q_ref, gate_ref, caf_ref, cab_ref, cv_ref, lbl_ref, gain_ref,
                 o_ref, of_scr, ob_scr, sf_scr, sb_scr, qin_scr, qt_scr, kt_scr, kd_scr, dec_scr,
                 *, n_chunks, n_ctx_chunks):
    c = SCAN_CHUNK
    mid = c // 2
    rows_p = PREP_CHUNKS * c
    prow = lax.broadcasted_iota(jnp.int32, (rows_p, rows_p), 0)
    pcol = lax.broadcasted_iota(jnp.int32, (rows_p, rows_p), 1)
    same = (prow // c) == (pcol // c)
    tri_fwd_p = jnp.where(same & (prow >= pcol), 1.0, 0.0).astype(BF16)
    tri_bwd_p = jnp.where(same & (pcol >= prow), 1.0, 0.0).astype(BF16)

    def lower_bound(direction):
        lg = lbl_ref[direction]
        ex = jnp.exp(lg - jnp.max(lg, axis=0, keepdims=True))
        return ex[0:1, :] / jnp.sum(ex, axis=0, keepdims=True)

    lb_f = lower_bound(0)
    lb_b = lower_bound(1)
    q_scale = HEAD_DIM ** -0.5

    n_blocks = n_chunks // PREP_CHUNKS
    n_steps = n_blocks // SCAN_BLOCKS

    def step_blocks(it):
        fwd = [(0, it * SCAN_BLOCKS + u) for u in range(SCAN_BLOCKS)]
        return fwd + [(1, n_blocks - 1 - bi) for _, bi in fwd]

    def prepare(it, s):
        it = jnp.minimum(it, n_steps - 1)
        blocks = step_blocks(it)
        los = [pl.multiple_of(bi * rows_p, rows_p) for _, bi in blocks]
        gates = [_gates((af_ref, ab_ref)[d][pl.ds(lo, rows_p), :], (lb_f, lb_b)[d]) for (d, _), lo in zip(blocks, los)]
        sums = {}
        for d, tri_p in ((0, tri_fwd_p), (1, tri_bwd_p)):
            slots_d = [slot for slot, (dd, _) in enumerate(blocks) if dd == d]
            wide = _dot(tri_p, jnp.concatenate([p for slot in slots_d for p in _split3(gates[slot][1])], axis=1))
            for n_s, slot in enumerate(slots_d):
                parts = [wide[:, (3 * n_s + m) * HEAD_DIM:(3 * n_s + m + 1) * HEAD_DIM] for m in range(3)]
                sums[slot] = parts[0] + parts[1] + parts[2]
        for slot, ((d, _), lo) in enumerate(zip(blocks, los)):
            end_row = (c - 1, 0)[d]
            k = gates[slot][0]
            b = sums[slot].reshape(PREP_CHUNKS, c, HEAD_DIM)
            b_mid = b[:, mid:mid + 1, :]
            b_end = b[:, end_row:end_row + 1, :]
            q_t = (q_ref[pl.ds(lo, rows_p), :] * q_scale).reshape(PREP_CHUNKS, c, HEAD_DIM) * jnp.exp(b - b_mid)
            k_t = k.reshape(PREP_CHUNKS, c, HEAD_DIM) * jnp.exp(b_mid - b)
            flat = lambda t: t.reshape(rows_p, HEAD_DIM).astype(BF16)
            qt_scr[s, slot] = flat(q_t)
            kt_scr[s, slot] = flat(k_t)
            qin_scr[s, slot] = flat(q_t * jnp.exp(b_mid))
            kd_scr[s, slot] = flat(k_t * jnp.exp(b_end - b_mid))
            dec_scr[s, slot] = jnp.broadcast_to(jnp.exp(b_end), (PREP_CHUNKS, 8, HEAD_DIM))

    zeros_blk = jnp.zeros((c, HEAD_DIM), BF16)

    def block_diag(t):
        return jnp.concatenate(
            [jnp.concatenate([t[j * c:(j + 1) * c] if m == j else zeros_blk for m in range(PREP_CHUNKS)], axis=1)
             for j in range(PREP_CHUNKS)], axis=0)

    masks = (same & (prow >= pcol), same & (pcol >= prow))
    st_refs = (sf_scr, sb_scr)
    out_refs = (of_scr, ob_scr)
    orders = (tuple(range(PREP_CHUNKS)), tuple(range(PREP_CHUNKS - 1, -1, -1)))

    gain = gain_ref[...]

    def scan(it, s, finish):
        streams = step_blocks(it)
        los = [pl.multiple_of(bi * rows_p, rows_p) for _, bi in streams]
        vbs = [v_ref[pl.ds(lo, rows_p), :].astype(BF16) for lo in los]
        scores = [_nt_dot(qt_scr[s, slot], kt_scr[s, slot]) for slot in range(len(streams))]
        upds = [_tn_dot(vb, block_diag(kd_scr[s, slot])) for slot, vb in enumerate(vbs)]
        outs = [_dot(jnp.where(masks[d], sc, 0.0).astype(BF16), vb) for (d, _), sc, vb in zip(streams, scores, vbs)]
        seen = [[None] * PREP_CHUNKS for _ in streams]
        for d in (0, 1):
            st = st_refs[d][...]
            for slot, (ds_, _) in enumerate(streams):
                if ds_ != d:
                    continue
                for j in orders[d]:
                    seen[slot][j] = st.astype(BF16)
                    st = st * dec_scr[s, slot, j][0:1, :] + upds[slot][:, j * HEAD_DIM:(j + 1) * HEAD_DIM]
            st_refs[d][...] = st
        for slot, ((d, _), lo) in enumerate(zip(streams, los)):
            q_in = qin_scr[s, slot]
            carried = jnp.concatenate([_nt_dot(q_in[j * c:(j + 1) * c], seen[slot][j]) for j in range(PREP_CHUNKS)],
                                      axis=0)
            rows = pl.ds(lo, rows_p)
            if finish:
                o = outs[slot] + carried + out_refs[1 - d][rows, :]
                o = o * lax.rsqrt(jnp.mean(o * o, axis=-1, keepdims=True) + EPS) * gain
                o_ref[rows, :] = (o * _silu(gate_ref[rows, :])).astype(o_ref.dtype)
            else:
                out_refs[d][rows, :] = outs[slot] + carried

    n_ctx_blocks = n_ctx_chunks // PREP_CHUNKS
    ctx_state = [jnp.zeros((HEAD_DIM, HEAD_DIM), F32), jnp.zeros((HEAD_DIM, HEAD_DIM), F32)]
    for step in range(n_ctx_blocks):
        upds, decs = [], []
        for d, bi in ((0, step), (1, n_ctx_blocks - 1 - step)):
            a_ref, lb, tri_p, end_row = ((caf_ref, lb_f, tri_fwd_p, c - 1), (cab_ref, lb_b, tri_bwd_p, 0))[d]
            rows_c = pl.ds(bi * rows_p, rows_p)
            k, g = _gates(a_ref[rows_c, :], lb)
            b = _split_dot(tri_p, g).reshape(PREP_CHUNKS, c, HEAD_DIM)
            b_end = b[:, end_row:end_row + 1, :]
            k_d = (k.reshape(PREP_CHUNKS, c, HEAD_DIM) * jnp.exp(b_end - b)).reshape(rows_p, HEAD_DIM).astype(BF16)
            upds.append(_tn_dot(cv_ref[rows_c, :].astype(BF16), block_diag(k_d)))
            decs.append(jnp.exp(b_end))
        for d in (0, 1):
            for j in orders[d]:
                ctx_state[d] = ctx_state[d] * decs[d][j] + upds[d][:, j * HEAD_DIM:(j + 1) * HEAD_DIM]
    sf_scr[...] = ctx_state[0]
    sb_scr[...] = ctx_state[1]

    prepare(0, 0)

    def body(finish):
        def pair(i, carry):
            scan(2 * i, 0, finish)
            prepare(2 * i + 1, 1)
            scan(2 * i + 1, 1, finish)
            prepare(2 * i + 2, 0)
            return carry
        return pair

    lax.fori_loop(0, n_steps // 4, body(False), 0)
    lax.fori_loop(n_steps // 4, n_steps // 2, body(True), 0)


def _hgrn(px, pc, lb_logits, gain, *, width):
    bsz, n, _ = px.shape
    n_ctx = pc.shape[1]
    heads = width // HEAD_DIM
    hd = HEAD_DIM

    def seg(s):
        return pl.BlockSpec((None, n, hd), lambda b, h: (b, 0, s * heads + h))

    def cseg(s):
        return pl.BlockSpec((None, n_ctx, hd), lambda b, h: (b, 0, s * heads + h))

    depth1 = lb_logits.shape[1]
    slots = 2 * SCAN_BLOCKS
    assert (n // SCAN_CHUNK) % (4 * SCAN_BLOCKS * PREP_CHUNKS) == 0, "scan steps come in pairs, per half"
    assert (n_ctx // SCAN_CHUNK) % PREP_CHUNKS == 0, "the context prefix is scanned in whole blocks"
    return pl.pallas_call(
        functools.partial(_hgrn_kernel, n_chunks=n // SCAN_CHUNK, n_ctx_chunks=n_ctx // SCAN_CHUNK),
        out_shape=jax.ShapeDtypeStruct((bsz, n, width), BF16),
        grid=(bsz, heads),
        in_specs=[seg(0), seg(1), seg(2), seg(3), seg(4), cseg(0), cseg(1), cseg(2),
                  pl.BlockSpec((2, depth1, hd), lambda b, h: (0, 0, h)),
                  pl.BlockSpec((1, hd), lambda b, h: (0, h))],
        out_specs=pl.BlockSpec((None, n, hd), lambda b, h: (b, 0, h)),
        scratch_shapes=[pltpu.VMEM((n, hd), F32), pltpu.VMEM((n, hd), F32),
                        pltpu.VMEM((hd, hd), F32), pltpu.VMEM((hd, hd), F32),
                        *[pltpu.VMEM((2, slots, PREP_CHUNKS * SCAN_CHUNK, hd), BF16) for _ in range(4)],
                        pltpu.VMEM((2, slots, PREP_CHUNKS, 8, hd), F32)],
        compiler_params=_params("parallel", "parallel"),
        name="hgrn_scan",
    )(px, px, px, px, px, pc, pc, pc, lb_logits, gain.reshape(1, width))


def _dft_constants(n):
    n2 = GRID_W
    n1h = n // n2
    m2r = np.arange(n2)[None, :]
    k2 = np.arange(n2)[:, None]
    f_ang = -2.0 * np.pi * k2 * m2r / n2
    fr, fi = np.cos(f_ang), np.sin(f_ang)
    f_fwd = np.block([[fr, -fi], [fi, fr]])
    f_inv = np.block([[fr, fi], [-fi, fr]])
    k1 = np.arange(n1h)[:, None, None]
    m1 = np.arange(n1h)[None, :, None]
    m2 = np.arange(n2)[None, None, :]
    theta = -2.0 * np.pi * (k1 + 0.5) * (m1 / (2 * n1h) + m2 / (2 * n))
    parts = np.stack([np.cos(theta), np.sin(theta)], axis=0)
    parts = parts.reshape(2, n1h, n1h, n2 // OUTER_ROWS, OUTER_ROWS)
    eye = np.eye(OUTER_ROWS)
    rows = n1h * OUTER_ROWS
    fwd = np.einsum("pknsj,jm->spkjnm", parts, eye).reshape(n2 // OUTER_ROWS, 2 * rows, rows)
    inv = np.einsum("pknsj,jm->snjpkm", parts, eye).reshape(n2 // OUTER_ROWS, rows, 2 * rows)
    return fwd, inv, f_fwd, f_inv


def _filt_mlp_kernel(z_ref, w1_ref, b1_ref, w2_ref, b2_ref, w3_ref, b3_ref, fr_ref, o_ref):
    fr = fr_ref[...]
    hp = lambda a, b: jnp.dot(a, b, precision=HIGHEST, preferred_element_type=F32)
    hid = jnp.sin(fr * (hp(z_ref[...], w1_ref[...]) + b1_ref[...]))
    hid = jnp.sin(fr * (hp(hid, w2_ref[...]) + b2_ref[...]))
    o_ref[...] = jnp.sin(fr * (hp(hid, w3_ref[...]) + b3_ref[...]))


def _filt_taps_kernel(hid_ref, wo_ref, dl_ref, h_ref, s_ref, *, n):
    hid = hid_ref[...]
    hid_hi = hid.astype(BF16)
    h = _dot3(hid_hi, (hid - hid_hi.astype(F32)).astype(BF16), wo_ref[...])
    t = lax.broadcasted_iota(jnp.int32, h.shape, 0).astype(F32) * (1.0 / max(n - 1, 1))
    hw = h * jnp.exp(-t * dl_ref[...])
    h_ref[...] = hw
    s_ref[0:1, :] = jnp.sum(jnp.abs(hw), axis=0, keepdims=True)
    s_ref[1:2, :] = jnp.abs(hw[0:1, :])


def _spectrum_kernel(af_ref, ab_ref, f_ref, fl_ref, s_ref, d_ref, o_ref, *, n):
    n2 = GRID_W
    l1 = s_ref[0:1, :] + s_ref[2:3, :] - s_ref[3:4, :]
    hb0 = s_ref[4:5, :]
    norm = 1.0 / l1
    scale = 2.0 / (2 * n)
    for i in range(af_ref.shape[1]):
        def inner(a_ref):
            z = _dot3(f_ref[...], fl_ref[...], jnp.concatenate([a_ref[0, i], a_ref[1, i]], axis=0))
            return z[:n2], z[n2:]

        zfr, zfi = inner(af_ref)
        zbr, zbi = inner(ab_ref)
        o_ref[0, i] = scale * ((zfr + zbr - hb0) * norm + d_ref[...])
        o_ref[1, i] = scale * ((zfi - zbi) * norm)


def _dot3(a_hi, a_lo, x):
    x_hi = x.astype(BF16)
    x_lo = (x - x_hi.astype(F32)).astype(BF16)
    return _dot(a_hi, x_hi) + _dot(a_hi, x_lo) + _dot(a_lo, x_hi)


def _hi_lo(a):
    hi = np.asarray(a, np.float64).astype(BF16)
    lo = (a - hi.astype(np.float64)).astype(BF16)
    return jnp.asarray(hi), jnp.asarray(lo)


def _slabs(x):
    return [x[..., h * OUTER_ROWS:(h + 1) * OUTER_ROWS, :] for h in range(x.shape[-2] // OUTER_ROWS)]


def _outer_dft_kernel(*refs, high):
    g_refs, x_ref, o_ref = refs[:-2], refs[-2], refs[-1]
    n1h, _, tl = x_ref.shape
    parts = []
    for h, xs in enumerate(_slabs(x_ref[...])):
        xs = xs.reshape(n1h * OUTER_ROWS, tl)
        r = _dot3(g_refs[0][h], g_refs[1][h], xs) if high else _dot(g_refs[0][h], xs.astype(BF16))
        parts.append(r.reshape(2, n1h, OUTER_ROWS, tl))
    o_ref[...] = jnp.concatenate(parts, axis=2).astype(o_ref.dtype)


def _outer_dft(gs, x4, seg, *, c, tl):
    high = len(gs) == 2
    s, n1h, n2, _ = x4.shape
    tl = min(tl, c)
    per = c // tl
    rows = OUTER_ROWS if high else BF16_ROWS
    g_spec = pl.BlockSpec((rows // OUTER_ROWS,) + gs[0].shape[1:], lambda j, b, l: (j, 0, 0))
    return pl.pallas_call(
        functools.partial(_outer_dft_kernel, high=high),
        out_shape=jax.ShapeDtypeStruct((s, 2, n1h, n2, c), F32 if high else BF16),
        grid=(n2 // rows, s, per),
        in_specs=[g_spec] * len(gs) + [pl.BlockSpec((None, n1h, rows, tl),
                                                    lambda j, b, l: (b, 0, j, seg * per + l))],
        out_specs=pl.BlockSpec((None, 2, n1h, rows, tl), lambda j, b, l: (b, 0, 0, j, l)),
        compiler_params=_params("parallel", "parallel", "parallel"),
        name="outer_dft",
    )(*gs, x4)


def _hyena_spectra(n, width, filt, hy_d, consts):
    w1, b1, w2, b2, w3, b3, freq, w_out = filt
    gt_fwd, _, f_fwd, _ = consts
    n1h = n // GRID_W
    order = w1.shape[1]
    pos = np.arange(n, dtype=np.float64)
    t = pos / max(n - 1, 1)
    bands = (FILT_EMB - 1) // 2
    fb = np.linspace(1e-4, bands - 1, bands)
    ang = (2 * math.pi / n) * pos[:, None] * fb[None, :]
    z = np.concatenate([t[:, None], np.cos(ang), -np.sin(ang)], axis=-1)
    emb_pad = 128
    z = jnp.asarray(np.pad(z, ((0, 0), (0, emb_pad - FILT_EMB))), F32)
    w1p = jnp.pad(w1, ((0, emb_pad - FILT_EMB), (0, 0)))
    row = lambda a: a.reshape(1, -1)
    full = lambda shape: pl.BlockSpec(shape, lambda: tuple(0 for _ in shape))
    hid = pl.pallas_call(
        _filt_mlp_kernel,
        out_shape=jax.ShapeDtypeStruct((n, order), F32),
        in_specs=[full((n, emb_pad)), full((emb_pad, order)), full((1, order)), full((order, order)),
                  full((1, order)), full((order, order)), full((1, order)), full((1, order))],
        out_specs=full((n, order)),
        compiler_params=pltpu.CompilerParams(vmem_limit_bytes=VMEM_LIMIT_BYTES),
        name="filter_mlp",
    )(z, w1p, row(b1), w2, row(b2), w3, row(b3), row(freq))

    cols = 4 * width
    deltas = np.abs(np.linspace(math.log(DECAY_TARGET) / DECAY_SLOW_PCT,
                                math.log(DECAY_TARGET) / DECAY_FAST_PCT, width))
    deltas4 = jnp.asarray(np.tile(deltas, 4)[None, :], F32)
    tc = min(512, cols)
    taps, sums = pl.pallas_call(
        functools.partial(_filt_taps_kernel, n=n),
        out_shape=(jax.ShapeDtypeStruct((n, cols), F32), jax.ShapeDtypeStruct((2, cols), F32)),
        grid=(cols // tc,),
        in_specs=[pl.BlockSpec((n, order), lambda j: (0, 0)),
                  pl.BlockSpec((order, tc), lambda j: (0, j)),
                  pl.BlockSpec((1, tc), lambda j: (0, j))],
        out_specs=(pl.BlockSpec((n, tc), lambda j: (0, j)), pl.BlockSpec((2, tc), lambda j: (0, j))),
        compiler_params=_params("parallel"),
        name="filter_taps",
    )(hid, w_out, deltas4)

    a = _outer_dft(_hi_lo(gt_fwd), taps.reshape(1, n1h, GRID_W, cols), 0, c=cols, tl=1024)[0]
    s4 = sums.reshape(2, 2, 2, width)
    bwd0 = taps[0].reshape(2, 2, width)[:, 1]
    stats = jnp.stack([s4[0, :, 0], s4[1, :, 0], s4[0, :, 1], s4[1, :, 1], bwd0], axis=1)

    kb = min(INNER_BINS, n1h)

    def tap_spec(side):
        return pl.BlockSpec((2, kb, GRID_W, width), lambda k1, f: (0, k1, 0, 2 * f + side))

    f_hi, f_lo = _hi_lo(f_fwd)
    f_spec = pl.BlockSpec((2 * GRID_W, 2 * GRID_W), lambda k1, f: (0, 0))
    return pl.pallas_call(
        functools.partial(_spectrum_kernel, n=n),
        out_shape=jax.ShapeDtypeStruct((2, 2, n1h, GRID_W, width), F32),
        grid=(n1h // kb, 2),
        in_specs=[tap_spec(0), tap_spec(1), f_spec, f_spec,
                  pl.BlockSpec((None, 5, width), lambda k1, f: (f, 0, 0)),
                  pl.BlockSpec((None, 1, width), lambda k1, f: (f, 0, 0))],
        out_specs=pl.BlockSpec((None, 2, kb, GRID_W, width), lambda k1, f: (f, 0, k1, 0, 0)),
        compiler_params=_params("parallel", "parallel"),
        name="filter_spectrum",
    )(a, a, f_hi, f_lo, stats, hy_d.reshape(2, 1, width))


def _inner_conv_kernel(a_ref, f_ref, fi_ref, h_ref, o_ref):
    n2 = GRID_W
    for i in range(a_ref.shape[1]):
        z = _dot(f_ref[...], jnp.concatenate([a_ref[0, i], a_ref[1, i]], axis=0))
        zr, zi = z[:n2], z[n2:]
        hr, hi = h_ref[0, i], h_ref[1, i]
        stacked = jnp.concatenate([zr * hr - zi * hi, zr * hi + zi * hr], axis=0).astype(BF16)
        w = _dot(fi_ref[...], stacked).astype(BF16)
        o_ref[0, i] = w[:n2]
        o_ref[1, i] = w[n2:]


def _inner_conv(a, f_fwd, f_inv, spec, filt):
    bsz, _, n1h, n2, c = a.shape
    kb = min(INNER_BINS, n1h)
    return pl.pallas_call(
        _inner_conv_kernel,
        out_shape=jax.ShapeDtypeStruct(a.shape, BF16),
        grid=(n1h // kb, bsz),
        in_specs=[pl.BlockSpec((None, 2, kb, n2, c), lambda k1, b: (b, 0, k1, 0, 0)),
                  pl.BlockSpec((2 * n2, 2 * n2), lambda k1, b: (0, 0)),
                  pl.BlockSpec((2 * n2, 2 * n2), lambda k1, b: (0, 0)),
                  pl.BlockSpec((None, 2, kb, n2, c), lambda k1, b: (filt, 0, k1, 0, 0))],
        out_specs=pl.BlockSpec((None, 2, kb, n2, c), lambda k1, b: (b, 0, k1, 0, 0)),
        compiler_params=_params("parallel", "arbitrary"),
        name="inner_conv",
    )(a, f_fwd, f_inv, spec)


def _outer_idft_gate_kernel(gi_ref, g_ref, b_ref, x_ref, o_ref, *, again):
    n1h, _, tl = x_ref.shape
    parts = []
    for h, (bs, xs) in enumerate(zip(_slabs(b_ref[...].astype(F32)), _slabs(x_ref[...]))):
        stacked = bs.reshape(2 * n1h * OUTER_ROWS, tl).astype(BF16)
        y = _dot(gi_ref[h], stacked) * xs.reshape(n1h * OUTER_ROWS, tl)
        if again:
            parts.append(_dot(g_ref[h], y.astype(BF16)).reshape(2, n1h, OUTER_ROWS, tl))
        else:
            parts.append(y.reshape(n1h, OUTER_ROWS, tl))
    o_ref[...] = jnp.concatenate(parts, axis=-2).astype(o_ref.dtype)


def _outer_idft_gate(g_inv, g_fwd, b, px4, seg, *, again, tl):
    bsz, _, n1h, n2, c = b.shape
    tl = min(tl, c)
    per = c // tl
    rows = BF16_ROWS
    if again:
        out_shape = jax.ShapeDtypeStruct((bsz, 2, n1h, n2, c), BF16)
        out_spec = pl.BlockSpec((None, 2, n1h, rows, tl), lambda j, bb, l: (bb, 0, 0, j, l))
    else:
        out_shape = jax.ShapeDtypeStruct((bsz, n1h, n2, c), BF16)
        out_spec = pl.BlockSpec((None, n1h, rows, tl), lambda j, bb, l: (bb, 0, j, l))
    return pl.pallas_call(
        functools.partial(_outer_idft_gate_kernel, again=again),
        out_shape=out_shape,
        grid=(n2 // rows, bsz, per),
        in_specs=[pl.BlockSpec((rows // OUTER_ROWS,) + g_inv.shape[1:], lambda j, bb, l: (j, 0, 0)),
                  pl.BlockSpec((rows // OUTER_ROWS,) + g_fwd.shape[1:], lambda j, bb, l: (j, 0, 0)),
                  pl.BlockSpec((None, 2, n1h, rows, tl), lambda j, bb, l: (bb, 0, 0, j, l)),
                  pl.BlockSpec((None, n1h, rows, tl), lambda j, bb, l: (bb, 0, j, seg * per + l))],
        out_specs=out_spec,
        compiler_params=_params("parallel", "parallel", "parallel"),
        name="outer_idft_gate",
    )(g_inv, g_fwd, b, px4)


def _hyena_mix(px, spec, consts, *, bsz, n, width):
    g_fwd, g_inv, f_fwd, f_inv = consts
    g_fwd_b, g_inv_b = jnp.asarray(g_fwd, BF16), jnp.asarray(g_inv, BF16)
    f_fwd_b, f_inv_b = jnp.asarray(f_fwd, BF16), jnp.asarray(f_inv, BF16)
    n1h = n // GRID_W
    px4 = px.reshape(bsz, n1h, GRID_W, -1)
    tl = width
    a = _outer_dft((g_fwd_b,), px4, 5, c=width, tl=tl)
    a = _inner_conv(a, f_fwd_b, f_inv_b, spec, 0)
    a = _outer_idft_gate(g_inv_b, g_fwd_b, a, px4, 6, again=True, tl=tl)
    a = _inner_conv(a, f_fwd_b, f_inv_b, spec, 1)
    hy = _outer_idft_gate(g_inv_b, g_fwd_b, a, px4, 7, again=False, tl=tl)
    return hy.reshape(bsz * n, width)


def _out_proj_kernel(x_ref, hg_ref, hy_ref, wa_ref, wb_ref, gt_ref, o_ref):
    mix = _dot(hg_ref[...], wa_ref[...]) + _dot(hy_ref[...], wb_ref[...])
    o_ref[...] = x_ref[...] + gt_ref[0] * mix


def _out_proj(x, hg, hy, w, gate, *, tokens_per_mod, tm):
    t, d = x.shape
    wdt = hg.shape[1]
    tm = min(tm, tokens_per_mod)
    return pl.pallas_call(
        _out_proj_kernel,
        out_shape=jax.ShapeDtypeStruct((t, d), F32),
        grid=(t // tm,),
        in_specs=[pl.BlockSpec((tm, d), lambda i: (i, 0)),
                  pl.BlockSpec((tm, wdt), lambda i: (i, 0)),
                  pl.BlockSpec((tm, wdt), lambda i: (i, 0)),
                  pl.BlockSpec((wdt, d), lambda i: (0, 0)),
                  pl.BlockSpec((wdt, d), lambda i: (1, 0)),
                  pl.BlockSpec((1, 1, d), lambda i: ((i * tm) // tokens_per_mod, 0, 0))],
        out_specs=pl.BlockSpec((tm, d), lambda i: (i, 0)),
        compiler_params=_params("parallel"),
        name="out_proj",
    )(x, hg, hy, w, w, gate)


def _tiles(d_ff):
    return TOKEN_TILE, 2 * TOKEN_TILE, HIDDEN_TILE if d_ff % HIDDEN_TILE == 0 else d_ff


def kernel(x, c, ctx, c_ctx, ada_w, ada_b, norm_ffn1, ffn1_w1, ffn1_w3, ffn1_w2, norm_mix, w_in, hg_lb_logits, hg_norm, hy_conv_w, hy_conv_b, filt_w1, filt_b1, filt_w2, filt_b2, filt_w3, filt_b3, filt_freq, filt_w_out, hy_d, w_out, norm_ffn2, ffn2_w1, ffn2_w3, ffn2_w2, final_norm):
    bsz, n, d = x.shape
    n_ctx = ctx.shape[1]
    depth = ada_w.shape[0]
    assert depth == 1, "single-layer block"
    width = d // 2
    n_seg = w_in.shape[2] // width
    assert n_seg == 8 and n % GRID_W == 0 and n_ctx % SCAN_CHUNK == 0 and width % HEAD_DIM == 0
    tm, tm_wide, tf = _tiles(ffn1_w1.shape[2])
    l = 0

    rows = -(-(bsz + 1) // 8) * 8
    cs = jnp.concatenate([c, c_ctx[None, :], jnp.zeros((rows - bsz - 1, d), F32)], axis=0)
    mods = _ada(cs, ada_w[l], ada_b[l]).reshape(rows, N_MOD, d)
    mx = [mods[:bsz, i][:, None, :] for i in range(N_MOD)]
    mc = [mods[bsz:bsz + 1, i][:, None, :] for i in range(N_MOD)]

    bf = _to_bf16
    w1a, w3a, w2a = bf(ffn1_w1[l]), bf(ffn1_w3[l]), bf(ffn1_w2[l])
    xt = x.reshape(bsz * n, d)
    yt = ctx.reshape(bsz * n_ctx, d)

    xt = _ffn(xt, mx[0], mx[1], mx[2], norm_ffn1[l], w1a, w3a, w2a, final_norm,
              tokens_per_mod=n, final_norm=False, tm=tm, tf=tf)
    yt = _ffn(yt, mc[0], mc[1], mc[2], norm_ffn1[l], w1a, w3a, w2a, final_norm,
              tokens_per_mod=bsz * n_ctx, final_norm=False, tm=tm, tf=tf)

    w_in_b = bf(w_in[l])
    pc = _proj(yt, mc[3], mc[4], norm_mix[l], w_in_b, hy_conv_w[l], hy_conv_b[l], n_seg=3, seg_w=width,
               conv_from=n_seg, tokens_per_mod=bsz * n_ctx, tm=tm)
    px = _proj(xt, mx[3], mx[4], norm_mix[l], w_in_b, hy_conv_w[l], hy_conv_b[l], n_seg=n_seg, seg_w=width,
               conv_from=5, tokens_per_mod=n, tm=tm_wide)
    cols = n_seg * width

    hg = _hgrn(px.reshape(bsz, n, cols), pc.reshape(bsz, n_ctx, 3 * width), hg_lb_logits, hg_norm[l], width=width)

    consts = _dft_constants(n)
    filt = (filt_w1[l], filt_b1[l], filt_w2[l], filt_b2[l], filt_w3[l], filt_b3[l], filt_freq[l], filt_w_out[l])
    spec = _hyena_spectra(n, width, filt, hy_d[l], consts)
    hy = _hyena_mix(px, spec, consts, bsz=bsz, n=n, width=width)

    xt = _out_proj(xt, hg.reshape(bsz * n, width), hy, bf(w_out[l]), mx[5],
                   tokens_per_mod=n, tm=tm)

    out = _ffn(xt, mx[6], mx[7], mx[8], norm_ffn2[l], bf(ffn2_w1[l]), bf(ffn2_w3[l]), bf(ffn2_w2[l]), final_norm,
               tokens_per_mod=n, final_norm=True, tm=tm, tf=tf)
    return out.reshape(bsz, n, d)
```

```python
import functools
import math

import numpy as np
import jax
import jax.numpy as jnp
from jax import lax
from jax.experimental import pallas as pl
from jax.experimental.pallas import tpu as pltpu

F32 = jnp.float32
BF16 = jnp.bfloat16
EPS = 1e-6
N_MOD = 9
HEAD_DIM = 128
GRID_W = 64
CAST_ROWS = 256
TOKEN_TILE = 512
HIDDEN_TILE = 512
SCAN_CHUNK = 64
PREP_CHUNKS = 4
SCAN_BLOCKS = 4
OUTER_ROWS = 8
BF16_ROWS = 16
INNER_BINS = 16
SPECTRUM_BINS = 8
FILT_EMB = 33
DECAY_TARGET = 1e-2
DECAY_FAST_PCT = 0.3
DECAY_SLOW_PCT = 1.5
VMEM_LIMIT_BYTES = 56 * 1024 * 1024
HIGHEST = lax.Precision.HIGHEST


def _params(*semantics):
    return pltpu.CompilerParams(dimension_semantics=semantics, vmem_limit_bytes=VMEM_LIMIT_BYTES)


def _dot(a, b):
    return jnp.dot(a, b, preferred_element_type=F32)


def _silu(a):
    return a * jax.nn.sigmoid(a)


def _cast_kernel(x_ref, o_ref):
    o_ref[...] = x_ref[...].astype(BF16)


def _to_bf16(w):
    r, c = w.shape
    rb = CAST_ROWS if r % CAST_ROWS == 0 else r
    return pl.pallas_call(
        _cast_kernel,
        out_shape=jax.ShapeDtypeStruct((r, c), BF16),
        grid=(r // rb,),
        in_specs=[pl.BlockSpec((rb, c), lambda i: (i, 0))],
        out_specs=pl.BlockSpec((rb, c), lambda i: (i, 0)),
        compiler_params=_params("parallel"),
        name="to_bf16",
    )(w)


def _ada_kernel(c_ref, w_ref, b_ref, o_ref):
    h = _silu(c_ref[...]).astype(BF16)
    o_ref[...] = _dot(h, w_ref[...].astype(BF16)) + b_ref[...]


def _ada(cs, w, b):
    rows, d = cs.shape
    n = w.shape[1]
    tn = d // 2
    return pl.pallas_call(
        _ada_kernel,
        out_shape=jax.ShapeDtypeStruct((rows, n), F32),
        grid=(n // tn,),
        in_specs=[pl.BlockSpec((rows, d), lambda j: (0, 0)),
                  pl.BlockSpec((d, tn), lambda j: (0, j)),
                  pl.BlockSpec((1, tn), lambda j: (0, j))],
        out_specs=pl.BlockSpec((rows, tn), lambda j: (0, j)),
        compiler_params=_params("arbitrary"),
        name="ada_mod",
    )(cs, w, b.reshape(1, n))


def _norm_mod(x, gain, shift, scale):
    y = x * lax.rsqrt(jnp.mean(x * x, axis=-1, keepdims=True) + EPS)
    return (y * (gain * (1.0 + scale)) + shift).astype(BF16)


def _ffn_kernel(x_ref, sh_ref, sc_ref, gt_ref, g_ref, w1_ref, w3_ref, w2_ref, fin_ref, o_ref, h_scr, *, final_norm):
    j = pl.program_id(1)

    @pl.when(j == 0)
    def _():
        h_scr[...] = _norm_mod(x_ref[...], g_ref[...], sh_ref[0], sc_ref[0])
        o_ref[...] = jnp.zeros_like(o_ref)

    h = h_scr[...]
    a = _dot(h, w1_ref[...])
    b = _dot(h, w3_ref[...])
    act = (_silu(a) * b).astype(BF16)
    o_ref[...] += _dot(act, w2_ref[...])

    @pl.when(j == pl.num_programs(1) - 1)
    def _():
        out = x_ref[...] + (0.5 * gt_ref[0]) * o_ref[...]
        if final_norm:
            out = out * lax.rsqrt(jnp.mean(out * out, axis=-1, keepdims=True) + EPS) * fin_ref[...]
        o_ref[...] = out


def _ffn(x, shift, scale, gate, gain, w1, w3, w2, fin, *, tokens_per_mod, final_norm, tm, tf):
    t, d = x.shape
    f = w1.shape[1]
    tm = min(tm, tokens_per_mod)
    mod_spec = pl.BlockSpec((1, 1, d), lambda i, j: ((i * tm) // tokens_per_mod, 0, 0))
    vec_spec = pl.BlockSpec((1, d), lambda i, j: (0, 0))
    return pl.pallas_call(
        functools.partial(_ffn_kernel, final_norm=final_norm),
        out_shape=jax.ShapeDtypeStruct((t, d), F32),
        grid=(t // tm, f // tf),
        in_specs=[pl.BlockSpec((tm, d), lambda i, j: (i, 0)),
                  mod_spec, mod_spec, mod_spec, vec_spec,
                  pl.BlockSpec((d, tf), lambda i, j: (0, j)),
                  pl.BlockSpec((d, tf), lambda i, j: (0, j)),
                  pl.BlockSpec((tf, d), lambda i, j: (j, 0)),
                  vec_spec],
        out_specs=pl.BlockSpec((tm, d), lambda i, j: (i, 0)),
        scratch_shapes=[pltpu.VMEM((tm, d), BF16)],
        compiler_params=_params("parallel", "arbitrary"),
        name="swiglu_ffn",
    )(x, shift, scale, gate, gain.reshape(1, d), w1, w3, w2, fin.reshape(1, d))


def _proj_kernel(x_ref, sh_ref, sc_ref, g_ref, w_ref, cw_ref, cb_ref, o_ref, h_scr, *, conv_from):
    j = pl.program_id(1)

    @pl.when(j == 0)
    def _():
        h_scr[...] = _norm_mod(x_ref[...], g_ref[...], sh_ref[0], sc_ref[0])

    p = _dot(h_scr[...], w_ref[...])

    @pl.when(j < conv_from)
    def _():
        o_ref[...] = p

    @pl.when(j >= conv_from)
    def _():
        tm, w = p.shape
        rows3 = lambda t: t.reshape(tm // GRID_W, GRID_W, w)
        y = (rows3(pltpu.roll(p, 1, axis=0)) * cw_ref[0] + rows3(p) * cw_ref[1]
             + rows3(pltpu.roll(p, tm - 1, axis=0)) * cw_ref[2] + cb_ref[...])
        o_ref[...] = y.reshape(tm, w)


def _proj(x, shift, scale, gain, w, conv_w, conv_b, *, n_seg, seg_w, conv_from, tokens_per_mod, tm):
    t, d = x.shape
    tm = min(tm, tokens_per_mod)
    mod_spec = pl.BlockSpec((1, 1, d), lambda i, j: ((i * tm) // tokens_per_mod, 0, 0))
    conv_idx = lambda i, j: (0, jnp.maximum(j - conv_from, 0))
    pos = jnp.arange(GRID_W)[None, :, None]
    edge = jnp.stack([pos > 0, pos >= 0, pos < GRID_W - 1], axis=0)[:, 0]
    conv_w = jnp.where(edge, conv_w[:, None, :], 0.0)
    return pl.pallas_call(
        functools.partial(_proj_kernel, conv_from=conv_from),
        out_shape=jax.ShapeDtypeStruct((t, n_seg * seg_w), F32),
        grid=(t // tm, n_seg),
        in_specs=[pl.BlockSpec((tm, d), lambda i, j: (i, 0)),
                  mod_spec, mod_spec,
                  pl.BlockSpec((1, d), lambda i, j: (0, 0)),
                  pl.BlockSpec((d, seg_w), lambda i, j: (0, j)),
                  pl.BlockSpec((3, GRID_W, seg_w), lambda i, j: (0, 0, jnp.maximum(j - conv_from, 0))),
                  pl.BlockSpec((1, seg_w), conv_idx)],
        out_specs=pl.BlockSpec((tm, seg_w), lambda i, j: (i, j)),
        scratch_shapes=[pltpu.VMEM((tm, d), BF16)],
        compiler_params=_params("parallel", "arbitrary"),
        name="in_proj",
    )(x, shift, scale, gain.reshape(1, d), w, conv_w, conv_b.reshape(1, -1))


def _split3(g):
    g1 = g.astype(BF16)
    r1 = g - g1.astype(F32)
    g2 = r1.astype(BF16)
    return g1, g2, (r1 - g2.astype(F32)).astype(BF16)


def _split_dot(tri, g):
    g1, g2, g3 = _split3(g)
    return _dot(tri, g1) + _dot(tri, g2) + _dot(tri, g3)


def _gates(a, lb):
    f = lb + (1.0 - lb) * jax.nn.sigmoid(a)
    return 1.0 - f, jnp.log(f)


def _tn_dot(a, b):
    return lax.dot_general(a, b, (((0,), (0,)), ((), ())), preferred_element_type=F32)


def _nt_dot(a, b):
    return lax.dot_general(a, b, (((1,), (1,)), ((), ())), preferred_element_type=F32)


def _hgrn_kernel(af_ref, ab_ref, v_ref, q_ref, gate_ref, caf_ref, cab_ref, cv_ref, lbl_ref, gain_ref,
                 o_ref, of_scr, ob_scr, sf_scr, sb_scr, qin_scr, qt_scr, kt_scr, kd_scr, dec_scr,
                 *, n_chunks, n_ctx_chunks):
    c = SCAN_CHUNK
    mid = c // 2
    rows_p = PREP_CHUNKS * c
    prow = lax.broadcasted_iota(jnp.int32, (rows_p, rows_p), 0)
    pcol = lax.broadcasted_iota(jnp.int32, (rows_p, rows_p), 1)
    same = (prow // c) == (pcol // c)
    tri_fwd_p = jnp.where(same & (prow >= pcol), 1.0, 0.0).astype(BF16)
    tri_bwd_p = jnp.where(same & (pcol >= prow), 1.0, 0.0).astype(BF16)

    def lower_bound(direction):
        lg = lbl_ref[direction]
        ex = jnp.exp(lg - jnp.max(lg, axis=0, keepdims=True))
        return ex[0:1, :] / jnp.sum(ex, axis=0, keepdims=True)

    lb_f = lower_bound(0)
    lb_b = lower_bound(1)
    q_scale = HEAD_DIM ** -0.5

    n_blocks = n_chunks // PREP_CHUNKS
    n_steps = n_blocks // SCAN_BLOCKS

    def step_blocks(it):
        fwd = [(0, it * SCAN_BLOCKS + u) for u in range(SCAN_BLOCKS)]
        return fwd + [(1, n_blocks - 1 - bi) for _, bi in fwd]

    def prepare(it, s):
        it = jnp.minimum(it, n_steps - 1)
        blocks = step_blocks(it)
        los = [pl.multiple_of(bi * rows_p, rows_p) for _, bi in blocks]
        gates = [_gates((af_ref, ab_ref)[d][pl.ds(lo, rows_p), :], (lb_f, lb_b)[d]) for (d, _), lo in zip(blocks, los)]
        sums = {}
        for d, tri_p in ((0, tri_fwd_p), (1, tri_bwd_p)):
            slots_d = [slot for slot, (dd, _) in enumerate(blocks) if dd == d]
            wide = _dot(tri_p, jnp.concatenate([p for slot in slots_d for p in _split3(gates[slot][1])], axis=1))
            for n_s, slot in enumerate(slots_d):
                parts = [wide[:, (3 * n_s + m) * HEAD_DIM:(3 * n_s + m + 1) * HEAD_DIM] for m in range(3)]
                sums[slot] = parts[0] + parts[1] + parts[2]
        for slot, ((d, _), lo) in enumerate(zip(blocks, los)):
            end_row = (c - 1, 0)[d]
            k = gates[slot][0]
            b = sums[slot].reshape(PREP_CHUNKS, c, HEAD_DIM)
            b_mid = b[:, mid:mid + 1, :]
            b_end = b[:, end_row:end_row + 1, :]
            q_t = (q_ref[pl.ds(lo, rows_p), :] * q_scale).reshape(PREP_CHUNKS, c, HEAD_DIM) * jnp.exp(b - b_mid)
            k_t = k.reshape(PREP_CHUNKS, c, HEAD_DIM) * jnp.exp(b_mid - b)
            flat = lambda t: t.reshape(rows_p, HEAD_DIM).astype(BF16)
            qt_scr[s, slot] = flat(q_t)
            kt_scr[s, slot] = flat(k_t)
            qin_scr[s, slot] = flat(q_t * jnp.exp(b_mid))
            kd_scr[s, slot] = flat(k_t * jnp.exp(b_end - b_mid))
            dec_scr[s, slot] = jnp.broadcast_to(jnp.exp(b_end), (PREP_CHUNKS, 8, HEAD_DIM))

    zeros_blk = jnp.zeros((c, HEAD_DIM), BF16)

    def block_diag(t):
        return jnp.concatenate(
            [jnp.concatenate([t[j * c:(j + 1) * c] if m == j else zeros_blk for m in range(PREP_CHUNKS)], axis=1)
             for j in range(PREP_CHUNKS)], axis=0)

    masks = (same & (prow >= pcol), same & (pcol >= prow))
    st_refs = (sf_scr, sb_scr)
    out_refs = (of_scr, ob_scr)
    orders = (tuple(range(PREP_CHUNKS)), tuple(range(PREP_CHUNKS - 1, -1, -1)))

    gain = gain_ref[...]

    def scan(it, s, finish):
        streams = step_blocks(it)
        los = [pl.multiple_of(bi * rows_p, rows_p) for _, bi in streams]
        vbs = [v_ref[pl.ds(lo, rows_p), :].astype(BF16) for lo in los]
        scores = [_nt_dot(qt_scr[s, slot], kt_scr[s, slot]) for slot in range(len(streams))]
        upds = [_tn_dot(vb, block_diag(kd_scr[s, slot])) for slot, vb in enumerate(vbs)]
        outs = [_dot(jnp.where(masks[d], sc, 0.0).astype(BF16), vb) for (d, _), sc, vb in zip(streams, scores, vbs)]
        seen = [[None] * PREP_CHUNKS for _ in streams]
        for d in (0, 1):
            st = st_refs[d][...]
            for slot, (ds_, _) in enumerate(streams):
                if ds_ != d:
                    continue
                for j in orders[d]:
                    seen[slot][j] = st.astype(BF16)
                    st = st * dec_scr[s, slot, j][0:1, :] + upds[slot][:, j * HEAD_DIM:(j + 1) * HEAD_DIM]
            st_refs[d][...] = st
        for slot, ((d, _), lo) in enumerate(zip(streams, los)):
            q_in = qin_scr[s, slot]
            carried = jnp.concatenate([_nt_dot(q_in[j * c:(j + 1) * c], seen[slot][j]) for j in range(PREP_CHUNKS)],
                                      axis=0)
            rows = pl.ds(lo, rows_p)
            if finish:
                o = outs[slot] + carried + out_refs[1 - d][rows, :]
                o = o * lax.rsqrt(jnp.mean(o * o, axis=-1, keepdims=True) + EPS) * gain
                o_ref[rows, :] = (o * _silu(gate_ref[rows, :])).astype(o_ref.dtype)
            else:
                out_refs[d][rows, :] = outs[slot] + carried

    n_ctx_blocks = n_ctx_chunks // PREP_CHUNKS
    ctx_state = [jnp.zeros((HEAD_DIM, HEAD_DIM), F32), jnp.zeros((HEAD_DIM, HEAD_DIM), F32)]
    for step in range(n_ctx_blocks):
        upds, decs = [], []
        for d, bi in ((0, step), (1, n_ctx_blocks - 1 - step)):
            a_ref, lb, tri_p, end_row = ((caf_ref, lb_f, tri_fwd_p, c - 1), (cab_ref, lb_b, tri_bwd_p, 0))[d]
            rows_c = pl.ds(bi * rows_p, rows_p)
            k, g = _gates(a_ref[rows_c, :], lb)
            b = _split_dot(tri_p, g).reshape(PREP_CHUNKS, c, HEAD_DIM)
            b_end = b[:, end_row:end_row + 1, :]
            k_d = (k.reshape(PREP_CHUNKS, c, HEAD_DIM) * jnp.exp(b_end - b)).reshape(rows_p, HEAD_DIM).astype(BF16)
            upds.append(_tn_dot(cv_ref[rows_c, :].astype(BF16), block_diag(k_d)))
            decs.append(jnp.exp(b_end))
        for d in (0, 1):
            for j in orders[d]:
                ctx_state[d] = ctx_state[d] * decs[d][j] + upds[d][:, j * HEAD_DIM:(j + 1) * HEAD_DIM]
    sf_scr[...] = ctx_state[0]
    sb_scr[...] = ctx_state[1]

    prepare(0, 0)

    def body(finish):
        def pair(i, carry):
            scan(2 * i, 0, finish)
            prepare(2 * i + 1, 1)
            scan(2 * i + 1, 1, finish)
            prepare(2 * i + 2, 0)
            return carry
        return pair

    lax.fori_loop(0, n_steps // 4, body(False), 0)
    lax.fori_loop(n_steps // 4, n_steps // 2, body(True), 0)


def _hgrn(px, pc, lb_logits, gain, *, width):
    bsz, n, _ = px.shape
    n_ctx = pc.shape[1]
    heads = width // HEAD_DIM
    hd = HEAD_DIM

    def seg(s):
        return pl.BlockSpec((None, n, hd), lambda b, h: (b, 0, s * heads + h))

    def cseg(s):
        return pl.BlockSpec((None, n_ctx, hd), lambda b, h: (b, 0, s * heads + h))

    depth1 = lb_logits.shape[1]
    slots = 2 * SCAN_BLOCKS
    assert (n // SCAN_CHUNK) % (4 * SCAN_BLOCKS * PREP_CHUNKS) == 0, "scan steps come in pairs, per half"
    assert (n_ctx // SCAN_CHUNK) % PREP_CHUNKS == 0, "the context prefix is scanned in whole blocks"
    return pl.pallas_call(
        functools.partial(_hgrn_kernel, n_chunks=n // SCAN_CHUNK, n_ctx_chunks=n_ctx // SCAN_CHUNK),
        out_shape=jax.ShapeDtypeStruct((bsz, n, width), BF16),
        grid=(bsz, heads),
        in_specs=[seg(0), seg(1), seg(2), seg(3), seg(4), cseg(0), cseg(1), cseg(2),
                  pl.BlockSpec((2, depth1, hd), lambda b, h: (0, 0, h)),
                  pl.BlockSpec((1, hd), lambda b, h: (0, h))],
        out_specs=pl.BlockSpec((None, n, hd), lambda b, h: (b, 0, h)),
        scratch_shapes=[pltpu.VMEM((n, hd), F32), pltpu.VMEM((n, hd), F32),
                        pltpu.VMEM((hd, hd), F32), pltpu.VMEM((hd, hd), F32),
                        *[pltpu.VMEM((2, slots, PREP_CHUNKS * SCAN_CHUNK, hd), BF16) for _ in range(4)],
                        pltpu.VMEM((2, slots, PREP_CHUNKS, 8, hd), F32)],
        compiler_params=_params("parallel", "parallel"),
        name="hgrn_scan",
    )(px, px, px, px, px, pc, pc, pc, lb_logits, gain.reshape(1, width))


def _dft_constants(n):
    n2 = GRID_W
    n1h = n // n2
    m2r = np.arange(n2)[None, :]
    k2 = np.arange(n2)[:, None]
    f_ang = -2.0 * np.pi * k2 * m2r / n2
    fr, fi = np.cos(f_ang), np.sin(f_ang)
    f_fwd = np.block([[fr, -fi], [fi, fr]])
    f_inv = np.block([[fr, fi], [-fi, fr]])
    k1 = np.arange(n1h)[:, None, None]
    m1 = np.arange(n1h)[None, :, None]
    m2 = np.arange(n2)[None, None, :]
    theta = -2.0 * np.pi * (k1 + 0.5) * (m1 / (2 * n1h) + m2 / (2 * n))
    parts = np.stack([np.cos(theta), np.sin(theta)], axis=0)
    parts = parts.reshape(2, n1h, n1h, n2 // OUTER_ROWS, OUTER_ROWS)
    eye = np.eye(OUTER_ROWS)
    rows = n1h * OUTER_ROWS
    fwd = np.einsum("pknsj,jm->spkjnm", parts, eye).reshape(n2 // OUTER_ROWS, 2 * rows, rows)
    inv = np.einsum("pknsj,jm->snjpkm", parts, eye).reshape(n2 // OUTER_ROWS, rows, 2 * rows)
    return fwd, inv, f_fwd, f_inv


def _filt_mlp_kernel(z_ref, w1_ref, b1_ref, w2_ref, b2_ref, w3_ref, b3_ref, fr_ref, o_ref):
    fr = fr_ref[...]
    hp = lambda a, b: jnp.dot(a, b, precision=HIGHEST, preferred_element_type=F32)
    hid = jnp.sin(fr * (hp(z_ref[...], w1_ref[...]) + b1_ref[...]))
    hid = jnp.sin(fr * (hp(hid, w2_ref[...]) + b2_ref[...]))
    o_ref[...] = jnp.sin(fr * (hp(hid, w3_ref[...]) + b3_ref[...]))


def _filt_taps_kernel(hid_ref, wo_ref, dl_ref, h_ref, s_ref, *, n):
    hid = hid_ref[...]
    hid_hi = hid.astype(BF16)
    h = _dot3(hid_hi, (hid - hid_hi.astype(F32)).astype(BF16), wo_ref[...])
    t = lax.broadcasted_iota(jnp.int32, h.shape, 0).astype(F32) * (1.0 / max(n - 1, 1))
    hw = h * jnp.exp(-t * dl_ref[...])
    h_ref[...] = hw
    s_ref[0:1, :] = jnp.sum(jnp.abs(hw), axis=0, keepdims=True)
    s_ref[1:2, :] = jnp.abs(hw[0:1, :])


def _spectrum_kernel(af_ref, ab_ref, f_ref, fl_ref, s_ref, d_ref, o_ref, *, n):
    n2 = GRID_W
    l1 = s_ref[0:1, :] + s_ref[2:3, :] - s_ref[3:4, :]
    hb0 = s_ref[4:5, :]
    norm = 1.0 / l1
    scale = 2.0 / (2 * n)
    for i in range(af_ref.shape[1]):
        def inner(a_ref):
            z = _dot3(f_ref[...], fl_ref[...], jnp.concatenate([a_ref[0, i], a_ref[1, i]], axis=0))
            return z[:n2], z[n2:]

        zfr, zfi = inner(af_ref)
        zbr, zbi = inner(ab_ref)
        o_ref[0, i] = scale * ((zfr + zbr - hb0) * norm + d_ref[...])
        o_ref[1, i] = scale * ((zfi - zbi) * norm)


def _dot3(a_hi, a_lo, x):
    x_hi = x.astype(BF16)
    x_lo = (x - x_hi.astype(F32)).astype(BF16)
    return _dot(a_hi, x_hi) + _dot(a_hi, x_lo) + _dot(a_lo, x_hi)


def _hi_lo(a):
    hi = np.asarray(a, np.float64).astype(BF16)
    lo = (a - hi.astype(np.float64)).astype(BF16)
    return jnp.asarray(hi), jnp.asarray(lo)


def _slabs(x):
    return [x[..., h * OUTER_ROWS:(h + 1) * OUTER_ROWS, :] for h in range(x.shape[-2] // OUTER_ROWS)]


def _outer_dft_kernel(*refs, high):
    g_refs, x_ref, o_ref = refs[:-2], refs[-2], refs[-1]
    n1h, _, tl = x_ref.shape
    parts = []
    for h, xs in enumerate(_slabs(x_ref[...])):
        xs = xs.reshape(n1h * OUTER_ROWS, tl)
        r = _dot3(g_refs[0][h], g_refs[1][h], xs) if high else _dot(g_refs[0][h], xs.astype(BF16))
        parts.append(r.reshape(2, n1h, OUTER_ROWS, tl))
    o_ref[...] = jnp.concatenate(parts, axis=2).astype(o_ref.dtype)


def _outer_dft(gs, x4, seg, *, c, tl):
    high = len(gs) == 2
    s, n1h, n2, _ = x4.shape
    tl = min(tl, c)
    per = c // tl
    rows = OUTER_ROWS if high else BF16_ROWS
    g_spec = pl.BlockSpec((rows // OUTER_ROWS,) + gs[0].shape[1:], lambda j, b, l: (j, 0, 0))
    return pl.pallas_call(
        functools.partial(_outer_dft_kernel, high=high),
        out_shape=jax.ShapeDtypeStruct((s, 2, n1h, n2, c), F32 if high else BF16),
        grid=(n2 // rows, s, per),
        in_specs=[g_spec] * len(gs) + [pl.BlockSpec((None, n1h, rows, tl),
                                                    lambda j, b, l: (b, 0, j, seg * per + l))],
        out_specs=pl.BlockSpec((None, 2, n1h, rows, tl), lambda j, b, l: (b, 0, 0, j, l)),
        compiler_params=_params("parallel", "parallel", "parallel"),
        name="outer_dft",
    )(*gs, x4)


def _hyena_spectra(n, width, filt, hy_d, consts):
    w1, b1, w2, b2, w3, b3, freq, w_out = filt
    gt_fwd, _, f_fwd, _ = consts
    n1h = n // GRID_W
    order = w1.shape[1]
    pos = np.arange(n, dtype=np.float64)
    t = pos / max(n - 1, 1)
    bands = (FILT_EMB - 1) // 2
    fb = np.linspace(1e-4, bands - 1, bands)
    ang = (2 * math.pi / n) * pos[:, None] * fb[None, :]
    z = np.concatenate([t[:, None], np.cos(ang), -np.sin(ang)], axis=-1)
    emb_pad = 128
    z = jnp.asarray(np.pad(z, ((0, 0), (0, emb_pad - FILT_EMB))), F32)
    w1p = jnp.pad(w1, ((0, emb_pad - FILT_EMB), (0, 0)))
    row = lambda a: a.reshape(1, -1)
    full = lambda shape: pl.BlockSpec(shape, lambda: tuple(0 for _ in shape))
    hid = pl.pallas_call(
        _filt_mlp_kernel,
        out_shape=jax.ShapeDtypeStruct((n, order), F32),
        in_specs=[full((n, emb_pad)), full((emb_pad, order)), full((1, order)), full((order, order)),
                  full((1, order)), full((order, order)), full((1, order)), full((1, order))],
        out_specs=full((n, order)),
        compiler_params=pltpu.CompilerParams(vmem_limit_bytes=VMEM_LIMIT_BYTES),
        name="filter_mlp",
    )(z, w1p, row(b1), w2, row(b2), w3, row(b3), row(freq))

    cols = 4 * width
    deltas = np.abs(np.linspace(math.log(DECAY_TARGET) / DECAY_SLOW_PCT,
                                math.log(DECAY_TARGET) / DECAY_FAST_PCT, width))
    deltas4 = jnp.asarray(np.tile(deltas, 4)[None, :], F32)
    tc = min(512, cols)
    taps, sums = pl.pallas_call(
        functools.partial(_filt_taps_kernel, n=n),
        out_shape=(jax.ShapeDtypeStruct((n, cols), F32), jax.ShapeDtypeStruct((2, cols), F32)),
        grid=(cols // tc,),
        in_specs=[pl.BlockSpec((n, order), lambda j: (0, 0)),
                  pl.BlockSpec((order, tc), lambda j: (0, j)),
                  pl.BlockSpec((1, tc), lambda j: (0, j))],
        out_specs=(pl.BlockSpec((n, tc), lambda j: (0, j)), pl.BlockSpec((2, tc), lambda j: (0, j))),
        compiler_params=_params("parallel"),
        name="filter_taps",
    )(hid, w_out, deltas4)

    a = _outer_dft(_hi_lo(gt_fwd), taps.reshape(1, n1h, GRID_W, cols), 0, c=cols, tl=1024)[0]
    s4 = sums.reshape(2, 2, 2, width)
    bwd0 = taps[0].reshape(2, 2, width)[:, 1]
    stats = jnp.stack([s4[0, :, 0], s4[1, :, 0], s4[0, :, 1], s4[1, :, 1], bwd0], axis=1)

    kb = min(SPECTRUM_BINS, n1h)

    def tap_spec(side):
        return pl.BlockSpec((2, kb, GRID_W, width), lambda k1, f: (0, k1, 0, 2 * f + side))

    f_hi, f_lo = _hi_lo(f_fwd)
    f_spec = pl.BlockSpec((2 * GRID_W, 2 * GRID_W), lambda k1, f: (0, 0))
    return pl.pallas_call(
        functools.partial(_spectrum_kernel, n=n),
        out_shape=jax.ShapeDtypeStruct((2, 2, n1h, GRID_W, width), F32),
        grid=(n1h // kb, 2),
        in_specs=[tap_spec(0), tap_spec(1), f_spec, f_spec,
                  pl.BlockSpec((None, 5, width), lambda k1, f: (f, 0, 0)),
                  pl.BlockSpec((None, 1, width), lambda k1, f: (f, 0, 0))],
        out_specs=pl.BlockSpec((None, 2, kb, GRID_W, width), lambda k1, f: (f, 0, k1, 0, 0)),
        compiler_params=_params("parallel", "parallel"),
        name="filter_spectrum",
    )(a, a, f_hi, f_lo, stats, hy_d.reshape(2, 1, width))


def _inner_conv_kernel(a_ref, f_ref, fi_ref, h_ref, o_ref):
    n2 = GRID_W
    for i in range(a_ref.shape[1]):
        z = _dot(f_ref[...], jnp.concatenate([a_ref[0, i], a_ref[1, i]], axis=0))
        zr, zi = z[:n2], z[n2:]
        hr, hi = h_ref[0, i], h_ref[1, i]
        stacked = jnp.concatenate([zr * hr - zi * hi, zr * hi + zi * hr], axis=0).astype(BF16)
        w = _dot(fi_ref[...], stacked).astype(BF16)
        o_ref[0, i] = w[:n2]
        o_ref[1, i] = w[n2:]


def _inner_conv(a, f_fwd, f_inv, spec, filt):
    bsz, _, n1h, n2, c = a.shape
    kb = min(INNER_BINS, n1h)
    return pl.pallas_call(
        _inner_conv_kernel,
        out_shape=jax.ShapeDtypeStruct(a.shape, BF16),
        grid=(n1h // kb, bsz),
        in_specs=[pl.BlockSpec((None, 2, kb, n2, c), lambda k1, b: (b, 0, k1, 0, 0)),
                  pl.BlockSpec((2 * n2, 2 * n2), lambda k1, b: (0, 0)),
                  pl.BlockSpec((2 * n2, 2 * n2), lambda k1, b: (0, 0)),
                  pl.BlockSpec((None, 2, kb, n2, c), lambda k1, b: (filt, 0, k1, 0, 0))],
        out_specs=pl.BlockSpec((None, 2, kb, n2, c), lambda k1, b: (b, 0, k1, 0, 0)),
        compiler_params=_params("parallel", "arbitrary"),
        name="inner_conv",
    )(a, f_fwd, f_inv, spec)


def _outer_idft_gate_kernel(gi_ref, g_ref, b_ref, x_ref, o_ref, *, again):
    n1h, _, tl = x_ref.shape
    parts = []
    for h, (bs, xs) in enumerate(zip(_slabs(b_ref[...].astype(F32)), _slabs(x_ref[...]))):
        stacked = bs.reshape(2 * n1h * OUTER_ROWS, tl).astype(BF16)
        y = _dot(gi_ref[h], stacked) * xs.reshape(n1h * OUTER_ROWS, tl)
        if again:
            parts.append(_dot(g_ref[h], y.astype(BF16)).reshape(2, n1h, OUTER_ROWS, tl))
        else:
            parts.append(y.reshape(n1h, OUTER_ROWS, tl))
    o_ref[...] = jnp.concatenate(parts, axis=-2).astype(o_ref.dtype)


def _outer_idft_gate(g_inv, g_fwd, b, px4, seg, *, again, tl):
    bsz, _, n1h, n2, c = b.shape
    tl = min(tl, c)
    per = c // tl
    rows = BF16_ROWS
    if again:
        out_shape = jax.ShapeDtypeStruct((bsz, 2, n1h, n2, c), BF16)
        out_spec = pl.BlockSpec((None, 2, n1h, rows, tl), lambda j, bb, l: (bb, 0, 0, j, l))
    else:
        out_shape = jax.ShapeDtypeStruct((bsz, n1h, n2, c), BF16)
        out_spec = pl.BlockSpec((None, n1h, rows, tl), lambda j, bb, l: (bb, 0, j, l))
    return pl.pallas_call(
        functools.partial(_outer_idft_gate_kernel, again=again),
        out_shape=out_shape,
        grid=(n2 // rows, bsz, per),
        in_specs=[pl.BlockSpec((rows // OUTER_ROWS,) + g_inv.shape[1:], lambda j, bb, l: (j, 0, 0)),
                  pl.BlockSpec((rows // OUTER_ROWS,) + g_fwd.shape[1:], lambda j, bb, l: (j, 0, 0)),
                  pl.BlockSpec((None, 2, n1h, rows, tl), lambda j, bb, l: (bb, 0, 0, j, l)),
                  pl.BlockSpec((None, n1h, rows, tl), lambda j, bb, l: (bb, 0, j, seg * per + l))],
        out_specs=out_spec,
        compiler_params=_params("parallel", "parallel", "parallel"),
        name="outer_idft_gate",
    )(g_inv, g_fwd, b, px4)


def _hyena_mix(px, spec, consts, *, bsz, n, width):
    g_fwd, g_inv, f_fwd, f_inv = consts
    g_fwd_b, g_inv_b = jnp.asarray(g_fwd, BF16), jnp.asarray(g_inv, BF16)
    f_fwd_b, f_inv_b = jnp.asarray(f_fwd, BF16), jnp.asarray(f_inv, BF16)
    n1h = n // GRID_W
    px4 = px.reshape(bsz, n1h, GRID_W, -1)
    tl = width
    a = _outer_dft((g_fwd_b,), px4, 5, c=width, tl=tl)
    a = _inner_conv(a, f_fwd_b, f_inv_b, spec, 0)
    a = _outer_idft_gate(g_inv_b, g_fwd_b, a, px4, 6, again=True, tl=tl)
    a = _inner_conv(a, f_fwd_b, f_inv_b, spec, 1)
    hy = _outer_idft_gate(g_inv_b, g_fwd_b, a, px4, 7, again=False, tl=tl)
    return hy.reshape(bsz * n, width)


def _out_proj_kernel(x_ref, hg_ref, hy_ref, wa_ref, wb_ref, gt_ref, o_ref):
    mix = _dot(hg_ref[...], wa_ref[...]) + _dot(hy_ref[...], wb_ref[...])
    o_ref[...] = x_ref[...] + gt_ref[0] * mix


def _out_proj(x, hg, hy, w, gate, *, tokens_per_mod, tm):
    t, d = x.shape
    wdt = hg.shape[1]
    tm = min(tm, tokens_per_mod)
    return pl.pallas_call(
        _out_proj_kernel,
        out_shape=jax.ShapeDtypeStruct((t, d), F32),
        grid=(t // tm,),
        in_specs=[pl.BlockSpec((tm, d), lambda i: (i, 0)),
                  pl.BlockSpec((tm, wdt), lambda i: (i, 0)),
                  pl.BlockSpec((tm, wdt), lambda i: (i, 0)),
                  pl.BlockSpec((wdt, d), lambda i: (0, 0)),
                  pl.BlockSpec((wdt, d), lambda i: (1, 0)),
                  pl.BlockSpec((1, 1, d), lambda i: ((i * tm) // tokens_per_mod, 0, 0))],
        out_specs=pl.BlockSpec((tm, d), lambda i: (i, 0)),
        compiler_params=_params("parallel"),
        name="out_proj",
    )(x, hg, hy, w, w, gate)


def _tiles(d_ff):
    return TOKEN_TILE, 2 * TOKEN_TILE, HIDDEN_TILE if d_ff % HIDDEN_TILE == 0 else d_ff


def kernel(x, c, ctx, c_ctx, ada_w, ada_b, norm_ffn1, ffn1_w1, ffn1_w3, ffn1_w2, norm_mix, w_in, hg_lb_logits, hg_norm, hy_conv_w, hy_conv_b, filt_w1, filt_b1, filt_w2, filt_b2, filt_w3, filt_b3, filt_freq, filt_w_out, hy_d, w_out, norm_ffn2, ffn2_w1, ffn2_w3, ffn2_w2, final_norm):
    bsz, n, d = x.shape
    n_ctx = ctx.shape[1]
    depth = ada_w.shape[0]
    assert depth == 1, "single-layer block"
    width = d // 2
    n_seg = w_in.shape[2] // width
    assert n_seg == 8 and n % GRID_W == 0 and n_ctx % SCAN_CHUNK == 0 and width % HEAD_DIM == 0
    tm, tm_wide, tf = _tiles(ffn1_w1.shape[2])
    l = 0

    rows = -(-(bsz + 1) // 8) * 8
    cs = jnp.concatenate([c, c_ctx[None, :], jnp.zeros((rows - bsz - 1, d), F32)], axis=0)
    mods = _ada(cs, ada_w[l], ada_b[l]).reshape(rows, N_MOD, d)
    mx = [mods[:bsz, i][:, None, :] for i in range(N_MOD)]
    mc = [mods[bsz:bsz + 1, i][:, None, :] for i in range(N_MOD)]

    bf = _to_bf16
    w1a, w3a, w2a = bf(ffn1_w1[l]), bf(ffn1_w3[l]), bf(ffn1_w2[l])
    xt = x.reshape(bsz * n, d)
    yt = ctx.reshape(bsz * n_ctx, d)

    xt = _ffn(xt, mx[0], mx[1], mx[2], norm_ffn1[l], w1a, w3a, w2a, final_norm,
              tokens_per_mod=n, final_norm=False, tm=tm, tf=tf)
    yt = _ffn(yt, mc[0], mc[1], mc[2], norm_ffn1[l], w1a, w3a, w2a, final_norm,
              tokens_per_mod=bsz * n_ctx, final_norm=False, tm=tm, tf=tf)

    w_in_b = bf(w_in[l])
    pc = _proj(yt, mc[3], mc[4], norm_mix[l], w_in_b, hy_conv_w[l], hy_conv_b[l], n_seg=3, seg_w=width,
               conv_from=n_seg, tokens_per_mod=bsz * n_ctx, tm=tm)
    px = _proj(xt, mx[3], mx[4], norm_mix[l], w_in_b, hy_conv_w[l], hy_conv_b[l], n_seg=n_seg, seg_w=width,
               conv_from=5, tokens_per_mod=n, tm=tm_wide)
    cols = n_seg * width

    hg = _hgrn(px.reshape(bsz, n, cols), pc.reshape(bsz, n_ctx, 3 * width), hg_lb_logits, hg_norm[l], width=width)

    consts = _dft_constants(n)
    filt = (filt_w1[l], filt_b1[l], filt_w2[l], filt_b2[l], filt_w3[l], filt_b3[l], filt_freq[l], filt_w_out[l])
    spec = _hyena_spectra(n, width, filt, hy_d[l], consts)
    hy = _hyena_mix(px, spec, consts, bsz=bsz, n=n, width=width)

    xt = _out_proj(xt, hg.reshape(bsz * n, width), hy, bf(w_out[l]), mx[5],
                   tokens_per_mod=n, tm=tm)

    out = _ffn(xt, mx[6], mx[7], mx[8], norm_ffn2[l], bf(ffn2_w1[l]), bf(ffn2_w3[l]), bf(ffn2_w2[l]), final_norm,
               tokens_per_mod=n, final_norm=True, tm=tm, tf=tf)
    return out.reshape(bsz, n, d)
```

```python
import functools
import math

import numpy as np
import jax
import jax.numpy as jnp
from jax import lax
from jax.experimental import pallas as pl
from jax.experimental.pallas import tpu as pltpu

F32 = jnp.float32
BF16 = jnp.bfloat16
EPS = 1e-6
N_MOD = 9
HEAD_DIM = 128
GRID_W = 64
CAST_ROWS = 256
TOKEN_TILE = 512
HIDDEN_TILE = 512
SCAN_CHUNK = 64
PREP_CHUNKS = 4
SCAN_BLOCKS = 4
OUTER_ROWS = 8
BF16_ROWS = 16
INNER_BINS = 16
SPECTRUM_BINS = 8
FILT_EMB = 33
DECAY_TARGET = 1e-2
DECAY_FAST_PCT = 0.3
DECAY_SLOW_PCT = 1.5
VMEM_LIMIT_BYTES = 56 * 1024 * 1024
HIGHEST = lax.Precision.HIGHEST


def _params(*semantics):
    return pltpu.CompilerParams(dimension_semantics=semantics, vmem_limit_bytes=VMEM_LIMIT_BYTES)


def _dot(a, b):
    return jnp.dot(a, b, preferred_element_type=F32)


def _silu(a):
    return a * jax.nn.sigmoid(a)


def _cast_kernel(x_ref, o_ref):
    o_ref[...] = x_ref[...].astype(BF16)


def _to_bf16(w):
    r, c = w.shape
    rb = CAST_ROWS if r % CAST_ROWS == 0 else r
    return pl.pallas_call(
        _cast_kernel,
        out_shape=jax.ShapeDtypeStruct((r, c), BF16),
        grid=(r // rb,),
        in_specs=[pl.BlockSpec((rb, c), lambda i: (i, 0))],
        out_specs=pl.BlockSpec((rb, c), lambda i: (i, 0)),
        compiler_params=_params("parallel"),
        name="to_bf16",
    )(w)


def _ada_kernel(c_ref, w_ref, b_ref, o_ref):
    h = _silu(c_ref[...]).astype(BF16)
    o_ref[...] = _dot(h, w_ref[...].astype(BF16)) + b_ref[...]


def _ada(cs, w, b):
    rows, d = cs.shape
    n = w.shape[1]
    tn = d // 2
    return pl.pallas_call(
        _ada_kernel,
        out_shape=jax.ShapeDtypeStruct((rows, n), F32),
        grid=(n // tn,),
        in_specs=[pl.BlockSpec((rows, d), lambda j: (0, 0)),
                  pl.BlockSpec((d, tn), lambda j: (0, j)),
                  pl.BlockSpec((1, tn), lambda j: (0, j))],
        out_specs=pl.BlockSpec((rows, tn), lambda j: (0, j)),
        compiler_params=_params("arbitrary"),
        name="ada_mod",
    )(cs, w, b.reshape(1, n))


def _norm_mod(x, gain, shift, scale):
    y = x * lax.rsqrt(jnp.mean(x * x, axis=-1, keepdims=True) + EPS)
    return (y * (gain * (1.0 + scale)) + shift).astype(BF16)


def _ffn_kernel(x_ref, sh_ref, sc_ref, gt_ref, g_ref, w1_ref, w3_ref, w2_ref, fin_ref, o_ref, h_scr, *, final_norm):
    j = pl.program_id(1)

    def hidden_tile():
        h = h_scr[...]
        a = _dot(h, w1_ref[...])
        b = _dot(h, w3_ref[...])
        return _dot((_silu(a) * b).astype(BF16), w2_ref[...])

    @pl.when(j == 0)
    def _():
        h_scr[...] = _norm_mod(x_ref[...], g_ref[...], sh_ref[0], sc_ref[0])
        o_ref[...] = hidden_tile()

    @pl.when(j > 0)
    def _():
        o_ref[...] += hidden_tile()

    @pl.when(j == pl.num_programs(1) - 1)
    def _():
        out = x_ref[...] + (0.5 * gt_ref[0]) * o_ref[...]
        if final_norm:
            out = out * lax.rsqrt(jnp.mean(out * out, axis=-1, keepdims=True) + EPS) * fin_ref[...]
        o_ref[...] = out


def _ffn(x, shift, scale, gate, gain, w1, w3, w2, fin, *, tokens_per_mod, final_norm, tm, tf):
    t, d = x.shape
    f = w1.shape[1]
    tm = min(tm, tokens_per_mod)
    mod_spec = pl.BlockSpec((1, 1, d), lambda i, j: ((i * tm) // tokens_per_mod, 0, 0))
    vec_spec = pl.BlockSpec((1, d), lambda i, j: (0, 0))
    return pl.pallas_call(
        functools.partial(_ffn_kernel, final_norm=final_norm),
        out_shape=jax.ShapeDtypeStruct((t, d), F32),
        grid=(t // tm, f // tf),
        in_specs=[pl.BlockSpec((tm, d), lambda i, j: (i, 0)),
                  mod_spec, mod_spec, mod_spec, vec_spec,
                  pl.BlockSpec((d, tf), lambda i, j: (0, j)),
                  pl.BlockSpec((d, tf), lambda i, j: (0, j)),
                  pl.BlockSpec((tf, d), lambda i, j: (j, 0)),
                  vec_spec],
        out_specs=pl.BlockSpec((tm, d), lambda i, j: (i, 0)),
        scratch_shapes=[pltpu.VMEM((tm, d), BF16)],
        compiler_params=_params("parallel", "arbitrary"),
        name="swiglu_ffn",
    )(x, shift, scale, gate, gain.reshape(1, d), w1, w3, w2, fin.reshape(1, d))


def _proj_kernel(x_ref, sh_ref, sc_ref, g_ref, w_ref, cw_ref, cb_ref, o_ref, h_scr, *, conv_from):
    j = pl.program_id(1)

    def segment():
        return _dot(h_scr[...], w_ref[...])

    @pl.when(j == 0)
    def _():
        h_scr[...] = _norm_mod(x_ref[...], g_ref[...], sh_ref[0], sc_ref[0])
        o_ref[...] = segment()

    @pl.when((j > 0) & (j < conv_from))
    def _():
        o_ref[...] = segment()

    @pl.when(j >= conv_from)
    def _():
        p = segment()
        tm, w = p.shape
        rows3 = lambda t: t.reshape(tm // GRID_W, GRID_W, w)
        y = (rows3(pltpu.roll(p, 1, axis=0)) * cw_ref[0] + rows3(p) * cw_ref[1]
             + rows3(pltpu.roll(p, tm - 1, axis=0)) * cw_ref[2] + cb_ref[...])
        o_ref[...] = y.reshape(tm, w)


def _proj(x, shift, scale, gain, w, conv_w, conv_b, *, n_seg, seg_w, conv_from, tokens_per_mod, tm):
    t, d = x.shape
    tm = min(tm, tokens_per_mod)
    mod_spec = pl.BlockSpec((1, 1, d), lambda i, j: ((i * tm) // tokens_per_mod, 0, 0))
    conv_idx = lambda i, j: (0, jnp.maximum(j - conv_from, 0))
    pos = jnp.arange(GRID_W)[None, :, None]
    edge = jnp.stack([pos > 0, pos >= 0, pos < GRID_W - 1], axis=0)[:, 0]
    conv_w = jnp.where(edge, conv_w[:, None, :], 0.0)
    return pl.pallas_call(
        functools.partial(_proj_kernel, conv_from=conv_from),
        out_shape=jax.ShapeDtypeStruct((t, n_seg * seg_w), F32),
        grid=(t // tm, n_seg),
        in_specs=[pl.BlockSpec((tm, d), lambda i, j: (i, 0)),
                  mod_spec, mod_spec,
                  pl.BlockSpec((1, d), lambda i, j: (0, 0)),
                  pl.BlockSpec((d, seg_w), lambda i, j: (0, j)),
                  pl.BlockSpec((3, GRID_W, seg_w), lambda i, j: (0, 0, jnp.maximum(j - conv_from, 0))),
                  pl.BlockSpec((1, seg_w), conv_idx)],
        out_specs=pl.BlockSpec((tm, seg_w), lambda i, j: (i, j)),
        scratch_shapes=[pltpu.VMEM((tm, d), BF16)],
        compiler_params=_params("parallel", "arbitrary"),
        name="in_proj",
    )(x, shift, scale, gain.reshape(1, d), w, conv_w, conv_b.reshape(1, -1))


def _split3(g):
    g1 = g.astype(BF16)
    r1 = g - g1.astype(F32)
    g2 = r1.astype(BF16)
    return g1, g2, (r1 - g2.astype(F32)).astype(BF16)


def _split_dot(tri, g):
    g1, g2, g3 = _split3(g)
    return _dot(tri, g1) + _dot(tri, g2) + _dot(tri, g3)


def _gates(a, lb):
    f = lb + (1.0 - lb) * jax.nn.sigmoid(a)
    return 1.0 - f, jnp.log(f)


def _tn_dot(a, b):
    return lax.dot_general(a, b, (((0,), (0,)), ((), ())), preferred_element_type=F32)


def _nt_dot(a, b):
    return lax.dot_general(a, b, (((1,), (1,)), ((), ())), preferred_element_type=F32)


def _hgrn_kernel(af_ref, ab_ref, v_ref, q_ref, gate_ref, caf_ref, cab_ref, cv_ref, lbl_ref, gain_ref,
                 o_ref, of_scr, ob_scr, sf_scr, sb_scr, qin_scr, qt_scr, kt_scr, kd_scr, dec_scr,
                 *, n_chunks, n_ctx_chunks):
    c = SCAN_CHUNK
    mid = c // 2
    rows_p = PREP_CHUNKS * c
    prow = lax.broadcasted_iota(jnp.int32, (rows_p, rows_p), 0)
    pcol = lax.broadcasted_iota(jnp.int32, (rows_p, rows_p), 1)
    same = (prow // c) == (pcol // c)
    tri_fwd_p = jnp.where(same & (prow >= pcol), 1.0, 0.0).astype(BF16)
    tri_bwd_p = jnp.where(same & (pcol >= prow), 1.0, 0.0).astype(BF16)

    def lower_bound(direction):
        lg = lbl_ref[direction]
        ex = jnp.exp(lg - jnp.max(lg, axis=0, keepdims=True))
        return ex[0:1, :] / jnp.sum(ex, axis=0, keepdims=True)

    lb_f = lower_bound(0)
    lb_b = lower_bound(1)
    q_scale = HEAD_DIM ** -0.5

    n_blocks = n_chunks // PREP_CHUNKS
    n_steps = n_blocks // SCAN_BLOCKS

    def step_blocks(it):
        fwd = [(0, it * SCAN_BLOCKS + u) for u in range(SCAN_BLOCKS)]
        return fwd + [(1, n_blocks - 1 - bi) for _, bi in fwd]

    def prepare(it, s):
        it = jnp.minimum(it, n_steps - 1)
        blocks = step_blocks(it)
        los = [pl.multiple_of(bi * rows_p, rows_p) for _, bi in blocks]
        gates = [_gates((af_ref, ab_ref)[d][pl.ds(lo, rows_p), :], (lb_f, lb_b)[d]) for (d, _), lo in zip(blocks, los)]
        sums = {}
        for d, tri_p in ((0, tri_fwd_p), (1, tri_bwd_p)):
            slots_d = [slot for slot, (dd, _) in enumerate(blocks) if dd == d]
            wide = _dot(tri_p, jnp.concatenate([p for slot in slots_d for p in _split3(gates[slot][1])], axis=1))
            for n_s, slot in enumerate(slots_d):
                parts = [wide[:, (3 * n_s + m) * HEAD_DIM:(3 * n_s + m + 1) * HEAD_DIM] for m in range(3)]
                sums[slot] = parts[0] + parts[1] + parts[2]
        for slot, ((d, _), lo) in enumerate(zip(blocks, los)):
            end_row = (c - 1, 0)[d]
            k = gates[slot][0]
            b = sums[slot].reshape(PREP_CHUNKS, c, HEAD_DIM)
            b_mid = b[:, mid:mid + 1, :]
            b_end = b[:, end_row:end_row + 1, :]
            q_t = (q_ref[pl.ds(lo, rows_p), :] * q_scale).reshape(PREP_CHUNKS, c, HEAD_DIM) * jnp.exp(b - b_mid)
            k_t = k.reshape(PREP_CHUNKS, c, HEAD_DIM) * jnp.exp(b_mid - b)
            flat = lambda t: t.reshape(rows_p, HEAD_DIM).astype(BF16)
            qt_scr[s, slot] = flat(q_t)
            kt_scr[s, slot] = flat(k_t)
            qin_scr[s, slot] = flat(q_t * jnp.exp(b_mid))
            kd_scr[s, slot] = flat(k_t * jnp.exp(b_end - b_mid))
            dec_scr[s, slot] = jnp.broadcast_to(jnp.exp(b_end), (PREP_CHUNKS, 8, HEAD_DIM))

    zeros_blk = jnp.zeros((c, HEAD_DIM), BF16)

    def block_diag(t):
        return jnp.concatenate(
            [jnp.concatenate([t[j * c:(j + 1) * c] if m == j else zeros_blk for m in range(PREP_CHUNKS)], axis=1)
             for j in range(PREP_CHUNKS)], axis=0)

    masks = (same & (prow >= pcol), same & (pcol >= prow))
    st_refs = (sf_scr, sb_scr)
    out_refs = (of_scr, ob_scr)
    orders = (tuple(range(PREP_CHUNKS)), tuple(range(PREP_CHUNKS - 1, -1, -1)))

    gain = gain_ref[...]

    def scan(it, s, finish):
        streams = step_blocks(it)
        los = [pl.multiple_of(bi * rows_p, rows_p) for _, bi in streams]
        vbs = [v_ref[pl.ds(lo, rows_p), :].astype(BF16) for lo in los]
        scores = [_nt_dot(qt_scr[s, slot], kt_scr[s, slot]) for slot in range(len(streams))]
        upds = [_tn_dot(vb, block_diag(kd_scr[s, slot])) for slot, vb in enumerate(vbs)]
        outs = [_dot(jnp.where(masks[d], sc, 0.0).astype(BF16), vb) for (d, _), sc, vb in zip(streams, scores, vbs)]
        seen = [[None] * PREP_CHUNKS for _ in streams]
        for d in (0, 1):
            st = st_refs[d][...]
            for slot, (ds_, _) in enumerate(streams):
                if ds_ != d:
                    continue
                for j in orders[d]:
                    seen[slot][j] = st.astype(BF16)
                    st = st * dec_scr[s, slot, j][0:1, :] + upds[slot][:, j * HEAD_DIM:(j + 1) * HEAD_DIM]
            st_refs[d][...] = st
        for slot, ((d, _), lo) in enumerate(zip(streams, los)):
            q_in = qin_scr[s, slot]
            carried = jnp.concatenate([_nt_dot(q_in[j * c:(j + 1) * c], seen[slot][j]) for j in range(PREP_CHUNKS)],
                                      axis=0)
            rows = pl.ds(lo, rows_p)
            if finish:
                o = outs[slot] + carried + out_refs[1 - d][rows, :]
                o = o * lax.rsqrt(jnp.mean(o * o, axis=-1, keepdims=True) + EPS) * gain
                o_ref[rows, :] = (o * _silu(gate_ref[rows, :])).astype(o_ref.dtype)
            else:
                out_refs[d][rows, :] = outs[slot] + carried

    n_ctx_blocks = n_ctx_chunks // PREP_CHUNKS
    ctx_state = [jnp.zeros((HEAD_DIM, HEAD_DIM), F32), jnp.zeros((HEAD_DIM, HEAD_DIM), F32)]
    for step in range(n_ctx_blocks):
        upds, decs = [], []
        for d, bi in ((0, step), (1, n_ctx_blocks - 1 - step)):
            a_ref, lb, tri_p, end_row = ((caf_ref, lb_f, tri_fwd_p, c - 1), (cab_ref, lb_b, tri_bwd_p, 0))[d]
            rows_c = pl.ds(bi * rows_p, rows_p)
            k, g = _gates(a_ref[rows_c, :], lb)
            b = _split_dot(tri_p, g).reshape(PREP_CHUNKS, c, HEAD_DIM)
            b_end = b[:, end_row:end_row + 1, :]
            k_d = (k.reshape(PREP_CHUNKS, c, HEAD_DIM) * jnp.exp(b_end - b)).reshape(rows_p, HEAD_DIM).astype(BF16)
            upds.append(_tn_dot(cv_ref[rows_c, :].astype(BF16), block_diag(k_d)))
            decs.append(jnp.exp(b_end))
        for d in (0, 1):
            for j in orders[d]:
                ctx_state[d] = ctx_state[d] * decs[d][j] + upds[d][:, j * HEAD_DIM:(j + 1) * HEAD_DIM]
    sf_scr[...] = ctx_state[0]
    sb_scr[...] = ctx_state[1]

    prepare(0, 0)

    def body(finish):
        def pair(i, carry):
            scan(2 * i, 0, finish)
            prepare(2 * i + 1, 1)
            scan(2 * i + 1, 1, finish)
            prepare(2 * i + 2, 0)
            return carry
        return pair

    lax.fori_loop(0, n_steps // 4, body(False), 0)
    lax.fori_loop(n_steps // 4, n_steps // 2, body(True), 0)


def _hgrn(px, pc, lb_logits, gain, *, width):
    bsz, n, _ = px.shape
    n_ctx = pc.shape[1]
    heads = width // HEAD_DIM
    hd = HEAD_DIM

    def seg(s):
        return pl.BlockSpec((None, n, hd), lambda b, h: (b, 0, s * heads + h))

    def cseg(s):
        return pl.BlockSpec((None, n_ctx, hd), lambda b, h: (b, 0, s * heads + h))

    depth1 = lb_logits.shape[1]
    slots = 2 * SCAN_BLOCKS
    assert (n // SCAN_CHUNK) % (4 * SCAN_BLOCKS * PREP_CHUNKS) == 0, "scan steps come in pairs, per half"
    assert (n_ctx // SCAN_CHUNK) % PREP_CHUNKS == 0, "the context prefix is scanned in whole blocks"
    return pl.pallas_call(
        functools.partial(_hgrn_kernel, n_chunks=n // SCAN_CHUNK, n_ctx_chunks=n_ctx // SCAN_CHUNK),
        out_shape=jax.ShapeDtypeStruct((bsz, n, width), BF16),
        grid=(bsz, heads),
        in_specs=[seg(0), seg(1), seg(2), seg(3), seg(4), cseg(0), cseg(1), cseg(2),
                  pl.BlockSpec((2, depth1, hd), lambda b, h: (0, 0, h)),
                  pl.BlockSpec((1, hd), lambda b, h: (0, h))],
        out_specs=pl.BlockSpec((None, n, hd), lambda b, h: (b, 0, h)),
        scratch_shapes=[pltpu.VMEM((n, hd), F32), pltpu.VMEM((n, hd), F32),
                        pltpu.VMEM((hd, hd), F32), pltpu.VMEM((hd, hd), F32),
                        *[pltpu.VMEM((2, slots, PREP_CHUNKS * SCAN_CHUNK, hd), BF16) for _ in range(4)],
                        pltpu.VMEM((2, slots, PREP_CHUNKS, 8, hd), F32)],
        compiler_params=_params("parallel", "parallel"),
        name="hgrn_scan",
    )(px, px, px, px, px, pc, pc, pc, lb_logits, gain.reshape(1, width))


def _dft_constants(n):
    n2 = GRID_W
    n1h = n // n2
    m2r = np.arange(n2)[None, :]
    k2 = np.arange(n2)[:, None]
    f_ang = -2.0 * np.pi * k2 * m2r / n2
    fr, fi = np.cos(f_ang), np.sin(f_ang)
    f_fwd = np.block([[fr, -fi], [fi, fr]])
    f_inv = np.block([[fr, fi], [-fi, fr]])
    k1 = np.arange(n1h)[:, None, None]
    m1 = np.arange(n1h)[None, :, None]
    m2 = np.arange(n2)[None, None, :]
    theta = -2.0 * np.pi * (k1 + 0.5) * (m1 / (2 * n1h) + m2 / (2 * n))
    parts = np.stack([np.cos(theta), np.sin(theta)], axis=0)
    parts = parts.reshape(2, n1h, n1h, n2 // OUTER_ROWS, OUTER_ROWS)
    eye = np.eye(OUTER_ROWS)
    rows = n1h * OUTER_ROWS
    fwd = np.einsum("pknsj,jm->spkjnm", parts, eye).reshape(n2 // OUTER_ROWS, 2 * rows, rows)
    inv = np.einsum("pknsj,jm->snjpkm", parts, eye).reshape(n2 // OUTER_ROWS, rows, 2 * rows)
    return fwd, inv, f_fwd, f_inv


def _filt_mlp_kernel(z_ref, w1_ref, b1_ref, w2_ref, b2_ref, w3_ref, b3_ref, fr_ref, o_ref):
    fr = fr_ref[...]
    hp = lambda a, b: jnp.dot(a, b, precision=HIGHEST, preferred_element_type=F32)
    hid = jnp.sin(fr * (hp(z_ref[...], w1_ref[...]) + b1_ref[...]))
    hid = jnp.sin(fr * (hp(hid, w2_ref[...]) + b2_ref[...]))
    o_ref[...] = jnp.sin(fr * (hp(hid, w3_ref[...]) + b3_ref[...]))


def _filt_taps_kernel(hid_ref, wo_ref, dl_ref, h_ref, s_ref, *, n):
    hid = hid_ref[...]
    hid_hi = hid.astype(BF16)
    h = _dot3(hid_hi, (hid - hid_hi.astype(F32)).astype(BF16), wo_ref[...])
    t = lax.broadcasted_iota(jnp.int32, h.shape, 0).astype(F32) * (1.0 / max(n - 1, 1))
    hw = h * jnp.exp(-t * dl_ref[...])
    h_ref[...] = hw
    s_ref[0:1, :] = jnp.sum(jnp.abs(hw), axis=0, keepdims=True)
    s_ref[1:2, :] = jnp.abs(hw[0:1, :])


def _spectrum_kernel(af_ref, ab_ref, f_ref, fl_ref, s_ref, d_ref, o_ref, *, n):
    n2 = GRID_W
    l1 = s_ref[0:1, :] + s_ref[2:3, :] - s_ref[3:4, :]
    hb0 = s_ref[4:5, :]
    norm = 1.0 / l1
    scale = 2.0 / (2 * n)
    for i in range(af_ref.shape[1]):
        def inner(a_ref):
            z = _dot3(f_ref[...], fl_ref[...], jnp.concatenate([a_ref[0, i], a_ref[1, i]], axis=0))
            return z[:n2], z[n2:]

        zfr, zfi = inner(af_ref)
        zbr, zbi = inner(ab_ref)
        o_ref[0, i] = scale * ((zfr + zbr - hb0) * norm + d_ref[...])
        o_ref[1, i] = scale * ((zfi - zbi) * norm)


def _dot3(a_hi, a_lo, x):
    x_hi = x.astype(BF16)
    x_lo = (x - x_hi.astype(F32)).astype(BF16)
    return _dot(a_hi, x_hi) + _dot(a_hi, x_lo) + _dot(a_lo, x_hi)


def _hi_lo(a):
    hi = np.asarray(a, np.float64).astype(BF16)
    lo = (a - hi.astype(np.float64)).astype(BF16)
    return jnp.asarray(hi), jnp.asarray(lo)


def _slabs(x):
    return [x[..., h * OUTER_ROWS:(h + 1) * OUTER_ROWS, :] for h in range(x.shape[-2] // OUTER_ROWS)]


def _outer_dft_kernel(*refs, high):
    g_refs, x_ref, o_ref = refs[:-2], refs[-2], refs[-1]
    n1h, _, tl = x_ref.shape
    parts = []
    for h, xs in enumerate(_slabs(x_ref[...])):
        xs = xs.reshape(n1h * OUTER_ROWS, tl)
        r = _dot3(g_refs[0][h], g_refs[1][h], xs) if high else _dot(g_refs[0][h], xs.astype(BF16))
        parts.append(r.reshape(2, n1h, OUTER_ROWS, tl))
    o_ref[...] = jnp.concatenate(parts, axis=2).astype(o_ref.dtype)


def _outer_dft(gs, x4, seg, *, c, tl):
    high = len(gs) == 2
    s, n1h, n2, _ = x4.shape
    tl = min(tl, c)
    per = c // tl
    rows = OUTER_ROWS if high else BF16_ROWS
    g_spec = pl.BlockSpec((rows // OUTER_ROWS,) + gs[0].shape[1:], lambda j, b, l: (j, 0, 0))
    return pl.pallas_call(
        functools.partial(_outer_dft_kernel, high=high),
        out_shape=jax.ShapeDtypeStruct((s, 2, n1h, n2, c), F32 if high else BF16),
        grid=(n2 // rows, s, per),
        in_specs=[g_spec] * len(gs) + [pl.BlockSpec((None, n1h, rows, tl),
                                                    lambda j, b, l: (b, 0, j, seg * per + l))],
        out_specs=pl.BlockSpec((None, 2, n1h, rows, tl), lambda j, b, l: (b, 0, 0, j, l)),
        compiler_params=_params("parallel", "parallel", "parallel"),
        name="outer_dft",
    )(*gs, x4)


def _hyena_spectra(n, width, filt, hy_d, consts):
    w1, b1, w2, b2, w3, b3, freq, w_out = filt
    gt_fwd, _, f_fwd, _ = consts
    n1h = n // GRID_W
    order = w1.shape[1]
    pos = np.arange(n, dtype=np.float64)
    t = pos / max(n - 1, 1)
    bands = (FILT_EMB - 1) // 2
    fb = np.linspace(1e-4, bands - 1, bands)
    ang = (2 * math.pi / n) * pos[:, None] * fb[None, :]
    z = np.concatenate([t[:, None], np.cos(ang), -np.sin(ang)], axis=-1)
    emb_pad = 128
    z = jnp.asarray(np.pad(z, ((0, 0), (0, emb_pad - FILT_EMB))), F32)
    w1p = jnp.pad(w1, ((0, emb_pad - FILT_EMB), (0, 0)))
    row = lambda a: a.reshape(1, -1)
    full = lambda shape: pl.BlockSpec(shape, lambda: tuple(0 for _ in shape))
    hid = pl.pallas_call(
        _filt_mlp_kernel,
        out_shape=jax.ShapeDtypeStruct((n, order), F32),
        in_specs=[full((n, emb_pad)), full((emb_pad, order)), full((1, order)), full((order, order)),
                  full((1, order)), full((order, order)), full((1, order)), full((1, order))],
        out_specs=full((n, order)),
        compiler_params=pltpu.CompilerParams(vmem_limit_bytes=VMEM_LIMIT_BYTES),
        name="filter_mlp",
    )(z, w1p, row(b1), w2, row(b2), w3, row(b3), row(freq))

    cols = 4 * width
    deltas = np.abs(np.linspace(math.log(DECAY_TARGET) / DECAY_SLOW_PCT,
                                math.log(DECAY_TARGET) / DECAY_FAST_PCT, width))
    deltas4 = jnp.asarray(np.tile(deltas, 4)[None, :], F32)
    tc = min(512, cols)
    taps, sums = pl.pallas_call(
        functools.partial(_filt_taps_kernel, n=n),
        out_shape=(jax.ShapeDtypeStruct((n, cols), F32), jax.ShapeDtypeStruct((2, cols), F32)),
        grid=(cols // tc,),
        in_specs=[pl.BlockSpec((n, order), lambda j: (0, 0)),
                  pl.BlockSpec((order, tc), lambda j: (0, j)),
                  pl.BlockSpec((1, tc), lambda j: (0, j))],
        out_specs=(pl.BlockSpec((n, tc), lambda j: (0, j)), pl.BlockSpec((2, tc), lambda j: (0, j))),
        compiler_params=_params("parallel"),
        name="filter_taps",
    )(hid, w_out, deltas4)

    a = _outer_dft(_hi_lo(gt_fwd), taps.reshape(1, n1h, GRID_W, cols), 0, c=cols, tl=1024)[0]
    s4 = sums.reshape(2, 2, 2, width)
    bwd0 = taps[0].reshape(2, 2, width)[:, 1]
    stats = jnp.stack([s4[0, :, 0], s4[1, :, 0], s4[0, :, 1], s4[1, :, 1], bwd0], axis=1)

    kb = min(SPECTRUM_BINS, n1h)

    def tap_spec(side):
        return pl.BlockSpec((2, kb, GRID_W, width), lambda k1, f: (0, k1, 0, 2 * f + side))

    f_hi, f_lo = _hi_lo(f_fwd)
    f_spec = pl.BlockSpec((2 * GRID_W, 2 * GRID_W), lambda k1, f: (0, 0))
    return pl.pallas_call(
        functools.partial(_spectrum_kernel, n=n),
        out_shape=jax.ShapeDtypeStruct((2, 2, n1h, GRID_W, width), F32),
        grid=(n1h // kb, 2),
        in_specs=[tap_spec(0), tap_spec(1), f_spec, f_spec,
                  pl.BlockSpec((None, 5, width), lambda k1, f: (f, 0, 0)),
                  pl.BlockSpec((None, 1, width), lambda k1, f: (f, 0, 0))],
        out_specs=pl.BlockSpec((None, 2, kb, GRID_W, width), lambda k1, f: (f, 0, k1, 0, 0)),
        compiler_params=_params("parallel", "parallel"),
        name="filter_spectrum",
    )(a, a, f_hi, f_lo, stats, hy_d.reshape(2, 1, width))


def _inner_conv_kernel(a_ref, f_ref, fi_ref, h_ref, o_ref):
    n2 = GRID_W
    for i in range(a_ref.shape[1]):
        z = _dot(f_ref[...], jnp.concatenate([a_ref[0, i], a_ref[1, i]], axis=0))
        zr, zi = z[:n2], z[n2:]
        hr, hi = h_ref[0, i], h_ref[1, i]
        stacked = jnp.concatenate([zr * hr - zi * hi, zr * hi + zi * hr], axis=0).astype(BF16)
        w = _dot(fi_ref[...], stacked).astype(BF16)
        o_ref[0, i] = w[:n2]
        o_ref[1, i] = w[n2:]


def _inner_conv(a, f_fwd, f_inv, spec, filt):
    bsz, _, n1h, n2, c = a.shape
    kb = min(INNER_BINS, n1h)
    return pl.pallas_call(
        _inner_conv_kernel,
        out_shape=jax.ShapeDtypeStruct(a.shape, BF16),
        grid=(n1h // kb, bsz),
        in_specs=[pl.BlockSpec((None, 2, kb, n2, c), lambda k1, b: (b, 0, k1, 0, 0)),
                  pl.BlockSpec((2 * n2, 2 * n2), lambda k1, b: (0, 0)),
                  pl.BlockSpec((2 * n2, 2 * n2), lambda k1, b: (0, 0)),
                  pl.BlockSpec((None, 2, kb, n2, c), lambda k1, b: (filt, 0, k1, 0, 0))],
        out_specs=pl.BlockSpec((None, 2, kb, n2, c), lambda k1, b: (b, 0, k1, 0, 0)),
        compiler_params=_params("parallel", "arbitrary"),
        name="inner_conv",
    )(a, f_fwd, f_inv, spec)


def _outer_idft_gate_kernel(gi_ref, g_ref, b_ref, x_ref, o_ref, *, again):
    n1h, _, tl = x_ref.shape
    parts = []
    for h, (bs, xs) in enumerate(zip(_slabs(b_ref[...].astype(F32)), _slabs(x_ref[...]))):
        stacked = bs.reshape(2 * n1h * OUTER_ROWS, tl).astype(BF16)
        y = _dot(gi_ref[h], stacked) * xs.reshape(n1h * OUTER_ROWS, tl)
        if again:
            parts.append(_dot(g_ref[h], y.astype(BF16)).reshape(2, n1h, OUTER_ROWS, tl))
        else:
            parts.append(y.reshape(n1h, OUTER_ROWS, tl))
    o_ref[...] = jnp.concatenate(parts, axis=-2).astype(o_ref.dtype)


def _outer_idft_gate(g_inv, g_fwd, b, px4, seg, *, again, tl):
    bsz, _, n1h, n2, c = b.shape
    tl = min(tl, c)
    per = c // tl
    rows = BF16_ROWS
    if again:
        out_shape = jax.ShapeDtypeStruct((bsz, 2, n1h, n2, c), BF16)
        out_spec = pl.BlockSpec((None, 2, n1h, rows, tl), lambda j, bb, l: (bb, 0, 0, j, l))
    else:
        out_shape = jax.ShapeDtypeStruct((bsz, n1h, n2, c), BF16)
        out_spec = pl.BlockSpec((None, n1h, rows, tl), lambda j, bb, l: (bb, 0, j, l))
    return pl.pallas_call(
        functools.partial(_outer_idft_gate_kernel, again=again),
        out_shape=out_shape,
        grid=(n2 // rows, bsz, per),
        in_specs=[pl.BlockSpec((rows // OUTER_ROWS,) + g_inv.shape[1:], lambda j, bb, l: (j, 0, 0)),
                  pl.BlockSpec((rows // OUTER_ROWS,) + g_fwd.shape[1:], lambda j, bb, l: (j, 0, 0)),
                  pl.BlockSpec((None, 2, n1h, rows, tl), lambda j, bb, l: (bb, 0, 0, j, l)),
                  pl.BlockSpec((None, n1h, rows, tl), lambda j, bb, l: (bb, 0, j, seg * per + l))],
        out_specs=out_spec,
        compiler_params=_params("parallel", "parallel", "parallel"),
        name="outer_idft_gate",
    )(g_inv, g_fwd, b, px4)


def _hyena_mix(px, spec, consts, *, bsz, n, width):
    g_fwd, g_inv, f_fwd, f_inv = consts
    g_fwd_b, g_inv_b = jnp.asarray(g_fwd, BF16), jnp.asarray(g_inv, BF16)
    f_fwd_b, f_inv_b = jnp.asarray(f_fwd, BF16), jnp.asarray(f_inv, BF16)
    n1h = n // GRID_W
    px4 = px.reshape(bsz, n1h, GRID_W, -1)
    tl = width
    a = _outer_dft((g_fwd_b,), px4, 5, c=width, tl=tl)
    a = _inner_conv(a, f_fwd_b, f_inv_b, spec, 0)
    a = _outer_idft_gate(g_inv_b, g_fwd_b, a, px4, 6, again=True, tl=tl)
    a = _inner_conv(a, f_fwd_b, f_inv_b, spec, 1)
    hy = _outer_idft_gate(g_inv_b, g_fwd_b, a, px4, 7, again=False, tl=tl)
    return hy.reshape(bsz * n, width)


def _out_proj_kernel(x_ref, hg_ref, hy_ref, wa_ref, wb_ref, gt_ref, o_ref):
    mix = _dot(hg_ref[...], wa_ref[...]) + _dot(hy_ref[...], wb_ref[...])
    o_ref[...] = x_ref[...] + gt_ref[0] * mix


def _out_proj(x, hg, hy, w, gate, *, tokens_per_mod, tm):
    t, d = x.shape
    wdt = hg.shape[1]
    tm = min(tm, tokens_per_mod)
    return pl.pallas_call(
        _out_proj_kernel,
        out_shape=jax.ShapeDtypeStruct((t, d), F32),
        grid=(t // tm,),
        in_specs=[pl.BlockSpec((tm, d), lambda i: (i, 0)),
                  pl.BlockSpec((tm, wdt), lambda i: (i, 0)),
                  pl.BlockSpec((tm, wdt), lambda i: (i, 0)),
                  pl.BlockSpec((wdt, d), lambda i: (0, 0)),
                  pl.BlockSpec((wdt, d), lambda i: (1, 0)),
                  pl.BlockSpec((1, 1, d), lambda i: ((i * tm) // tokens_per_mod, 0, 0))],
        out_specs=pl.BlockSpec((tm, d), lambda i: (i, 0)),
        compiler_params=_params("parallel"),
        name="out_proj",
    )(x, hg, hy, w, w, gate)


def _tiles(d_ff):
    return TOKEN_TILE, 2 * TOKEN_TILE, HIDDEN_TILE if d_ff % HIDDEN_TILE == 0 else d_ff


def kernel(x, c, ctx, c_ctx, ada_w, ada_b, norm_ffn1, ffn1_w1, ffn1_w3, ffn1_w2, norm_mix, w_in, hg_lb_logits, hg_norm, hy_conv_w, hy_conv_b, filt_w1, filt_b1, filt_w2, filt_b2, filt_w3, filt_b3, filt_freq, filt_w_out, hy_d, w_out, norm_ffn2, ffn2_w1, ffn2_w3, ffn2_w2, final_norm):
    bsz, n, d = x.shape
    n_ctx = ctx.shape[1]
    depth = ada_w.shape[0]
    assert depth == 1, "single-layer block"
    width = d // 2
    n_seg = w_in.shape[2] // width
    assert n_seg == 8 and n % GRID_W == 0 and n_ctx % SCAN_CHUNK == 0 and width % HEAD_DIM == 0
    tm, tm_wide, tf = _tiles(ffn1_w1.shape[2])
    l = 0

    rows = -(-(bsz + 1) // 8) * 8
    cs = jnp.concatenate([c, c_ctx[None, :], jnp.zeros((rows - bsz - 1, d), F32)], axis=0)
    mods = _ada(cs, ada_w[l], ada_b[l]).reshape(rows, N_MOD, d)
    mx = [mods[:bsz, i][:, None, :] for i in range(N_MOD)]
    mc = [mods[bsz:bsz + 1, i][:, None, :] for i in range(N_MOD)]

    bf = _to_bf16
    w1a, w3a, w2a = bf(ffn1_w1[l]), bf(ffn1_w3[l]), bf(ffn1_w2[l])
    xt = x.reshape(bsz * n, d)
    yt = ctx.reshape(bsz * n_ctx, d)

    xt = _ffn(xt, mx[0], mx[1], mx[2], norm_ffn1[l], w1a, w3a, w2a, final_norm,
              tokens_per_mod=n, final_norm=False, tm=tm, tf=tf)
    yt = _ffn(yt, mc[0], mc[1], mc[2], norm_ffn1[l], w1a, w3a, w2a, final_norm,
              tokens_per_mod=bsz * n_ctx, final_norm=False, tm=tm, tf=tf)

    w_in_b = bf(w_in[l])
    pc = _proj(yt, mc[3], mc[4], norm_mix[l], w_in_b, hy_conv_w[l], hy_conv_b[l], n_seg=3, seg_w=width,
               conv_from=n_seg, tokens_per_mod=bsz * n_ctx, tm=tm)
    px = _proj(xt, mx[3], mx[4], norm_mix[l], w_in_b, hy_conv_w[l], hy_conv_b[l], n_seg=n_seg, seg_w=width,
               conv_from=5, tokens_per_mod=n, tm=tm_wide)
    cols = n_seg * width

    hg = _hgrn(px.reshape(bsz, n, cols), pc.reshape(bsz, n_ctx, 3 * width), hg_lb_logits, hg_norm[l], width=width)

    consts = _dft_constants(n)
    filt = (filt_w1[l], filt_b1[l], filt_w2[l], filt_b2[l], filt_w3[l], filt_b3[l], filt_freq[l], filt_w_out[l])
    spec = _hyena_spectra(n, width, filt, hy_d[l], consts)
    hy = _hyena_mix(px, spec, consts, bsz=bsz, n=n, width=width)

    xt = _out_proj(xt, hg.reshape(bsz * n, width), hy, bf(w_out[l]), mx[5],
                   tokens_per_mod=n, tm=tm)

    out = _ffn(xt, mx[6], mx[7], mx[8], norm_ffn2[l], bf(ffn2_w1[l]), bf(ffn2_w3[l]), bf(ffn2_w2[l]), final_norm,
               tokens_per_mod=n, final_norm=True, tm=tm, tf=tf)
    return out.reshape(bsz, n, d)
```

```python
import functools
import math

import numpy as np
import jax
import jax.numpy as jnp
from jax import lax
from jax.experimental import pallas as pl
from jax.experimental.pallas import tpu as pltpu

F32 = jnp.float32
BF16 = jnp.bfloat16
EPS = 1e-6
N_MOD = 9
HEAD_DIM = 128
GRID_W = 64
CAST_ROWS = 256
TOKEN_TILE = 512
HIDDEN_TILE = 512
SCAN_CHUNK = 64
PREP_CHUNKS = 4
SCAN_BLOCKS = 4
OUTER_ROWS = 8
BF16_ROWS = 16
INNER_BINS = 16
SPECTRUM_BINS = 8
FILT_EMB = 33
DECAY_TARGET = 1e-2
DECAY_FAST_PCT = 0.3
DECAY_SLOW_PCT = 1.5
VMEM_LIMIT_BYTES = 56 * 1024 * 1024
HIGHEST = lax.Precision.HIGHEST


def _params(*semantics):
    return pltpu.CompilerParams(dimension_semantics=semantics, vmem_limit_bytes=VMEM_LIMIT_BYTES)


def _dot(a, b):
    return jnp.dot(a, b, preferred_element_type=F32)


def _silu(a):
    return a * jax.nn.sigmoid(a)


def _cast_kernel(x_ref, o_ref):
    o_ref[...] = x_ref[...].astype(BF16)


def _to_bf16(w):
    r, c = w.shape
    rb = CAST_ROWS if r % CAST_ROWS == 0 else r
    return pl.pallas_call(
        _cast_kernel,
        out_shape=jax.ShapeDtypeStruct((r, c), BF16),
        grid=(r // rb,),
        in_specs=[pl.BlockSpec((rb, c), lambda i: (i, 0))],
        out_specs=pl.BlockSpec((rb, c), lambda i: (i, 0)),
        compiler_params=_params("parallel"),
        name="to_bf16",
    )(w)


def _ada_kernel(c_ref, w_ref, b_ref, o_ref):
    h = _silu(c_ref[...]).astype(BF16)
    o_ref[...] = _dot(h, w_ref[...].astype(BF16)) + b_ref[...]


def _ada(cs, w, b):
    rows, d = cs.shape
    n = w.shape[1]
    tn = d // 2
    return pl.pallas_call(
        _ada_kernel,
        out_shape=jax.ShapeDtypeStruct((rows, n), F32),
        grid=(n // tn,),
        in_specs=[pl.BlockSpec((rows, d), lambda j: (0, 0)),
                  pl.BlockSpec((d, tn), lambda j: (0, j)),
                  pl.BlockSpec((1, tn), lambda j: (0, j))],
        out_specs=pl.BlockSpec((rows, tn), lambda j: (0, j)),
        compiler_params=_params("arbitrary"),
        name="ada_mod",
    )(cs, w, b.reshape(1, n))


def _norm_mod(x, gain, shift, scale):
    y = x * lax.rsqrt(jnp.mean(x * x, axis=-1, keepdims=True) + EPS)
    return (y * (gain * (1.0 + scale)) + shift).astype(BF16)


def _ffn_kernel(x_ref, sh_ref, sc_ref, gt_ref, g_ref, w1_ref, w3_ref, w2_ref, fin_ref, o_ref, h_scr,
                *, final_norm, n_hidden):
    j = pl.program_id(1)

    def hidden_tile():
        h = h_scr[...]
        a = _dot(h, w1_ref[...])
        b = _dot(h, w3_ref[...])
        return _dot((_silu(a) * b).astype(BF16), w2_ref[...])

    def finish(acc):
        out = x_ref[...] + (0.5 * gt_ref[0]) * acc
        if final_norm:
            out = out * lax.rsqrt(jnp.mean(out * out, axis=-1, keepdims=True) + EPS) * fin_ref[...]
        return out

    last = n_hidden - 1

    @pl.when(j == 0)
    def _():
        h_scr[...] = _norm_mod(x_ref[...], g_ref[...], sh_ref[0], sc_ref[0])
        o_ref[...] = finish(hidden_tile()) if last == 0 else hidden_tile()

    @pl.when((j > 0) & (j < last))
    def _():
        o_ref[...] += hidden_tile()

    if last > 0:
        @pl.when(j == last)
        def _():
            o_ref[...] = finish(o_ref[...] + hidden_tile())


def _ffn(x, shift, scale, gate, gain, w1, w3, w2, fin, *, tokens_per_mod, final_norm, tm, tf):
    t, d = x.shape
    f = w1.shape[1]
    tm = min(tm, tokens_per_mod)
    mod_spec = pl.BlockSpec((1, 1, d), lambda i, j: ((i * tm) // tokens_per_mod, 0, 0))
    vec_spec = pl.BlockSpec((1, d), lambda i, j: (0, 0))
    return pl.pallas_call(
        functools.partial(_ffn_kernel, final_norm=final_norm, n_hidden=f // tf),
        out_shape=jax.ShapeDtypeStruct((t, d), F32),
        grid=(t // tm, f // tf),
        in_specs=[pl.BlockSpec((tm, d), lambda i, j: (i, 0)),
                  mod_spec, mod_spec, mod_spec, vec_spec,
                  pl.BlockSpec((d, tf), lambda i, j: (0, j)),
                  pl.BlockSpec((d, tf), lambda i, j: (0, j)),
                  pl.BlockSpec((tf, d), lambda i, j: (j, 0)),
                  vec_spec],
        out_specs=pl.BlockSpec((tm, d), lambda i, j: (i, 0)),
        scratch_shapes=[pltpu.VMEM((tm, d), BF16)],
        compiler_params=_params("parallel", "arbitrary"),
        name="swiglu_ffn",
    )(x, shift, scale, gate, gain.reshape(1, d), w1, w3, w2, fin.reshape(1, d))


def _proj_kernel(x_ref, sh_ref, sc_ref, g_ref, w_ref, cw_ref, cb_ref, o_ref, h_scr, *, conv_from):
    j = pl.program_id(1)

    def segment():
        return _dot(h_scr[...], w_ref[...])

    @pl.when(j == 0)
    def _():
        h_scr[...] = _norm_mod(x_ref[...], g_ref[...], sh_ref[0], sc_ref[0])
        o_ref[...] = segment()

    @pl.when((j > 0) & (j < conv_from))
    def _():
        o_ref[...] = segment()

    @pl.when(j >= conv_from)
    def _():
        p = segment()
        tm, w = p.shape
        rows3 = lambda t: t.reshape(tm // GRID_W, GRID_W, w)
        y = (rows3(pltpu.roll(p, 1, axis=0)) * cw_ref[0] + rows3(p) * cw_ref[1]
             + rows3(pltpu.roll(p, tm - 1, axis=0)) * cw_ref[2] + cb_ref[...])
        o_ref[...] = y.reshape(tm, w)


def _proj(x, shift, scale, gain, w, conv_w, conv_b, *, n_seg, seg_w, conv_from, tokens_per_mod, tm):
    t, d = x.shape
    tm = min(tm, tokens_per_mod)
    mod_spec = pl.BlockSpec((1, 1, d), lambda i, j: ((i * tm) // tokens_per_mod, 0, 0))
    conv_idx = lambda i, j: (0, jnp.maximum(j - conv_from, 0))
    pos = jnp.arange(GRID_W)[None, :, None]
    edge = jnp.stack([pos > 0, pos >= 0, pos < GRID_W - 1], axis=0)[:, 0]
    conv_w = jnp.where(edge, conv_w[:, None, :], 0.0)
    return pl.pallas_call(
        functools.partial(_proj_kernel, conv_from=conv_from),
        out_shape=jax.ShapeDtypeStruct((t, n_seg * seg_w), F32),
        grid=(t // tm, n_seg),
        in_specs=[pl.BlockSpec((tm, d), lambda i, j: (i, 0)),
                  mod_spec, mod_spec,
                  pl.BlockSpec((1, d), lambda i, j: (0, 0)),
                  pl.BlockSpec((d, seg_w), lambda i, j: (0, j)),
                  pl.BlockSpec((3, GRID_W, seg_w), lambda i, j: (0, 0, jnp.maximum(j - conv_from, 0))),
                  pl.BlockSpec((1, seg_w), conv_idx)],
        out_specs=pl.BlockSpec((tm, seg_w), lambda i, j: (i, j)),
        scratch_shapes=[pltpu.VMEM((tm, d), BF16)],
        compiler_params=_params("parallel", "arbitrary"),
        name="in_proj",
    )(x, shift, scale, gain.reshape(1, d), w, conv_w, conv_b.reshape(1, -1))


def _split3(g):
    g1 = g.astype(BF16)
    r1 = g - g1.astype(F32)
    g2 = r1.astype(BF16)
    return g1, g2, (r1 - g2.astype(F32)).astype(BF16)


def _split_dot(tri, g):
    g1, g2, g3 = _split3(g)
    return _dot(tri, g1) + _dot(tri, g2) + _dot(tri, g3)


def _gates(a, lb):
    f = lb + (1.0 - lb) * jax.nn.sigmoid(a)
    return 1.0 - f, jnp.log(f)


def _tn_dot(a, b):
    return lax.dot_general(a, b, (((0,), (0,)), ((), ())), preferred_element_type=F32)


def _nt_dot(a, b):
    return lax.dot_general(a, b, (((1,), (1,)), ((), ())), preferred_element_type=F32)


def _hgrn_kernel(af_ref, ab_ref, v_ref, q_ref, gate_ref, caf_ref, cab_ref, cv_ref, lbl_ref, gain_ref,
                 o_ref, of_scr, ob_scr, sf_scr, sb_scr, qin_scr, qt_scr, kt_scr, kd_scr, dec_scr,
                 *, n_chunks, n_ctx_chunks):
    c = SCAN_CHUNK
    mid = c // 2
    rows_p = PREP_CHUNKS * c
    prow = lax.broadcasted_iota(jnp.int32, (rows_p, rows_p), 0)
    pcol = lax.broadcasted_iota(jnp.int32, (rows_p, rows_p), 1)
    same = (prow // c) == (pcol // c)
    tri_fwd_p = jnp.where(same & (prow >= pcol), 1.0, 0.0).astype(BF16)
    tri_bwd_p = jnp.where(same & (pcol >= prow), 1.0, 0.0).astype(BF16)

    def lower_bound(direction):
        lg = lbl_ref[direction]
        ex = jnp.exp(lg - jnp.max(lg, axis=0, keepdims=True))
        return ex[0:1, :] / jnp.sum(ex, axis=0, keepdims=True)

    lb_f = lower_bound(0)
    lb_b = lower_bound(1)
    q_scale = HEAD_DIM ** -0.5

    n_blocks = n_chunks // PREP_CHUNKS
    n_steps = n_blocks // SCAN_BLOCKS

    def step_blocks(it):
        fwd = [(0, it * SCAN_BLOCKS + u) for u in range(SCAN_BLOCKS)]
        return fwd + [(1, n_blocks - 1 - bi) for _, bi in fwd]

    def prepare(it, s):
        it = jnp.minimum(it, n_steps - 1)
        blocks = step_blocks(it)
        los = [pl.multiple_of(bi * rows_p, rows_p) for _, bi in blocks]
        gates = [_gates((af_ref, ab_ref)[d][pl.ds(lo, rows_p), :], (lb_f, lb_b)[d]) for (d, _), lo in zip(blocks, los)]
        sums = {}
        for d, tri_p in ((0, tri_fwd_p), (1, tri_bwd_p)):
            slots_d = [slot for slot, (dd, _) in enumerate(blocks) if dd == d]
            wide = _dot(tri_p, jnp.concatenate([p for slot in slots_d for p in _split3(gates[slot][1])], axis=1))
            for n_s, slot in enumerate(slots_d):
                parts = [wide[:, (3 * n_s + m) * HEAD_DIM:(3 * n_s + m + 1) * HEAD_DIM] for m in range(3)]
                sums[slot] = parts[0] + parts[1] + parts[2]
        for slot, ((d, _), lo) in enumerate(zip(blocks, los)):
            end_row = (c - 1, 0)[d]
            k = gates[slot][0]
            b = sums[slot].reshape(PREP_CHUNKS, c, HEAD_DIM)
            b_mid = b[:, mid:mid + 1, :]
            b_end = b[:, end_row:end_row + 1, :]
            q_t = (q_ref[pl.ds(lo, rows_p), :] * q_scale).reshape(PREP_CHUNKS, c, HEAD_DIM) * jnp.exp(b - b_mid)
            k_t = k.reshape(PREP_CHUNKS, c, HEAD_DIM) * jnp.exp(b_mid - b)
            flat = lambda t: t.reshape(rows_p, HEAD_DIM).astype(BF16)
            qt_scr[s, slot] = flat(q_t)
            kt_scr[s, slot] = flat(k_t)
            qin_scr[s, slot] = flat(q_t * jnp.exp(b_mid))
            kd_scr[s, slot] = flat(k_t * jnp.exp(b_end - b_mid))
            dec_scr[s, slot] = jnp.broadcast_to(jnp.exp(b_end), (PREP_CHUNKS, 8, HEAD_DIM))

    zeros_blk = jnp.zeros((c, HEAD_DIM), BF16)

    def block_diag(t):
        return jnp.concatenate(
            [jnp.concatenate([t[j * c:(j + 1) * c] if m == j else zeros_blk for m in range(PREP_CHUNKS)], axis=1)
             for j in range(PREP_CHUNKS)], axis=0)

    masks = (same & (prow >= pcol), same & (pcol >= prow))
    st_refs = (sf_scr, sb_scr)
    out_refs = (of_scr, ob_scr)
    orders = (tuple(range(PREP_CHUNKS)), tuple(range(PREP_CHUNKS - 1, -1, -1)))

    gain = gain_ref[...]

    def scan(it, s, finish):
        streams = step_blocks(it)
        los = [pl.multiple_of(bi * rows_p, rows_p) for _, bi in streams]
        vbs = [v_ref[pl.ds(lo, rows_p), :].astype(BF16) for lo in los]
        scores = [_nt_dot(qt_scr[s, slot], kt_scr[s, slot]) for slot in range(len(streams))]
        upds = [_tn_dot(vb, block_diag(kd_scr[s, slot])) for slot, vb in enumerate(vbs)]
        outs = [_dot(jnp.where(masks[d], sc, 0.0).astype(BF16), vb) for (d, _), sc, vb in zip(streams, scores, vbs)]
        seen = [[None] * PREP_CHUNKS for _ in streams]
        for d in (0, 1):
            st = st_refs[d][...]
            for slot, (ds_, _) in enumerate(streams):
                if ds_ != d:
                    continue
                for j in orders[d]:
                    seen[slot][j] = st.astype(BF16)
                    st = st * dec_scr[s, slot, j][0:1, :] + upds[slot][:, j * HEAD_DIM:(j + 1) * HEAD_DIM]
            st_refs[d][...] = st
        for slot, ((d, _), lo) in enumerate(zip(streams, los)):
            q_in = qin_scr[s, slot]
            carried = jnp.concatenate([_nt_dot(q_in[j * c:(j + 1) * c], seen[slot][j]) for j in range(PREP_CHUNKS)],
                                      axis=0)
            rows = pl.ds(lo, rows_p)
            if finish:
                o = outs[slot] + carried + out_refs[1 - d][rows, :]
                o = o * lax.rsqrt(jnp.mean(o * o, axis=-1, keepdims=True) + EPS) * gain
                o_ref[rows, :] = (o * _silu(gate_ref[rows, :])).astype(o_ref.dtype)
            else:
                out_refs[d][rows, :] = outs[slot] + carried

    n_ctx_blocks = n_ctx_chunks // PREP_CHUNKS
    ctx_state = [jnp.zeros((HEAD_DIM, HEAD_DIM), F32), jnp.zeros((HEAD_DIM, HEAD_DIM), F32)]
    for step in range(n_ctx_blocks):
        upds, decs = [], []
        for d, bi in ((0, step), (1, n_ctx_blocks - 1 - step)):
            a_ref, lb, tri_p, end_row = ((caf_ref, lb_f, tri_fwd_p, c - 1), (cab_ref, lb_b, tri_bwd_p, 0))[d]
            rows_c = pl.ds(bi * rows_p, rows_p)
            k, g = _gates(a_ref[rows_c, :], lb)
            b = _split_dot(tri_p, g).reshape(PREP_CHUNKS, c, HEAD_DIM)
            b_end = b[:, end_row:end_row + 1, :]
            k_d = (k.reshape(PREP_CHUNKS, c, HEAD_DIM) * jnp.exp(b_end - b)).reshape(rows_p, HEAD_DIM).astype(BF16)
            upds.append(_tn_dot(cv_ref[rows_c, :].astype(BF16), block_diag(k_d)))
            decs.append(jnp.exp(b_end))
        for d in (0, 1):
            for j in orders[d]:
                ctx_state[d] = ctx_state[d] * decs[d][j] + upds[d][:, j * HEAD_DIM:(j + 1) * HEAD_DIM]
    sf_scr[...] = ctx_state[0]
    sb_scr[...] = ctx_state[1]

    prepare(0, 0)

    def body(finish):
        def pair(i, carry):
            scan(2 * i, 0, finish)
            prepare(2 * i + 1, 1)
            scan(2 * i + 1, 1, finish)
            prepare(2 * i + 2, 0)
            return carry
        return pair

    lax.fori_loop(0, n_steps // 4, body(False), 0)
    lax.fori_loop(n_steps // 4, n_steps // 2, body(True), 0)


def _hgrn(px, pc, lb_logits, gain, *, width):
    bsz, n, _ = px.shape
    n_ctx = pc.shape[1]
    heads = width // HEAD_DIM
    hd = HEAD_DIM

    def seg(s):
        return pl.BlockSpec((None, n, hd), lambda b, h: (b, 0, s * heads + h))

    def cseg(s):
        return pl.BlockSpec((None, n_ctx, hd), lambda b, h: (b, 0, s * heads + h))

    depth1 = lb_logits.shape[1]
    slots = 2 * SCAN_BLOCKS
    assert (n // SCAN_CHUNK) % (4 * SCAN_BLOCKS * PREP_CHUNKS) == 0, "scan steps come in pairs, per half"
    assert (n_ctx // SCAN_CHUNK) % PREP_CHUNKS == 0, "the context prefix is scanned in whole blocks"
    return pl.pallas_call(
        functools.partial(_hgrn_kernel, n_chunks=n // SCAN_CHUNK, n_ctx_chunks=n_ctx // SCAN_CHUNK),
        out_shape=jax.ShapeDtypeStruct((bsz, n, width), BF16),
        grid=(bsz, heads),
        in_specs=[seg(0), seg(1), seg(2), seg(3), seg(4), cseg(0), cseg(1), cseg(2),
                  pl.BlockSpec((2, depth1, hd), lambda b, h: (0, 0, h)),
                  pl.BlockSpec((1, hd), lambda b, h: (0, h))],
        out_specs=pl.BlockSpec((None, n, hd), lambda b, h: (b, 0, h)),
        scratch_shapes=[pltpu.VMEM((n, hd), F32), pltpu.VMEM((n, hd), F32),
                        pltpu.VMEM((hd, hd), F32), pltpu.VMEM((hd, hd), F32),
                        *[pltpu.VMEM((2, slots, PREP_CHUNKS * SCAN_CHUNK, hd), BF16) for _ in range(4)],
                        pltpu.VMEM((2, slots, PREP_CHUNKS, 8, hd), F32)],
        compiler_params=_params("parallel", "parallel"),
        name="hgrn_scan",
    )(px, px, px, px, px, pc, pc, pc, lb_logits, gain.reshape(1, width))


def _dft_constants(n):
    n2 = GRID_W
    n1h = n // n2
    m2r = np.arange(n2)[None, :]
    k2 = np.arange(n2)[:, None]
    f_ang = -2.0 * np.pi * k2 * m2r / n2
    fr, fi = np.cos(f_ang), np.sin(f_ang)
    f_fwd = np.block([[fr, -fi], [fi, fr]])
    f_inv = np.block([[fr, fi], [-fi, fr]])
    k1 = np.arange(n1h)[:, None, None]
    m1 = np.arange(n1h)[None, :, None]
    m2 = np.arange(n2)[None, None, :]
    theta = -2.0 * np.pi * (k1 + 0.5) * (m1 / (2 * n1h) + m2 / (2 * n))
    parts = np.stack([np.cos(theta), np.sin(theta)], axis=0)
    parts = parts.reshape(2, n1h, n1h, n2 // OUTER_ROWS, OUTER_ROWS)
    eye = np.eye(OUTER_ROWS)
    rows = n1h * OUTER_ROWS
    fwd = np.einsum("pknsj,jm->spkjnm", parts, eye).reshape(n2 // OUTER_ROWS, 2 * rows, rows)
    inv = np.einsum("pknsj,jm->snjpkm", parts, eye).reshape(n2 // OUTER_ROWS, rows, 2 * rows)
    return fwd, inv, f_fwd, f_inv


def _filt_mlp_kernel(z_ref, w1_ref, b1_ref, w2_ref, b2_ref, w3_ref, b3_ref, fr_ref, o_ref):
    fr = fr_ref[...]
    hp = lambda a, b: jnp.dot(a, b, precision=HIGHEST, preferred_element_type=F32)
    hid = jnp.sin(fr * (hp(z_ref[...], w1_ref[...]) + b1_ref[...]))
    hid = jnp.sin(fr * (hp(hid, w2_ref[...]) + b2_ref[...]))
    o_ref[...] = jnp.sin(fr * (hp(hid, w3_ref[...]) + b3_ref[...]))


def _filt_taps_kernel(hid_ref, wo_ref, dl_ref, h_ref, s_ref, *, n):
    hid = hid_ref[...]
    hid_hi = hid.astype(BF16)
    h = _dot3(hid_hi, (hid - hid_hi.astype(F32)).astype(BF16), wo_ref[...])
    t = lax.broadcasted_iota(jnp.int32, h.shape, 0).astype(F32) * (1.0 / max(n - 1, 1))
    hw = h * jnp.exp(-t * dl_ref[...])
    h_ref[...] = hw
    s_ref[0:1, :] = jnp.sum(jnp.abs(hw), axis=0, keepdims=True)
    s_ref[1:2, :] = jnp.abs(hw[0:1, :])


def _spectrum_kernel(af_ref, ab_ref, f_ref, fl_ref, s_ref, d_ref, o_ref, *, n):
    n2 = GRID_W
    l1 = s_ref[0:1, :] + s_ref[2:3, :] - s_ref[3:4, :]
    hb0 = s_ref[4:5, :]
    norm = 1.0 / l1
    scale = 2.0 / (2 * n)
    for i in range(af_ref.shape[1]):
        def inner(a_ref):
            z = _dot3(f_ref[...], fl_ref[...], jnp.concatenate([a_ref[0, i], a_ref[1, i]], axis=0))
            return z[:n2], z[n2:]

        zfr, zfi = inner(af_ref)
        zbr, zbi = inner(ab_ref)
        o_ref[0, i] = scale * ((zfr + zbr - hb0) * norm + d_ref[...])
        o_ref[1, i] = scale * ((zfi - zbi) * norm)


def _dot3(a_hi, a_lo, x):
    x_hi = x.astype(BF16)
    x_lo = (x - x_hi.astype(F32)).astype(BF16)
    return _dot(a_hi, x_hi) + _dot(a_hi, x_lo) + _dot(a_lo, x_hi)


def _hi_lo(a):
    hi = np.asarray(a, np.float64).astype(BF16)
    lo = (a - hi.astype(np.float64)).astype(BF16)
    return jnp.asarray(hi), jnp.asarray(lo)


def _slabs(x):
    return [x[..., h * OUTER_ROWS:(h + 1) * OUTER_ROWS, :] for h in range(x.shape[-2] // OUTER_ROWS)]


def _outer_dft_kernel(*refs, high):
    g_refs, x_ref, o_ref = refs[:-2], refs[-2], refs[-1]
    n1h, _, tl = x_ref.shape
    parts = []
    for h, xs in enumerate(_slabs(x_ref[...])):
        xs = xs.reshape(n1h * OUTER_ROWS, tl)
        r = _dot3(g_refs[0][h], g_refs[1][h], xs) if high else _dot(g_refs[0][h], xs.astype(BF16))
        parts.append(r.reshape(2, n1h, OUTER_ROWS, tl))
    o_ref[...] = jnp.concatenate(parts, axis=2).astype(o_ref.dtype)


def _outer_dft(gs, x4, seg, *, c, tl):
    high = len(gs) == 2
    s, n1h, n2, _ = x4.shape
    tl = min(tl, c)
    per = c // tl
    rows = OUTER_ROWS if high else BF16_ROWS
    g_spec = pl.BlockSpec((rows // OUTER_ROWS,) + gs[0].shape[1:], lambda j, b, l: (j, 0, 0))
    return pl.pallas_call(
        functools.partial(_outer_dft_kernel, high=high),
        out_shape=jax.ShapeDtypeStruct((s, 2, n1h, n2, c), F32 if high else BF16),
        grid=(n2 // rows, s, per),
        in_specs=[g_spec] * len(gs) + [pl.BlockSpec((None, n1h, rows, tl),
                                                    lambda j, b, l: (b, 0, j, seg * per + l))],
        out_specs=pl.BlockSpec((None, 2, n1h, rows, tl), lambda j, b, l: (b, 0, 0, j, l)),
        compiler_params=_params("parallel", "parallel", "parallel"),
        name="outer_dft",
    )(*gs, x4)


def _hyena_spectra(n, width, filt, hy_d, consts):
    w1, b1, w2, b2, w3, b3, freq, w_out = filt
    gt_fwd, _, f_fwd, _ = consts
    n1h = n // GRID_W
    order = w1.shape[1]
    pos = np.arange(n, dtype=np.float64)
    t = pos / max(n - 1, 1)
    bands = (FILT_EMB - 1) // 2
    fb = np.linspace(1e-4, bands - 1, bands)
    ang = (2 * math.pi / n) * pos[:, None] * fb[None, :]
    z = np.concatenate([t[:, None], np.cos(ang), -np.sin(ang)], axis=-1)
    emb_pad = 128
    z = jnp.asarray(np.pad(z, ((0, 0), (0, emb_pad - FILT_EMB))), F32)
    w1p = jnp.pad(w1, ((0, emb_pad - FILT_EMB), (0, 0)))
    row = lambda a: a.reshape(1, -1)
    full = lambda shape: pl.BlockSpec(shape, lambda: tuple(0 for _ in shape))
    hid = pl.pallas_call(
        _filt_mlp_kernel,
        out_shape=jax.ShapeDtypeStruct((n, order), F32),
        in_specs=[full((n, emb_pad)), full((emb_pad, order)), full((1, order)), full((order, order)),
                  full((1, order)), full((order, order)), full((1, order)), full((1, order))],
        out_specs=full((n, order)),
        compiler_params=pltpu.CompilerParams(vmem_limit_bytes=VMEM_LIMIT_BYTES),
        name="filter_mlp",
    )(z, w1p, row(b1), w2, row(b2), w3, row(b3), row(freq))

    cols = 4 * width
    deltas = np.abs(np.linspace(math.log(DECAY_TARGET) / DECAY_SLOW_PCT,
                                math.log(DECAY_TARGET) / DECAY_FAST_PCT, width))
    deltas4 = jnp.asarray(np.tile(deltas, 4)[None, :], F32)
    tc = min(512, cols)
    taps, sums = pl.pallas_call(
        functools.partial(_filt_taps_kernel, n=n),
        out_shape=(jax.ShapeDtypeStruct((n, cols), F32), jax.ShapeDtypeStruct((2, cols), F32)),
        grid=(cols // tc,),
        in_specs=[pl.BlockSpec((n, order), lambda j: (0, 0)),
                  pl.BlockSpec((order, tc), lambda j: (0, j)),
                  pl.BlockSpec((1, tc), lambda j: (0, j))],
        out_specs=(pl.BlockSpec((n, tc), lambda j: (0, j)), pl.BlockSpec((2, tc), lambda j: (0, j))),
        compiler_params=_params("parallel"),
        name="filter_taps",
    )(hid, w_out, deltas4)

    a = _outer_dft(_hi_lo(gt_fwd), taps.reshape(1, n1h, GRID_W, cols), 0, c=cols, tl=1024)[0]
    s4 = sums.reshape(2, 2, 2, width)
    bwd0 = taps[0].reshape(2, 2, width)[:, 1]
    stats = jnp.stack([s4[0, :, 0], s4[1, :, 0], s4[0, :, 1], s4[1, :, 1], bwd0], axis=1)

    kb = min(SPECTRUM_BINS, n1h)

    def tap_spec(side):
        return pl.BlockSpec((2, kb, GRID_W, width), lambda k1, f: (0, k1, 0, 2 * f + side))

    f_hi, f_lo = _hi_lo(f_fwd)
    f_spec = pl.BlockSpec((2 * GRID_W, 2 * GRID_W), lambda k1, f: (0, 0))
    return pl.pallas_call(
        functools.partial(_spectrum_kernel, n=n),
        out_shape=jax.ShapeDtypeStruct((2, 2, n1h, GRID_W, width), F32),
        grid=(n1h // kb, 2),
        in_specs=[tap_spec(0), tap_spec(1), f_spec, f_spec,
                  pl.BlockSpec((None, 5, width), lambda k1, f: (f, 0, 0)),
                  pl.BlockSpec((None, 1, width), lambda k1, f: (f, 0, 0))],
        out_specs=pl.BlockSpec((None, 2, kb, GRID_W, width), lambda k1, f: (f, 0, k1, 0, 0)),
        compiler_params=_params("parallel", "parallel"),
        name="filter_spectrum",
    )(a, a, f_hi, f_lo, stats, hy_d.reshape(2, 1, width))


def _inner_conv_kernel(a_ref, f_ref, fi_ref, h_ref, o_ref):
    n2 = GRID_W
    for i in range(a_ref.shape[1]):
        z = _dot(f_ref[...], jnp.concatenate([a_ref[0, i], a_ref[1, i]], axis=0))
        zr, zi = z[:n2], z[n2:]
        hr, hi = h_ref[0, i], h_ref[1, i]
        stacked = jnp.concatenate([zr * hr - zi * hi, zr * hi + zi * hr], axis=0).astype(BF16)
        w = _dot(fi_ref[...], stacked).astype(BF16)
        o_ref[0, i] = w[:n2]
        o_ref[1, i] = w[n2:]


def _inner_conv(a, f_fwd, f_inv, spec, filt):
    bsz, _, n1h, n2, c = a.shape
    kb = min(INNER_BINS, n1h)
    return pl.pallas_call(
        _inner_conv_kernel,
        out_shape=jax.ShapeDtypeStruct(a.shape, BF16),
        grid=(n1h // kb, bsz),
        in_specs=[pl.BlockSpec((None, 2, kb, n2, c), lambda k1, b: (b, 0, k1, 0, 0)),
                  pl.BlockSpec((2 * n2, 2 * n2), lambda k1, b: (0, 0)),
                  pl.BlockSpec((2 * n2, 2 * n2), lambda k1, b: (0, 0)),
                  pl.BlockSpec((None, 2, kb, n2, c), lambda k1, b: (filt, 0, k1, 0, 0))],
        out_specs=pl.BlockSpec((None, 2, kb, n2, c), lambda k1, b: (b, 0, k1, 0, 0)),
        compiler_params=_params("parallel", "arbitrary"),
        name="inner_conv",
    )(a, f_fwd, f_inv, spec)


def _outer_idft_gate_kernel(gi_ref, g_ref, b_ref, x_ref, o_ref, *, again):
    n1h, _, tl = x_ref.shape
    parts = []
    for h, (bs, xs) in enumerate(zip(_slabs(b_ref[...].astype(F32)), _slabs(x_ref[...]))):
        stacked = bs.reshape(2 * n1h * OUTER_ROWS, tl).astype(BF16)
        y = _dot(gi_ref[h], stacked) * xs.reshape(n1h * OUTER_ROWS, tl)
        if again:
            parts.append(_dot(g_ref[h], y.astype(BF16)).reshape(2, n1h, OUTER_ROWS, tl))
        else:
            parts.append(y.reshape(n1h, OUTER_ROWS, tl))
    o_ref[...] = jnp.concatenate(parts, axis=-2).astype(o_ref.dtype)


def _outer_idft_gate(g_inv, g_fwd, b, px4, seg, *, again, tl):
    bsz, _, n1h, n2, c = b.shape
    tl = min(tl, c)
    per = c // tl
    rows = BF16_ROWS
    if again:
        out_shape = jax.ShapeDtypeStruct((bsz, 2, n1h, n2, c), BF16)
        out_spec = pl.BlockSpec((None, 2, n1h, rows, tl), lambda j, bb, l: (bb, 0, 0, j, l))
    else:
        out_shape = jax.ShapeDtypeStruct((bsz, n1h, n2, c), BF16)
        out_spec = pl.BlockSpec((None, n1h, rows, tl), lambda j, bb, l: (bb, 0, j, l))
    return pl.pallas_call(
        functools.partial(_outer_idft_gate_kernel, again=again),
        out_shape=out_shape,
        grid=(n2 // rows, bsz, per),
        in_specs=[pl.BlockSpec((rows // OUTER_ROWS,) + g_inv.shape[1:], lambda j, bb, l: (j, 0, 0)),
                  pl.BlockSpec((rows // OUTER_ROWS,) + g_fwd.shape[1:], lambda j, bb, l: (j, 0, 0)),
                  pl.BlockSpec((None, 2, n1h, rows, tl), lambda j, bb, l: (bb, 0, 0, j, l)),
                  pl.BlockSpec((None, n1h, rows, tl), lambda j, bb, l: (bb, 0, j, seg * per + l))],
        out_specs=out_spec,
        compiler_params=_params("parallel", "parallel", "parallel"),
        name="outer_idft_gate",
    )(g_inv, g_fwd, b, px4)


def _hyena_mix(px, spec, consts, *, bsz, n, width):
    g_fwd, g_inv, f_fwd, f_inv = consts
    g_fwd_b, g_inv_b = jnp.asarray(g_fwd, BF16), jnp.asarray(g_inv, BF16)
    f_fwd_b, f_inv_b = jnp.asarray(f_fwd, BF16), jnp.asarray(f_inv, BF16)
    n1h = n // GRID_W
    px4 = px.reshape(bsz, n1h, GRID_W, -1)
    tl = width
    a = _outer_dft((g_fwd_b,), px4, 5, c=width, tl=tl)
    a = _inner_conv(a, f_fwd_b, f_inv_b, spec, 0)
    a = _outer_idft_gate(g_inv_b, g_fwd_b, a, px4, 6, again=True, tl=tl)
    a = _inner_conv(a, f_fwd_b, f_inv_b, spec, 1)
    hy = _outer_idft_gate(g_inv_b, g_fwd_b, a, px4, 7, again=False, tl=tl)
    return hy.reshape(bsz * n, width)


def _out_proj_kernel(x_ref, hg_ref, hy_ref, wa_ref, wb_ref, gt_ref, o_ref):
    mix = _dot(hg_ref[...], wa_ref[...]) + _dot(hy_ref[...], wb_ref[...])
    o_ref[...] = x_ref[...] + gt_ref[0] * mix


def _out_proj(x, hg, hy, w, gate, *, tokens_per_mod, tm):
    t, d = x.shape
    wdt = hg.shape[1]
    tm = min(tm, tokens_per_mod)
    return pl.pallas_call(
        _out_proj_kernel,
        out_shape=jax.ShapeDtypeStruct((t, d), F32),
        grid=(t // tm,),
        in_specs=[pl.BlockSpec((tm, d), lambda i: (i, 0)),
                  pl.BlockSpec((tm, wdt), lambda i: (i, 0)),
                  pl.BlockSpec((tm, wdt), lambda i: (i, 0)),
                  pl.BlockSpec((wdt, d), lambda i: (0, 0)),
                  pl.BlockSpec((wdt, d), lambda i: (1, 0)),
                  pl.BlockSpec((1, 1, d), lambda i: ((i * tm) // tokens_per_mod, 0, 0))],
        out_specs=pl.BlockSpec((tm, d), lambda i: (i, 0)),
        compiler_params=_params("parallel"),
        name="out_proj",
    )(x, hg, hy, w, w, gate)


def _tiles(d_ff):
    return TOKEN_TILE, 2 * TOKEN_TILE, HIDDEN_TILE if d_ff % HIDDEN_TILE == 0 else d_ff


def kernel(x, c, ctx, c_ctx, ada_w, ada_b, norm_ffn1, ffn1_w1, ffn1_w3, ffn1_w2, norm_mix, w_in, hg_lb_logits, hg_norm, hy_conv_w, hy_conv_b, filt_w1, filt_b1, filt_w2, filt_b2, filt_w3, filt_b3, filt_freq, filt_w_out, hy_d, w_out, norm_ffn2, ffn2_w1, ffn2_w3, ffn2_w2, final_norm):
    bsz, n, d = x.shape
    n_ctx = ctx.shape[1]
    depth = ada_w.shape[0]
    assert depth == 1, "single-layer block"
    width = d // 2
    n_seg = w_in.shape[2] // width
    assert n_seg == 8 and n % GRID_W == 0 and n_ctx % SCAN_CHUNK == 0 and width % HEAD_DIM == 0
    tm, tm_wide, tf = _tiles(ffn1_w1.shape[2])
    l = 0

    rows = -(-(bsz + 1) // 8) * 8
    cs = jnp.concatenate([c, c_ctx[None, :], jnp.zeros((rows - bsz - 1, d), F32)], axis=0)
    mods = _ada(cs, ada_w[l], ada_b[l]).reshape(rows, N_MOD, d)
    mx = [mods[:bsz, i][:, None, :] for i in range(N_MOD)]
    mc = [mods[bsz:bsz + 1, i][:, None, :] for i in range(N_MOD)]

    bf = _to_bf16
    w1a, w3a, w2a = bf(ffn1_w1[l]), bf(ffn1_w3[l]), bf(ffn1_w2[l])
    xt = x.reshape(bsz * n, d)
    yt = ctx.reshape(bsz * n_ctx, d)

    xt = _ffn(xt, mx[0], mx[1], mx[2], norm_ffn1[l], w1a, w3a, w2a, final_norm,
              tokens_per_mod=n, final_norm=False, tm=tm, tf=tf)
    yt = _ffn(yt, mc[0], mc[1], mc[2], norm_ffn1[l], w1a, w3a, w2a, final_norm,
              tokens_per_mod=bsz * n_ctx, final_norm=False, tm=tm, tf=tf)

    w_in_b = bf(w_in[l])
    pc = _proj(yt, mc[3], mc[4], norm_mix[l], w_in_b, hy_conv_w[l], hy_conv_b[l], n_seg=3, seg_w=width,
               conv_from=n_seg, tokens_per_mod=bsz * n_ctx, tm=tm)
    px = _proj(xt, mx[3], mx[4], norm_mix[l], w_in_b, hy_conv_w[l], hy_conv_b[l], n_seg=n_seg, seg_w=width,
               conv_from=5, tokens_per_mod=n, tm=tm_wide)
    cols = n_seg * width

    hg = _hgrn(px.reshape(bsz, n, cols), pc.reshape(bsz, n_ctx, 3 * width), hg_lb_logits, hg_norm[l], width=width)

    consts = _dft_constants(n)
    filt = (filt_w1[l], filt_b1[l], filt_w2[l], filt_b2[l], filt_w3[l], filt_b3[l], filt_freq[l], filt_w_out[l])
    spec = _hyena_spectra(n, width, filt, hy_d[l], consts)
    hy = _hyena_mix(px, spec, consts, bsz=bsz, n=n, width=width)

    xt = _out_proj(xt, hg.reshape(bsz * n, width), hy, bf(w_out[l]), mx[5],
                   tokens_per_mod=n, tm=tm)

    out = _ffn(xt, mx[6], mx[7], mx[8], norm_ffn2[l], bf(ffn2_w1[l]), bf(ffn2_w3[l]), bf(ffn2_w2[l]), final_norm,
               tokens_per_mod=n, final_norm=True, tm=tm, tf=tf)
    return out.reshape(bsz, n, d)
```

```python
import functools
import math

import numpy as np
import jax
import jax.numpy as jnp
from jax import lax
from jax.experimental import pallas as pl
from jax.experimental.pallas import tpu as pltpu

F32 = jnp.float32
BF16 = jnp.bfloat16
EPS = 1e-6
N_MOD = 9
HEAD_DIM = 128
GRID_W = 64
CAST_ROWS = 256
TOKEN_TILE = 512
HIDDEN_TILE = 1024
SCAN_CHUNK = 64
PREP_CHUNKS = 4
SCAN_BLOCKS = 4
OUTER_ROWS = 8
BF16_ROWS = 16
INNER_BINS = 16
SPECTRUM_BINS = 8
FILT_EMB = 33
DECAY_TARGET = 1e-2
DECAY_FAST_PCT = 0.3
DECAY_SLOW_PCT = 1.5
VMEM_LIMIT_BYTES = 56 * 1024 * 1024
HIGHEST = lax.Precision.HIGHEST


def _params(*semantics):
    return pltpu.CompilerParams(dimension_semantics=semantics, vmem_limit_bytes=VMEM_LIMIT_BYTES)


def _dot(a, b):
    return jnp.dot(a, b, preferred_element_type=F32)


def _silu(a):
    return a * jax.nn.sigmoid(a)


def _cast_kernel(x_ref, o_ref):
    o_ref[...] = x_ref[...].astype(BF16)


def _to_bf16(w):
    r, c = w.shape
    rb = CAST_ROWS if r % CAST_ROWS == 0 else r
    return pl.pallas_call(
        _cast_kernel,
        out_shape=jax.ShapeDtypeStruct((r, c), BF16),
        grid=(r // rb,),
        in_specs=[pl.BlockSpec((rb, c), lambda i: (i, 0))],
        out_specs=pl.BlockSpec((rb, c), lambda i: (i, 0)),
        compiler_params=_params("parallel"),
        name="to_bf16",
    )(w)


def _ada_kernel(c_ref, w_ref, b_ref, o_ref):
    h = _silu(c_ref[...]).astype(BF16)
    o_ref[...] = _dot(h, w_ref[...].astype(BF16)) + b_ref[...]


def _ada(cs, w, b):
    rows, d = cs.shape
    n = w.shape[1]
    tn = d // 2
    return pl.pallas_call(
        _ada_kernel,
        out_shape=jax.ShapeDtypeStruct((rows, n), F32),
        grid=(n // tn,),
        in_specs=[pl.BlockSpec((rows, d), lambda j: (0, 0)),
                  pl.BlockSpec((d, tn), lambda j: (0, j)),
                  pl.BlockSpec((1, tn), lambda j: (0, j))],
        out_specs=pl.BlockSpec((rows, tn), lambda j: (0, j)),
        compiler_params=_params("arbitrary"),
        name="ada_mod",
    )(cs, w, b.reshape(1, n))


def _norm_mod(x, gain, shift, scale):
    y = x * lax.rsqrt(jnp.mean(x * x, axis=-1, keepdims=True) + EPS)
    return (y * (gain * (1.0 + scale)) + shift).astype(BF16)


def _ffn_kernel(x_ref, sh_ref, sc_ref, gt_ref, g_ref, w1_ref, w3_ref, w2_ref, fin_ref, o_ref, h_scr,
                *, final_norm, n_hidden, last_cols):
    j = pl.program_id(1)
    tf = w1_ref.shape[1]

    def hidden_tile(cols=tf):
        h = h_scr[...]
        a = _dot(h, w1_ref[:, :cols])
        b = _dot(h, w3_ref[:, :cols])
        return _dot((_silu(a) * b).astype(BF16), w2_ref[:cols, :])

    def finish(acc):
        out = x_ref[...] + (0.5 * gt_ref[0]) * acc
        if final_norm:
            out = out * lax.rsqrt(jnp.mean(out * out, axis=-1, keepdims=True) + EPS) * fin_ref[...]
        return out

    last = n_hidden - 1

    @pl.when(j == 0)
    def _():
        h_scr[...] = _norm_mod(x_ref[...], g_ref[...], sh_ref[0], sc_ref[0])
        o_ref[...] = finish(hidden_tile(last_cols)) if last == 0 else hidden_tile()

    @pl.when((j > 0) & (j < last))
    def _():
        o_ref[...] += hidden_tile()

    if last > 0:
        @pl.when(j == last)
        def _():
            o_ref[...] = finish(o_ref[...] + hidden_tile(last_cols))


def _ffn(x, shift, scale, gate, gain, w1, w3, w2, fin, *, tokens_per_mod, final_norm, tm, tf):
    t, d = x.shape
    f = w1.shape[1]
    tm = min(tm, tokens_per_mod)
    mod_spec = pl.BlockSpec((1, 1, d), lambda i, j: ((i * tm) // tokens_per_mod, 0, 0))
    vec_spec = pl.BlockSpec((1, d), lambda i, j: (0, 0))
    n_hidden = pl.cdiv(f, tf)
    return pl.pallas_call(
        functools.partial(_ffn_kernel, final_norm=final_norm, n_hidden=n_hidden, last_cols=f - (n_hidden - 1) * tf),
        out_shape=jax.ShapeDtypeStruct((t, d), F32),
        grid=(t // tm, n_hidden),
        in_specs=[pl.BlockSpec((tm, d), lambda i, j: (i, 0)),
                  mod_spec, mod_spec, mod_spec, vec_spec,
                  pl.BlockSpec((d, tf), lambda i, j: (0, j)),
                  pl.BlockSpec((d, tf), lambda i, j: (0, j)),
                  pl.BlockSpec((tf, d), lambda i, j: (j, 0)),
                  vec_spec],
        out_specs=pl.BlockSpec((tm, d), lambda i, j: (i, 0)),
        scratch_shapes=[pltpu.VMEM((tm, d), BF16)],
        compiler_params=_params("parallel", "arbitrary"),
        name="swiglu_ffn",
    )(x, shift, scale, gate, gain.reshape(1, d), w1, w3, w2, fin.reshape(1, d))


def _proj_kernel(x_ref, sh_ref, sc_ref, g_ref, w_ref, cw_ref, cb_ref, o_ref, h_scr, *, conv_from):
    j = pl.program_id(1)

    def segment():
        return _dot(h_scr[...], w_ref[...])

    @pl.when(j == 0)
    def _():
        h_scr[...] = _norm_mod(x_ref[...], g_ref[...], sh_ref[0], sc_ref[0])
        o_ref[...] = segment()

    @pl.when((j > 0) & (j < conv_from))
    def _():
        o_ref[...] = segment()

    @pl.when(j >= conv_from)
    def _():
        p = segment()
        tm, w = p.shape
        rows3 = lambda t: t.reshape(tm // GRID_W, GRID_W, w)
        y = (rows3(pltpu.roll(p, 1, axis=0)) * cw_ref[0] + rows3(p) * cw_ref[1]
             + rows3(pltpu.roll(p, tm - 1, axis=0)) * cw_ref[2] + cb_ref[...])
        o_ref[...] = y.reshape(tm, w)


def _proj(x, shift, scale, gain, w, conv_w, conv_b, *, n_seg, seg_w, conv_from, tokens_per_mod, tm):
    t, d = x.shape
    tm = min(tm, tokens_per_mod)
    mod_spec = pl.BlockSpec((1, 1, d), lambda i, j: ((i * tm) // tokens_per_mod, 0, 0))
    conv_idx = lambda i, j: (0, jnp.maximum(j - conv_from, 0))
    pos = jnp.arange(GRID_W)[None, :, None]
    edge = jnp.stack([pos > 0, pos >= 0, pos < GRID_W - 1], axis=0)[:, 0]
    conv_w = jnp.where(edge, conv_w[:, None, :], 0.0)
    return pl.pallas_call(
        functools.partial(_proj_kernel, conv_from=conv_from),
        out_shape=jax.ShapeDtypeStruct((t, n_seg * seg_w), F32),
        grid=(t // tm, n_seg),
        in_specs=[pl.BlockSpec((tm, d), lambda i, j: (i, 0)),
                  mod_spec, mod_spec,
                  pl.BlockSpec((1, d), lambda i, j: (0, 0)),
                  pl.BlockSpec((d, seg_w), lambda i, j: (0, j)),
                  pl.BlockSpec((3, GRID_W, seg_w), lambda i, j: (0, 0, jnp.maximum(j - conv_from, 0))),
                  pl.BlockSpec((1, seg_w), conv_idx)],
        out_specs=pl.BlockSpec((tm, seg_w), lambda i, j: (i, j)),
        scratch_shapes=[pltpu.VMEM((tm, d), BF16)],
        compiler_params=_params("parallel", "arbitrary"),
        name="in_proj",
    )(x, shift, scale, gain.reshape(1, d), w, conv_w, conv_b.reshape(1, -1))


def _split3(g):
    g1 = g.astype(BF16)
    r1 = g - g1.astype(F32)
    g2 = r1.astype(BF16)
    return g1, g2, (r1 - g2.astype(F32)).astype(BF16)


def _split_dot(tri, g):
    g1, g2, g3 = _split3(g)
    return _dot(tri, g1) + _dot(tri, g2) + _dot(tri, g3)


def _gates(a, lb):
    f = lb + (1.0 - lb) * jax.nn.sigmoid(a)
    return 1.0 - f, jnp.log(f)


def _tn_dot(a, b):
    return lax.dot_general(a, b, (((0,), (0,)), ((), ())), preferred_element_type=F32)


def _nt_dot(a, b):
    return lax.dot_general(a, b, (((1,), (1,)), ((), ())), preferred_element_type=F32)


def _hgrn_kernel(af_ref, ab_ref, v_ref, q_ref, gate_ref, caf_ref, cab_ref, cv_ref, lbl_ref, gain_ref,
                 o_ref, of_scr, ob_scr, sf_scr, sb_scr, qin_scr, qt_scr, kt_scr, kd_scr, dec_scr,
                 *, n_chunks, n_ctx_chunks):
    c = SCAN_CHUNK
    mid = c // 2
    rows_p = PREP_CHUNKS * c
    prow = lax.broadcasted_iota(jnp.int32, (rows_p, rows_p), 0)
    pcol = lax.broadcasted_iota(jnp.int32, (rows_p, rows_p), 1)
    same = (prow // c) == (pcol // c)
    tri_fwd_p = jnp.where(same & (prow >= pcol), 1.0, 0.0).astype(BF16)
    tri_bwd_p = jnp.where(same & (pcol >= prow), 1.0, 0.0).astype(BF16)

    def lower_bound(direction):
        lg = lbl_ref[direction]
        ex = jnp.exp(lg - jnp.max(lg, axis=0, keepdims=True))
        return ex[0:1, :] / jnp.sum(ex, axis=0, keepdims=True)

    lb_f = lower_bound(0)
    lb_b = lower_bound(1)
    q_scale = HEAD_DIM ** -0.5

    n_blocks = n_chunks // PREP_CHUNKS
    n_steps = n_blocks // SCAN_BLOCKS

    def step_blocks(it):
        fwd = [(0, it * SCAN_BLOCKS + u) for u in range(SCAN_BLOCKS)]
        return fwd + [(1, n_blocks - 1 - bi) for _, bi in fwd]

    def prepare(it, s):
        it = jnp.minimum(it, n_steps - 1)
        blocks = step_blocks(it)
        los = [pl.multiple_of(bi * rows_p, rows_p) for _, bi in blocks]
        gates = [_gates((af_ref, ab_ref)[d][pl.ds(lo, rows_p), :], (lb_f, lb_b)[d]) for (d, _), lo in zip(blocks, los)]
        sums = {}
        for d, tri_p in ((0, tri_fwd_p), (1, tri_bwd_p)):
            slots_d = [slot for slot, (dd, _) in enumerate(blocks) if dd == d]
            wide = _dot(tri_p, jnp.concatenate([p for slot in slots_d for p in _split3(gates[slot][1])], axis=1))
            for n_s, slot in enumerate(slots_d):
                parts = [wide[:, (3 * n_s + m) * HEAD_DIM:(3 * n_s + m + 1) * HEAD_DIM] for m in range(3)]
                sums[slot] = parts[0] + parts[1] + parts[2]
        for slot, ((d, _), lo) in enumerate(zip(blocks, los)):
            end_row = (c - 1, 0)[d]
            k = gates[slot][0]
            b = sums[slot].reshape(PREP_CHUNKS, c, HEAD_DIM)
            b_mid = b[:, mid:mid + 1, :]
            b_end = b[:, end_row:end_row + 1, :]
            q_t = (q_ref[pl.ds(lo, rows_p), :] * q_scale).reshape(PREP_CHUNKS, c, HEAD_DIM) * jnp.exp(b - b_mid)
            k_t = k.reshape(PREP_CHUNKS, c, HEAD_DIM) * jnp.exp(b_mid - b)
            flat = lambda t: t.reshape(rows_p, HEAD_DIM).astype(BF16)
            qt_scr[s, slot] = flat(q_t)
            kt_scr[s, slot] = flat(k_t)
            qin_scr[s, slot] = flat(q_t * jnp.exp(b_mid))
            kd_scr[s, slot] = flat(k_t * jnp.exp(b_end - b_mid))
            dec_scr[s, slot] = jnp.broadcast_to(jnp.exp(b_end), (PREP_CHUNKS, 8, HEAD_DIM))

    zeros_blk = jnp.zeros((c, HEAD_DIM), BF16)

    def block_diag(t):
        return jnp.concatenate(
            [jnp.concatenate([t[j * c:(j + 1) * c] if m == j else zeros_blk for m in range(PREP_CHUNKS)], axis=1)
             for j in range(PREP_CHUNKS)], axis=0)

    masks = (same & (prow >= pcol), same & (pcol >= prow))
    st_refs = (sf_scr, sb_scr)
    out_refs = (of_scr, ob_scr)
    orders = (tuple(range(PREP_CHUNKS)), tuple(range(PREP_CHUNKS - 1, -1, -1)))

    gain = gain_ref[...]

    def scan(it, s, finish):
        streams = step_blocks(it)
        los = [pl.multiple_of(bi * rows_p, rows_p) for _, bi in streams]
        vbs = [v_ref[pl.ds(lo, rows_p), :].astype(BF16) for lo in los]
        scores = [_nt_dot(qt_scr[s, slot], kt_scr[s, slot]) for slot in range(len(streams))]
        upds = [_tn_dot(vb, block_diag(kd_scr[s, slot])) for slot, vb in enumerate(vbs)]
        outs = [_dot(jnp.where(masks[d], sc, 0.0).astype(BF16), vb) for (d, _), sc, vb in zip(streams, scores, vbs)]
        seen = [[None] * PREP_CHUNKS for _ in streams]
        for d in (0, 1):
            st = st_refs[d][...]
            for slot, (ds_, _) in enumerate(streams):
                if ds_ != d:
                    continue
                for j in orders[d]:
                    seen[slot][j] = st.astype(BF16)
                    st = st * dec_scr[s, slot, j][0:1, :] + upds[slot][:, j * HEAD_DIM:(j + 1) * HEAD_DIM]
            st_refs[d][...] = st
        for slot, ((d, _), lo) in enumerate(zip(streams, los)):
            q_in = qin_scr[s, slot]
            carried = jnp.concatenate([_nt_dot(q_in[j * c:(j + 1) * c], seen[slot][j]) for j in range(PREP_CHUNKS)],
                                      axis=0)
            rows = pl.ds(lo, rows_p)
            if finish:
                o = outs[slot] + carried + out_refs[1 - d][rows, :]
                o = o * lax.rsqrt(jnp.mean(o * o, axis=-1, keepdims=True) + EPS) * gain
                o_ref[rows, :] = (o * _silu(gate_ref[rows, :])).astype(o_ref.dtype)
            else:
                out_refs[d][rows, :] = outs[slot] + carried

    n_ctx_blocks = n_ctx_chunks // PREP_CHUNKS
    ctx_state = [jnp.zeros((HEAD_DIM, HEAD_DIM), F32), jnp.zeros((HEAD_DIM, HEAD_DIM), F32)]
    for step in range(n_ctx_blocks):
        upds, decs = [], []
        for d, bi in ((0, step), (1, n_ctx_blocks - 1 - step)):
            a_ref, lb, tri_p, end_row = ((caf_ref, lb_f, tri_fwd_p, c - 1), (cab_ref, lb_b, tri_bwd_p, 0))[d]
            rows_c = pl.ds(bi * rows_p, rows_p)
            k, g = _gates(a_ref[rows_c, :], lb)
            b = _split_dot(tri_p, g).reshape(PREP_CHUNKS, c, HEAD_DIM)
            b_end = b[:, end_row:end_row + 1, :]
            k_d = (k.reshape(PREP_CHUNKS, c, HEAD_DIM) * jnp.exp(b_end - b)).reshape(rows_p, HEAD_DIM).astype(BF16)
            upds.append(_tn_dot(cv_ref[rows_c, :].astype(BF16), block_diag(k_d)))
            decs.append(jnp.exp(b_end))
        for d in (0, 1):
            for j in orders[d]:
                ctx_state[d] = ctx_state[d] * decs[d][j] + upds[d][:, j * HEAD_DIM:(j + 1) * HEAD_DIM]
    sf_scr[...] = ctx_state[0]
    sb_scr[...] = ctx_state[1]

    prepare(0, 0)

    def body(finish):
        def pair(i, carry):
            scan(2 * i, 0, finish)
            prepare(2 * i + 1, 1)
            scan(2 * i + 1, 1, finish)
            prepare(2 * i + 2, 0)
            return carry
        return pair

    lax.fori_loop(0, n_steps // 4, body(False), 0)
    lax.fori_loop(n_steps // 4, n_steps // 2, body(True), 0)


def _hgrn(px, pc, lb_logits, gain, *, width):
    bsz, n, _ = px.shape
    n_ctx = pc.shape[1]
    heads = width // HEAD_DIM
    hd = HEAD_DIM

    def seg(s):
        return pl.BlockSpec((None, n, hd), lambda b, h: (b, 0, s * heads + h))

    def cseg(s):
        return pl.BlockSpec((None, n_ctx, hd), lambda b, h: (b, 0, s * heads + h))

    depth1 = lb_logits.shape[1]
    slots = 2 * SCAN_BLOCKS
    assert (n // SCAN_CHUNK) % (4 * SCAN_BLOCKS * PREP_CHUNKS) == 0, "scan steps come in pairs, per half"
    assert (n_ctx // SCAN_CHUNK) % PREP_CHUNKS == 0, "the context prefix is scanned in whole blocks"
    return pl.pallas_call(
        functools.partial(_hgrn_kernel, n_chunks=n // SCAN_CHUNK, n_ctx_chunks=n_ctx // SCAN_CHUNK),
        out_shape=jax.ShapeDtypeStruct((bsz, n, width), BF16),
        grid=(bsz, heads),
        in_specs=[seg(0), seg(1), seg(2), seg(3), seg(4), cseg(0), cseg(1), cseg(2),
                  pl.BlockSpec((2, depth1, hd), lambda b, h: (0, 0, h)),
                  pl.BlockSpec((1, hd), lambda b, h: (0, h))],
        out_specs=pl.BlockSpec((None, n, hd), lambda b, h: (b, 0, h)),
        scratch_shapes=[pltpu.VMEM((n, hd), F32), pltpu.VMEM((n, hd), F32),
                        pltpu.VMEM((hd, hd), F32), pltpu.VMEM((hd, hd), F32),
                        *[pltpu.VMEM((2, slots, PREP_CHUNKS * SCAN_CHUNK, hd), BF16) for _ in range(4)],
                        pltpu.VMEM((2, slots, PREP_CHUNKS, 8, hd), F32)],
        compiler_params=_params("parallel", "parallel"),
        name="hgrn_scan",
    )(px, px, px, px, px, pc, pc, pc, lb_logits, gain.reshape(1, width))


def _dft_constants(n):
    n2 = GRID_W
    n1h = n // n2
    m2r = np.arange(n2)[None, :]
    k2 = np.arange(n2)[:, None]
    f_ang = -2.0 * np.pi * k2 * m2r / n2
    fr, fi = np.cos(f_ang), np.sin(f_ang)
    f_fwd = np.block([[fr, -fi], [fi, fr]])
    f_inv = np.block([[fr, fi], [-fi, fr]])
    k1 = np.arange(n1h)[:, None, None]
    m1 = np.arange(n1h)[None, :, None]
    m2 = np.arange(n2)[None, None, :]
    theta = -2.0 * np.pi * (k1 + 0.5) * (m1 / (2 * n1h) + m2 / (2 * n))
    parts = np.stack([np.cos(theta), np.sin(theta)], axis=0)
    parts = parts.reshape(2, n1h, n1h, n2 // OUTER_ROWS, OUTER_ROWS)
    eye = np.eye(OUTER_ROWS)
    rows = n1h * OUTER_ROWS
    fwd = np.einsum("pknsj,jm->spkjnm", parts, eye).reshape(n2 // OUTER_ROWS, 2 * rows, rows)
    inv = np.einsum("pknsj,jm->snjpkm", parts, eye).reshape(n2 // OUTER_ROWS, rows, 2 * rows)
    return fwd, inv, f_fwd, f_inv


def _filt_mlp_kernel(z_ref, w1_ref, b1_ref, w2_ref, b2_ref, w3_ref, b3_ref, fr_ref, o_ref):
    fr = fr_ref[...]
    hp = lambda a, b: jnp.dot(a, b, precision=HIGHEST, preferred_element_type=F32)
    hid = jnp.sin(fr * (hp(z_ref[...], w1_ref[...]) + b1_ref[...]))
    hid = jnp.sin(fr * (hp(hid, w2_ref[...]) + b2_ref[...]))
    o_ref[...] = jnp.sin(fr * (hp(hid, w3_ref[...]) + b3_ref[...]))


def _filt_taps_kernel(hid_ref, wo_ref, dl_ref, h_ref, s_ref, *, n):
    hid = hid_ref[...]
    hid_hi = hid.astype(BF16)
    h = _dot3(hid_hi, (hid - hid_hi.astype(F32)).astype(BF16), wo_ref[...])
    t = lax.broadcasted_iota(jnp.int32, h.shape, 0).astype(F32) * (1.0 / max(n - 1, 1))
    hw = h * jnp.exp(-t * dl_ref[...])
    h_ref[...] = hw
    s_ref[0:1, :] = jnp.sum(jnp.abs(hw), axis=0, keepdims=True)
    s_ref[1:2, :] = jnp.abs(hw[0:1, :])


def _spectrum_kernel(af_ref, ab_ref, f_ref, fl_ref, s_ref, d_ref, o_ref, *, n):
    n2 = GRID_W
    l1 = s_ref[0:1, :] + s_ref[2:3, :] - s_ref[3:4, :]
    hb0 = s_ref[4:5, :]
    norm = 1.0 / l1
    scale = 2.0 / (2 * n)
    for i in range(af_ref.shape[1]):
        def inner(a_ref):
            z = _dot3(f_ref[...], fl_ref[...], jnp.concatenate([a_ref[0, i], a_ref[1, i]], axis=0))
            return z[:n2], z[n2:]

        zfr, zfi = inner(af_ref)
        zbr, zbi = inner(ab_ref)
        o_ref[0, i] = scale * ((zfr + zbr - hb0) * norm + d_ref[...])
        o_ref[1, i] = scale * ((zfi - zbi) * norm)


def _dot3(a_hi, a_lo, x):
    x_hi = x.astype(BF16)
    x_lo = (x - x_hi.astype(F32)).astype(BF16)
    return _dot(a_hi, x_hi) + _dot(a_hi, x_lo) + _dot(a_lo, x_hi)


def _hi_lo(a):
    hi = np.asarray(a, np.float64).astype(BF16)
    lo = (a - hi.astype(np.float64)).astype(BF16)
    return jnp.asarray(hi), jnp.asarray(lo)


def _slabs(x):
    return [x[..., h * OUTER_ROWS:(h + 1) * OUTER_ROWS, :] for h in range(x.shape[-2] // OUTER_ROWS)]


def _outer_dft_kernel(*refs, high):
    g_refs, x_ref, o_ref = refs[:-2], refs[-2], refs[-1]
    n1h, _, tl = x_ref.shape
    parts = []
    for h, xs in enumerate(_slabs(x_ref[...])):
        xs = xs.reshape(n1h * OUTER_ROWS, tl)
        r = _dot3(g_refs[0][h], g_refs[1][h], xs) if high else _dot(g_refs[0][h], xs.astype(BF16))
        parts.append(r.reshape(2, n1h, OUTER_ROWS, tl))
    o_ref[...] = jnp.concatenate(parts, axis=2).astype(o_ref.dtype)


def _outer_dft(gs, x4, seg, *, c, tl):
    high = len(gs) == 2
    s, n1h, n2, _ = x4.shape
    tl = min(tl, c)
    per = c // tl
    rows = OUTER_ROWS if high else BF16_ROWS
    g_spec = pl.BlockSpec((rows // OUTER_ROWS,) + gs[0].shape[1:], lambda j, b, l: (j, 0, 0))
    return pl.pallas_call(
        functools.partial(_outer_dft_kernel, high=high),
        out_shape=jax.ShapeDtypeStruct((s, 2, n1h, n2, c), F32 if high else BF16),
        grid=(n2 // rows, s, per),
        in_specs=[g_spec] * len(gs) + [pl.BlockSpec((None, n1h, rows, tl),
                                                    lambda j, b, l: (b, 0, j, seg * per + l))],
        out_specs=pl.BlockSpec((None, 2, n1h, rows, tl), lambda j, b, l: (b, 0, 0, j, l)),
        compiler_params=_params("parallel", "parallel", "parallel"),
        name="outer_dft",
    )(*gs, x4)


def _hyena_spectra(n, width, filt, hy_d, consts):
    w1, b1, w2, b2, w3, b3, freq, w_out = filt
    gt_fwd, _, f_fwd, _ = consts
    n1h = n // GRID_W
    order = w1.shape[1]
    pos = np.arange(n, dtype=np.float64)
    t = pos / max(n - 1, 1)
    bands = (FILT_EMB - 1) // 2
    fb = np.linspace(1e-4, bands - 1, bands)
    ang = (2 * math.pi / n) * pos[:, None] * fb[None, :]
    z = np.concatenate([t[:, None], np.cos(ang), -np.sin(ang)], axis=-1)
    emb_pad = 128
    z = jnp.asarray(np.pad(z, ((0, 0), (0, emb_pad - FILT_EMB))), F32)
    w1p = jnp.pad(w1, ((0, emb_pad - FILT_EMB), (0, 0)))
    row = lambda a: a.reshape(1, -1)
    full = lambda shape: pl.BlockSpec(shape, lambda: tuple(0 for _ in shape))
    hid = pl.pallas_call(
        _filt_mlp_kernel,
        out_shape=jax.ShapeDtypeStruct((n, order), F32),
        in_specs=[full((n, emb_pad)), full((emb_pad, order)), full((1, order)), full((order, order)),
                  full((1, order)), full((order, order)), full((1, order)), full((1, order))],
        out_specs=full((n, order)),
        compiler_params=pltpu.CompilerParams(vmem_limit_bytes=VMEM_LIMIT_BYTES),
        name="filter_mlp",
    )(z, w1p, row(b1), w2, row(b2), w3, row(b3), row(freq))

    cols = 4 * width
    deltas = np.abs(np.linspace(math.log(DECAY_TARGET) / DECAY_SLOW_PCT,
                                math.log(DECAY_TARGET) / DECAY_FAST_PCT, width))
    deltas4 = jnp.asarray(np.tile(deltas, 4)[None, :], F32)
    tc = min(512, cols)
    taps, sums = pl.pallas_call(
        functools.partial(_filt_taps_kernel, n=n),
        out_shape=(jax.ShapeDtypeStruct((n, cols), F32), jax.ShapeDtypeStruct((2, cols), F32)),
        grid=(cols // tc,),
        in_specs=[pl.BlockSpec((n, order), lambda j: (0, 0)),
                  pl.BlockSpec((order, tc), lambda j: (0, j)),
                  pl.BlockSpec((1, tc), lambda j: (0, j))],
        out_specs=(pl.BlockSpec((n, tc), lambda j: (0, j)), pl.BlockSpec((2, tc), lambda j: (0, j))),
        compiler_params=_params("parallel"),
        name="filter_taps",
    )(hid, w_out, deltas4)

    a = _outer_dft(_hi_lo(gt_fwd), taps.reshape(1, n1h, GRID_W, cols), 0, c=cols, tl=1024)[0]
    s4 = sums.reshape(2, 2, 2, width)
    bwd0 = taps[0].reshape(2, 2, width)[:, 1]
    stats = jnp.stack([s4[0, :, 0], s4[1, :, 0], s4[0, :, 1], s4[1, :, 1], bwd0], axis=1)

    kb = min(SPECTRUM_BINS, n1h)

    def tap_spec(side):
        return pl.BlockSpec((2, kb, GRID_W, width), lambda k1, f: (0, k1, 0, 2 * f + side))

    f_hi, f_lo = _hi_lo(f_fwd)
    f_spec = pl.BlockSpec((2 * GRID_W, 2 * GRID_W), lambda k1, f: (0, 0))
    return pl.pallas_call(
        functools.partial(_spectrum_kernel, n=n),
        out_shape=jax.ShapeDtypeStruct((2, 2, n1h, GRID_W, width), F32),
        grid=(n1h // kb, 2),
        in_specs=[tap_spec(0), tap_spec(1), f_spec, f_spec,
                  pl.BlockSpec((None, 5, width), lambda k1, f: (f, 0, 0)),
                  pl.BlockSpec((None, 1, width), lambda k1, f: (f, 0, 0))],
        out_specs=pl.BlockSpec((None, 2, kb, GRID_W, width), lambda k1, f: (f, 0, k1, 0, 0)),
        compiler_params=_params("parallel", "parallel"),
        name="filter_spectrum",
    )(a, a, f_hi, f_lo, stats, hy_d.reshape(2, 1, width))


def _inner_conv_kernel(a_ref, f_ref, fi_ref, h_ref, o_ref):
    n2 = GRID_W
    for i in range(a_ref.shape[1]):
        z = _dot(f_ref[...], jnp.concatenate([a_ref[0, i], a_ref[1, i]], axis=0))
        zr, zi = z[:n2], z[n2:]
        hr, hi = h_ref[0, i], h_ref[1, i]
        stacked = jnp.concatenate([zr * hr - zi * hi, zr * hi + zi * hr], axis=0).astype(BF16)
        w = _dot(fi_ref[...], stacked).astype(BF16)
        o_ref[0, i] = w[:n2]
        o_ref[1, i] = w[n2:]


def _inner_conv(a, f_fwd, f_inv, spec, filt):
    bsz, _, n1h, n2, c = a.shape
    kb = min(INNER_BINS, n1h)
    return pl.pallas_call(
        _inner_conv_kernel,
        out_shape=jax.ShapeDtypeStruct(a.shape, BF16),
        grid=(n1h // kb, bsz),
        in_specs=[pl.BlockSpec((None, 2, kb, n2, c), lambda k1, b: (b, 0, k1, 0, 0)),
                  pl.BlockSpec((2 * n2, 2 * n2), lambda k1, b: (0, 0)),
                  pl.BlockSpec((2 * n2, 2 * n2), lambda k1, b: (0, 0)),
                  pl.BlockSpec((None, 2, kb, n2, c), lambda k1, b: (filt, 0, k1, 0, 0))],
        out_specs=pl.BlockSpec((None, 2, kb, n2, c), lambda k1, b: (b, 0, k1, 0, 0)),
        compiler_params=_params("parallel", "arbitrary"),
        name="inner_conv",
    )(a, f_fwd, f_inv, spec)


def _outer_idft_gate_kernel(gi_ref, g_ref, b_ref, x_ref, o_ref, *, again):
    n1h, _, tl = x_ref.shape
    parts = []
    for h, (bs, xs) in enumerate(zip(_slabs(b_ref[...].astype(F32)), _slabs(x_ref[...]))):
        stacked = bs.reshape(2 * n1h * OUTER_ROWS, tl).astype(BF16)
        y = _dot(gi_ref[h], stacked) * xs.reshape(n1h * OUTER_ROWS, tl)
        if again:
            parts.append(_dot(g_ref[h], y.astype(BF16)).reshape(2, n1h, OUTER_ROWS, tl))
        else:
            parts.append(y.reshape(n1h, OUTER_ROWS, tl))
    o_ref[...] = jnp.concatenate(parts, axis=-2).astype(o_ref.dtype)


def _outer_idft_gate(g_inv, g_fwd, b, px4, seg, *, again, tl):
    bsz, _, n1h, n2, c = b.shape
    tl = min(tl, c)
    per = c // tl
    rows = BF16_ROWS
    if again:
        out_shape = jax.ShapeDtypeStruct((bsz, 2, n1h, n2, c), BF16)
        out_spec = pl.BlockSpec((None, 2, n1h, rows, tl), lambda j, bb, l: (bb, 0, 0, j, l))
    else:
        out_shape = jax.ShapeDtypeStruct((bsz, n1h, n2, c), BF16)
        out_spec = pl.BlockSpec((None, n1h, rows, tl), lambda j, bb, l: (bb, 0, j, l))
    return pl.pallas_call(
        functools.partial(_outer_idft_gate_kernel, again=again),
        out_shape=out_shape,
        grid=(n2 // rows, bsz, per),
        in_specs=[pl.BlockSpec((rows // OUTER_ROWS,) + g_inv.shape[1:], lambda j, bb, l: (j, 0, 0)),
                  pl.BlockSpec((rows // OUTER_ROWS,) + g_fwd.shape[1:], lambda j, bb, l: (j, 0, 0)),
                  pl.BlockSpec((None, 2, n1h, rows, tl), lambda j, bb, l: (bb, 0, 0, j, l)),
                  pl.BlockSpec((None, n1h, rows, tl), lambda j, bb, l: (bb, 0, j, seg * per + l))],
        out_specs=out_spec,
        compiler_params=_params("parallel", "parallel", "parallel"),
        name="outer_idft_gate",
    )(g_inv, g_fwd, b, px4)


def _hyena_mix(px, spec, consts, *, bsz, n, width):
    g_fwd, g_inv, f_fwd, f_inv = consts
    g_fwd_b, g_inv_b = jnp.asarray(g_fwd, BF16), jnp.asarray(g_inv, BF16)
    f_fwd_b, f_inv_b = jnp.asarray(f_fwd, BF16), jnp.asarray(f_inv, BF16)
    n1h = n // GRID_W
    px4 = px.reshape(bsz, n1h, GRID_W, -1)
    tl = width
    a = _outer_dft((g_fwd_b,), px4, 5, c=width, tl=tl)
    a = _inner_conv(a, f_fwd_b, f_inv_b, spec, 0)
    a = _outer_idft_gate(g_inv_b, g_fwd_b, a, px4, 6, again=True, tl=tl)
    a = _inner_conv(a, f_fwd_b, f_inv_b, spec, 1)
    hy = _outer_idft_gate(g_inv_b, g_fwd_b, a, px4, 7, again=False, tl=tl)
    return hy.reshape(bsz * n, width)


def _out_proj_kernel(x_ref, hg_ref, hy_ref, wa_ref, wb_ref, gt_ref, o_ref):
    mix = _dot(hg_ref[...], wa_ref[...]) + _dot(hy_ref[...], wb_ref[...])
    o_ref[...] = x_ref[...] + gt_ref[0] * mix


def _out_proj(x, hg, hy, w, gate, *, tokens_per_mod, tm):
    t, d = x.shape
    wdt = hg.shape[1]
    tm = min(tm, tokens_per_mod)
    return pl.pallas_call(
        _out_proj_kernel,
        out_shape=jax.ShapeDtypeStruct((t, d), F32),
        grid=(t // tm,),
        in_specs=[pl.BlockSpec((tm, d), lambda i: (i, 0)),
                  pl.BlockSpec((tm, wdt), lambda i: (i, 0)),
                  pl.BlockSpec((tm, wdt), lambda i: (i, 0)),
                  pl.BlockSpec((wdt, d), lambda i: (0, 0)),
                  pl.BlockSpec((wdt, d), lambda i: (1, 0)),
                  pl.BlockSpec((1, 1, d), lambda i: ((i * tm) // tokens_per_mod, 0, 0))],
        out_specs=pl.BlockSpec((tm, d), lambda i: (i, 0)),
        compiler_params=_params("parallel"),
        name="out_proj",
    )(x, hg, hy, w, w, gate)


def _tiles(d_ff):
    assert d_ff % 128 == 0, "a ragged last hidden tile still has to be lane-tile aligned"
    return TOKEN_TILE, 2 * TOKEN_TILE, min(HIDDEN_TILE, d_ff)


def kernel(x, c, ctx, c_ctx, ada_w, ada_b, norm_ffn1, ffn1_w1, ffn1_w3, ffn1_w2, norm_mix, w_in, hg_lb_logits, hg_norm, hy_conv_w, hy_conv_b, filt_w1, filt_b1, filt_w2, filt_b2, filt_w3, filt_b3, filt_freq, filt_w_out, hy_d, w_out, norm_ffn2, ffn2_w1, ffn2_w3, ffn2_w2, final_norm):
    bsz, n, d = x.shape
    n_ctx = ctx.shape[1]
    depth = ada_w.shape[0]
    assert depth == 1, "single-layer block"
    width = d // 2
    n_seg = w_in.shape[2] // width
    assert n_seg == 8 and n % GRID_W == 0 and n_ctx % SCAN_CHUNK == 0 and width % HEAD_DIM == 0
    tm, tm_wide, tf = _tiles(ffn1_w1.shape[2])
    l = 0

    rows = -(-(bsz + 1) // 8) * 8
    cs = jnp.concatenate([c, c_ctx[None, :], jnp.zeros((rows - bsz - 1, d), F32)], axis=0)
    mods = _ada(cs, ada_w[l], ada_b[l]).reshape(rows, N_MOD, d)
    mx = [mods[:bsz, i][:, None, :] for i in range(N_MOD)]
    mc = [mods[bsz:bsz + 1, i][:, None, :] for i in range(N_MOD)]

    bf = _to_bf16
    w1a, w3a, w2a = bf(ffn1_w1[l]), bf(ffn1_w3[l]), bf(ffn1_w2[l])
    xt = x.reshape(bsz * n, d)
    yt = ctx.reshape(bsz * n_ctx, d)

    xt = _ffn(xt, mx[0], mx[1], mx[2], norm_ffn1[l], w1a, w3a, w2a, final_norm,
              tokens_per_mod=n, final_norm=False, tm=tm, tf=tf)
    yt = _ffn(yt, mc[0], mc[1], mc[2], norm_ffn1[l], w1a, w3a, w2a, final_norm,
              tokens_per_mod=bsz * n_ctx, final_norm=False, tm=tm, tf=tf)

    w_in_b = bf(w_in[l])
    pc = _proj(yt, mc[3], mc[4], norm_mix[l], w_in_b, hy_conv_w[l], hy_conv_b[l], n_seg=3, seg_w=width,
               conv_from=n_seg, tokens_per_mod=bsz * n_ctx, tm=tm)
    px = _proj(xt, mx[3], mx[4], norm_mix[l], w_in_b, hy_conv_w[l], hy_conv_b[l], n_seg=n_seg, seg_w=width,
               conv_from=5, tokens_per_mod=n, tm=tm_wide)
    cols = n_seg * width

    hg = _hgrn(px.reshape(bsz, n, cols), pc.reshape(bsz, n_ctx, 3 * width), hg_lb_logits, hg_norm[l], width=width)

    consts = _dft_constants(n)
    filt = (filt_w1[l], filt_b1[l], filt_w2[l], filt_b2[l], filt_w3[l], filt_b3[l], filt_freq[l], filt_w_out[l])
    spec = _hyena_spectra(n, width, filt, hy_d[l], consts)
    hy = _hyena_mix(px, spec, consts, bsz=bsz, n=n, width=width)

    xt = _out_proj(xt, hg.reshape(bsz * n, width), hy, bf(w_out[l]), mx[5],
                   tokens_per_mod=n, tm=tm)

    out = _ffn(xt, mx[6], mx[7], mx[8], norm_ffn2[l], bf(ffn2_w1[l]), bf(ffn2_w3[l]), bf(ffn2_w2[l]), final_norm,
               tokens_per_mod=n, final_norm=True, tm=tm, tf=tf)
    return out.reshape(bsz, n, d)
```

```python
import functools
import math

import numpy as np
import jax
import jax.numpy as jnp
from jax import lax
from jax.experimental import pallas as pl
from jax.experimental.pallas import tpu as pltpu

F32 = jnp.float32
BF16 = jnp.bfloat16
EPS = 1e-6
N_MOD = 9
HEAD_DIM = 128
GRID_W = 64
CAST_ROWS = 256
TOKEN_TILE = 512
HIDDEN_TILE = 1024
SCAN_CHUNK = 64
PREP_CHUNKS = 4
SCAN_BLOCKS = 4
OUTER_ROWS = 8
BF16_ROWS = 16
INNER_BINS = 16
SPECTRUM_BINS = 8
FILT_EMB = 33
DECAY_TARGET = 1e-2
DECAY_FAST_PCT = 0.3
DECAY_SLOW_PCT = 1.5
VMEM_LIMIT_BYTES = 56 * 1024 * 1024
HIGHEST = lax.Precision.HIGHEST


def _params(*semantics):
    return pltpu.CompilerParams(dimension_semantics=semantics, vmem_limit_bytes=VMEM_LIMIT_BYTES)


def _dot(a, b):
    return jnp.dot(a, b, preferred_element_type=F32)


def _silu(a):
    return a * jax.nn.sigmoid(a)


def _cast_kernel(x_ref, o_ref):
    o_ref[...] = x_ref[...].astype(BF16)


def _to_bf16(w):
    r, c = w.shape
    rb = CAST_ROWS if r % CAST_ROWS == 0 else r
    return pl.pallas_call(
        _cast_kernel,
        out_shape=jax.ShapeDtypeStruct((r, c), BF16),
        grid=(r // rb,),
        in_specs=[pl.BlockSpec((rb, c), lambda i: (i, 0))],
        out_specs=pl.BlockSpec((rb, c), lambda i: (i, 0)),
        compiler_params=_params("parallel"),
        name="to_bf16",
    )(w)


def _ada_kernel(c_ref, w_ref, b_ref, o_ref):
    h = _silu(c_ref[...]).astype(BF16)
    o_ref[...] = _dot(h, w_ref[...].astype(BF16)) + b_ref[...]


def _ada(cs, w, b):
    rows, d = cs.shape
    n = w.shape[1]
    tn = d // 2
    return pl.pallas_call(
        _ada_kernel,
        out_shape=jax.ShapeDtypeStruct((rows, n), F32),
        grid=(n // tn,),
        in_specs=[pl.BlockSpec((rows, d), lambda j: (0, 0)),
                  pl.BlockSpec((d, tn), lambda j: (0, j)),
                  pl.BlockSpec((1, tn), lambda j: (0, j))],
        out_specs=pl.BlockSpec((rows, tn), lambda j: (0, j)),
        compiler_params=_params("arbitrary"),
        name="ada_mod",
    )(cs, w, b.reshape(1, n))


def _norm_mod(x, gain, shift, scale):
    y = x * lax.rsqrt(jnp.mean(x * x, axis=-1, keepdims=True) + EPS)
    return (y * (gain * (1.0 + scale)) + shift).astype(BF16)


def _ffn_kernel(x_ref, sh_ref, sc_ref, gt_ref, g_ref, w1_ref, w3_ref, w2_ref, fin_ref, o_ref, h_scr,
                *, final_norm, n_hidden, last_cols):
    j = pl.program_id(1)
    tf = w1_ref.shape[1]

    def hidden_tile(cols=tf):
        h = h_scr[...]
        a = _dot(h, w1_ref[:, :cols])
        b = _dot(h, w3_ref[:, :cols])
        return _dot((_silu(a) * b).astype(BF16), w2_ref[:cols, :])

    def finish(acc):
        out = x_ref[...] + (0.5 * gt_ref[0]) * acc
        if final_norm:
            out = out * lax.rsqrt(jnp.mean(out * out, axis=-1, keepdims=True) + EPS) * fin_ref[...]
        return out

    last = n_hidden - 1

    @pl.when(j == 0)
    def _():
        h_scr[...] = _norm_mod(x_ref[...], g_ref[...], sh_ref[0], sc_ref[0])
        o_ref[...] = finish(hidden_tile(last_cols)) if last == 0 else hidden_tile(last_cols)

    @pl.when((j > 0) & (j < last))
    def _():
        o_ref[...] += hidden_tile()

    if last > 0:
        @pl.when(j == last)
        def _():
            o_ref[...] = finish(o_ref[...] + hidden_tile())


def _ffn(x, shift, scale, gate, gain, w1, w3, w2, fin, *, tokens_per_mod, final_norm, tm, tf):
    t, d = x.shape
    f = w1.shape[1]
    tm = min(tm, tokens_per_mod)
    mod_spec = pl.BlockSpec((1, 1, d), lambda i, j: ((i * tm) // tokens_per_mod, 0, 0))
    vec_spec = pl.BlockSpec((1, d), lambda i, j: (0, 0))
    n_hidden = pl.cdiv(f, tf)
    hidden = lambda j: (j + n_hidden - 1) % n_hidden
    return pl.pallas_call(
        functools.partial(_ffn_kernel, final_norm=final_norm, n_hidden=n_hidden, last_cols=f - (n_hidden - 1) * tf),
        out_shape=jax.ShapeDtypeStruct((t, d), F32),
        grid=(t // tm, n_hidden),
        in_specs=[pl.BlockSpec((tm, d), lambda i, j: (i, 0)),
                  mod_spec, mod_spec, mod_spec, vec_spec,
                  pl.BlockSpec((d, tf), lambda i, j: (0, hidden(j))),
                  pl.BlockSpec((d, tf), lambda i, j: (0, hidden(j))),
                  pl.BlockSpec((tf, d), lambda i, j: (hidden(j), 0)),
                  vec_spec],
        out_specs=pl.BlockSpec((tm, d), lambda i, j: (i, 0)),
        scratch_shapes=[pltpu.VMEM((tm, d), BF16)],
        compiler_params=_params("parallel", "arbitrary"),
        name="swiglu_ffn",
    )(x, shift, scale, gate, gain.reshape(1, d), w1, w3, w2, fin.reshape(1, d))


def _proj_kernel(x_ref, sh_ref, sc_ref, g_ref, w_ref, cw_ref, cb_ref, o_ref, h_scr, *, conv_from):
    j = pl.program_id(1)

    def segment():
        return _dot(h_scr[...], w_ref[...])

    @pl.when(j == 0)
    def _():
        h_scr[...] = _norm_mod(x_ref[...], g_ref[...], sh_ref[0], sc_ref[0])
        o_ref[...] = segment()

    @pl.when((j > 0) & (j < conv_from))
    def _():
        o_ref[...] = segment()

    @pl.when(j >= conv_from)
    def _():
        p = segment()
        tm, w = p.shape
        rows3 = lambda t: t.reshape(tm // GRID_W, GRID_W, w)
        y = (rows3(pltpu.roll(p, 1, axis=0)) * cw_ref[0] + rows3(p) * cw_ref[1]
             + rows3(pltpu.roll(p, tm - 1, axis=0)) * cw_ref[2] + cb_ref[...])
        o_ref[...] = y.reshape(tm, w)


def _proj(x, shift, scale, gain, w, conv_w, conv_b, *, n_seg, seg_w, conv_from, tokens_per_mod, tm):
    t, d = x.shape
    tm = min(tm, tokens_per_mod)
    mod_spec = pl.BlockSpec((1, 1, d), lambda i, j: ((i * tm) // tokens_per_mod, 0, 0))
    conv_idx = lambda i, j: (0, jnp.maximum(j - conv_from, 0))
    pos = jnp.arange(GRID_W)[None, :, None]
    edge = jnp.stack([pos > 0, pos >= 0, pos < GRID_W - 1], axis=0)[:, 0]
    conv_w = jnp.where(edge, conv_w[:, None, :], 0.0)
    return pl.pallas_call(
        functools.partial(_proj_kernel, conv_from=conv_from),
        out_shape=jax.ShapeDtypeStruct((t, n_seg * seg_w), F32),
        grid=(t // tm, n_seg),
        in_specs=[pl.BlockSpec((tm, d), lambda i, j: (i, 0)),
                  mod_spec, mod_spec,
                  pl.BlockSpec((1, d), lambda i, j: (0, 0)),
                  pl.BlockSpec((d, seg_w), lambda i, j: (0, j)),
                  pl.BlockSpec((3, GRID_W, seg_w), lambda i, j: (0, 0, jnp.maximum(j - conv_from, 0))),
                  pl.BlockSpec((1, seg_w), conv_idx)],
        out_specs=pl.BlockSpec((tm, seg_w), lambda i, j: (i, j)),
        scratch_shapes=[pltpu.VMEM((tm, d), BF16)],
        compiler_params=_params("parallel", "arbitrary"),
        name="in_proj",
    )(x, shift, scale, gain.reshape(1, d), w, conv_w, conv_b.reshape(1, -1))


def _split3(g):
    g1 = g.astype(BF16)
    r1 = g - g1.astype(F32)
    g2 = r1.astype(BF16)
    return g1, g2, (r1 - g2.astype(F32)).astype(BF16)


def _split_dot(tri, g):
    g1, g2, g3 = _split3(g)
    return _dot(tri, g1) + _dot(tri, g2) + _dot(tri, g3)


def _gates(a, lb):
    f = lb + (1.0 - lb) * jax.nn.sigmoid(a)
    return 1.0 - f, jnp.log(f)


def _tn_dot(a, b):
    return lax.dot_general(a, b, (((0,), (0,)), ((), ())), preferred_element_type=F32)


def _nt_dot(a, b):
    return lax.dot_general(a, b, (((1,), (1,)), ((), ())), preferred_element_type=F32)


def _hgrn_kernel(af_ref, ab_ref, v_ref, q_ref, gate_ref, caf_ref, cab_ref, cv_ref, lbl_ref, gain_ref,
                 o_ref, of_scr, ob_scr, sf_scr, sb_scr, qin_scr, qt_scr, kt_scr, kd_scr, dec_scr,
                 *, n_chunks, n_ctx_chunks):
    c = SCAN_CHUNK
    mid = c // 2
    rows_p = PREP_CHUNKS * c
    prow = lax.broadcasted_iota(jnp.int32, (rows_p, rows_p), 0)
    pcol = lax.broadcasted_iota(jnp.int32, (rows_p, rows_p), 1)
    same = (prow // c) == (pcol // c)
    tri_fwd_p = jnp.where(same & (prow >= pcol), 1.0, 0.0).astype(BF16)
    tri_bwd_p = jnp.where(same & (pcol >= prow), 1.0, 0.0).astype(BF16)

    def lower_bound(direction):
        lg = lbl_ref[direction]
        ex = jnp.exp(lg - jnp.max(lg, axis=0, keepdims=True))
        return ex[0:1, :] / jnp.sum(ex, axis=0, keepdims=True)

    lb_f = lower_bound(0)
    lb_b = lower_bound(1)
    q_scale = HEAD_DIM ** -0.5

    n_blocks = n_chunks // PREP_CHUNKS
    n_steps = n_blocks // SCAN_BLOCKS

    def step_blocks(it):
        fwd = [(0, it * SCAN_BLOCKS + u) for u in range(SCAN_BLOCKS)]
        return fwd + [(1, n_blocks - 1 - bi) for _, bi in fwd]

    def prepare(it, s):
        it = jnp.minimum(it, n_steps - 1)
        blocks = step_blocks(it)
        los = [pl.multiple_of(bi * rows_p, rows_p) for _, bi in blocks]
        gates = [_gates((af_ref, ab_ref)[d][pl.ds(lo, rows_p), :], (lb_f, lb_b)[d]) for (d, _), lo in zip(blocks, los)]
        sums = {}
        for d, tri_p in ((0, tri_fwd_p), (1, tri_bwd_p)):
            slots_d = [slot for slot, (dd, _) in enumerate(blocks) if dd == d]
            wide = _dot(tri_p, jnp.concatenate([p for slot in slots_d for p in _split3(gates[slot][1])], axis=1))
            for n_s, slot in enumerate(slots_d):
                parts = [wide[:, (3 * n_s + m) * HEAD_DIM:(3 * n_s + m + 1) * HEAD_DIM] for m in range(3)]
                sums[slot] = parts[0] + parts[1] + parts[2]
        for slot, ((d, _), lo) in enumerate(zip(blocks, los)):
            end_row = (c - 1, 0)[d]
            k = gates[slot][0]
            b = sums[slot].reshape(PREP_CHUNKS, c, HEAD_DIM)
            b_mid = b[:, mid:mid + 1, :]
            b_end = b[:, end_row:end_row + 1, :]
            q_t = (q_ref[pl.ds(lo, rows_p), :] * q_scale).reshape(PREP_CHUNKS, c, HEAD_DIM) * jnp.exp(b - b_mid)
            k_t = k.reshape(PREP_CHUNKS, c, HEAD_DIM) * jnp.exp(b_mid - b)
            flat = lambda t: t.reshape(rows_p, HEAD_DIM).astype(BF16)
            qt_scr[s, slot] = flat(q_t)
            kt_scr[s, slot] = flat(k_t)
            qin_scr[s, slot] = flat(q_t * jnp.exp(b_mid))
            kd_scr[s, slot] = flat(k_t * jnp.exp(b_end - b_mid))
            dec_scr[s, slot] = jnp.broadcast_to(jnp.exp(b_end), (PREP_CHUNKS, 8, HEAD_DIM))

    zeros_blk = jnp.zeros((c, HEAD_DIM), BF16)

    def block_diag(t):
        return jnp.concatenate(
            [jnp.concatenate([t[j * c:(j + 1) * c] if m == j else zeros_blk for m in range(PREP_CHUNKS)], axis=1)
             for j in range(PREP_CHUNKS)], axis=0)

    masks = (same & (prow >= pcol), same & (pcol >= prow))
    st_refs = (sf_scr, sb_scr)
    out_refs = (of_scr, ob_scr)
    orders = (tuple(range(PREP_CHUNKS)), tuple(range(PREP_CHUNKS - 1, -1, -1)))

    gain = gain_ref[...]

    def scan(it, s, finish):
        streams = step_blocks(it)
        los = [pl.multiple_of(bi * rows_p, rows_p) for _, bi in streams]
        vbs = [v_ref[pl.ds(lo, rows_p), :].astype(BF16) for lo in los]
        scores = [_nt_dot(qt_scr[s, slot], kt_scr[s, slot]) for slot in range(len(streams))]
        upds = [_tn_dot(vb, block_diag(kd_scr[s, slot])) for slot, vb in enumerate(vbs)]
        outs = [_dot(jnp.where(masks[d], sc, 0.0).astype(BF16), vb) for (d, _), sc, vb in zip(streams, scores, vbs)]
        seen = [[None] * PREP_CHUNKS for _ in streams]
        for d in (0, 1):
            st = st_refs[d][...]
            for slot, (ds_, _) in enumerate(streams):
                if ds_ != d:
                    continue
                for j in orders[d]:
                    seen[slot][j] = st.astype(BF16)
                    st = st * dec_scr[s, slot, j][0:1, :] + upds[slot][:, j * HEAD_DIM:(j + 1) * HEAD_DIM]
            st_refs[d][...] = st
        for slot, ((d, _), lo) in enumerate(zip(streams, los)):
            q_in = qin_scr[s, slot]
            carried = jnp.concatenate([_nt_dot(q_in[j * c:(j + 1) * c], seen[slot][j]) for j in range(PREP_CHUNKS)],
                                      axis=0)
            rows = pl.ds(lo, rows_p)
            if finish:
                o = outs[slot] + carried + out_refs[1 - d][rows, :]
                o = o * lax.rsqrt(jnp.mean(o * o, axis=-1, keepdims=True) + EPS) * gain
                o_ref[rows, :] = (o * _silu(gate_ref[rows, :])).astype(o_ref.dtype)
            else:
                out_refs[d][rows, :] = outs[slot] + carried

    n_ctx_blocks = n_ctx_chunks // PREP_CHUNKS
    ctx_state = [jnp.zeros((HEAD_DIM, HEAD_DIM), F32), jnp.zeros((HEAD_DIM, HEAD_DIM), F32)]
    for step in range(n_ctx_blocks):
        upds, decs = [], []
        for d, bi in ((0, step), (1, n_ctx_blocks - 1 - step)):
            a_ref, lb, tri_p, end_row = ((caf_ref, lb_f, tri_fwd_p, c - 1), (cab_ref, lb_b, tri_bwd_p, 0))[d]
            rows_c = pl.ds(bi * rows_p, rows_p)
            k, g = _gates(a_ref[rows_c, :], lb)
            b = _split_dot(tri_p, g).reshape(PREP_CHUNKS, c, HEAD_DIM)
            b_end = b[:, end_row:end_row + 1, :]
            k_d = (k.reshape(PREP_CHUNKS, c, HEAD_DIM) * jnp.exp(b_end - b)).reshape(rows_p, HEAD_DIM).astype(BF16)
            upds.append(_tn_dot(cv_ref[rows_c, :].astype(BF16), block_diag(k_d)))
            decs.append(jnp.exp(b_end))
        for d in (0, 1):
            for j in orders[d]:
                ctx_state[d] = ctx_state[d] * decs[d][j] + upds[d][:, j * HEAD_DIM:(j + 1) * HEAD_DIM]
    sf_scr[...] = ctx_state[0]
    sb_scr[...] = ctx_state[1]

    prepare(0, 0)

    def body(finish):
        def pair(i, carry):
            scan(2 * i, 0, finish)
            prepare(2 * i + 1, 1)
            scan(2 * i + 1, 1, finish)
            prepare(2 * i + 2, 0)
            return carry
        return pair

    lax.fori_loop(0, n_steps // 4, body(False), 0)
    lax.fori_loop(n_steps // 4, n_steps // 2, body(True), 0)


def _hgrn(px, pc, lb_logits, gain, *, width):
    bsz, n, _ = px.shape
    n_ctx = pc.shape[1]
    heads = width // HEAD_DIM
    hd = HEAD_DIM

    def seg(s):
        return pl.BlockSpec((None, n, hd), lambda b, h: (b, 0, s * heads + h))

    def cseg(s):
        return pl.BlockSpec((None, n_ctx, hd), lambda b, h: (b, 0, s * heads + h))

    depth1 = lb_logits.shape[1]
    slots = 2 * SCAN_BLOCKS
    assert (n // SCAN_CHUNK) % (4 * SCAN_BLOCKS * PREP_CHUNKS) == 0, "scan steps come in pairs, per half"
    assert (n_ctx // SCAN_CHUNK) % PREP_CHUNKS == 0, "the context prefix is scanned in whole blocks"
    return pl.pallas_call(
        functools.partial(_hgrn_kernel, n_chunks=n // SCAN_CHUNK, n_ctx_chunks=n_ctx // SCAN_CHUNK),
        out_shape=jax.ShapeDtypeStruct((bsz, n, width), BF16),
        grid=(bsz, heads),
        in_specs=[seg(0), seg(1), seg(2), seg(3), seg(4), cseg(0), cseg(1), cseg(2),
                  pl.BlockSpec((2, depth1, hd), lambda b, h: (0, 0, h)),
                  pl.BlockSpec((1, hd), lambda b, h: (0, h))],
        out_specs=pl.BlockSpec((None, n, hd), lambda b, h: (b, 0, h)),
        scratch_shapes=[pltpu.VMEM((n, hd), F32), pltpu.VMEM((n, hd), F32),
                        pltpu.VMEM((hd, hd), F32), pltpu.VMEM((hd, hd), F32),
                        *[pltpu.VMEM((2, slots, PREP_CHUNKS * SCAN_CHUNK, hd), BF16) for _ in range(4)],
                        pltpu.VMEM((2, slots, PREP_CHUNKS, 8, hd), F32)],
        compiler_params=_params("parallel", "parallel"),
        name="hgrn_scan",
    )(px, px, px, px, px, pc, pc, pc, lb_logits, gain.reshape(1, width))


def _dft_constants(n):
    n2 = GRID_W
    n1h = n // n2
    m2r = np.arange(n2)[None, :]
    k2 = np.arange(n2)[:, None]
    f_ang = -2.0 * np.pi * k2 * m2r / n2
    fr, fi = np.cos(f_ang), np.sin(f_ang)
    f_fwd = np.block([[fr, -fi], [fi, fr]])
    f_inv = np.block([[fr, fi], [-fi, fr]])
    k1 = np.arange(n1h)[:, None, None]
    m1 = np.arange(n1h)[None, :, None]
    m2 = np.arange(n2)[None, None, :]
    theta = -2.0 * np.pi * (k1 + 0.5) * (m1 / (2 * n1h) + m2 / (2 * n))
    parts = np.stack([np.cos(theta), np.sin(theta)], axis=0)
    parts = parts.reshape(2, n1h, n1h, n2 // OUTER_ROWS, OUTER_ROWS)
    eye = np.eye(OUTER_ROWS)
    rows = n1h * OUTER_ROWS
    fwd = np.einsum("pknsj,jm->spkjnm", parts, eye).reshape(n2 // OUTER_ROWS, 2 * rows, rows)
    inv = np.einsum("pknsj,jm->snjpkm", parts, eye).reshape(n2 // OUTER_ROWS, rows, 2 * rows)
    return fwd, inv, f_fwd, f_inv


def _filt_mlp_kernel(z_ref, w1_ref, b1_ref, w2_ref, b2_ref, w3_ref, b3_ref, fr_ref, o_ref):
    fr = fr_ref[...]
    hp = lambda a, b: jnp.dot(a, b, precision=HIGHEST, preferred_element_type=F32)
    hid = jnp.sin(fr * (hp(z_ref[...], w1_ref[...]) + b1_ref[...]))
    hid = jnp.sin(fr * (hp(hid, w2_ref[...]) + b2_ref[...]))
    o_ref[...] = jnp.sin(fr * (hp(hid, w3_ref[...]) + b3_ref[...]))


def _filt_taps_kernel(hid_ref, wo_ref, dl_ref, h_ref, s_ref, *, n):
    hid = hid_ref[...]
    hid_hi = hid.astype(BF16)
    h = _dot3(hid_hi, (hid - hid_hi.astype(F32)).astype(BF16), wo_ref[...])
    t = lax.broadcasted_iota(jnp.int32, h.shape, 0).astype(F32) * (1.0 / max(n - 1, 1))
    hw = h * jnp.exp(-t * dl_ref[...])
    h_ref[...] = hw
    s_ref[0:1, :] = jnp.sum(jnp.abs(hw), axis=0, keepdims=True)
    s_ref[1:2, :] = jnp.abs(hw[0:1, :])


def _spectrum_kernel(af_ref, ab_ref, f_ref, fl_ref, s_ref, d_ref, o_ref, *, n):
    n2 = GRID_W
    l1 = s_ref[0:1, :] + s_ref[2:3, :] - s_ref[3:4, :]
    hb0 = s_ref[4:5, :]
    norm = 1.0 / l1
    scale = 2.0 / (2 * n)
    for i in range(af_ref.shape[1]):
        def inner(a_ref):
            z = _dot3(f_ref[...], fl_ref[...], jnp.concatenate([a_ref[0, i], a_ref[1, i]], axis=0))
            return z[:n2], z[n2:]

        zfr, zfi = inner(af_ref)
        zbr, zbi = inner(ab_ref)
        o_ref[0, i] = scale * ((zfr + zbr - hb0) * norm + d_ref[...])
        o_ref[1, i] = scale * ((zfi - zbi) * norm)


def _dot3(a_hi, a_lo, x):
    x_hi = x.astype(BF16)
    x_lo = (x - x_hi.astype(F32)).astype(BF16)
    return _dot(a_hi, x_hi) + _dot(a_hi, x_lo) + _dot(a_lo, x_hi)


def _hi_lo(a):
    hi = np.asarray(a, np.float64).astype(BF16)
    lo = (a - hi.astype(np.float64)).astype(BF16)
    return jnp.asarray(hi), jnp.asarray(lo)


def _slabs(x):
    return [x[..., h * OUTER_ROWS:(h + 1) * OUTER_ROWS, :] for h in range(x.shape[-2] // OUTER_ROWS)]


def _outer_dft_kernel(*refs, high):
    g_refs, x_ref, o_ref = refs[:-2], refs[-2], refs[-1]
    n1h, _, tl = x_ref.shape
    parts = []
    for h, xs in enumerate(_slabs(x_ref[...])):
        xs = xs.reshape(n1h * OUTER_ROWS, tl)
        r = _dot3(g_refs[0][h], g_refs[1][h], xs) if high else _dot(g_refs[0][h], xs.astype(BF16))
        parts.append(r.reshape(2, n1h, OUTER_ROWS, tl))
    o_ref[...] = jnp.concatenate(parts, axis=2).astype(o_ref.dtype)


def _outer_dft(gs, x4, seg, *, c, tl):
    high = len(gs) == 2
    s, n1h, n2, _ = x4.shape
    tl = min(tl, c)
    per = c // tl
    rows = OUTER_ROWS if high else BF16_ROWS
    g_spec = pl.BlockSpec((rows // OUTER_ROWS,) + gs[0].shape[1:], lambda j, b, l: (j, 0, 0))
    return pl.pallas_call(
        functools.partial(_outer_dft_kernel, high=high),
        out_shape=jax.ShapeDtypeStruct((s, 2, n1h, n2, c), F32 if high else BF16),
        grid=(n2 // rows, s, per),
        in_specs=[g_spec] * len(gs) + [pl.BlockSpec((None, n1h, rows, tl),
                                                    lambda j, b, l: (b, 0, j, seg * per + l))],
        out_specs=pl.BlockSpec((None, 2, n1h, rows, tl), lambda j, b, l: (b, 0, 0, j, l)),
        compiler_params=_params("parallel", "parallel", "parallel"),
        name="outer_dft",
    )(*gs, x4)


def _hyena_spectra(n, width, filt, hy_d, consts):
    w1, b1, w2, b2, w3, b3, freq, w_out = filt
    gt_fwd, _, f_fwd, _ = consts
    n1h = n // GRID_W
    order = w1.shape[1]
    pos = np.arange(n, dtype=np.float64)
    t = pos / max(n - 1, 1)
    bands = (FILT_EMB - 1) // 2
    fb = np.linspace(1e-4, bands - 1, bands)
    ang = (2 * math.pi / n) * pos[:, None] * fb[None, :]
    z = np.concatenate([t[:, None], np.cos(ang), -np.sin(ang)], axis=-1)
    emb_pad = 128
    z = jnp.asarray(np.pad(z, ((0, 0), (0, emb_pad - FILT_EMB))), F32)
    w1p = jnp.pad(w1, ((0, emb_pad - FILT_EMB), (0, 0)))
    row = lambda a: a.reshape(1, -1)
    full = lambda shape: pl.BlockSpec(shape, lambda: tuple(0 for _ in shape))
    hid = pl.pallas_call(
        _filt_mlp_kernel,
        out_shape=jax.ShapeDtypeStruct((n, order), F32),
        in_specs=[full((n, emb_pad)), full((emb_pad, order)), full((1, order)), full((order, order)),
                  full((1, order)), full((order, order)), full((1, order)), full((1, order))],
        out_specs=full((n, order)),
        compiler_params=pltpu.CompilerParams(vmem_limit_bytes=VMEM_LIMIT_BYTES),
        name="filter_mlp",
    )(z, w1p, row(b1), w2, row(b2), w3, row(b3), row(freq))

    cols = 4 * width
    deltas = np.abs(np.linspace(math.log(DECAY_TARGET) / DECAY_SLOW_PCT,
                                math.log(DECAY_TARGET) / DECAY_FAST_PCT, width))
    deltas4 = jnp.asarray(np.tile(deltas, 4)[None, :], F32)
    tc = min(512, cols)
    taps, sums = pl.pallas_call(
        functools.partial(_filt_taps_kernel, n=n),
        out_shape=(jax.ShapeDtypeStruct((n, cols), F32), jax.ShapeDtypeStruct((2, cols), F32)),
        grid=(cols // tc,),
        in_specs=[pl.BlockSpec((n, order), lambda j: (0, 0)),
                  pl.BlockSpec((order, tc), lambda j: (0, j)),
                  pl.BlockSpec((1, tc), lambda j: (0, j))],
        out_specs=(pl.BlockSpec((n, tc), lambda j: (0, j)), pl.BlockSpec((2, tc), lambda j: (0, j))),
        compiler_params=_params("parallel"),
        name="filter_taps",
    )(hid, w_out, deltas4)

    a = _outer_dft(_hi_lo(gt_fwd), taps.reshape(1, n1h, GRID_W, cols), 0, c=cols, tl=1024)[0]
    s4 = sums.reshape(2, 2, 2, width)
    bwd0 = taps[0].reshape(2, 2, width)[:, 1]
    stats = jnp.stack([s4[0, :, 0], s4[1, :, 0], s4[0, :, 1], s4[1, :, 1], bwd0], axis=1)

    kb = min(SPECTRUM_BINS, n1h)

    def tap_spec(side):
        return pl.BlockSpec((2, kb, GRID_W, width), lambda k1, f: (0, k1, 0, 2 * f + side))

    f_hi, f_lo = _hi_lo(f_fwd)
    f_spec = pl.BlockSpec((2 * GRID_W, 2 * GRID_W), lambda k1, f: (0, 0))
    return pl.pallas_call(
        functools.partial(_spectrum_kernel, n=n),
        out_shape=jax.ShapeDtypeStruct((2, 2, n1h, GRID_W, width), F32),
        grid=(n1h // kb, 2),
        in_specs=[tap_spec(0), tap_spec(1), f_spec, f_spec,
                  pl.BlockSpec((None, 5, width), lambda k1, f: (f, 0, 0)),
                  pl.BlockSpec((None, 1, width), lambda k1, f: (f, 0, 0))],
        out_specs=pl.BlockSpec((None, 2, kb, GRID_W, width), lambda k1, f: (f, 0, k1, 0, 0)),
        compiler_params=_params("parallel", "parallel"),
        name="filter_spectrum",
    )(a, a, f_hi, f_lo, stats, hy_d.reshape(2, 1, width))


def _inner_conv_kernel(a_ref, f_ref, fi_ref, h_ref, o_ref):
    n2 = GRID_W
    for i in range(a_ref.shape[1]):
        z = _dot(f_ref[...], jnp.concatenate([a_ref[0, i], a_ref[1, i]], axis=0))
        zr, zi = z[:n2], z[n2:]
        hr, hi = h_ref[0, i], h_ref[1, i]
        stacked = jnp.concatenate([zr * hr - zi * hi, zr * hi + zi * hr], axis=0).astype(BF16)
        w = _dot(fi_ref[...], stacked).astype(BF16)
        o_ref[0, i] = w[:n2]
        o_ref[1, i] = w[n2:]


def _inner_conv(a, f_fwd, f_inv, spec, filt):
    bsz, _, n1h, n2, c = a.shape
    kb = min(INNER_BINS, n1h)
    return pl.pallas_call(
        _inner_conv_kernel,
        out_shape=jax.ShapeDtypeStruct(a.shape, BF16),
        grid=(n1h // kb, bsz),
        in_specs=[pl.BlockSpec((None, 2, kb, n2, c), lambda k1, b: (b, 0, k1, 0, 0)),
                  pl.BlockSpec((2 * n2, 2 * n2), lambda k1, b: (0, 0)),
                  pl.BlockSpec((2 * n2, 2 * n2), lambda k1, b: (0, 0)),
                  pl.BlockSpec((None, 2, kb, n2, c), lambda k1, b: (filt, 0, k1, 0, 0))],
        out_specs=pl.BlockSpec((None, 2, kb, n2, c), lambda k1, b: (b, 0, k1, 0, 0)),
        compiler_params=_params("parallel", "arbitrary"),
        name="inner_conv",
    )(a, f_fwd, f_inv, spec)


def _outer_idft_gate_kernel(gi_ref, g_ref, b_ref, x_ref, o_ref, *, again):
    n1h, _, tl = x_ref.shape
    parts = []
    for h, (bs, xs) in enumerate(zip(_slabs(b_ref[...].astype(F32)), _slabs(x_ref[...]))):
        stacked = bs.reshape(2 * n1h * OUTER_ROWS, tl).astype(BF16)
        y = _dot(gi_ref[h], stacked) * xs.reshape(n1h * OUTER_ROWS, tl)
        if again:
            parts.append(_dot(g_ref[h], y.astype(BF16)).reshape(2, n1h, OUTER_ROWS, tl))
        else:
            parts.append(y.reshape(n1h, OUTER_ROWS, tl))
    o_ref[...] = jnp.concatenate(parts, axis=-2).astype(o_ref.dtype)


def _outer_idft_gate(g_inv, g_fwd, b, px4, seg, *, again, tl):
    bsz, _, n1h, n2, c = b.shape
    tl = min(tl, c)
    per = c // tl
    rows = BF16_ROWS
    if again:
        out_shape = jax.ShapeDtypeStruct((bsz, 2, n1h, n2, c), BF16)
        out_spec = pl.BlockSpec((None, 2, n1h, rows, tl), lambda j, bb, l: (bb, 0, 0, j, l))
    else:
        out_shape = jax.ShapeDtypeStruct((bsz, n1h, n2, c), BF16)
        out_spec = pl.BlockSpec((None, n1h, rows, tl), lambda j, bb, l: (bb, 0, j, l))
    return pl.pallas_call(
        functools.partial(_outer_idft_gate_kernel, again=again),
        out_shape=out_shape,
        grid=(n2 // rows, bsz, per),
        in_specs=[pl.BlockSpec((rows // OUTER_ROWS,) + g_inv.shape[1:], lambda j, bb, l: (j, 0, 0)),
                  pl.BlockSpec((rows // OUTER_ROWS,) + g_fwd.shape[1:], lambda j, bb, l: (j, 0, 0)),
                  pl.BlockSpec((None, 2, n1h, rows, tl), lambda j, bb, l: (bb, 0, 0, j, l)),
                  pl.BlockSpec((None, n1h, rows, tl), lambda j, bb, l: (bb, 0, j, seg * per + l))],
        out_specs=out_spec,
        compiler_params=_params("parallel", "parallel", "parallel"),
        name="outer_idft_gate",
    )(g_inv, g_fwd, b, px4)


def _hyena_mix(px, spec, consts, *, bsz, n, width):
    g_fwd, g_inv, f_fwd, f_inv = consts
    g_fwd_b, g_inv_b = jnp.asarray(g_fwd, BF16), jnp.asarray(g_inv, BF16)
    f_fwd_b, f_inv_b = jnp.asarray(f_fwd, BF16), jnp.asarray(f_inv, BF16)
    n1h = n // GRID_W
    px4 = px.reshape(bsz, n1h, GRID_W, -1)
    tl = width
    a = _outer_dft((g_fwd_b,), px4, 5, c=width, tl=tl)
    a = _inner_conv(a, f_fwd_b, f_inv_b, spec, 0)
    a = _outer_idft_gate(g_inv_b, g_fwd_b, a, px4, 6, again=True, tl=tl)
    a = _inner_conv(a, f_fwd_b, f_inv_b, spec, 1)
    hy = _outer_idft_gate(g_inv_b, g_fwd_b, a, px4, 7, again=False, tl=tl)
    return hy.reshape(bsz * n, width)


def _out_proj_kernel(x_ref, hg_ref, hy_ref, wa_ref, wb_ref, gt_ref, o_ref):
    mix = _dot(hg_ref[...], wa_ref[...]) + _dot(hy_ref[...], wb_ref[...])
    o_ref[...] = x_ref[...] + gt_ref[0] * mix


def _out_proj(x, hg, hy, w, gate, *, tokens_per_mod, tm):
    t, d = x.shape
    wdt = hg.shape[1]
    tm = min(tm, tokens_per_mod)
    return pl.pallas_call(
        _out_proj_kernel,
        out_shape=jax.ShapeDtypeStruct((t, d), F32),
        grid=(t // tm,),
        in_specs=[pl.BlockSpec((tm, d), lambda i: (i, 0)),
                  pl.BlockSpec((tm, wdt), lambda i: (i, 0)),
                  pl.BlockSpec((tm, wdt), lambda i: (i, 0)),
                  pl.BlockSpec((wdt, d), lambda i: (0, 0)),
                  pl.BlockSpec((wdt, d), lambda i: (1, 0)),
                  pl.BlockSpec((1, 1, d), lambda i: ((i * tm) // tokens_per_mod, 0, 0))],
        out_specs=pl.BlockSpec((tm, d), lambda i: (i, 0)),
        compiler_params=_params("parallel"),
        name="out_proj",
    )(x, hg, hy, w, w, gate)


def _tiles(d_ff):
    assert d_ff % 128 == 0, "a ragged last hidden tile still has to be lane-tile aligned"
    return TOKEN_TILE, 2 * TOKEN_TILE, min(HIDDEN_TILE, d_ff)


def kernel(x, c, ctx, c_ctx, ada_w, ada_b, norm_ffn1, ffn1_w1, ffn1_w3, ffn1_w2, norm_mix, w_in, hg_lb_logits, hg_norm, hy_conv_w, hy_conv_b, filt_w1, filt_b1, filt_w2, filt_b2, filt_w3, filt_b3, filt_freq, filt_w_out, hy_d, w_out, norm_ffn2, ffn2_w1, ffn2_w3, ffn2_w2, final_norm):
    bsz, n, d = x.shape
    n_ctx = ctx.shape[1]
    depth = ada_w.shape[0]
    assert depth == 1, "single-layer block"
    width = d // 2
    n_seg = w_in.shape[2] // width
    assert n_seg == 8 and n % GRID_W == 0 and n_ctx % SCAN_CHUNK == 0 and width % HEAD_DIM == 0
    tm, tm_wide, tf = _tiles(ffn1_w1.shape[2])
    l = 0

    rows = -(-(bsz + 1) // 8) * 8
    cs = jnp.concatenate([c, c_ctx[None, :], jnp.zeros((rows - bsz - 1, d), F32)], axis=0)
    mods = _ada(cs, ada_w[l], ada_b[l]).reshape(rows, N_MOD, d)
    mx = [mods[:bsz, i][:, None, :] for i in range(N_MOD)]
    mc = [mods[bsz:bsz + 1, i][:, None, :] for i in range(N_MOD)]

    bf = _to_bf16
    w1a, w3a, w2a = bf(ffn1_w1[l]), bf(ffn1_w3[l]), bf(ffn1_w2[l])
    xt = x.reshape(bsz * n, d)
    yt = ctx.reshape(bsz * n_ctx, d)

    xt = _ffn(xt, mx[0], mx[1], mx[2], norm_ffn1[l], w1a, w3a, w2a, final_norm,
              tokens_per_mod=n, final_norm=False, tm=tm, tf=tf)
    yt = _ffn(yt, mc[0], mc[1], mc[2], norm_ffn1[l], w1a, w3a, w2a, final_norm,
              tokens_per_mod=bsz * n_ctx, final_norm=False, tm=tm, tf=tf)

    w_in_b = bf(w_in[l])
    pc = _proj(yt, mc[3], mc[4], norm_mix[l], w_in_b, hy_conv_w[l], hy_conv_b[l], n_seg=3, seg_w=width,
               conv_from=n_seg, tokens_per_mod=bsz * n_ctx, tm=tm)
    px = _proj(xt, mx[3], mx[4], norm_mix[l], w_in_b, hy_conv_w[l], hy_conv_b[l], n_seg=n_seg, seg_w=width,
               conv_from=5, tokens_per_mod=n, tm=tm_wide)
    cols = n_seg * width

    hg = _hgrn(px.reshape(bsz, n, cols), pc.reshape(bsz, n_ctx, 3 * width), hg_lb_logits, hg_norm[l], width=width)

    consts = _dft_constants(n)
    filt = (filt_w1[l], filt_b1[l], filt_w2[l], filt_b2[l], filt_w3[l], filt_b3[l], filt_freq[l], filt_w_out[l])
    spec = _hyena_spectra(n, width, filt, hy_d[l], consts)
    hy = _hyena_mix(px, spec, consts, bsz=bsz, n=n, width=width)

    xt = _out_proj(xt, hg.reshape(bsz * n, width), hy, bf(w_out[l]), mx[5],
                   tokens_per_mod=n, tm=tm)

    out = _ffn(xt, mx[6], mx[7], mx[8], norm_ffn2[l], bf(ffn2_w1[l]), bf(ffn2_w3[l]), bf(ffn2_w2[l]), final_norm,
               tokens_per_mod=n, final_norm=True, tm=tm, tf=tf)
    return out.reshape(bsz, n, d)
```

```python
import functools
import math

import numpy as np
import jax
import jax.numpy as jnp
from jax import lax
from jax.experimental import pallas as pl
from jax.experimental.pallas import tpu as pltpu

F32 = jnp.float32
BF16 = jnp.bfloat16
EPS = 1e-6
N_MOD = 9
HEAD_DIM = 128
GRID_W = 64
CAST_ROWS = 256
TOKEN_TILE = 512
HIDDEN_TILE = 1024
SCAN_CHUNK = 64
PREP_CHUNKS = 4
SCAN_BLOCKS = 4
OUTER_ROWS = 8
BF16_ROWS = 16
INNER_BINS = 16
SPECTRUM_BINS = 8
FILT_EMB = 33
DECAY_TARGET = 1e-2
DECAY_FAST_PCT = 0.3
DECAY_SLOW_PCT = 1.5
VMEM_LIMIT_BYTES = 56 * 1024 * 1024
HIGHEST = lax.Precision.HIGHEST


def _params(*semantics):
    return pltpu.CompilerParams(dimension_semantics=semantics, vmem_limit_bytes=VMEM_LIMIT_BYTES)


def _dot(a, b):
    return jnp.dot(a, b, preferred_element_type=F32)


def _silu(a):
    return a * jax.nn.sigmoid(a)


def _cast_kernel(x_ref, o_ref):
    o_ref[...] = x_ref[...].astype(BF16)


def _to_bf16(w):
    r, c = w.shape
    rb = CAST_ROWS if r % CAST_ROWS == 0 else r
    return pl.pallas_call(
        _cast_kernel,
        out_shape=jax.ShapeDtypeStruct((r, c), BF16),
        grid=(r // rb,),
        in_specs=[pl.BlockSpec((rb, c), lambda i: (i, 0))],
        out_specs=pl.BlockSpec((rb, c), lambda i: (i, 0)),
        compiler_params=_params("parallel"),
        name="to_bf16",
    )(w)


def _ada_kernel(c_ref, w_ref, b_ref, o_ref):
    h = _silu(c_ref[...]).astype(BF16)
    o_ref[...] = _dot(h, w_ref[...].astype(BF16)) + b_ref[...]


def _ada(cs, w, b):
    rows, d = cs.shape
    n = w.shape[1]
    tn = d // 2
    return pl.pallas_call(
        _ada_kernel,
        out_shape=jax.ShapeDtypeStruct((rows, n), F32),
        grid=(n // tn,),
        in_specs=[pl.BlockSpec((rows, d), lambda j: (0, 0)),
                  pl.BlockSpec((d, tn), lambda j: (0, j)),
                  pl.BlockSpec((1, tn), lambda j: (0, j))],
        out_specs=pl.BlockSpec((rows, tn), lambda j: (0, j)),
        compiler_params=_params("arbitrary"),
        name="ada_mod",
    )(cs, w, b.reshape(1, n))


def _norm_mod(x, gain, shift, scale):
    y = x * lax.rsqrt(jnp.mean(x * x, axis=-1, keepdims=True) + EPS)
    return (y * (gain * (1.0 + scale)) + shift).astype(BF16)


def _ffn_kernel(x_ref, sh_ref, sc_ref, gt_ref, g_ref, w1_ref, w3_ref, w2_ref, fin_ref, o_ref, h_scr,
                *, final_norm, n_hidden, last_cols):
    j = pl.program_id(1)
    tf = w1_ref.shape[1]

    def hidden_tile(cols=tf):
        h = h_scr[...]
        a = _dot(h, w1_ref[:, :cols])
        b = _dot(h, w3_ref[:, :cols])
        return _dot((_silu(a) * b).astype(BF16), w2_ref[:cols, :])

    def finish(acc):
        out = x_ref[...] + (0.5 * gt_ref[0]) * acc
        if final_norm:
            out = out * lax.rsqrt(jnp.mean(out * out, axis=-1, keepdims=True) + EPS) * fin_ref[...]
        return out

    last = n_hidden - 1

    @pl.when(j == 0)
    def _():
        h_scr[...] = _norm_mod(x_ref[...], g_ref[...], sh_ref[0], sc_ref[0])
        o_ref[...] = finish(hidden_tile(last_cols)) if last == 0 else hidden_tile(last_cols)

    @pl.when((j > 0) & (j < last))
    def _():
        o_ref[...] += hidden_tile()

    if last > 0:
        @pl.when(j == last)
        def _():
            o_ref[...] = finish(o_ref[...] + hidden_tile())


def _ffn(x, shift, scale, gate, gain, w1, w3, w2, fin, *, tokens_per_mod, final_norm, tm, tf):
    t, d = x.shape
    f = w1.shape[1]
    tm = min(tm, tokens_per_mod)
    mod_spec = pl.BlockSpec((1, 1, d), lambda i, j: ((i * tm) // tokens_per_mod, 0, 0))
    vec_spec = pl.BlockSpec((1, d), lambda i, j: (0, 0))
    n_hidden = pl.cdiv(f, tf)
    hidden = lambda j: (j + n_hidden - 1) % n_hidden
    return pl.pallas_call(
        functools.partial(_ffn_kernel, final_norm=final_norm, n_hidden=n_hidden, last_cols=f - (n_hidden - 1) * tf),
        out_shape=jax.ShapeDtypeStruct((t, d), F32),
        grid=(t // tm, n_hidden),
        in_specs=[pl.BlockSpec((tm, d), lambda i, j: (i, 0)),
                  mod_spec, mod_spec, mod_spec, vec_spec,
                  pl.BlockSpec((d, tf), lambda i, j: (0, hidden(j))),
                  pl.BlockSpec((d, tf), lambda i, j: (0, hidden(j))),
                  pl.BlockSpec((tf, d), lambda i, j: (hidden(j), 0)),
                  vec_spec],
        out_specs=pl.BlockSpec((tm, d), lambda i, j: (i, 0)),
        scratch_shapes=[pltpu.VMEM((tm, d), BF16)],
        compiler_params=_params("parallel", "arbitrary"),
        name="swiglu_ffn",
    )(x, shift, scale, gate, gain.reshape(1, d), w1, w3, w2, fin.reshape(1, d))


def _proj_kernel(x_ref, sh_ref, sc_ref, g_ref, w_ref, cw_ref, cb_ref, *rest, conv_from):
    (ohg_ref, ohy_ref, h_scr) = rest if len(rest) == 3 else (rest[0], None, rest[1])
    j = pl.program_id(1)

    def segment():
        return _dot(h_scr[...], w_ref[...])

    def store_heads(p):
        for h in range(ohg_ref.shape[0]):
            ohg_ref[h] = p[:, h * HEAD_DIM:(h + 1) * HEAD_DIM]

    @pl.when(j == 0)
    def _():
        h_scr[...] = _norm_mod(x_ref[...], g_ref[...], sh_ref[0], sc_ref[0])
        store_heads(segment())

    @pl.when((j > 0) & (j < conv_from))
    def _():
        store_heads(segment())

    if ohy_ref is not None:
        @pl.when(j >= conv_from)
        def _():
            p = segment()
            tm, w = p.shape
            rows3 = lambda t: t.reshape(tm // GRID_W, GRID_W, w)
            y = (rows3(pltpu.roll(p, 1, axis=0)) * cw_ref[0] + rows3(p) * cw_ref[1]
                 + rows3(pltpu.roll(p, tm - 1, axis=0)) * cw_ref[2] + cb_ref[...])
            ohy_ref[...] = y.reshape(tm, w)


def _proj(x, shift, scale, gain, w, conv_w, conv_b, *, n_seg, seg_w, conv_from, tokens_per_mod, seq_len, tm):
    t, d = x.shape
    tm = min(tm, tokens_per_mod, seq_len)
    per_seq = seq_len // tm
    heads = seg_w // HEAD_DIM
    n_conv = n_seg - conv_from
    mod_spec = pl.BlockSpec((1, 1, d), lambda i, j: ((i * tm) // tokens_per_mod, 0, 0))
    conv_idx = lambda i, j: (0, jnp.maximum(j - conv_from, 0))
    pos = jnp.arange(GRID_W)[None, :, None]
    edge = jnp.stack([pos > 0, pos >= 0, pos < GRID_W - 1], axis=0)[:, 0]
    conv_w = jnp.where(edge, conv_w[:, None, :], 0.0)
    out_shape = [jax.ShapeDtypeStruct((t // seq_len, conv_from, heads, seq_len, HEAD_DIM), F32)]
    out_specs = [pl.BlockSpec((None, None, heads, tm, HEAD_DIM),
                              lambda i, j: (i // per_seq, jnp.minimum(j, conv_from - 1), 0, i % per_seq, 0))]
    if n_conv:
        out_shape.append(jax.ShapeDtypeStruct((t, n_conv * seg_w), F32))
        out_specs.append(pl.BlockSpec((tm, seg_w), lambda i, j: (i, jnp.maximum(j - conv_from, 0))))
    outs = pl.pallas_call(
        functools.partial(_proj_kernel, conv_from=conv_from),
        out_shape=out_shape,
        grid=(t // tm, n_seg),
        in_specs=[pl.BlockSpec((tm, d), lambda i, j: (i, 0)),
                  mod_spec, mod_spec,
                  pl.BlockSpec((1, d), lambda i, j: (0, 0)),
                  pl.BlockSpec((d, seg_w), lambda i, j: (0, j)),
                  pl.BlockSpec((3, GRID_W, seg_w), lambda i, j: (0, 0, jnp.maximum(j - conv_from, 0))),
                  pl.BlockSpec((1, seg_w), conv_idx)],
        out_specs=out_specs,
        scratch_shapes=[pltpu.VMEM((tm, d), BF16)],
        compiler_params=_params("parallel", "arbitrary"),
        name="in_proj",
    )(x, shift, scale, gain.reshape(1, d), w, conv_w, conv_b.reshape(1, -1))
    return (outs[0], outs[1]) if n_conv else (outs[0], None)


def _split3(g):
    g1 = g.astype(BF16)
    r1 = g - g1.astype(F32)
    g2 = r1.astype(BF16)
    return g1, g2, (r1 - g2.astype(F32)).astype(BF16)


def _split_dot(tri, g):
    g1, g2, g3 = _split3(g)
    return _dot(tri, g1) + _dot(tri, g2) + _dot(tri, g3)


def _gates(a, lb):
    f = lb + (1.0 - lb) * jax.nn.sigmoid(a)
    return 1.0 - f, jnp.log(f)


def _tn_dot(a, b):
    return lax.dot_general(a, b, (((0,), (0,)), ((), ())), preferred_element_type=F32)


def _nt_dot(a, b):
    return lax.dot_general(a, b, (((1,), (1,)), ((), ())), preferred_element_type=F32)


def _hgrn_kernel(af_ref, ab_ref, v_ref, q_ref, gate_ref, caf_ref, cab_ref, cv_ref, lbl_ref, gain_ref,
                 o_ref, of_scr, ob_scr, sf_scr, sb_scr, qin_scr, qt_scr, kt_scr, kd_scr, dec_scr,
                 *, n_chunks, n_ctx_chunks):
    c = SCAN_CHUNK
    mid = c // 2
    rows_p = PREP_CHUNKS * c
    prow = lax.broadcasted_iota(jnp.int32, (rows_p, rows_p), 0)
    pcol = lax.broadcasted_iota(jnp.int32, (rows_p, rows_p), 1)
    same = (prow // c) == (pcol // c)
    tri_fwd_p = jnp.where(same & (prow >= pcol), 1.0, 0.0).astype(BF16)
    tri_bwd_p = jnp.where(same & (pcol >= prow), 1.0, 0.0).astype(BF16)

    def lower_bound(direction):
        lg = lbl_ref[direction]
        ex = jnp.exp(lg - jnp.max(lg, axis=0, keepdims=True))
        return ex[0:1, :] / jnp.sum(ex, axis=0, keepdims=True)

    lb_f = lower_bound(0)
    lb_b = lower_bound(1)
    q_scale = HEAD_DIM ** -0.5

    n_blocks = n_chunks // PREP_CHUNKS
    n_steps = n_blocks // SCAN_BLOCKS

    def step_blocks(it):
        fwd = [(0, it * SCAN_BLOCKS + u) for u in range(SCAN_BLOCKS)]
        return fwd + [(1, n_blocks - 1 - bi) for _, bi in fwd]

    def prepare(it, s):
        it = jnp.minimum(it, n_steps - 1)
        blocks = step_blocks(it)
        los = [pl.multiple_of(bi * rows_p, rows_p) for _, bi in blocks]
        gates = [_gates((af_ref, ab_ref)[d][pl.ds(lo, rows_p), :], (lb_f, lb_b)[d]) for (d, _), lo in zip(blocks, los)]
        sums = {}
        for d, tri_p in ((0, tri_fwd_p), (1, tri_bwd_p)):
            slots_d = [slot for slot, (dd, _) in enumerate(blocks) if dd == d]
            wide = _dot(tri_p, jnp.concatenate([p for slot in slots_d for p in _split3(gates[slot][1])], axis=1))
            for n_s, slot in enumerate(slots_d):
                parts = [wide[:, (3 * n_s + m) * HEAD_DIM:(3 * n_s + m + 1) * HEAD_DIM] for m in range(3)]
                sums[slot] = parts[0] + parts[1] + parts[2]
        for slot, ((d, _), lo) in enumerate(zip(blocks, los)):
            end_row = (c - 1, 0)[d]
            k = gates[slot][0]
            b = sums[slot].reshape(PREP_CHUNKS, c, HEAD_DIM)
            b_mid = b[:, mid:mid + 1, :]
            b_end = b[:, end_row:end_row + 1, :]
            q_t = (q_ref[pl.ds(lo, rows_p), :] * q_scale).reshape(PREP_CHUNKS, c, HEAD_DIM) * jnp.exp(b - b_mid)
            k_t = k.reshape(PREP_CHUNKS, c, HEAD_DIM) * jnp.exp(b_mid - b)
            flat = lambda t: t.reshape(rows_p, HEAD_DIM).astype(BF16)
            qt_scr[s, slot] = flat(q_t)
            kt_scr[s, slot] = flat(k_t)
            qin_scr[s, slot] = flat(q_t * jnp.exp(b_mid))
            kd_scr[s, slot] = flat(k_t * jnp.exp(b_end - b_mid))
            dec_scr[s, slot] = jnp.broadcast_to(jnp.exp(b_end), (PREP_CHUNKS, 8, HEAD_DIM))

    zeros_blk = jnp.zeros((c, HEAD_DIM), BF16)

    def block_diag(t):
        return jnp.concatenate(
            [jnp.concatenate([t[j * c:(j + 1) * c] if m == j else zeros_blk for m in range(PREP_CHUNKS)], axis=1)
             for j in range(PREP_CHUNKS)], axis=0)

    masks = (same & (prow >= pcol), same & (pcol >= prow))
    st_refs = (sf_scr, sb_scr)
    out_refs = (of_scr, ob_scr)
    orders = (tuple(range(PREP_CHUNKS)), tuple(range(PREP_CHUNKS - 1, -1, -1)))

    gain = gain_ref[...]

    def scan(it, s, finish):
        streams = step_blocks(it)
        los = [pl.multiple_of(bi * rows_p, rows_p) for _, bi in streams]
        vbs = [v_ref[pl.ds(lo, rows_p), :].astype(BF16) for lo in los]
        scores = [_nt_dot(qt_scr[s, slot], kt_scr[s, slot]) for slot in range(len(streams))]
        upds = [_tn_dot(vb, block_diag(kd_scr[s, slot])) for slot, vb in enumerate(vbs)]
        outs = [_dot(jnp.where(masks[d], sc, 0.0).astype(BF16), vb) for (d, _), sc, vb in zip(streams, scores, vbs)]
        seen = [[None] * PREP_CHUNKS for _ in streams]
        for d in (0, 1):
            st = st_refs[d][...]
            for slot, (ds_, _) in enumerate(streams):
                if ds_ != d:
                    continue
                for j in orders[d]:
                    seen[slot][j] = st.astype(BF16)
                    st = st * dec_scr[s, slot, j][0:1, :] + upds[slot][:, j * HEAD_DIM:(j + 1) * HEAD_DIM]
            st_refs[d][...] = st
        for slot, ((d, _), lo) in enumerate(zip(streams, los)):
            q_in = qin_scr[s, slot]
            carried = jnp.concatenate([_nt_dot(q_in[j * c:(j + 1) * c], seen[slot][j]) for j in range(PREP_CHUNKS)],
                                      axis=0)
            rows = pl.ds(lo, rows_p)
            if finish:
                o = outs[slot] + carried + out_refs[1 - d][rows, :]
                o = o * lax.rsqrt(jnp.mean(o * o, axis=-1, keepdims=True) + EPS) * gain
                o_ref[rows, :] = (o * _silu(gate_ref[rows, :])).astype(o_ref.dtype)
            else:
                out_refs[d][rows, :] = outs[slot] + carried

    n_ctx_blocks = n_ctx_chunks // PREP_CHUNKS
    ctx_state = [jnp.zeros((HEAD_DIM, HEAD_DIM), F32), jnp.zeros((HEAD_DIM, HEAD_DIM), F32)]
    for step in range(n_ctx_blocks):
        upds, decs = [], []
        for d, bi in ((0, step), (1, n_ctx_blocks - 1 - step)):
            a_ref, lb, tri_p, end_row = ((caf_ref, lb_f, tri_fwd_p, c - 1), (cab_ref, lb_b, tri_bwd_p, 0))[d]
            rows_c = pl.ds(bi * rows_p, rows_p)
            k, g = _gates(a_ref[rows_c, :], lb)
            b = _split_dot(tri_p, g).reshape(PREP_CHUNKS, c, HEAD_DIM)
            b_end = b[:, end_row:end_row + 1, :]
            k_d = (k.reshape(PREP_CHUNKS, c, HEAD_DIM) * jnp.exp(b_end - b)).reshape(rows_p, HEAD_DIM).astype(BF16)
            upds.append(_tn_dot(cv_ref[rows_c, :].astype(BF16), block_diag(k_d)))
            decs.append(jnp.exp(b_end))
        for d in (0, 1):
            for j in orders[d]:
                ctx_state[d] = ctx_state[d] * decs[d][j] + upds[d][:, j * HEAD_DIM:(j + 1) * HEAD_DIM]
    sf_scr[...] = ctx_state[0]
    sb_scr[...] = ctx_state[1]

    prepare(0, 0)

    def body(finish):
        def pair(i, carry):
            scan(2 * i, 0, finish)
            prepare(2 * i + 1, 1)
            scan(2 * i + 1, 1, finish)
            prepare(2 * i + 2, 0)
            return carry
        return pair

    lax.fori_loop(0, n_steps // 4, body(False), 0)
    lax.fori_loop(n_steps // 4, n_steps // 2, body(True), 0)


def _hgrn(px, pc, lb_logits, gain, *, width):
    bsz, _, heads, n, hd = px.shape
    n_ctx = pc.shape[3]

    def seg(s):
        return pl.BlockSpec((None, None, None, n, hd), lambda b, h: (b, s, h, 0, 0))

    def cseg(s):
        return pl.BlockSpec((None, None, None, n_ctx, hd), lambda b, h: (b, s, h, 0, 0))

    depth1 = lb_logits.shape[1]
    slots = 2 * SCAN_BLOCKS
    assert (n // SCAN_CHUNK) % (4 * SCAN_BLOCKS * PREP_CHUNKS) == 0, "scan steps come in pairs, per half"
    assert (n_ctx // SCAN_CHUNK) % PREP_CHUNKS == 0, "the context prefix is scanned in whole blocks"
    return pl.pallas_call(
        functools.partial(_hgrn_kernel, n_chunks=n // SCAN_CHUNK, n_ctx_chunks=n_ctx // SCAN_CHUNK),
        out_shape=jax.ShapeDtypeStruct((bsz, heads, n, hd), BF16),
        grid=(bsz, heads),
        in_specs=[seg(0), seg(1), seg(2), seg(3), seg(4), cseg(0), cseg(1), cseg(2),
                  pl.BlockSpec((2, depth1, hd), lambda b, h: (0, 0, h)),
                  pl.BlockSpec((1, hd), lambda b, h: (0, h))],
        out_specs=pl.BlockSpec((None, None, n, hd), lambda b, h: (b, h, 0, 0)),
        scratch_shapes=[pltpu.VMEM((n, hd), F32), pltpu.VMEM((n, hd), F32),
                        pltpu.VMEM((hd, hd), F32), pltpu.VMEM((hd, hd), F32),
                        *[pltpu.VMEM((2, slots, PREP_CHUNKS * SCAN_CHUNK, hd), BF16) for _ in range(4)],
                        pltpu.VMEM((2, slots, PREP_CHUNKS, 8, hd), F32)],
        compiler_params=_params("parallel", "parallel"),
        name="hgrn_scan",
    )(px, px, px, px, px, pc, pc, pc, lb_logits, gain.reshape(1, width))


def _dft_constants(n):
    n2 = GRID_W
    n1h = n // n2
    m2r = np.arange(n2)[None, :]
    k2 = np.arange(n2)[:, None]
    f_ang = -2.0 * np.pi * k2 * m2r / n2
    fr, fi = np.cos(f_ang), np.sin(f_ang)
    f_fwd = np.block([[fr, -fi], [fi, fr]])
    f_inv = np.block([[fr, fi], [-fi, fr]])
    k1 = np.arange(n1h)[:, None, None]
    m1 = np.arange(n1h)[None, :, None]
    m2 = np.arange(n2)[None, None, :]
    theta = -2.0 * np.pi * (k1 + 0.5) * (m1 / (2 * n1h) + m2 / (2 * n))
    parts = np.stack([np.cos(theta), np.sin(theta)], axis=0)
    parts = parts.reshape(2, n1h, n1h, n2 // OUTER_ROWS, OUTER_ROWS)
    eye = np.eye(OUTER_ROWS)
    rows = n1h * OUTER_ROWS
    fwd = np.einsum("pknsj,jm->spkjnm", parts, eye).reshape(n2 // OUTER_ROWS, 2 * rows, rows)
    inv = np.einsum("pknsj,jm->snjpkm", parts, eye).reshape(n2 // OUTER_ROWS, rows, 2 * rows)
    return fwd, inv, f_fwd, f_inv


def _filt_mlp_kernel(z_ref, w1_ref, b1_ref, w2_ref, b2_ref, w3_ref, b3_ref, fr_ref, o_ref):
    fr = fr_ref[...]
    hp = lambda a, b: jnp.dot(a, b, precision=HIGHEST, preferred_element_type=F32)
    hid = jnp.sin(fr * (hp(z_ref[...], w1_ref[...]) + b1_ref[...]))
    hid = jnp.sin(fr * (hp(hid, w2_ref[...]) + b2_ref[...]))
    o_ref[...] = jnp.sin(fr * (hp(hid, w3_ref[...]) + b3_ref[...]))


def _filt_taps_kernel(hid_ref, wo_ref, dl_ref, h_ref, s_ref, *, n):
    hid = hid_ref[...]
    hid_hi = hid.astype(BF16)
    h = _dot3(hid_hi, (hid - hid_hi.astype(F32)).astype(BF16), wo_ref[...])
    t = lax.broadcasted_iota(jnp.int32, h.shape, 0).astype(F32) * (1.0 / max(n - 1, 1))
    hw = h * jnp.exp(-t * dl_ref[...])
    h_ref[...] = hw
    s_ref[0:1, :] = jnp.sum(jnp.abs(hw), axis=0, keepdims=True)
    s_ref[1:2, :] = jnp.abs(hw[0:1, :])


def _spectrum_kernel(af_ref, ab_ref, f_ref, fl_ref, s_ref, d_ref, o_ref, *, n):
    n2 = GRID_W
    l1 = s_ref[0:1, :] + s_ref[2:3, :] - s_ref[3:4, :]
    hb0 = s_ref[4:5, :]
    norm = 1.0 / l1
    scale = 2.0 / (2 * n)
    for i in range(af_ref.shape[1]):
        def inner(a_ref):
            z = _dot3(f_ref[...], fl_ref[...], jnp.concatenate([a_ref[0, i], a_ref[1, i]], axis=0))
            return z[:n2], z[n2:]

        zfr, zfi = inner(af_ref)
        zbr, zbi = inner(ab_ref)
        o_ref[0, i] = scale * ((zfr + zbr - hb0) * norm + d_ref[...])
        o_ref[1, i] = scale * ((zfi - zbi) * norm)


def _dot3(a_hi, a_lo, x):
    x_hi = x.astype(BF16)
    x_lo = (x - x_hi.astype(F32)).astype(BF16)
    return _dot(a_hi, x_hi) + _dot(a_hi, x_lo) + _dot(a_lo, x_hi)


def _hi_lo(a):
    hi = np.asarray(a, np.float64).astype(BF16)
    lo = (a - hi.astype(np.float64)).astype(BF16)
    return jnp.asarray(hi), jnp.asarray(lo)


def _slabs(x):
    return [x[..., h * OUTER_ROWS:(h + 1) * OUTER_ROWS, :] for h in range(x.shape[-2] // OUTER_ROWS)]


def _outer_dft_kernel(*refs, high):
    g_refs, x_ref, o_ref = refs[:-2], refs[-2], refs[-1]
    n1h, _, tl = x_ref.shape
    parts = []
    for h, xs in enumerate(_slabs(x_ref[...])):
        xs = xs.reshape(n1h * OUTER_ROWS, tl)
        r = _dot3(g_refs[0][h], g_refs[1][h], xs) if high else _dot(g_refs[0][h], xs.astype(BF16))
        parts.append(r.reshape(2, n1h, OUTER_ROWS, tl))
    o_ref[...] = jnp.concatenate(parts, axis=2).astype(o_ref.dtype)


def _outer_dft(gs, x4, seg, *, c, tl):
    high = len(gs) == 2
    s, n1h, n2, _ = x4.shape
    tl = min(tl, c)
    per = c // tl
    rows = OUTER_ROWS if high else BF16_ROWS
    g_spec = pl.BlockSpec((rows // OUTER_ROWS,) + gs[0].shape[1:], lambda j, b, l: (j, 0, 0))
    return pl.pallas_call(
        functools.partial(_outer_dft_kernel, high=high),
        out_shape=jax.ShapeDtypeStruct((s, 2, n1h, n2, c), F32 if high else BF16),
        grid=(n2 // rows, s, per),
        in_specs=[g_spec] * len(gs) + [pl.BlockSpec((None, n1h, rows, tl),
                                                    lambda j, b, l: (b, 0, j, seg * per + l))],
        out_specs=pl.BlockSpec((None, 2, n1h, rows, tl), lambda j, b, l: (b, 0, 0, j, l)),
        compiler_params=_params("parallel", "parallel", "parallel"),
        name="outer_dft",
    )(*gs, x4)


def _hyena_spectra(n, width, filt, hy_d, consts):
    w1, b1, w2, b2, w3, b3, freq, w_out = filt
    gt_fwd, _, f_fwd, _ = consts
    n1h = n // GRID_W
    order = w1.shape[1]
    pos = np.arange(n, dtype=np.float64)
    t = pos / max(n - 1, 1)
    bands = (FILT_EMB - 1) // 2
    fb = np.linspace(1e-4, bands - 1, bands)
    ang = (2 * math.pi / n) * pos[:, None] * fb[None, :]
    z = np.concatenate([t[:, None], np.cos(ang), -np.sin(ang)], axis=-1)
    emb_pad = 128
    z = jnp.asarray(np.pad(z, ((0, 0), (0, emb_pad - FILT_EMB))), F32)
    w1p = jnp.pad(w1, ((0, emb_pad - FILT_EMB), (0, 0)))
    row = lambda a: a.reshape(1, -1)
    full = lambda shape: pl.BlockSpec(shape, lambda: tuple(0 for _ in shape))
    hid = pl.pallas_call(
        _filt_mlp_kernel,
        out_shape=jax.ShapeDtypeStruct((n, order), F32),
        in_specs=[full((n, emb_pad)), full((emb_pad, order)), full((1, order)), full((order, order)),
                  full((1, order)), full((order, order)), full((1, order)), full((1, order))],
        out_specs=full((n, order)),
        compiler_params=pltpu.CompilerParams(vmem_limit_bytes=VMEM_LIMIT_BYTES),
        name="filter_mlp",
    )(z, w1p, row(b1), w2, row(b2), w3, row(b3), row(freq))

    cols = 4 * width
    deltas = np.abs(np.linspace(math.log(DECAY_TARGET) / DECAY_SLOW_PCT,
                                math.log(DECAY_TARGET) / DECAY_FAST_PCT, width))
    deltas4 = jnp.asarray(np.tile(deltas, 4)[None, :], F32)
    tc = min(512, cols)
    taps, sums = pl.pallas_call(
        functools.partial(_filt_taps_kernel, n=n),
        out_shape=(jax.ShapeDtypeStruct((n, cols), F32), jax.ShapeDtypeStruct((2, cols), F32)),
        grid=(cols // tc,),
        in_specs=[pl.BlockSpec((n, order), lambda j: (0, 0)),
                  pl.BlockSpec((order, tc), lambda j: (0, j)),
                  pl.BlockSpec((1, tc), lambda j: (0, j))],
        out_specs=(pl.BlockSpec((n, tc), lambda j: (0, j)), pl.BlockSpec((2, tc), lambda j: (0, j))),
        compiler_params=_params("parallel"),
        name="filter_taps",
    )(hid, w_out, deltas4)

    a = _outer_dft(_hi_lo(gt_fwd), taps.reshape(1, n1h, GRID_W, cols), 0, c=cols, tl=1024)[0]
    s4 = sums.reshape(2, 2, 2, width)
    bwd0 = taps[0].reshape(2, 2, width)[:, 1]
    stats = jnp.stack([s4[0, :, 0], s4[1, :, 0], s4[0, :, 1], s4[1, :, 1], bwd0], axis=1)

    kb = min(SPECTRUM_BINS, n1h)

    def tap_spec(side):
        return pl.BlockSpec((2, kb, GRID_W, width), lambda k1, f: (0, k1, 0, 2 * f + side))

    f_hi, f_lo = _hi_lo(f_fwd)
    f_spec = pl.BlockSpec((2 * GRID_W, 2 * GRID_W), lambda k1, f: (0, 0))
    return pl.pallas_call(
        functools.partial(_spectrum_kernel, n=n),
        out_shape=jax.ShapeDtypeStruct((2, 2, n1h, GRID_W, width), F32),
        grid=(n1h // kb, 2),
        in_specs=[tap_spec(0), tap_spec(1), f_spec, f_spec,
                  pl.BlockSpec((None, 5, width), lambda k1, f: (f, 0, 0)),
                  pl.BlockSpec((None, 1, width), lambda k1, f: (f, 0, 0))],
        out_specs=pl.BlockSpec((None, 2, kb, GRID_W, width), lambda k1, f: (f, 0, k1, 0, 0)),
        compiler_params=_params("parallel", "parallel"),
        name="filter_spectrum",
    )(a, a, f_hi, f_lo, stats, hy_d.reshape(2, 1, width))


def _inner_conv_kernel(a_ref, f_ref, fi_ref, h_ref, o_ref):
    n2 = GRID_W
    for i in range(a_ref.shape[1]):
        z = _dot(f_ref[...], jnp.concatenate([a_ref[0, i], a_ref[1, i]], axis=0))
        zr, zi = z[:n2], z[n2:]
        hr, hi = h_ref[0, i], h_ref[1, i]
        stacked = jnp.concatenate([zr * hr - zi * hi, zr * hi + zi * hr], axis=0).astype(BF16)
        w = _dot(fi_ref[...], stacked).astype(BF16)
        o_ref[0, i] = w[:n2]
        o_ref[1, i] = w[n2:]


def _inner_conv(a, f_fwd, f_inv, spec, filt):
    bsz, _, n1h, n2, c = a.shape
    kb = min(INNER_BINS, n1h)
    return pl.pallas_call(
        _inner_conv_kernel,
        out_shape=jax.ShapeDtypeStruct(a.shape, BF16),
        grid=(n1h // kb, bsz),
        in_specs=[pl.BlockSpec((None, 2, kb, n2, c), lambda k1, b: (b, 0, k1, 0, 0)),
                  pl.BlockSpec((2 * n2, 2 * n2), lambda k1, b: (0, 0)),
                  pl.BlockSpec((2 * n2, 2 * n2), lambda k1, b: (0, 0)),
                  pl.BlockSpec((None, 2, kb, n2, c), lambda k1, b: (filt, 0, k1, 0, 0))],
        out_specs=pl.BlockSpec((None, 2, kb, n2, c), lambda k1, b: (b, 0, k1, 0, 0)),
        compiler_params=_params("parallel", "arbitrary"),
        name="inner_conv",
    )(a, f_fwd, f_inv, spec)


def _outer_idft_gate_kernel(gi_ref, g_ref, b_ref, x_ref, o_ref, *, again):
    n1h, _, tl = x_ref.shape
    parts = []
    for h, (bs, xs) in enumerate(zip(_slabs(b_ref[...].astype(F32)), _slabs(x_ref[...]))):
        stacked = bs.reshape(2 * n1h * OUTER_ROWS, tl).astype(BF16)
        y = _dot(gi_ref[h], stacked) * xs.reshape(n1h * OUTER_ROWS, tl)
        if again:
            parts.append(_dot(g_ref[h], y.astype(BF16)).reshape(2, n1h, OUTER_ROWS, tl))
        else:
            parts.append(y.reshape(n1h, OUTER_ROWS, tl))
    o_ref[...] = jnp.concatenate(parts, axis=-2).astype(o_ref.dtype)


def _outer_idft_gate(g_inv, g_fwd, b, px4, seg, *, again, tl):
    bsz, _, n1h, n2, c = b.shape
    tl = min(tl, c)
    per = c // tl
    rows = BF16_ROWS
    if again:
        out_shape = jax.ShapeDtypeStruct((bsz, 2, n1h, n2, c), BF16)
        out_spec = pl.BlockSpec((None, 2, n1h, rows, tl), lambda j, bb, l: (bb, 0, 0, j, l))
    else:
        out_shape = jax.ShapeDtypeStruct((bsz, n1h, n2, c), BF16)
        out_spec = pl.BlockSpec((None, n1h, rows, tl), lambda j, bb, l: (bb, 0, j, l))
    return pl.pallas_call(
        functools.partial(_outer_idft_gate_kernel, again=again),
        out_shape=out_shape,
        grid=(n2 // rows, bsz, per),
        in_specs=[pl.BlockSpec((rows // OUTER_ROWS,) + g_inv.shape[1:], lambda j, bb, l: (j, 0, 0)),
                  pl.BlockSpec((rows // OUTER_ROWS,) + g_fwd.shape[1:], lambda j, bb, l: (j, 0, 0)),
                  pl.BlockSpec((None, 2, n1h, rows, tl), lambda j, bb, l: (bb, 0, 0, j, l)),
                  pl.BlockSpec((None, n1h, rows, tl), lambda j, bb, l: (bb, 0, j, seg * per + l))],
        out_specs=out_spec,
        compiler_params=_params("parallel", "parallel", "parallel"),
        name="outer_idft_gate",
    )(g_inv, g_fwd, b, px4)


def _hyena_mix(px, spec, consts, *, bsz, n, width):
    g_fwd, g_inv, f_fwd, f_inv = consts
    g_fwd_b, g_inv_b = jnp.asarray(g_fwd, BF16), jnp.asarray(g_inv, BF16)
    f_fwd_b, f_inv_b = jnp.asarray(f_fwd, BF16), jnp.asarray(f_inv, BF16)
    n1h = n // GRID_W
    px4 = px.reshape(bsz, n1h, GRID_W, -1)
    tl = width
    a = _outer_dft((g_fwd_b,), px4, 0, c=width, tl=tl)
    a = _inner_conv(a, f_fwd_b, f_inv_b, spec, 0)
    a = _outer_idft_gate(g_inv_b, g_fwd_b, a, px4, 1, again=True, tl=tl)
    a = _inner_conv(a, f_fwd_b, f_inv_b, spec, 1)
    hy = _outer_idft_gate(g_inv_b, g_fwd_b, a, px4, 2, again=False, tl=tl)
    return hy.reshape(bsz * n, width)


def _out_proj_kernel(x_ref, hg_ref, hy_ref, wa_ref, wb_ref, gt_ref, o_ref):
    hg = jnp.concatenate([hg_ref[h] for h in range(hg_ref.shape[0])], axis=1)
    mix = _dot(hg, wa_ref[...]) + _dot(hy_ref[...], wb_ref[...])
    o_ref[...] = x_ref[...] + gt_ref[0] * mix


def _out_proj(x, hg, hy, w, gate, *, tokens_per_mod, tm):
    t, d = x.shape
    _, heads, n, hd = hg.shape
    wdt = heads * hd
    tm = min(tm, tokens_per_mod, n)
    per_seq = n // tm
    return pl.pallas_call(
        _out_proj_kernel,
        out_shape=jax.ShapeDtypeStruct((t, d), F32),
        grid=(t // tm,),
        in_specs=[pl.BlockSpec((tm, d), lambda i: (i, 0)),
                  pl.BlockSpec((None, heads, tm, hd), lambda i: (i // per_seq, 0, i % per_seq, 0)),
                  pl.BlockSpec((tm, wdt), lambda i: (i, 0)),
                  pl.BlockSpec((wdt, d), lambda i: (0, 0)),
                  pl.BlockSpec((wdt, d), lambda i: (1, 0)),
                  pl.BlockSpec((1, 1, d), lambda i: ((i * tm) // tokens_per_mod, 0, 0))],
        out_specs=pl.BlockSpec((tm, d), lambda i: (i, 0)),
        compiler_params=_params("parallel"),
        name="out_proj",
    )(x, hg, hy, w, w, gate)


def _tiles(d_ff):
    assert d_ff % 128 == 0, "a ragged last hidden tile still has to be lane-tile aligned"
    return TOKEN_TILE, 2 * TOKEN_TILE, min(HIDDEN_TILE, d_ff)


def kernel(x, c, ctx, c_ctx, ada_w, ada_b, norm_ffn1, ffn1_w1, ffn1_w3, ffn1_w2, norm_mix, w_in, hg_lb_logits, hg_norm, hy_conv_w, hy_conv_b, filt_w1, filt_b1, filt_w2, filt_b2, filt_w3, filt_b3, filt_freq, filt_w_out, hy_d, w_out, norm_ffn2, ffn2_w1, ffn2_w3, ffn2_w2, final_norm):
    bsz, n, d = x.shape
    n_ctx = ctx.shape[1]
    depth = ada_w.shape[0]
    assert depth == 1, "single-layer block"
    width = d // 2
    n_seg = w_in.shape[2] // width
    assert n_seg == 8 and n % GRID_W == 0 and n_ctx % SCAN_CHUNK == 0 and width % HEAD_DIM == 0
    tm, tm_wide, tf = _tiles(ffn1_w1.shape[2])
    l = 0

    rows = -(-(bsz + 1) // 8) * 8
    cs = jnp.concatenate([c, c_ctx[None, :], jnp.zeros((rows - bsz - 1, d), F32)], axis=0)
    mods = _ada(cs, ada_w[l], ada_b[l]).reshape(rows, N_MOD, d)
    mx = [mods[:bsz, i][:, None, :] for i in range(N_MOD)]
    mc = [mods[bsz:bsz + 1, i][:, None, :] for i in range(N_MOD)]

    bf = _to_bf16
    w1a, w3a, w2a = bf(ffn1_w1[l]), bf(ffn1_w3[l]), bf(ffn1_w2[l])
    xt = x.reshape(bsz * n, d)
    yt = ctx.reshape(bsz * n_ctx, d)

    xt = _ffn(xt, mx[0], mx[1], mx[2], norm_ffn1[l], w1a, w3a, w2a, final_norm,
              tokens_per_mod=n, final_norm=False, tm=tm, tf=tf)
    yt = _ffn(yt, mc[0], mc[1], mc[2], norm_ffn1[l], w1a, w3a, w2a, final_norm,
              tokens_per_mod=bsz * n_ctx, final_norm=False, tm=tm, tf=tf)

    w_in_b = bf(w_in[l])
    pc, _ = _proj(yt, mc[3], mc[4], norm_mix[l], w_in_b, hy_conv_w[l], hy_conv_b[l], n_seg=3, seg_w=width,
                  conv_from=3, tokens_per_mod=bsz * n_ctx, seq_len=n_ctx, tm=tm)
    px_hg, px = _proj(xt, mx[3], mx[4], norm_mix[l], w_in_b, hy_conv_w[l], hy_conv_b[l], n_seg=n_seg, seg_w=width,
                      conv_from=5, tokens_per_mod=n, seq_len=n, tm=tm_wide)

    hg = _hgrn(px_hg, pc, hg_lb_logits, hg_norm[l], width=width)

    consts = _dft_constants(n)
    filt = (filt_w1[l], filt_b1[l], filt_w2[l], filt_b2[l], filt_w3[l], filt_b3[l], filt_freq[l], filt_w_out[l])
    spec = _hyena_spectra(n, width, filt, hy_d[l], consts)
    hy = _hyena_mix(px, spec, consts, bsz=bsz, n=n, width=width)

    xt = _out_proj(xt, hg, hy, bf(w_out[l]), mx[5],
                   tokens_per_mod=n, tm=tm)

    out = _ffn(xt, mx[6], mx[7], mx[8], norm_ffn2[l], bf(ffn2_w1[l]), bf(ffn2_w3[l]), bf(ffn2_w2[l]), final_norm,
               tokens_per_mod=n, final_norm=True, tm=tm, tf=tf)
    return out.reshape(bsz, n, d)
```

```python
import functools
import math

import numpy as np
import jax
import jax.numpy as jnp
from jax import lax
from jax.experimental import pallas as pl
from jax.experimental.pallas import tpu as pltpu

F32 = jnp.float32
BF16 = jnp.bfloat16
EPS = 1e-6
N_MOD = 9
HEAD_DIM = 128
GRID_W = 64
CAST_ROWS = 256
TOKEN_TILE = 512
HIDDEN_TILE = 1024
SCAN_CHUNK = 64
PREP_CHUNKS = 4
SCAN_BLOCKS = 4
OUTER_ROWS = 8
BF16_ROWS = 16
INNER_BINS = 16
SPECTRUM_BINS = 8
FILT_EMB = 33
DECAY_TARGET = 1e-2
DECAY_FAST_PCT = 0.3
DECAY_SLOW_PCT = 1.5
VMEM_LIMIT_BYTES = 56 * 1024 * 1024
HIGHEST = lax.Precision.HIGHEST


def _params(*semantics):
    return pltpu.CompilerParams(dimension_semantics=semantics, vmem_limit_bytes=VMEM_LIMIT_BYTES)


def _dot(a, b):
    return jnp.dot(a, b, preferred_element_type=F32)


def _silu(a):
    return a * jax.nn.sigmoid(a)


def _cast_kernel(x_ref, o_ref):
    o_ref[...] = x_ref[...].astype(BF16)


def _to_bf16(w):
    r, c = w.shape
    rb = CAST_ROWS if r % CAST_ROWS == 0 else r
    return pl.pallas_call(
        _cast_kernel,
        out_shape=jax.ShapeDtypeStruct((r, c), BF16),
        grid=(r // rb,),
        in_specs=[pl.BlockSpec((rb, c), lambda i: (i, 0))],
        out_specs=pl.BlockSpec((rb, c), lambda i: (i, 0)),
        compiler_params=_params("parallel"),
        name="to_bf16",
    )(w)


def _ada_kernel(c_ref, w_ref, b_ref, o_ref):
    h = _silu(c_ref[...]).astype(BF16)
    o_ref[...] = _dot(h, w_ref[...].astype(BF16)) + b_ref[...]


def _ada(cs, w, b):
    rows, d = cs.shape
    n = w.shape[1]
    tn = d // 2
    return pl.pallas_call(
        _ada_kernel,
        out_shape=jax.ShapeDtypeStruct((rows, n), F32),
        grid=(n // tn,),
        in_specs=[pl.BlockSpec((rows, d), lambda j: (0, 0)),
                  pl.BlockSpec((d, tn), lambda j: (0, j)),
                  pl.BlockSpec((1, tn), lambda j: (0, j))],
        out_specs=pl.BlockSpec((rows, tn), lambda j: (0, j)),
        compiler_params=_params("arbitrary"),
        name="ada_mod",
    )(cs, w, b.reshape(1, n))


def _norm_mod(x, gain, shift, scale):
    y = x * lax.rsqrt(jnp.mean(x * x, axis=-1, keepdims=True) + EPS)
    return (y * (gain * (1.0 + scale)) + shift).astype(BF16)


def _ffn_kernel(x_ref, sh_ref, sc_ref, gt_ref, g_ref, w1_ref, w3_ref, w2_ref, fin_ref, *rest,
                final_norm, n_hidden, last_cols):
    if len(rest) == 2:
        (o_ref, h_scr), mix, res_ref = rest, None, x_ref
    else:
        *mix, o_ref, h_scr, res_ref = rest
    j = pl.program_id(1)
    tf = w1_ref.shape[1]

    def hidden_tile(cols=tf):
        h = h_scr[...]
        a = _dot(h, w1_ref[:, :cols])
        b = _dot(h, w3_ref[:, :cols])
        return _dot((_silu(a) * b).astype(BF16), w2_ref[:cols, :])

    def finish(acc):
        out = res_ref[...] + (0.5 * gt_ref[0]) * acc
        if final_norm:
            out = out * lax.rsqrt(jnp.mean(out * out, axis=-1, keepdims=True) + EPS) * fin_ref[...]
        return out

    last = n_hidden - 1

    @pl.when(j == 0)
    def _():
        x = x_ref[...]
        if mix is not None:
            hg_ref, hy_ref, wa_ref, wb_ref, gm_ref = mix
            x = x + gm_ref[0] * (_dot(hg_ref[...], wa_ref[...]) + _dot(hy_ref[...], wb_ref[...]))
            res_ref[...] = x
        h_scr[...] = _norm_mod(x, g_ref[...], sh_ref[0], sc_ref[0])
        o_ref[...] = finish(hidden_tile(last_cols)) if last == 0 else hidden_tile(last_cols)

    @pl.when((j > 0) & (j < last))
    def _():
        o_ref[...] += hidden_tile()

    if last > 0:
        @pl.when(j == last)
        def _():
            o_ref[...] = finish(o_ref[...] + hidden_tile())


def _ffn(x, shift, scale, gate, gain, w1, w3, w2, fin, *, tokens_per_mod, final_norm, tm, tf, mix=None):
    t, d = x.shape
    f = w1.shape[1]
    tm = min(tm, tokens_per_mod)
    mod_spec = pl.BlockSpec((1, 1, d), lambda i, j: ((i * tm) // tokens_per_mod, 0, 0))
    vec_spec = pl.BlockSpec((1, d), lambda i, j: (0, 0))
    n_hidden = pl.cdiv(f, tf)
    hidden = lambda j: (j + n_hidden - 1) % n_hidden
    in_specs = [pl.BlockSpec((tm, d), lambda i, j: (i, 0)),
                mod_spec, mod_spec, mod_spec, vec_spec,
                pl.BlockSpec((d, tf), lambda i, j: (0, hidden(j))),
                pl.BlockSpec((d, tf), lambda i, j: (0, hidden(j))),
                pl.BlockSpec((tf, d), lambda i, j: (hidden(j), 0)),
                vec_spec]
    args = [x, shift, scale, gate, gain.reshape(1, d), w1, w3, w2, fin.reshape(1, d)]
    scratch = [pltpu.VMEM((tm, d), BF16)]
    if mix is not None:
        hg, hy, w_out, gate_mix = mix
        wdt = hg.shape[1]
        half = lambda part: pl.BlockSpec((wdt, d), lambda i, j: (part, 0), pipeline_mode=pl.Buffered(1))
        in_specs += [pl.BlockSpec((tm, wdt), lambda i, j: (i, 0)), pl.BlockSpec((tm, wdt), lambda i, j: (i, 0)),
                     half(0), half(1), mod_spec]
        args += [hg, hy, w_out, w_out, gate_mix]
        scratch.append(pltpu.VMEM((tm, d), F32))
    return pl.pallas_call(
        functools.partial(_ffn_kernel, final_norm=final_norm, n_hidden=n_hidden, last_cols=f - (n_hidden - 1) * tf),
        out_shape=jax.ShapeDtypeStruct((t, d), F32),
        grid=(t // tm, n_hidden),
        in_specs=in_specs,
        out_specs=pl.BlockSpec((tm, d), lambda i, j: (i, 0)),
        scratch_shapes=scratch,
        compiler_params=_params("parallel", "arbitrary"),
        name="swiglu_ffn",
    )(*args)


def _proj_kernel(x_ref, sh_ref, sc_ref, g_ref, w_ref, cw_ref, cb_ref, o_ref, h_scr, *, conv_from):
    j = pl.program_id(1)

    def segment():
        return _dot(h_scr[...], w_ref[...])

    @pl.when(j == 0)
    def _():
        h_scr[...] = _norm_mod(x_ref[...], g_ref[...], sh_ref[0], sc_ref[0])
        o_ref[...] = segment()

    @pl.when((j > 0) & (j < conv_from))
    def _():
        o_ref[...] = segment()

    @pl.when(j >= conv_from)
    def _():
        p = segment()
        tm, w = p.shape
        rows3 = lambda t: t.reshape(tm // GRID_W, GRID_W, w)
        y = (rows3(pltpu.roll(p, 1, axis=0)) * cw_ref[0] + rows3(p) * cw_ref[1]
             + rows3(pltpu.roll(p, tm - 1, axis=0)) * cw_ref[2] + cb_ref[...])
        o_ref[...] = y.reshape(tm, w)


def _proj(x, shift, scale, gain, w, conv_w, conv_b, *, n_seg, seg_w, conv_from, tokens_per_mod, tm):
    t, d = x.shape
    tm = min(tm, tokens_per_mod)
    mod_spec = pl.BlockSpec((1, 1, d), lambda i, j: ((i * tm) // tokens_per_mod, 0, 0))
    conv_idx = lambda i, j: (0, jnp.maximum(j - conv_from, 0))
    pos = jnp.arange(GRID_W)[None, :, None]
    edge = jnp.stack([pos > 0, pos >= 0, pos < GRID_W - 1], axis=0)[:, 0]
    conv_w = jnp.where(edge, conv_w[:, None, :], 0.0)
    return pl.pallas_call(
        functools.partial(_proj_kernel, conv_from=conv_from),
        out_shape=jax.ShapeDtypeStruct((t, n_seg * seg_w), F32),
        grid=(t // tm, n_seg),
        in_specs=[pl.BlockSpec((tm, d), lambda i, j: (i, 0)),
                  mod_spec, mod_spec,
                  pl.BlockSpec((1, d), lambda i, j: (0, 0)),
                  pl.BlockSpec((d, seg_w), lambda i, j: (0, j)),
                  pl.BlockSpec((3, GRID_W, seg_w), lambda i, j: (0, 0, jnp.maximum(j - conv_from, 0))),
                  pl.BlockSpec((1, seg_w), conv_idx)],
        out_specs=pl.BlockSpec((tm, seg_w), lambda i, j: (i, j)),
        scratch_shapes=[pltpu.VMEM((tm, d), BF16)],
        compiler_params=_params("parallel", "arbitrary"),
        name="in_proj",
    )(x, shift, scale, gain.reshape(1, d), w, conv_w, conv_b.reshape(1, -1))


def _split3(g):
    g1 = g.astype(BF16)
    r1 = g - g1.astype(F32)
    g2 = r1.astype(BF16)
    return g1, g2, (r1 - g2.astype(F32)).astype(BF16)


def _split_dot(tri, g):
    g1, g2, g3 = _split3(g)
    return _dot(tri, g1) + _dot(tri, g2) + _dot(tri, g3)


def _gates(a, lb):
    f = lb + (1.0 - lb) * jax.nn.sigmoid(a)
    return 1.0 - f, jnp.log(f)


def _tn_dot(a, b):
    return lax.dot_general(a, b, (((0,), (0,)), ((), ())), preferred_element_type=F32)


def _nt_dot(a, b):
    return lax.dot_general(a, b, (((1,), (1,)), ((), ())), preferred_element_type=F32)


def _hgrn_kernel(af_ref, ab_ref, v_ref, q_ref, gate_ref, caf_ref, cab_ref, cv_ref, lbl_ref, gain_ref,
                 o_ref, of_scr, ob_scr, sf_scr, sb_scr, qin_scr, qt_scr, kt_scr, kd_scr, dec_scr,
                 *, n_chunks, n_ctx_chunks):
    c = SCAN_CHUNK
    mid = c // 2
    rows_p = PREP_CHUNKS * c
    prow = lax.broadcasted_iota(jnp.int32, (rows_p, rows_p), 0)
    pcol = lax.broadcasted_iota(jnp.int32, (rows_p, rows_p), 1)
    same = (prow // c) == (pcol // c)
    tri_fwd_p = jnp.where(same & (prow >= pcol), 1.0, 0.0).astype(BF16)
    tri_bwd_p = jnp.where(same & (pcol >= prow), 1.0, 0.0).astype(BF16)

    def lower_bound(direction):
        lg = lbl_ref[direction]
        ex = jnp.exp(lg - jnp.max(lg, axis=0, keepdims=True))
        return ex[0:1, :] / jnp.sum(ex, axis=0, keepdims=True)

    lb_f = lower_bound(0)
    lb_b = lower_bound(1)
    q_scale = HEAD_DIM ** -0.5

    n_blocks = n_chunks // PREP_CHUNKS
    n_steps = n_blocks // SCAN_BLOCKS

    def step_blocks(it):
        fwd = [(0, it * SCAN_BLOCKS + u) for u in range(SCAN_BLOCKS)]
        return fwd + [(1, n_blocks - 1 - bi) for _, bi in fwd]

    def prepare(it, s):
        it = jnp.minimum(it, n_steps - 1)
        blocks = step_blocks(it)
        los = [pl.multiple_of(bi * rows_p, rows_p) for _, bi in blocks]
        gates = [_gates((af_ref, ab_ref)[d][pl.ds(lo, rows_p), :], (lb_f, lb_b)[d]) for (d, _), lo in zip(blocks, los)]
        sums = {}
        for d, tri_p in ((0, tri_fwd_p), (1, tri_bwd_p)):
            slots_d = [slot for slot, (dd, _) in enumerate(blocks) if dd == d]
            wide = _dot(tri_p, jnp.concatenate([p for slot in slots_d for p in _split3(gates[slot][1])], axis=1))
            for n_s, slot in enumerate(slots_d):
                parts = [wide[:, (3 * n_s + m) * HEAD_DIM:(3 * n_s + m + 1) * HEAD_DIM] for m in range(3)]
                sums[slot] = parts[0] + parts[1] + parts[2]
        for slot, ((d, _), lo) in enumerate(zip(blocks, los)):
            end_row = (c - 1, 0)[d]
            k = gates[slot][0]
            b = sums[slot].reshape(PREP_CHUNKS, c, HEAD_DIM)
            b_mid = b[:, mid:mid + 1, :]
            b_end = b[:, end_row:end_row + 1, :]
            q_t = (q_ref[pl.ds(lo, rows_p), :] * q_scale).reshape(PREP_CHUNKS, c, HEAD_DIM) * jnp.exp(b - b_mid)
            k_t = k.reshape(PREP_CHUNKS, c, HEAD_DIM) * jnp.exp(b_mid - b)
            flat = lambda t: t.reshape(rows_p, HEAD_DIM).astype(BF16)
            qt_scr[s, slot] = flat(q_t)
            kt_scr[s, slot] = flat(k_t)
            qin_scr[s, slot] = flat(q_t * jnp.exp(b_mid))
            kd_scr[s, slot] = flat(k_t * jnp.exp(b_end - b_mid))
            dec_scr[s, slot] = jnp.broadcast_to(jnp.exp(b_end), (PREP_CHUNKS, 8, HEAD_DIM))

    zeros_blk = jnp.zeros((c, HEAD_DIM), BF16)

    def block_diag(t):
        return jnp.concatenate(
            [jnp.concatenate([t[j * c:(j + 1) * c] if m == j else zeros_blk for m in range(PREP_CHUNKS)], axis=1)
             for j in range(PREP_CHUNKS)], axis=0)

    masks = (same & (prow >= pcol), same & (pcol >= prow))
    st_refs = (sf_scr, sb_scr)
    out_refs = (of_scr, ob_scr)
    orders = (tuple(range(PREP_CHUNKS)), tuple(range(PREP_CHUNKS - 1, -1, -1)))

    gain = gain_ref[...]

    def scan(it, s, finish):
        streams = step_blocks(it)
        los = [pl.multiple_of(bi * rows_p, rows_p) for _, bi in streams]
        vbs = [v_ref[pl.ds(lo, rows_p), :].astype(BF16) for lo in los]
        scores = [_nt_dot(qt_scr[s, slot], kt_scr[s, slot]) for slot in range(len(streams))]
        upds = [_tn_dot(vb, block_diag(kd_scr[s, slot])) for slot, vb in enumerate(vbs)]
        outs = [_dot(jnp.where(masks[d], sc, 0.0).astype(BF16), vb) for (d, _), sc, vb in zip(streams, scores, vbs)]
        seen = [[None] * PREP_CHUNKS for _ in streams]
        for d in (0, 1):
            st = st_refs[d][...]
            for slot, (ds_, _) in enumerate(streams):
                if ds_ != d:
                    continue
                for j in orders[d]:
                    seen[slot][j] = st.astype(BF16)
                    st = st * dec_scr[s, slot, j][0:1, :] + upds[slot][:, j * HEAD_DIM:(j + 1) * HEAD_DIM]
            st_refs[d][...] = st
        for slot, ((d, _), lo) in enumerate(zip(streams, los)):
            q_in = qin_scr[s, slot]
            carried = jnp.concatenate([_nt_dot(q_in[j * c:(j + 1) * c], seen[slot][j]) for j in range(PREP_CHUNKS)],
                                      axis=0)
            rows = pl.ds(lo, rows_p)
            if finish:
                o = outs[slot] + carried + out_refs[1 - d][rows, :]
                o = o * lax.rsqrt(jnp.mean(o * o, axis=-1, keepdims=True) + EPS) * gain
                o_ref[rows, :] = (o * _silu(gate_ref[rows, :])).astype(o_ref.dtype)
            else:
                out_refs[d][rows, :] = outs[slot] + carried

    n_ctx_blocks = n_ctx_chunks // PREP_CHUNKS
    ctx_state = [jnp.zeros((HEAD_DIM, HEAD_DIM), F32), jnp.zeros((HEAD_DIM, HEAD_DIM), F32)]
    for step in range(n_ctx_blocks):
        upds, decs = [], []
        for d, bi in ((0, step), (1, n_ctx_blocks - 1 - step)):
            a_ref, lb, tri_p, end_row = ((caf_ref, lb_f, tri_fwd_p, c - 1), (cab_ref, lb_b, tri_bwd_p, 0))[d]
            rows_c = pl.ds(bi * rows_p, rows_p)
            k, g = _gates(a_ref[rows_c, :], lb)
            b = _split_dot(tri_p, g).reshape(PREP_CHUNKS, c, HEAD_DIM)
            b_end = b[:, end_row:end_row + 1, :]
            k_d = (k.reshape(PREP_CHUNKS, c, HEAD_DIM) * jnp.exp(b_end - b)).reshape(rows_p, HEAD_DIM).astype(BF16)
            upds.append(_tn_dot(cv_ref[rows_c, :].astype(BF16), block_diag(k_d)))
            decs.append(jnp.exp(b_end))
        for d in (0, 1):
            for j in orders[d]:
                ctx_state[d] = ctx_state[d] * decs[d][j] + upds[d][:, j * HEAD_DIM:(j + 1) * HEAD_DIM]
    sf_scr[...] = ctx_state[0]
    sb_scr[...] = ctx_state[1]

    prepare(0, 0)

    def body(finish):
        def pair(i, carry):
            scan(2 * i, 0, finish)
            prepare(2 * i + 1, 1)
            scan(2 * i + 1, 1, finish)
            prepare(2 * i + 2, 0)
            return carry
        return pair

    lax.fori_loop(0, n_steps // 4, body(False), 0)
    lax.fori_loop(n_steps // 4, n_steps // 2, body(True), 0)


def _hgrn(px, pc, lb_logits, gain, *, width):
    bsz, n, _ = px.shape
    n_ctx = pc.shape[1]
    heads = width // HEAD_DIM
    hd = HEAD_DIM

    def seg(s):
        return pl.BlockSpec((None, n, hd), lambda b, h: (b, 0, s * heads + h))

    def cseg(s):
        return pl.BlockSpec((None, n_ctx, hd), lambda b, h: (b, 0, s * heads + h))

    depth1 = lb_logits.shape[1]
    slots = 2 * SCAN_BLOCKS
    assert (n // SCAN_CHUNK) % (4 * SCAN_BLOCKS * PREP_CHUNKS) == 0, "scan steps come in pairs, per half"
    assert (n_ctx // SCAN_CHUNK) % PREP_CHUNKS == 0, "the context prefix is scanned in whole blocks"
    return pl.pallas_call(
        functools.partial(_hgrn_kernel, n_chunks=n // SCAN_CHUNK, n_ctx_chunks=n_ctx // SCAN_CHUNK),
        out_shape=jax.ShapeDtypeStruct((bsz, n, width), BF16),
        grid=(bsz, heads),
        in_specs=[seg(0), seg(1), seg(2), seg(3), seg(4), cseg(0), cseg(1), cseg(2),
                  pl.BlockSpec((2, depth1, hd), lambda b, h: (0, 0, h)),
                  pl.BlockSpec((1, hd), lambda b, h: (0, h))],
        out_specs=pl.BlockSpec((None, n, hd), lambda b, h: (b, 0, h)),
        scratch_shapes=[pltpu.VMEM((n, hd), F32), pltpu.VMEM((n, hd), F32),
                        pltpu.VMEM((hd, hd), F32), pltpu.VMEM((hd, hd), F32),
                        *[pltpu.VMEM((2, slots, PREP_CHUNKS * SCAN_CHUNK, hd), BF16) for _ in range(4)],
                        pltpu.VMEM((2, slots, PREP_CHUNKS, 8, hd), F32)],
        compiler_params=_params("parallel", "parallel"),
        name="hgrn_scan",
    )(px, px, px, px, px, pc, pc, pc, lb_logits, gain.reshape(1, width))


def _dft_constants(n):
    n2 = GRID_W
    n1h = n // n2
    m2r = np.arange(n2)[None, :]
    k2 = np.arange(n2)[:, None]
    f_ang = -2.0 * np.pi * k2 * m2r / n2
    fr, fi = np.cos(f_ang), np.sin(f_ang)
    f_fwd = np.block([[fr, -fi], [fi, fr]])
    f_inv = np.block([[fr, fi], [-fi, fr]])
    k1 = np.arange(n1h)[:, None, None]
    m1 = np.arange(n1h)[None, :, None]
    m2 = np.arange(n2)[None, None, :]
    theta = -2.0 * np.pi * (k1 + 0.5) * (m1 / (2 * n1h) + m2 / (2 * n))
    parts = np.stack([np.cos(theta), np.sin(theta)], axis=0)
    parts = parts.reshape(2, n1h, n1h, n2 // OUTER_ROWS, OUTER_ROWS)
    eye = np.eye(OUTER_ROWS)
    rows = n1h * OUTER_ROWS
    fwd = np.einsum("pknsj,jm->spkjnm", parts, eye).reshape(n2 // OUTER_ROWS, 2 * rows, rows)
    inv = np.einsum("pknsj,jm->snjpkm", parts, eye).reshape(n2 // OUTER_ROWS, rows, 2 * rows)
    return fwd, inv, f_fwd, f_inv


def _filt_mlp_kernel(z_ref, w1_ref, b1_ref, w2_ref, b2_ref, w3_ref, b3_ref, fr_ref, o_ref):
    fr = fr_ref[...]
    hp = lambda a, b: jnp.dot(a, b, precision=HIGHEST, preferred_element_type=F32)
    hid = jnp.sin(fr * (hp(z_ref[...], w1_ref[...]) + b1_ref[...]))
    hid = jnp.sin(fr * (hp(hid, w2_ref[...]) + b2_ref[...]))
    o_ref[...] = jnp.sin(fr * (hp(hid, w3_ref[...]) + b3_ref[...]))


def _filt_taps_kernel(hid_ref, wo_ref, dl_ref, h_ref, s_ref, *, n):
    hid = hid_ref[...]
    hid_hi = hid.astype(BF16)
    h = _dot3(hid_hi, (hid - hid_hi.astype(F32)).astype(BF16), wo_ref[...])
    t = lax.broadcasted_iota(jnp.int32, h.shape, 0).astype(F32) * (1.0 / max(n - 1, 1))
    hw = h * jnp.exp(-t * dl_ref[...])
    h_ref[...] = hw
    s_ref[0:1, :] = jnp.sum(jnp.abs(hw), axis=0, keepdims=True)
    s_ref[1:2, :] = jnp.abs(hw[0:1, :])


def _spectrum_kernel(af_ref, ab_ref, f_ref, fl_ref, s_ref, d_ref, o_ref, *, n):
    n2 = GRID_W
    l1 = s_ref[0:1, :] + s_ref[2:3, :] - s_ref[3:4, :]
    hb0 = s_ref[4:5, :]
    norm = 1.0 / l1
    scale = 2.0 / (2 * n)
    for i in range(af_ref.shape[1]):
        def inner(a_ref):
            z = _dot3(f_ref[...], fl_ref[...], jnp.concatenate([a_ref[0, i], a_ref[1, i]], axis=0))
            return z[:n2], z[n2:]

        zfr, zfi = inner(af_ref)
        zbr, zbi = inner(ab_ref)
        o_ref[0, i] = scale * ((zfr + zbr - hb0) * norm + d_ref[...])
        o_ref[1, i] = scale * ((zfi - zbi) * norm)


def _dot3(a_hi, a_lo, x):
    x_hi = x.astype(BF16)
    x_lo = (x - x_hi.astype(F32)).astype(BF16)
    return _dot(a_hi, x_hi) + _dot(a_hi, x_lo) + _dot(a_lo, x_hi)


def _hi_lo(a):
    hi = np.asarray(a, np.float64).astype(BF16)
    lo = (a - hi.astype(np.float64)).astype(BF16)
    return jnp.asarray(hi), jnp.asarray(lo)


def _slabs(x):
    return [x[..., h * OUTER_ROWS:(h + 1) * OUTER_ROWS, :] for h in range(x.shape[-2] // OUTER_ROWS)]


def _outer_dft_kernel(*refs, high):
    g_refs, x_ref, o_ref = refs[:-2], refs[-2], refs[-1]
    n1h, _, tl = x_ref.shape
    parts = []
    for h, xs in enumerate(_slabs(x_ref[...])):
        xs = xs.reshape(n1h * OUTER_ROWS, tl)
        r = _dot3(g_refs[0][h], g_refs[1][h], xs) if high else _dot(g_refs[0][h], xs.astype(BF16))
        parts.append(r.reshape(2, n1h, OUTER_ROWS, tl))
    o_ref[...] = jnp.concatenate(parts, axis=2).astype(o_ref.dtype)


def _outer_dft(gs, x4, seg, *, c, tl):
    high = len(gs) == 2
    s, n1h, n2, _ = x4.shape
    tl = min(tl, c)
    per = c // tl
    rows = OUTER_ROWS if high else BF16_ROWS
    g_spec = pl.BlockSpec((rows // OUTER_ROWS,) + gs[0].shape[1:], lambda j, b, l: (j, 0, 0))
    return pl.pallas_call(
        functools.partial(_outer_dft_kernel, high=high),
        out_shape=jax.ShapeDtypeStruct((s, 2, n1h, n2, c), F32 if high else BF16),
        grid=(n2 // rows, s, per),
        in_specs=[g_spec] * len(gs) + [pl.BlockSpec((None, n1h, rows, tl),
                                                    lambda j, b, l: (b, 0, j, seg * per + l))],
        out_specs=pl.BlockSpec((None, 2, n1h, rows, tl), lambda j, b, l: (b, 0, 0, j, l)),
        compiler_params=_params("parallel", "parallel", "parallel"),
        name="outer_dft",
    )(*gs, x4)


def _hyena_spectra(n, width, filt, hy_d, consts):
    w1, b1, w2, b2, w3, b3, freq, w_out = filt
    gt_fwd, _, f_fwd, _ = consts
    n1h = n // GRID_W
    order = w1.shape[1]
    pos = np.arange(n, dtype=np.float64)
    t = pos / max(n - 1, 1)
    bands = (FILT_EMB - 1) // 2
    fb = np.linspace(1e-4, bands - 1, bands)
    ang = (2 * math.pi / n) * pos[:, None] * fb[None, :]
    z = np.concatenate([t[:, None], np.cos(ang), -np.sin(ang)], axis=-1)
    emb_pad = 128
    z = jnp.asarray(np.pad(z, ((0, 0), (0, emb_pad - FILT_EMB))), F32)
    w1p = jnp.pad(w1, ((0, emb_pad - FILT_EMB), (0, 0)))
    row = lambda a: a.reshape(1, -1)
    full = lambda shape: pl.BlockSpec(shape, lambda: tuple(0 for _ in shape))
    hid = pl.pallas_call(
        _filt_mlp_kernel,
        out_shape=jax.ShapeDtypeStruct((n, order), F32),
        in_specs=[full((n, emb_pad)), full((emb_pad, order)), full((1, order)), full((order, order)),
                  full((1, order)), full((order, order)), full((1, order)), full((1, order))],
        out_specs=full((n, order)),
        compiler_params=pltpu.CompilerParams(vmem_limit_bytes=VMEM_LIMIT_BYTES),
        name="filter_mlp",
    )(z, w1p, row(b1), w2, row(b2), w3, row(b3), row(freq))

    cols = 4 * width
    deltas = np.abs(np.linspace(math.log(DECAY_TARGET) / DECAY_SLOW_PCT,
                                math.log(DECAY_TARGET) / DECAY_FAST_PCT, width))
    deltas4 = jnp.asarray(np.tile(deltas, 4)[None, :], F32)
    tc = min(512, cols)
    taps, sums = pl.pallas_call(
        functools.partial(_filt_taps_kernel, n=n),
        out_shape=(jax.ShapeDtypeStruct((n, cols), F32), jax.ShapeDtypeStruct((2, cols), F32)),
        grid=(cols // tc,),
        in_specs=[pl.BlockSpec((n, order), lambda j: (0, 0)),
                  pl.BlockSpec((order, tc), lambda j: (0, j)),
                  pl.BlockSpec((1, tc), lambda j: (0, j))],
        out_specs=(pl.BlockSpec((n, tc), lambda j: (0, j)), pl.BlockSpec((2, tc), lambda j: (0, j))),
        compiler_params=_params("parallel"),
        name="filter_taps",
    )(hid, w_out, deltas4)

    a = _outer_dft(_hi_lo(gt_fwd), taps.reshape(1, n1h, GRID_W, cols), 0, c=cols, tl=1024)[0]
    s4 = sums.reshape(2, 2, 2, width)
    bwd0 = taps[0].reshape(2, 2, width)[:, 1]
    stats = jnp.stack([s4[0, :, 0], s4[1, :, 0], s4[0, :, 1], s4[1, :, 1], bwd0], axis=1)

    kb = min(SPECTRUM_BINS, n1h)

    def tap_spec(side):
        return pl.BlockSpec((2, kb, GRID_W, width), lambda k1, f: (0, k1, 0, 2 * f + side))

    f_hi, f_lo = _hi_lo(f_fwd)
    f_spec = pl.BlockSpec((2 * GRID_W, 2 * GRID_W), lambda k1, f: (0, 0))
    return pl.pallas_call(
        functools.partial(_spectrum_kernel, n=n),
        out_shape=jax.ShapeDtypeStruct((2, 2, n1h, GRID_W, width), F32),
        grid=(n1h // kb, 2),
        in_specs=[tap_spec(0), tap_spec(1), f_spec, f_spec,
                  pl.BlockSpec((None, 5, width), lambda k1, f: (f, 0, 0)),
                  pl.BlockSpec((None, 1, width), lambda k1, f: (f, 0, 0))],
        out_specs=pl.BlockSpec((None, 2, kb, GRID_W, width), lambda k1, f: (f, 0, k1, 0, 0)),
        compiler_params=_params("parallel", "parallel"),
        name="filter_spectrum",
    )(a, a, f_hi, f_lo, stats, hy_d.reshape(2, 1, width))


def _inner_conv_kernel(a_ref, f_ref, fi_ref, h_ref, o_ref):
    n2 = GRID_W
    for i in range(a_ref.shape[1]):
        z = _dot(f_ref[...], jnp.concatenate([a_ref[0, i], a_ref[1, i]], axis=0))
        zr, zi = z[:n2], z[n2:]
        hr, hi = h_ref[0, i], h_ref[1, i]
        stacked = jnp.concatenate([zr * hr - zi * hi, zr * hi + zi * hr], axis=0).astype(BF16)
        w = _dot(fi_ref[...], stacked).astype(BF16)
        o_ref[0, i] = w[:n2]
        o_ref[1, i] = w[n2:]


def _inner_conv(a, f_fwd, f_inv, spec, filt):
    bsz, _, n1h, n2, c = a.shape
    kb = min(INNER_BINS, n1h)
    return pl.pallas_call(
        _inner_conv_kernel,
        out_shape=jax.ShapeDtypeStruct(a.shape, BF16),
        grid=(n1h // kb, bsz),
        in_specs=[pl.BlockSpec((None, 2, kb, n2, c), lambda k1, b: (b, 0, k1, 0, 0)),
                  pl.BlockSpec((2 * n2, 2 * n2), lambda k1, b: (0, 0)),
                  pl.BlockSpec((2 * n2, 2 * n2), lambda k1, b: (0, 0)),
                  pl.BlockSpec((None, 2, kb, n2, c), lambda k1, b: (filt, 0, k1, 0, 0))],
        out_specs=pl.BlockSpec((None, 2, kb, n2, c), lambda k1, b: (b, 0, k1, 0, 0)),
        compiler_params=_params("parallel", "arbitrary"),
        name="inner_conv",
    )(a, f_fwd, f_inv, spec)


def _outer_idft_gate_kernel(gi_ref, g_ref, b_ref, x_ref, o_ref, *, again):
    n1h, _, tl = x_ref.shape
    parts = []
    for h, (bs, xs) in enumerate(zip(_slabs(b_ref[...].astype(F32)), _slabs(x_ref[...]))):
        stacked = bs.reshape(2 * n1h * OUTER_ROWS, tl).astype(BF16)
        y = _dot(gi_ref[h], stacked) * xs.reshape(n1h * OUTER_ROWS, tl)
        if again:
            parts.append(_dot(g_ref[h], y.astype(BF16)).reshape(2, n1h, OUTER_ROWS, tl))
        else:
            parts.append(y.reshape(n1h, OUTER_ROWS, tl))
    o_ref[...] = jnp.concatenate(parts, axis=-2).astype(o_ref.dtype)


def _outer_idft_gate(g_inv, g_fwd, b, px4, seg, *, again, tl):
    bsz, _, n1h, n2, c = b.shape
    tl = min(tl, c)
    per = c // tl
    rows = BF16_ROWS
    if again:
        out_shape = jax.ShapeDtypeStruct((bsz, 2, n1h, n2, c), BF16)
        out_spec = pl.BlockSpec((None, 2, n1h, rows, tl), lambda j, bb, l: (bb, 0, 0, j, l))
    else:
        out_shape = jax.ShapeDtypeStruct((bsz, n1h, n2, c), BF16)
        out_spec = pl.BlockSpec((None, n1h, rows, tl), lambda j, bb, l: (bb, 0, j, l))
    return pl.pallas_call(
        functools.partial(_outer_idft_gate_kernel, again=again),
        out_shape=out_shape,
        grid=(n2 // rows, bsz, per),
        in_specs=[pl.BlockSpec((rows // OUTER_ROWS,) + g_inv.shape[1:], lambda j, bb, l: (j, 0, 0)),
                  pl.BlockSpec((rows // OUTER_ROWS,) + g_fwd.shape[1:], lambda j, bb, l: (j, 0, 0)),
                  pl.BlockSpec((None, 2, n1h, rows, tl), lambda j, bb, l: (bb, 0, 0, j, l)),
                  pl.BlockSpec((None, n1h, rows, tl), lambda j, bb, l: (bb, 0, j, seg * per + l))],
        out_specs=out_spec,
        compiler_params=_params("parallel", "parallel", "parallel"),
        name="outer_idft_gate",
    )(g_inv, g_fwd, b, px4)


def _hyena_mix(px, spec, consts, *, bsz, n, width):
    g_fwd, g_inv, f_fwd, f_inv = consts
    g_fwd_b, g_inv_b = jnp.asarray(g_fwd, BF16), jnp.asarray(g_inv, BF16)
    f_fwd_b, f_inv_b = jnp.asarray(f_fwd, BF16), jnp.asarray(f_inv, BF16)
    n1h = n // GRID_W
    px4 = px.reshape(bsz, n1h, GRID_W, -1)
    tl = width
    a = _outer_dft((g_fwd_b,), px4, 5, c=width, tl=tl)
    a = _inner_conv(a, f_fwd_b, f_inv_b, spec, 0)
    a = _outer_idft_gate(g_inv_b, g_fwd_b, a, px4, 6, again=True, tl=tl)
    a = _inner_conv(a, f_fwd_b, f_inv_b, spec, 1)
    hy = _outer_idft_gate(g_inv_b, g_fwd_b, a, px4, 7, again=False, tl=tl)
    return hy.reshape(bsz * n, width)


def _out_proj_kernel(x_ref, hg_ref, hy_ref, wa_ref, wb_ref, gt_ref, o_ref):
    mix = _dot(hg_ref[...], wa_ref[...]) + _dot(hy_ref[...], wb_ref[...])
    o_ref[...] = x_ref[...] + gt_ref[0] * mix


def _out_proj(x, hg, hy, w, gate, *, tokens_per_mod, tm):
    t, d = x.shape
    wdt = hg.shape[1]
    tm = min(tm, tokens_per_mod)
    return pl.pallas_call(
        _out_proj_kernel,
        out_shape=jax.ShapeDtypeStruct((t, d), F32),
        grid=(t // tm,),
        in_specs=[pl.BlockSpec((tm, d), lambda i: (i, 0)),
                  pl.BlockSpec((tm, wdt), lambda i: (i, 0)),
                  pl.BlockSpec((tm, wdt), lambda i: (i, 0)),
                  pl.BlockSpec((wdt, d), lambda i: (0, 0)),
                  pl.BlockSpec((wdt, d), lambda i: (1, 0)),
                  pl.BlockSpec((1, 1, d), lambda i: ((i * tm) // tokens_per_mod, 0, 0))],
        out_specs=pl.BlockSpec((tm, d), lambda i: (i, 0)),
        compiler_params=_params("parallel"),
        name="out_proj",
    )(x, hg, hy, w, w, gate)


def _tiles(d_ff):
    assert d_ff % 128 == 0, "a ragged last hidden tile still has to be lane-tile aligned"
    return TOKEN_TILE, 2 * TOKEN_TILE, min(HIDDEN_TILE, d_ff)


def kernel(x, c, ctx, c_ctx, ada_w, ada_b, norm_ffn1, ffn1_w1, ffn1_w3, ffn1_w2, norm_mix, w_in, hg_lb_logits, hg_norm, hy_conv_w, hy_conv_b, filt_w1, filt_b1, filt_w2, filt_b2, filt_w3, filt_b3, filt_freq, filt_w_out, hy_d, w_out, norm_ffn2, ffn2_w1, ffn2_w3, ffn2_w2, final_norm):
    bsz, n, d = x.shape
    n_ctx = ctx.shape[1]
    depth = ada_w.shape[0]
    assert depth == 1, "single-layer block"
    width = d // 2
    n_seg = w_in.shape[2] // width
    assert n_seg == 8 and n % GRID_W == 0 and n_ctx % SCAN_CHUNK == 0 and width % HEAD_DIM == 0
    tm, tm_wide, tf = _tiles(ffn1_w1.shape[2])
    tf_mix = max(tf // 2, 128)
    l = 0

    rows = -(-(bsz + 1) // 8) * 8
    cs = jnp.concatenate([c, c_ctx[None, :], jnp.zeros((rows - bsz - 1, d), F32)], axis=0)
    mods = _ada(cs, ada_w[l], ada_b[l]).reshape(rows, N_MOD, d)
    mx = [mods[:bsz, i][:, None, :] for i in range(N_MOD)]
    mc = [mods[bsz:bsz + 1, i][:, None, :] for i in range(N_MOD)]

    bf = _to_bf16
    w1a, w3a, w2a = bf(ffn1_w1[l]), bf(ffn1_w3[l]), bf(ffn1_w2[l])
    xt = x.reshape(bsz * n, d)
    yt = ctx.reshape(bsz * n_ctx, d)

    xt = _ffn(xt, mx[0], mx[1], mx[2], norm_ffn1[l], w1a, w3a, w2a, final_norm,
              tokens_per_mod=n, final_norm=False, tm=tm, tf=tf)
    yt = _ffn(yt, mc[0], mc[1], mc[2], norm_ffn1[l], w1a, w3a, w2a, final_norm,
              tokens_per_mod=bsz * n_ctx, final_norm=False, tm=tm, tf=tf)

    w_in_b = bf(w_in[l])
    pc = _proj(yt, mc[3], mc[4], norm_mix[l], w_in_b, hy_conv_w[l], hy_conv_b[l], n_seg=3, seg_w=width,
               conv_from=n_seg, tokens_per_mod=bsz * n_ctx, tm=tm)
    px = _proj(xt, mx[3], mx[4], norm_mix[l], w_in_b, hy_conv_w[l], hy_conv_b[l], n_seg=n_seg, seg_w=width,
               conv_from=5, tokens_per_mod=n, tm=tm_wide)
    cols = n_seg * width

    hg = _hgrn(px.reshape(bsz, n, cols), pc.reshape(bsz, n_ctx, 3 * width), hg_lb_logits, hg_norm[l], width=width)

    consts = _dft_constants(n)
    filt = (filt_w1[l], filt_b1[l], filt_w2[l], filt_b2[l], filt_w3[l], filt_b3[l], filt_freq[l], filt_w_out[l])
    spec = _hyena_spectra(n, width, filt, hy_d[l], consts)
    hy = _hyena_mix(px, spec, consts, bsz=bsz, n=n, width=width)

    out = _ffn(xt, mx[6], mx[7], mx[8], norm_ffn2[l], bf(ffn2_w1[l]), bf(ffn2_w3[l]), bf(ffn2_w2[l]), final_norm,
               tokens_per_mod=n, final_norm=True, tm=tm, tf=tf_mix,
               mix=(hg.reshape(bsz * n, width), hy, bf(w_out[l]), mx[5]))
    return out.reshape(bsz, n, d)
```

```python
import functools
import math

import numpy as np
import jax
import jax.numpy as jnp
from jax import lax
from jax.experimental import pallas as pl
from jax.experimental.pallas import tpu as pltpu

F32 = jnp.float32
BF16 = jnp.bfloat16
EPS = 1e-6
N_MOD = 9
HEAD_DIM = 128
GRID_W = 64
CAST_ROWS = 256
TOKEN_TILE = 512
HIDDEN_TILE = 1024
SCAN_CHUNK = 64
PREP_CHUNKS = 4
SCAN_BLOCKS = 4
OUTER_ROWS = 8
BF16_ROWS = 16
INNER_BINS = 16
SPECTRUM_BINS = 8
FILT_EMB = 33
DECAY_TARGET = 1e-2
DECAY_FAST_PCT = 0.3
DECAY_SLOW_PCT = 1.5
VMEM_LIMIT_BYTES = 56 * 1024 * 1024
HIGHEST = lax.Precision.HIGHEST


def _params(*semantics):
    return pltpu.CompilerParams(dimension_semantics=semantics, vmem_limit_bytes=VMEM_LIMIT_BYTES)


def _dot(a, b):
    return jnp.dot(a, b, preferred_element_type=F32)


def _silu(a):
    return a * jax.nn.sigmoid(a)


def _cast_kernel(x_ref, o_ref):
    o_ref[...] = x_ref[...].astype(BF16)


def _to_bf16(w):
    r, c = w.shape
    rb = CAST_ROWS if r % CAST_ROWS == 0 else r
    return pl.pallas_call(
        _cast_kernel,
        out_shape=jax.ShapeDtypeStruct((r, c), BF16),
        grid=(r // rb,),
        in_specs=[pl.BlockSpec((rb, c), lambda i: (i, 0))],
        out_specs=pl.BlockSpec((rb, c), lambda i: (i, 0)),
        compiler_params=_params("parallel"),
        name="to_bf16",
    )(w)


def _ada_kernel(c_ref, w_ref, b_ref, o_ref):
    h = _silu(c_ref[...]).astype(BF16)
    o_ref[...] = _dot(h, w_ref[...].astype(BF16)) + b_ref[...]


def _ada(cs, w, b):
    rows, d = cs.shape
    n = w.shape[1]
    tn = d // 2
    return pl.pallas_call(
        _ada_kernel,
        out_shape=jax.ShapeDtypeStruct((rows, n), F32),
        grid=(n // tn,),
        in_specs=[pl.BlockSpec((rows, d), lambda j: (0, 0)),
                  pl.BlockSpec((d, tn), lambda j: (0, j)),
                  pl.BlockSpec((1, tn), lambda j: (0, j))],
        out_specs=pl.BlockSpec((rows, tn), lambda j: (0, j)),
        compiler_params=_params("arbitrary"),
        name="ada_mod",
    )(cs, w, b.reshape(1, n))


def _norm_mod(x, gain, shift, scale):
    y = x * lax.rsqrt(jnp.mean(x * x, axis=-1, keepdims=True) + EPS)
    return (y * (gain * (1.0 + scale)) + shift).astype(BF16)


def _ffn_kernel(x_ref, sh_ref, sc_ref, gt_ref, g_ref, w1_ref, w3_ref, w2_ref, fin_ref, *rest,
                final_norm, n_hidden, last_cols):
    if len(rest) == 2:
        (o_ref, h_scr), mix, res_ref = rest, None, x_ref
    else:
        *mix, o_ref, h_scr, res_ref = rest
    j = pl.program_id(1)
    tf = w1_ref.shape[1]

    def hidden_tile(cols=tf):
        h = h_scr[...]
        a = _dot(h, w1_ref[:, :cols])
        b = _dot(h, w3_ref[:, :cols])
        return _dot((_silu(a) * b).astype(BF16), w2_ref[:cols, :])

    def finish(acc):
        out = res_ref[...] + (0.5 * gt_ref[0]) * acc
        if final_norm:
            out = out * lax.rsqrt(jnp.mean(out * out, axis=-1, keepdims=True) + EPS) * fin_ref[...]
        return out

    last = n_hidden - 1

    @pl.when(j == 0)
    def _():
        x = x_ref[...]
        if mix is not None:
            hg_ref, hy_ref, wa_ref, wb_ref, gm_ref = mix
            x = x + gm_ref[0] * (_dot(hg_ref[...], wa_ref[...]) + _dot(hy_ref[...], wb_ref[...]))
            res_ref[...] = x
        h_scr[...] = _norm_mod(x, g_ref[...], sh_ref[0], sc_ref[0])
        o_ref[...] = finish(hidden_tile(last_cols)) if last == 0 else hidden_tile(last_cols)

    @pl.when((j > 0) & (j < last))
    def _():
        o_ref[...] += hidden_tile()

    if last > 0:
        @pl.when(j == last)
        def _():
            o_ref[...] = finish(o_ref[...] + hidden_tile())


def _ffn(x, shift, scale, gate, gain, w1, w3, w2, fin, *, tokens_per_mod, final_norm, tm, tf, mix=None):
    t, d = x.shape
    f = w1.shape[1]
    tm = min(tm, tokens_per_mod)
    mod_spec = pl.BlockSpec((1, 1, d), lambda i, j: ((i * tm) // tokens_per_mod, 0, 0))
    vec_spec = pl.BlockSpec((1, d), lambda i, j: (0, 0))
    n_hidden = pl.cdiv(f, tf)
    hidden = lambda j: (j + n_hidden - 1) % n_hidden
    in_specs = [pl.BlockSpec((tm, d), lambda i, j: (i, 0)),
                mod_spec, mod_spec, mod_spec, vec_spec,
                pl.BlockSpec((d, tf), lambda i, j: (0, hidden(j))),
                pl.BlockSpec((d, tf), lambda i, j: (0, hidden(j))),
                pl.BlockSpec((tf, d), lambda i, j: (hidden(j), 0)),
                vec_spec]
    args = [x, shift, scale, gate, gain.reshape(1, d), w1, w3, w2, fin.reshape(1, d)]
    scratch = [pltpu.VMEM((tm, d), BF16)]
    if mix is not None:
        hg, hy, w_out, gate_mix = mix
        wdt = hg.shape[1]
        half = lambda part: pl.BlockSpec((wdt, d), lambda i, j: (part, 0), pipeline_mode=pl.Buffered(1))
        in_specs += [pl.BlockSpec((tm, wdt), lambda i, j: (i, 0)), pl.BlockSpec((tm, wdt), lambda i, j: (i, 0)),
                     half(0), half(1), mod_spec]
        args += [hg, hy, w_out, w_out, gate_mix]
        scratch.append(pltpu.VMEM((tm, d), F32))
    return pl.pallas_call(
        functools.partial(_ffn_kernel, final_norm=final_norm, n_hidden=n_hidden, last_cols=f - (n_hidden - 1) * tf),
        out_shape=jax.ShapeDtypeStruct((t, d), F32),
        grid=(t // tm, n_hidden),
        in_specs=in_specs,
        out_specs=pl.BlockSpec((tm, d), lambda i, j: (i, 0)),
        scratch_shapes=scratch,
        compiler_params=_params("parallel", "arbitrary"),
        name="swiglu_ffn",
    )(*args)


def _proj_kernel(x_ref, sh_ref, sc_ref, g_ref, w_ref, cw_ref, cb_ref, o_ref, h_scr, *, conv_from):
    j = pl.program_id(1)

    def segment():
        return _dot(h_scr[...], w_ref[...])

    @pl.when(j == 0)
    def _():
        h_scr[...] = _norm_mod(x_ref[...], g_ref[...], sh_ref[0], sc_ref[0])
        o_ref[...] = segment()

    @pl.when((j > 0) & (j < conv_from))
    def _():
        o_ref[...] = segment()

    @pl.when(j >= conv_from)
    def _():
        p = segment()
        tm, w = p.shape
        rows3 = lambda t: t.reshape(tm // GRID_W, GRID_W, w)
        y = (rows3(pltpu.roll(p, 1, axis=0)) * cw_ref[0] + rows3(p) * cw_ref[1]
             + rows3(pltpu.roll(p, tm - 1, axis=0)) * cw_ref[2] + cb_ref[...])
        o_ref[...] = y.reshape(tm, w)


def _proj(x, shift, scale, gain, w, conv_w, conv_b, *, n_seg, seg_w, conv_from, tokens_per_mod, tm):
    t, d = x.shape
    tm = min(tm, tokens_per_mod)
    mod_spec = pl.BlockSpec((1, 1, d), lambda i, j: ((i * tm) // tokens_per_mod, 0, 0))
    conv_idx = lambda i, j: (0, jnp.maximum(j - conv_from, 0))
    pos = jnp.arange(GRID_W)[None, :, None]
    edge = jnp.stack([pos > 0, pos >= 0, pos < GRID_W - 1], axis=0)[:, 0]
    conv_w = jnp.where(edge, conv_w[:, None, :], 0.0)
    return pl.pallas_call(
        functools.partial(_proj_kernel, conv_from=conv_from),
        out_shape=jax.ShapeDtypeStruct((t, n_seg * seg_w), F32),
        grid=(t // tm, n_seg),
        in_specs=[pl.BlockSpec((tm, d), lambda i, j: (i, 0)),
                  mod_spec, mod_spec,
                  pl.BlockSpec((1, d), lambda i, j: (0, 0)),
                  pl.BlockSpec((d, seg_w), lambda i, j: (0, j)),
                  pl.BlockSpec((3, GRID_W, seg_w), lambda i, j: (0, 0, jnp.maximum(j - conv_from, 0))),
                  pl.BlockSpec((1, seg_w), conv_idx)],
        out_specs=pl.BlockSpec((tm, seg_w), lambda i, j: (i, j)),
        scratch_shapes=[pltpu.VMEM((tm, d), BF16)],
        compiler_params=_params("parallel", "arbitrary"),
        name="in_proj",
    )(x, shift, scale, gain.reshape(1, d), w, conv_w, conv_b.reshape(1, -1))


def _split3(g):
    g1 = g.astype(BF16)
    r1 = g - g1.astype(F32)
    g2 = r1.astype(BF16)
    return g1, g2, (r1 - g2.astype(F32)).astype(BF16)


def _split_dot(tri, g):
    g1, g2, g3 = _split3(g)
    return _dot(tri, g1) + _dot(tri, g2) + _dot(tri, g3)


def _gates(a, lb):
    f = lb + (1.0 - lb) * jax.nn.sigmoid(a)
    return 1.0 - f, jnp.log(f)


def _tn_dot(a, b):
    return lax.dot_general(a, b, (((0,), (0,)), ((), ())), preferred_element_type=F32)


def _nt_dot(a, b):
    return lax.dot_general(a, b, (((1,), (1,)), ((), ())), preferred_element_type=F32)


def _hgrn_kernel(af_ref, ab_ref, v_ref, q_ref, gate_ref, caf_ref, cab_ref, cv_ref, lbl_ref, gain_ref,
                 o_ref, of_scr, ob_scr, sf_scr, sb_scr, qin_scr, qt_scr, kt_scr, kd_scr, dec_scr,
                 *, n_chunks, n_ctx_chunks):
    c = SCAN_CHUNK
    mid = c // 2
    rows_p = PREP_CHUNKS * c
    prow = lax.broadcasted_iota(jnp.int32, (rows_p, rows_p), 0)
    pcol = lax.broadcasted_iota(jnp.int32, (rows_p, rows_p), 1)
    same = (prow // c) == (pcol // c)
    tri_fwd_p = jnp.where(same & (prow >= pcol), 1.0, 0.0).astype(BF16)
    tri_bwd_p = jnp.where(same & (pcol >= prow), 1.0, 0.0).astype(BF16)

    def lower_bound(direction):
        lg = lbl_ref[direction]
        ex = jnp.exp(lg - jnp.max(lg, axis=0, keepdims=True))
        return ex[0:1, :] / jnp.sum(ex, axis=0, keepdims=True)

    lb_f = lower_bound(0)
    lb_b = lower_bound(1)
    q_scale = HEAD_DIM ** -0.5

    n_blocks = n_chunks // PREP_CHUNKS
    n_steps = n_blocks // SCAN_BLOCKS

    def step_blocks(it):
        fwd = [(0, it * SCAN_BLOCKS + u) for u in range(SCAN_BLOCKS)]
        return fwd + [(1, n_blocks - 1 - bi) for _, bi in fwd]

    def prepare(it, s):
        it = jnp.minimum(it, n_steps - 1)
        blocks = step_blocks(it)
        los = [pl.multiple_of(bi * rows_p, rows_p) for _, bi in blocks]
        gates = [_gates((af_ref, ab_ref)[d][pl.ds(lo, rows_p), :], (lb_f, lb_b)[d]) for (d, _), lo in zip(blocks, los)]
        sums = {}
        for d, tri_p in ((0, tri_fwd_p), (1, tri_bwd_p)):
            slots_d = [slot for slot, (dd, _) in enumerate(blocks) if dd == d]
            wide = _dot(tri_p, jnp.concatenate([p for slot in slots_d for p in _split3(gates[slot][1])], axis=1))
            for n_s, slot in enumerate(slots_d):
                parts = [wide[:, (3 * n_s + m) * HEAD_DIM:(3 * n_s + m + 1) * HEAD_DIM] for m in range(3)]
                sums[slot] = parts[0] + parts[1] + parts[2]
        for slot, ((d, _), lo) in enumerate(zip(blocks, los)):
            end_row = (c - 1, 0)[d]
            k = gates[slot][0]
            b = sums[slot].reshape(PREP_CHUNKS, c, HEAD_DIM)
            b_mid = b[:, mid:mid + 1, :]
            b_end = b[:, end_row:end_row + 1, :]
            q_t = (q_ref[pl.ds(lo, rows_p), :] * q_scale).reshape(PREP_CHUNKS, c, HEAD_DIM) * jnp.exp(b - b_mid)
            k_t = k.reshape(PREP_CHUNKS, c, HEAD_DIM) * jnp.exp(b_mid - b)
            flat = lambda t: t.reshape(rows_p, HEAD_DIM).astype(BF16)
            qt_scr[s, slot] = flat(q_t)
            kt_scr[s, slot] = flat(k_t)
            qin_scr[s, slot] = flat(q_t * jnp.exp(b_mid))
            kd_scr[s, slot] = flat(k_t * jnp.exp(b_end - b_mid))
            dec_scr[s, slot] = jnp.broadcast_to(jnp.exp(b_end), (PREP_CHUNKS, 8, HEAD_DIM))

    zeros_blk = jnp.zeros((c, HEAD_DIM), BF16)

    def block_diag(t):
        return jnp.concatenate(
            [jnp.concatenate([t[j * c:(j + 1) * c] if m == j else zeros_blk for m in range(PREP_CHUNKS)], axis=1)
             for j in range(PREP_CHUNKS)], axis=0)

    masks = (same & (prow >= pcol), same & (pcol >= prow))
    st_refs = (sf_scr, sb_scr)
    out_refs = (of_scr, ob_scr)
    orders = (tuple(range(PREP_CHUNKS)), tuple(range(PREP_CHUNKS - 1, -1, -1)))

    gain = gain_ref[...]

    def scan(it, s, finish):
        streams = step_blocks(it)
        los = [pl.multiple_of(bi * rows_p, rows_p) for _, bi in streams]
        vbs = [v_ref[pl.ds(lo, rows_p), :].astype(BF16) for lo in los]
        scores = [_nt_dot(qt_scr[s, slot], kt_scr[s, slot]) for slot in range(len(streams))]
        upds = [_tn_dot(vb, block_diag(kd_scr[s, slot])) for slot, vb in enumerate(vbs)]
        outs = [_dot(jnp.where(masks[d], sc, 0.0).astype(BF16), vb) for (d, _), sc, vb in zip(streams, scores, vbs)]
        seen = [[None] * PREP_CHUNKS for _ in streams]
        for d in (0, 1):
            st = st_refs[d][...]
            for slot, (ds_, _) in enumerate(streams):
                if ds_ != d:
                    continue
                for j in orders[d]:
                    seen[slot][j] = st.astype(BF16)
                    st = st * dec_scr[s, slot, j][0:1, :] + upds[slot][:, j * HEAD_DIM:(j + 1) * HEAD_DIM]
            st_refs[d][...] = st
        for slot, ((d, _), lo) in enumerate(zip(streams, los)):
            q_in = qin_scr[s, slot]
            carried = jnp.concatenate([_nt_dot(q_in[j * c:(j + 1) * c], seen[slot][j]) for j in range(PREP_CHUNKS)],
                                      axis=0)
            rows = pl.ds(lo, rows_p)
            if finish:
                o = outs[slot] + carried + out_refs[1 - d][rows, :]
                o = o * lax.rsqrt(jnp.mean(o * o, axis=-1, keepdims=True) + EPS) * gain
                o_ref[rows, :] = (o * _silu(gate_ref[rows, :])).astype(o_ref.dtype)
            else:
                out_refs[d][rows, :] = outs[slot] + carried

    n_ctx_blocks = n_ctx_chunks // PREP_CHUNKS
    ctx_state = [jnp.zeros((HEAD_DIM, HEAD_DIM), F32), jnp.zeros((HEAD_DIM, HEAD_DIM), F32)]
    for step in range(n_ctx_blocks):
        upds, decs = [], []
        for d, bi in ((0, step), (1, n_ctx_blocks - 1 - step)):
            a_ref, lb, tri_p, end_row = ((caf_ref, lb_f, tri_fwd_p, c - 1), (cab_ref, lb_b, tri_bwd_p, 0))[d]
            rows_c = pl.ds(bi * rows_p, rows_p)
            k, g = _gates(a_ref[rows_c, :], lb)
            b = _split_dot(tri_p, g).reshape(PREP_CHUNKS, c, HEAD_DIM)
            b_end = b[:, end_row:end_row + 1, :]
            k_d = (k.reshape(PREP_CHUNKS, c, HEAD_DIM) * jnp.exp(b_end - b)).reshape(rows_p, HEAD_DIM).astype(BF16)
            upds.append(_tn_dot(cv_ref[rows_c, :].astype(BF16), block_diag(k_d)))
            decs.append(jnp.exp(b_end))
        for d in (0, 1):
            for j in orders[d]:
                ctx_state[d] = ctx_state[d] * decs[d][j] + upds[d][:, j * HEAD_DIM:(j + 1) * HEAD_DIM]
    sf_scr[...] = ctx_state[0]
    sb_scr[...] = ctx_state[1]

    prepare(0, 0)

    def body(finish):
        def pair(i, carry):
            scan(2 * i, 0, finish)
            prepare(2 * i + 1, 1)
            scan(2 * i + 1, 1, finish)
            prepare(2 * i + 2, 0)
            return carry
        return pair

    lax.fori_loop(0, n_steps // 4, body(False), 0)
    lax.fori_loop(n_steps // 4, n_steps // 2, body(True), 0)


def _hgrn(px, pc, lb_logits, gain, *, width):
    bsz, n, _ = px.shape
    n_ctx = pc.shape[1]
    heads = width // HEAD_DIM
    hd = HEAD_DIM

    def seg(s):
        return pl.BlockSpec((None, n, hd), lambda b, h: (b, 0, s * heads + h))

    def cseg(s):
        return pl.BlockSpec((None, n_ctx, hd), lambda b, h: (b, 0, s * heads + h))

    depth1 = lb_logits.shape[1]
    slots = 2 * SCAN_BLOCKS
    assert (n // SCAN_CHUNK) % (4 * SCAN_BLOCKS * PREP_CHUNKS) == 0, "scan steps come in pairs, per half"
    assert (n_ctx // SCAN_CHUNK) % PREP_CHUNKS == 0, "the context prefix is scanned in whole blocks"
    return pl.pallas_call(
        functools.partial(_hgrn_kernel, n_chunks=n // SCAN_CHUNK, n_ctx_chunks=n_ctx // SCAN_CHUNK),
        out_shape=jax.ShapeDtypeStruct((bsz, n, width), BF16),
        grid=(bsz, heads),
        in_specs=[seg(0), seg(1), seg(2), seg(3), seg(4), cseg(0), cseg(1), cseg(2),
                  pl.BlockSpec((2, depth1, hd), lambda b, h: (0, 0, h)),
                  pl.BlockSpec((1, hd), lambda b, h: (0, h))],
        out_specs=pl.BlockSpec((None, n, hd), lambda b, h: (b, 0, h)),
        scratch_shapes=[pltpu.VMEM((n, hd), F32), pltpu.VMEM((n, hd), F32),
                        pltpu.VMEM((hd, hd), F32), pltpu.VMEM((hd, hd), F32),
                        *[pltpu.VMEM((2, slots, PREP_CHUNKS * SCAN_CHUNK, hd), BF16) for _ in range(4)],
                        pltpu.VMEM((2, slots, PREP_CHUNKS, 8, hd), F32)],
        compiler_params=_params("parallel", "parallel"),
        name="hgrn_scan",
    )(px, px, px, px, px, pc, pc, pc, lb_logits, gain.reshape(1, width))


def _dft_constants(n):
    n2 = GRID_W
    n1h = n // n2
    m2r = np.arange(n2)[None, :]
    k2 = np.arange(n2)[:, None]
    f_ang = -2.0 * np.pi * k2 * m2r / n2
    fr, fi = np.cos(f_ang), np.sin(f_ang)
    f_fwd = np.block([[fr, -fi], [fi, fr]])
    f_inv = np.block([[fr, fi], [-fi, fr]])
    k1 = np.arange(n1h)[:, None, None]
    m1 = np.arange(n1h)[None, :, None]
    m2 = np.arange(n2)[None, None, :]
    theta = -2.0 * np.pi * (k1 + 0.5) * (m1 / (2 * n1h) + m2 / (2 * n))
    parts = np.stack([np.cos(theta), np.sin(theta)], axis=0)
    parts = parts.reshape(2, n1h, n1h, n2 // OUTER_ROWS, OUTER_ROWS)
    eye = np.eye(OUTER_ROWS)
    rows = n1h * OUTER_ROWS
    fwd = np.einsum("pknsj,jm->spkjnm", parts, eye).reshape(n2 // OUTER_ROWS, 2 * rows, rows)
    inv = np.einsum("pknsj,jm->snjpkm", parts, eye).reshape(n2 // OUTER_ROWS, rows, 2 * rows)
    return fwd, inv, f_fwd, f_inv


def _filt_mlp_kernel(z_ref, w1_ref, b1_ref, w2_ref, b2_ref, w3_ref, b3_ref, fr_ref, o_ref):
    fr = fr_ref[...]
    hp = lambda a, b: jnp.dot(a, b, precision=HIGHEST, preferred_element_type=F32)
    hid = jnp.sin(fr * (hp(z_ref[...], w1_ref[...]) + b1_ref[...]))
    hid = jnp.sin(fr * (hp(hid, w2_ref[...]) + b2_ref[...]))
    o_ref[...] = jnp.sin(fr * (hp(hid, w3_ref[...]) + b3_ref[...]))


def _filt_taps_kernel(hid_ref, wo_ref, dl_ref, h_ref, s_ref, *, n):
    hid = hid_ref[...]
    hid_hi = hid.astype(BF16)
    h = _dot3(hid_hi, (hid - hid_hi.astype(F32)).astype(BF16), wo_ref[...])
    t = lax.broadcasted_iota(jnp.int32, h.shape, 0).astype(F32) * (1.0 / max(n - 1, 1))
    hw = h * jnp.exp(-t * dl_ref[...])
    h_ref[...] = hw
    s_ref[0:1, :] = jnp.sum(jnp.abs(hw), axis=0, keepdims=True)
    s_ref[1:2, :] = jnp.abs(hw[0:1, :])


def _spectrum_kernel(af_ref, ab_ref, f_ref, fl_ref, s_ref, d_ref, o_ref, *, n):
    n2 = GRID_W
    l1 = s_ref[0:1, :] + s_ref[2:3, :] - s_ref[3:4, :]
    hb0 = s_ref[4:5, :]
    norm = 1.0 / l1
    scale = 2.0 / (2 * n)
    for i in range(af_ref.shape[1]):
        def inner(a_ref):
            z = _dot3(f_ref[...], fl_ref[...], jnp.concatenate([a_ref[0, i], a_ref[1, i]], axis=0))
            return z[:n2], z[n2:]

        zfr, zfi = inner(af_ref)
        zbr, zbi = inner(ab_ref)
        o_ref[0, i] = scale * ((zfr + zbr - hb0) * norm + d_ref[...])
        o_ref[1, i] = scale * ((zfi - zbi) * norm)


def _dot3(a_hi, a_lo, x):
    x_hi = x.astype(BF16)
    x_lo = (x - x_hi.astype(F32)).astype(BF16)
    return _dot(a_hi, x_hi) + _dot(a_hi, x_lo) + _dot(a_lo, x_hi)


def _hi_lo(a):
    hi = np.asarray(a, np.float64).astype(BF16)
    lo = (a - hi.astype(np.float64)).astype(BF16)
    return jnp.asarray(hi), jnp.asarray(lo)


def _slabs(x):
    return [x[..., h * OUTER_ROWS:(h + 1) * OUTER_ROWS, :] for h in range(x.shape[-2] // OUTER_ROWS)]


def _outer_dft_kernel(*refs, high):
    g_refs, x_ref, o_ref = refs[:-2], refs[-2], refs[-1]
    n1h, _, tl = x_ref.shape
    parts = []
    for h, xs in enumerate(_slabs(x_ref[...])):
        xs = xs.reshape(n1h * OUTER_ROWS, tl)
        r = _dot3(g_refs[0][h], g_refs[1][h], xs) if high else _dot(g_refs[0][h], xs.astype(BF16))
        parts.append(r.reshape(2, n1h, OUTER_ROWS, tl))
    o_ref[...] = jnp.concatenate(parts, axis=2).astype(o_ref.dtype)


def _outer_dft(gs, x4, seg, *, c, tl):
    high = len(gs) == 2
    s, n1h, n2, _ = x4.shape
    tl = min(tl, c)
    per = c // tl
    rows = OUTER_ROWS if high else BF16_ROWS
    g_spec = pl.BlockSpec((rows // OUTER_ROWS,) + gs[0].shape[1:], lambda j, b, l: (j, 0, 0))
    return pl.pallas_call(
        functools.partial(_outer_dft_kernel, high=high),
        out_shape=jax.ShapeDtypeStruct((s, 2, n1h, n2, c), F32 if high else BF16),
        grid=(n2 // rows, s, per),
        in_specs=[g_spec] * len(gs) + [pl.BlockSpec((None, n1h, rows, tl),
                                                    lambda j, b, l: (b, 0, j, seg * per + l))],
        out_specs=pl.BlockSpec((None, 2, n1h, rows, tl), lambda j, b, l: (b, 0, 0, j, l)),
        compiler_params=_params("parallel", "parallel", "parallel"),
        name="outer_dft",
    )(*gs, x4)


def _hyena_spectra(n, width, filt, hy_d, consts):
    w1, b1, w2, b2, w3, b3, freq, w_out = filt
    gt_fwd, _, f_fwd, _ = consts
    n1h = n // GRID_W
    order = w1.shape[1]
    pos = np.arange(n, dtype=np.float64)
    t = pos / max(n - 1, 1)
    bands = (FILT_EMB - 1) // 2
    fb = np.linspace(1e-4, bands - 1, bands)
    ang = (2 * math.pi / n) * pos[:, None] * fb[None, :]
    z = np.concatenate([t[:, None], np.cos(ang), -np.sin(ang)], axis=-1)
    emb_pad = 128
    z = jnp.asarray(np.pad(z, ((0, 0), (0, emb_pad - FILT_EMB))), F32)
    w1p = jnp.pad(w1, ((0, emb_pad - FILT_EMB), (0, 0)))
    row = lambda a: a.reshape(1, -1)
    full = lambda shape: pl.BlockSpec(shape, lambda: tuple(0 for _ in shape))
    hid = pl.pallas_call(
        _filt_mlp_kernel,
        out_shape=jax.ShapeDtypeStruct((n, order), F32),
        in_specs=[full((n, emb_pad)), full((emb_pad, order)), full((1, order)), full((order, order)),
                  full((1, order)), full((order, order)), full((1, order)), full((1, order))],
        out_specs=full((n, order)),
        compiler_params=pltpu.CompilerParams(vmem_limit_bytes=VMEM_LIMIT_BYTES),
        name="filter_mlp",
    )(z, w1p, row(b1), w2, row(b2), w3, row(b3), row(freq))

    cols = 4 * width
    deltas = np.abs(np.linspace(math.log(DECAY_TARGET) / DECAY_SLOW_PCT,
                                math.log(DECAY_TARGET) / DECAY_FAST_PCT, width))
    deltas4 = jnp.asarray(np.tile(deltas, 4)[None, :], F32)
    tc = min(512, cols)
    taps, sums = pl.pallas_call(
        functools.partial(_filt_taps_kernel, n=n),
        out_shape=(jax.ShapeDtypeStruct((n, cols), F32), jax.ShapeDtypeStruct((2, cols), F32)),
        grid=(cols // tc,),
        in_specs=[pl.BlockSpec((n, order), lambda j: (0, 0)),
                  pl.BlockSpec((order, tc), lambda j: (0, j)),
                  pl.BlockSpec((1, tc), lambda j: (0, j))],
        out_specs=(pl.BlockSpec((n, tc), lambda j: (0, j)), pl.BlockSpec((2, tc), lambda j: (0, j))),
        compiler_params=_params("parallel"),
        name="filter_taps",
    )(hid, w_out, deltas4)

    a = _outer_dft(_hi_lo(gt_fwd), taps.reshape(1, n1h, GRID_W, cols), 0, c=cols, tl=1024)[0]
    s4 = sums.reshape(2, 2, 2, width)
    bwd0 = taps[0].reshape(2, 2, width)[:, 1]
    stats = jnp.stack([s4[0, :, 0], s4[1, :, 0], s4[0, :, 1], s4[1, :, 1], bwd0], axis=1)

    kb = min(SPECTRUM_BINS, n1h)

    def tap_spec(side):
        return pl.BlockSpec((2, kb, GRID_W, width), lambda k1, f: (0, k1, 0, 2 * f + side))

    f_hi, f_lo = _hi_lo(f_fwd)
    f_spec = pl.BlockSpec((2 * GRID_W, 2 * GRID_W), lambda k1, f: (0, 0))
    return pl.pallas_call(
        functools.partial(_spectrum_kernel, n=n),
        out_shape=jax.ShapeDtypeStruct((2, 2, n1h, GRID_W, width), F32),
        grid=(n1h // kb, 2),
        in_specs=[tap_spec(0), tap_spec(1), f_spec, f_spec,
                  pl.BlockSpec((None, 5, width), lambda k1, f: (f, 0, 0)),
                  pl.BlockSpec((None, 1, width), lambda k1, f: (f, 0, 0))],
        out_specs=pl.BlockSpec((None, 2, kb, GRID_W, width), lambda k1, f: (f, 0, k1, 0, 0)),
        compiler_params=_params("parallel", "parallel"),
        name="filter_spectrum",
    )(a, a, f_hi, f_lo, stats, hy_d.reshape(2, 1, width))


def _inner_conv_kernel(a_ref, f_ref, fi_ref, h_ref, o_ref):
    n2 = GRID_W
    for i in range(a_ref.shape[1]):
        z = _dot(f_ref[...], jnp.concatenate([a_ref[0, i], a_ref[1, i]], axis=0))
        zr, zi = z[:n2], z[n2:]
        hr, hi = h_ref[0, i], h_ref[1, i]
        stacked = jnp.concatenate([zr * hr - zi * hi, zr * hi + zi * hr], axis=0).astype(BF16)
        w = _dot(fi_ref[...], stacked).astype(BF16)
        o_ref[0, i] = w[:n2]
        o_ref[1, i] = w[n2:]


def _inner_conv(a, f_fwd, f_inv, spec, filt):
    bsz, _, n1h, n2, c = a.shape
    kb = min(INNER_BINS, n1h)
    return pl.pallas_call(
        _inner_conv_kernel,
        out_shape=jax.ShapeDtypeStruct(a.shape, BF16),
        grid=(n1h // kb, bsz),
        in_specs=[pl.BlockSpec((None, 2, kb, n2, c), lambda k1, b: (b, 0, k1, 0, 0)),
                  pl.BlockSpec((2 * n2, 2 * n2), lambda k1, b: (0, 0)),
                  pl.BlockSpec((2 * n2, 2 * n2), lambda k1, b: (0, 0)),
                  pl.BlockSpec((None, 2, kb, n2, c), lambda k1, b: (filt, 0, k1, 0, 0))],
        out_specs=pl.BlockSpec((None, 2, kb, n2, c), lambda k1, b: (b, 0, k1, 0, 0)),
        compiler_params=_params("parallel", "arbitrary"),
        name="inner_conv",
    )(a, f_fwd, f_inv, spec)


def _outer_idft_gate_kernel(gi_ref, g_ref, b_ref, x_ref, o_ref, *, again):
    n1h, _, tl = x_ref.shape
    parts = []
    for h, (bs, xs) in enumerate(zip(_slabs(b_ref[...].astype(F32)), _slabs(x_ref[...]))):
        stacked = bs.reshape(2 * n1h * OUTER_ROWS, tl).astype(BF16)
        y = _dot(gi_ref[h], stacked) * xs.reshape(n1h * OUTER_ROWS, tl)
        if again:
            parts.append(_dot(g_ref[h], y.astype(BF16)).reshape(2, n1h, OUTER_ROWS, tl))
        else:
            parts.append(y.reshape(n1h, OUTER_ROWS, tl))
    o_ref[...] = jnp.concatenate(parts, axis=-2).astype(o_ref.dtype)


def _outer_idft_gate(g_inv, g_fwd, b, px4, seg, *, again, tl):
    bsz, _, n1h, n2, c = b.shape
    tl = min(tl, c)
    per = c // tl
    rows = BF16_ROWS
    if again:
        out_shape = jax.ShapeDtypeStruct((bsz, 2, n1h, n2, c), BF16)
        out_spec = pl.BlockSpec((None, 2, n1h, rows, tl), lambda j, bb, l: (bb, 0, 0, j, l))
    else:
        out_shape = jax.ShapeDtypeStruct((bsz, n1h, n2, c), BF16)
        out_spec = pl.BlockSpec((None, n1h, rows, tl), lambda j, bb, l: (bb, 0, j, l))
    return pl.pallas_call(
        functools.partial(_outer_idft_gate_kernel, again=again),
        out_shape=out_shape,
        grid=(n2 // rows, bsz, per),
        in_specs=[pl.BlockSpec((rows // OUTER_ROWS,) + g_inv.shape[1:], lambda j, bb, l: (j, 0, 0)),
                  pl.BlockSpec((rows // OUTER_ROWS,) + g_fwd.shape[1:], lambda j, bb, l: (j, 0, 0)),
                  pl.BlockSpec((None, 2, n1h, rows, tl), lambda j, bb, l: (bb, 0, 0, j, l)),
                  pl.BlockSpec((None, n1h, rows, tl), lambda j, bb, l: (bb, 0, j, seg * per + l))],
        out_specs=out_spec,
        compiler_params=_params("parallel", "parallel", "parallel"),
        name="outer_idft_gate",
    )(g_inv, g_fwd, b, px4)


def _hyena_mix(px, spec, consts, *, bsz, n, width):
    g_fwd, g_inv, f_fwd, f_inv = consts
    g_fwd_b, g_inv_b = jnp.asarray(g_fwd, BF16), jnp.asarray(g_inv, BF16)
    f_fwd_b, f_inv_b = jnp.asarray(f_fwd, BF16), jnp.asarray(f_inv, BF16)
    n1h = n // GRID_W
    px4 = px.reshape(bsz, n1h, GRID_W, -1)
    tl = width
    a = _outer_dft((g_fwd_b,), px4, 5, c=width, tl=tl)
    a = _inner_conv(a, f_fwd_b, f_inv_b, spec, 0)
    a = _outer_idft_gate(g_inv_b, g_fwd_b, a, px4, 6, again=True, tl=tl)
    a = _inner_conv(a, f_fwd_b, f_inv_b, spec, 1)
    hy = _outer_idft_gate(g_inv_b, g_fwd_b, a, px4, 7, again=False, tl=tl)
    return hy.reshape(bsz * n, width)


def _tiles(d_ff):
    assert d_ff % 128 == 0, "a ragged last hidden tile still has to be lane-tile aligned"
    return TOKEN_TILE, 2 * TOKEN_TILE, min(HIDDEN_TILE, d_ff)


def kernel(x, c, ctx, c_ctx, ada_w, ada_b, norm_ffn1, ffn1_w1, ffn1_w3, ffn1_w2, norm_mix, w_in, hg_lb_logits, hg_norm, hy_conv_w, hy_conv_b, filt_w1, filt_b1, filt_w2, filt_b2, filt_w3, filt_b3, filt_freq, filt_w_out, hy_d, w_out, norm_ffn2, ffn2_w1, ffn2_w3, ffn2_w2, final_norm):
    bsz, n, d = x.shape
    n_ctx = ctx.shape[1]
    depth = ada_w.shape[0]
    assert depth == 1, "single-layer block"
    width = d // 2
    n_seg = w_in.shape[2] // width
    assert n_seg == 8 and n % GRID_W == 0 and n_ctx % SCAN_CHUNK == 0 and width % HEAD_DIM == 0
    tm, tm_wide, tf = _tiles(ffn1_w1.shape[2])
    tf_mix = max(tf // 2, 128)
    l = 0

    rows = -(-(bsz + 1) // 8) * 8
    cs = jnp.concatenate([c, c_ctx[None, :], jnp.zeros((rows - bsz - 1, d), F32)], axis=0)
    mods = _ada(cs, ada_w[l], ada_b[l]).reshape(rows, N_MOD, d)
    mx = [mods[:bsz, i][:, None, :] for i in range(N_MOD)]
    mc = [mods[bsz:bsz + 1, i][:, None, :] for i in range(N_MOD)]

    bf = _to_bf16
    w1a, w3a, w2a = bf(ffn1_w1[l]), bf(ffn1_w3[l]), bf(ffn1_w2[l])
    xt = x.reshape(bsz * n, d)
    yt = ctx.reshape(bsz * n_ctx, d)

    xt = _ffn(xt, mx[0], mx[1], mx[2], norm_ffn1[l], w1a, w3a, w2a, final_norm,
              tokens_per_mod=n, final_norm=False, tm=tm, tf=tf)
    yt = _ffn(yt, mc[0], mc[1], mc[2], norm_ffn1[l], w1a, w3a, w2a, final_norm,
              tokens_per_mod=bsz * n_ctx, final_norm=False, tm=tm, tf=tf)

    w_in_b = bf(w_in[l])
    pc = _proj(yt, mc[3], mc[4], norm_mix[l], w_in_b, hy_conv_w[l], hy_conv_b[l], n_seg=3, seg_w=width,
               conv_from=n_seg, tokens_per_mod=bsz * n_ctx, tm=tm)
    px = _proj(xt, mx[3], mx[4], norm_mix[l], w_in_b, hy_conv_w[l], hy_conv_b[l], n_seg=n_seg, seg_w=width,
               conv_from=5, tokens_per_mod=n, tm=tm_wide)
    cols = n_seg * width

    hg = _hgrn(px.reshape(bsz, n, cols), pc.reshape(bsz, n_ctx, 3 * width), hg_lb_logits, hg_norm[l], width=width)

    consts = _dft_constants(n)
    filt = (filt_w1[l], filt_b1[l], filt_w2[l], filt_b2[l], filt_w3[l], filt_b3[l], filt_freq[l], filt_w_out[l])
    spec = _hyena_spectra(n, width, filt, hy_d[l], consts)
    hy = _hyena_mix(px, spec, consts, bsz=bsz, n=n, width=width)

    out = _ffn(xt, mx[6], mx[7], mx[8], norm_ffn2[l], bf(ffn2_w1[l]), bf(ffn2_w3[l]), bf(ffn2_w2[l]), final_norm,
               tokens_per_mod=n, final_norm=True, tm=tm, tf=tf_mix,
               mix=(hg.reshape(bsz * n, width), hy, bf(w_out[l]), mx[5]))
    return out.reshape(bsz, n, d)
```

```python
import functools
import math

import numpy as np
import jax
import jax.numpy as jnp
from jax import lax
from jax.experimental import pallas as pl
from jax.experimental.pallas import tpu as pltpu

F32 = jnp.float32
BF16 = jnp.bfloat16
EPS = 1e-6
N_MOD = 9
HEAD_DIM = 128
GRID_W = 64
CAST_ROWS = 256
TOKEN_TILE = 512
HIDDEN_TILE = 1024
SCAN_CHUNK = 64
PREP_CHUNKS = 4
SCAN_BLOCKS = 4
OUTER_ROWS = 8
BF16_ROWS = 16
INNER_BINS = 16
SPECTRUM_BINS = 8
FILT_EMB = 33
DECAY_TARGET = 1e-2
DECAY_FAST_PCT = 0.3
DECAY_SLOW_PCT = 1.5
VMEM_LIMIT_BYTES = 60 * 1024 * 1024
HIGHEST = lax.Precision.HIGHEST


def _params(*semantics):
    return pltpu.CompilerParams(dimension_semantics=semantics, vmem_limit_bytes=VMEM_LIMIT_BYTES)


def _dot(a, b):
    return jnp.dot(a, b, preferred_element_type=F32)


def _silu(a):
    return a * jax.nn.sigmoid(a)


def _cast_kernel(x_ref, o_ref):
    o_ref[...] = x_ref[...].astype(BF16)


def _to_bf16(w):
    r, c = w.shape
    rb = CAST_ROWS if r % CAST_ROWS == 0 else r
    return pl.pallas_call(
        _cast_kernel,
        out_shape=jax.ShapeDtypeStruct((r, c), BF16),
        grid=(r // rb,),
        in_specs=[pl.BlockSpec((rb, c), lambda i: (i, 0))],
        out_specs=pl.BlockSpec((rb, c), lambda i: (i, 0)),
        compiler_params=_params("parallel"),
        name="to_bf16",
    )(w)


def _ada_kernel(c_ref, w_ref, b_ref, o_ref):
    h = _silu(c_ref[...]).astype(BF16)
    o_ref[...] = _dot(h, w_ref[...].astype(BF16)) + b_ref[...]


def _ada(cs, w, b):
    rows, d = cs.shape
    n = w.shape[1]
    tn = d // 2
    return pl.pallas_call(
        _ada_kernel,
        out_shape=jax.ShapeDtypeStruct((rows, n), F32),
        grid=(n // tn,),
        in_specs=[pl.BlockSpec((rows, d), lambda j: (0, 0)),
                  pl.BlockSpec((d, tn), lambda j: (0, j)),
                  pl.BlockSpec((1, tn), lambda j: (0, j))],
        out_specs=pl.BlockSpec((rows, tn), lambda j: (0, j)),
        compiler_params=_params("arbitrary"),
        name="ada_mod",
    )(cs, w, b.reshape(1, n))


def _norm_mod(x, gain, shift, scale):
    y = x * lax.rsqrt(jnp.mean(x * x, axis=-1, keepdims=True) + EPS)
    return (y * (gain * (1.0 + scale)) + shift).astype(BF16)


def _ffn_kernel(x_ref, sh_ref, sc_ref, gt_ref, g_ref, w1_ref, w3_ref, w2_ref, fin_ref, *rest,
                final_norm, n_hidden, last_cols):
    if len(rest) == 2:
        (o_ref, h_scr), mix, res_ref = rest, None, x_ref
    else:
        *mix, o_ref, h_scr, res_ref = rest
    j = pl.program_id(1)
    tf = w1_ref.shape[1]

    def hidden_tile(cols=tf):
        h = h_scr[...]
        a = _dot(h, w1_ref[:, :cols])
        b = _dot(h, w3_ref[:, :cols])
        return _dot((_silu(a) * b).astype(BF16), w2_ref[:cols, :])

    def finish(acc):
        out = res_ref[...] + (0.5 * gt_ref[0]) * acc
        if final_norm:
            out = out * lax.rsqrt(jnp.mean(out * out, axis=-1, keepdims=True) + EPS) * fin_ref[...]
        return out

    last = n_hidden - 1

    @pl.when(j == 0)
    def _():
        x = x_ref[...]
        if mix is not None:
            hg_ref, hy_ref, wa_ref, wb_ref, gm_ref = mix
            x = x + gm_ref[0] * (_dot(hg_ref[...], wa_ref[...]) + _dot(hy_ref[...], wb_ref[...]))
            res_ref[...] = x
        h_scr[...] = _norm_mod(x, g_ref[...], sh_ref[0], sc_ref[0])
        o_ref[...] = finish(hidden_tile(last_cols)) if last == 0 else hidden_tile(last_cols)

    @pl.when((j > 0) & (j < last))
    def _():
        o_ref[...] += hidden_tile()

    if last > 0:
        @pl.when(j == last)
        def _():
            o_ref[...] = finish(o_ref[...] + hidden_tile())


def _ffn(x, shift, scale, gate, gain, w1, w3, w2, fin, *, tokens_per_mod, final_norm, tm, tf, mix=None):
    t, d = x.shape
    f = w1.shape[1]
    tm = min(tm, tokens_per_mod)
    mod_spec = pl.BlockSpec((1, 1, d), lambda i, j: ((i * tm) // tokens_per_mod, 0, 0))
    vec_spec = pl.BlockSpec((1, d), lambda i, j: (0, 0))
    n_hidden = pl.cdiv(f, tf)
    hidden = lambda j: (j + n_hidden - 1) % n_hidden
    in_specs = [pl.BlockSpec((tm, d), lambda i, j: (i, 0)),
                mod_spec, mod_spec, mod_spec, vec_spec,
                pl.BlockSpec((d, tf), lambda i, j: (0, hidden(j))),
                pl.BlockSpec((d, tf), lambda i, j: (0, hidden(j))),
                pl.BlockSpec((tf, d), lambda i, j: (hidden(j), 0)),
                vec_spec]
    args = [x, shift, scale, gate, gain.reshape(1, d), w1, w3, w2, fin.reshape(1, d)]
    scratch = [pltpu.VMEM((tm, d), BF16)]
    if mix is not None:
        hg, hy, w_out, gate_mix = mix
        wdt = hg.shape[1]
        half = lambda part: pl.BlockSpec((wdt, d), lambda i, j: (part, 0), pipeline_mode=pl.Buffered(1))
        in_specs += [pl.BlockSpec((tm, wdt), lambda i, j: (i, 0)), pl.BlockSpec((tm, wdt), lambda i, j: (i, 0)),
                     half(0), half(1), mod_spec]
        args += [hg, hy, w_out, w_out, gate_mix]
        scratch.append(pltpu.VMEM((tm, d), F32))
    return pl.pallas_call(
        functools.partial(_ffn_kernel, final_norm=final_norm, n_hidden=n_hidden, last_cols=f - (n_hidden - 1) * tf),
        out_shape=jax.ShapeDtypeStruct((t, d), F32),
        grid=(t // tm, n_hidden),
        in_specs=in_specs,
        out_specs=pl.BlockSpec((tm, d), lambda i, j: (i, 0)),
        scratch_shapes=scratch,
        compiler_params=_params("parallel", "arbitrary"),
        name="swiglu_ffn",
    )(*args)


def _proj_kernel(x_ref, sh_ref, sc_ref, g_ref, w_ref, cw_ref, cb_ref, o_ref, h_scr, *, conv_from):
    j = pl.program_id(1)

    def segment():
        return _dot(h_scr[...], w_ref[...])

    @pl.when(j == 0)
    def _():
        h_scr[...] = _norm_mod(x_ref[...], g_ref[...], sh_ref[0], sc_ref[0])
        o_ref[...] = segment()

    @pl.when((j > 0) & (j < conv_from))
    def _():
        o_ref[...] = segment()

    @pl.when(j >= conv_from)
    def _():
        p = segment()
        tm, w = p.shape
        rows3 = lambda t: t.reshape(tm // GRID_W, GRID_W, w)
        y = (rows3(pltpu.roll(p, 1, axis=0)) * cw_ref[0] + rows3(p) * cw_ref[1]
             + rows3(pltpu.roll(p, tm - 1, axis=0)) * cw_ref[2] + cb_ref[...])
        o_ref[...] = y.reshape(tm, w)


def _proj(x, shift, scale, gain, w, conv_w, conv_b, *, n_seg, seg_w, conv_from, tokens_per_mod, tm):
    t, d = x.shape
    tm = min(tm, tokens_per_mod)
    mod_spec = pl.BlockSpec((1, 1, d), lambda i, j: ((i * tm) // tokens_per_mod, 0, 0))
    conv_idx = lambda i, j: (0, jnp.maximum(j - conv_from, 0))
    pos = jnp.arange(GRID_W)[None, :, None]
    edge = jnp.stack([pos > 0, pos >= 0, pos < GRID_W - 1], axis=0)[:, 0]
    conv_w = jnp.where(edge, conv_w[:, None, :], 0.0)
    return pl.pallas_call(
        functools.partial(_proj_kernel, conv_from=conv_from),
        out_shape=jax.ShapeDtypeStruct((t, n_seg * seg_w), F32),
        grid=(t // tm, n_seg),
        in_specs=[pl.BlockSpec((tm, d), lambda i, j: (i, 0)),
                  mod_spec, mod_spec,
                  pl.BlockSpec((1, d), lambda i, j: (0, 0)),
                  pl.BlockSpec((d, seg_w), lambda i, j: (0, j)),
                  pl.BlockSpec((3, GRID_W, seg_w), lambda i, j: (0, 0, jnp.maximum(j - conv_from, 0))),
                  pl.BlockSpec((1, seg_w), conv_idx)],
        out_specs=pl.BlockSpec((tm, seg_w), lambda i, j: (i, j)),
        scratch_shapes=[pltpu.VMEM((tm, d), BF16)],
        compiler_params=_params("parallel", "arbitrary"),
        name="in_proj",
    )(x, shift, scale, gain.reshape(1, d), w, conv_w, conv_b.reshape(1, -1))


def _split3(g):
    g1 = g.astype(BF16)
    r1 = g - g1.astype(F32)
    g2 = r1.astype(BF16)
    return g1, g2, (r1 - g2.astype(F32)).astype(BF16)


def _split_dot(tri, g):
    g1, g2, g3 = _split3(g)
    return _dot(tri, g1) + _dot(tri, g2) + _dot(tri, g3)


def _gates(a, lb):
    f = lb + (1.0 - lb) * jax.nn.sigmoid(a)
    return 1.0 - f, jnp.log(f)


def _tn_dot(a, b):
    return lax.dot_general(a, b, (((0,), (0,)), ((), ())), preferred_element_type=F32)


def _nt_dot(a, b):
    return lax.dot_general(a, b, (((1,), (1,)), ((), ())), preferred_element_type=F32)


def _hgrn_kernel(af_ref, ab_ref, v_ref, q_ref, gate_ref, caf_ref, cab_ref, cv_ref, lbl_ref, gain_ref,
                 o_ref, of_scr, ob_scr, sf_scr, sb_scr, qin_scr, qt_scr, kt_scr, kd_scr, dec_scr,
                 *, n_chunks, n_ctx_chunks):
    c = SCAN_CHUNK
    mid = c // 2
    rows_p = PREP_CHUNKS * c
    prow = lax.broadcasted_iota(jnp.int32, (rows_p, rows_p), 0)
    pcol = lax.broadcasted_iota(jnp.int32, (rows_p, rows_p), 1)
    same = (prow // c) == (pcol // c)
    tri_fwd_p = jnp.where(same & (prow >= pcol), 1.0, 0.0).astype(BF16)
    tri_bwd_p = jnp.where(same & (pcol >= prow), 1.0, 0.0).astype(BF16)

    def lower_bound(direction):
        lg = lbl_ref[direction]
        ex = jnp.exp(lg - jnp.max(lg, axis=0, keepdims=True))
        return ex[0:1, :] / jnp.sum(ex, axis=0, keepdims=True)

    lb_f = lower_bound(0)
    lb_b = lower_bound(1)
    q_scale = HEAD_DIM ** -0.5

    n_blocks = n_chunks // PREP_CHUNKS
    n_steps = n_blocks // SCAN_BLOCKS

    def step_blocks(it):
        fwd = [(0, it * SCAN_BLOCKS + u) for u in range(SCAN_BLOCKS)]
        return fwd + [(1, n_blocks - 1 - bi) for _, bi in fwd]

    def prepare(it, s):
        it = jnp.minimum(it, n_steps - 1)
        blocks = step_blocks(it)
        los = [pl.multiple_of(bi * rows_p, rows_p) for _, bi in blocks]
        gates = [_gates((af_ref, ab_ref)[d][pl.ds(lo, rows_p), :], (lb_f, lb_b)[d]) for (d, _), lo in zip(blocks, los)]
        sums = {}
        for d, tri_p in ((0, tri_fwd_p), (1, tri_bwd_p)):
            slots_d = [slot for slot, (dd, _) in enumerate(blocks) if dd == d]
            wide = _dot(tri_p, jnp.concatenate([p for slot in slots_d for p in _split3(gates[slot][1])], axis=1))
            for n_s, slot in enumerate(slots_d):
                parts = [wide[:, (3 * n_s + m) * HEAD_DIM:(3 * n_s + m + 1) * HEAD_DIM] for m in range(3)]
                sums[slot] = parts[0] + parts[1] + parts[2]
        for slot, ((d, _), lo) in enumerate(zip(blocks, los)):
            end_row = (c - 1, 0)[d]
            k = gates[slot][0]
            b = sums[slot].reshape(PREP_CHUNKS, c, HEAD_DIM)
            b_mid = b[:, mid:mid + 1, :]
            b_end = b[:, end_row:end_row + 1, :]
            q_t = (q_ref[pl.ds(lo, rows_p), :] * q_scale).reshape(PREP_CHUNKS, c, HEAD_DIM) * jnp.exp(b - b_mid)
            k_t = k.reshape(PREP_CHUNKS, c, HEAD_DIM) * jnp.exp(b_mid - b)
            flat = lambda t: t.reshape(rows_p, HEAD_DIM).astype(BF16)
            qt_scr[s, slot] = flat(q_t)
            kt_scr[s, slot] = flat(k_t)
            qin_scr[s, slot] = flat(q_t * jnp.exp(b_mid))
            kd_scr[s, slot] = flat(k_t * jnp.exp(b_end - b_mid))
            dec_scr[s, slot] = jnp.broadcast_to(jnp.exp(b_end), (PREP_CHUNKS, 8, HEAD_DIM))

    zeros_blk = jnp.zeros((c, HEAD_DIM), BF16)

    def block_diag(t):
        return jnp.concatenate(
            [jnp.concatenate([t[j * c:(j + 1) * c] if m == j else zeros_blk for m in range(PREP_CHUNKS)], axis=1)
             for j in range(PREP_CHUNKS)], axis=0)

    masks = (same & (prow >= pcol), same & (pcol >= prow))
    st_refs = (sf_scr, sb_scr)
    out_refs = (of_scr, ob_scr)
    orders = (tuple(range(PREP_CHUNKS)), tuple(range(PREP_CHUNKS - 1, -1, -1)))

    gain = gain_ref[...]

    def scan(it, s, finish):
        streams = step_blocks(it)
        los = [pl.multiple_of(bi * rows_p, rows_p) for _, bi in streams]
        vbs = [v_ref[pl.ds(lo, rows_p), :].astype(BF16) for lo in los]
        scores = [_nt_dot(qt_scr[s, slot], kt_scr[s, slot]) for slot in range(len(streams))]
        upds = [_tn_dot(vb, block_diag(kd_scr[s, slot])) for slot, vb in enumerate(vbs)]
        outs = [_dot(jnp.where(masks[d], sc, 0.0).astype(BF16), vb) for (d, _), sc, vb in zip(streams, scores, vbs)]
        seen = [[None] * PREP_CHUNKS for _ in streams]
        for d in (0, 1):
            st = st_refs[d][...]
            for slot, (ds_, _) in enumerate(streams):
                if ds_ != d:
                    continue
                for j in orders[d]:
                    seen[slot][j] = st.astype(BF16)
                    st = st * dec_scr[s, slot, j][0:1, :] + upds[slot][:, j * HEAD_DIM:(j + 1) * HEAD_DIM]
            st_refs[d][...] = st
        for slot, ((d, _), lo) in enumerate(zip(streams, los)):
            q_in = qin_scr[s, slot]
            carried = jnp.concatenate([_nt_dot(q_in[j * c:(j + 1) * c], seen[slot][j]) for j in range(PREP_CHUNKS)],
                                      axis=0)
            rows = pl.ds(lo, rows_p)
            if finish:
                o = outs[slot] + carried + out_refs[1 - d][rows, :]
                o = o * lax.rsqrt(jnp.mean(o * o, axis=-1, keepdims=True) + EPS) * gain
                o_ref[rows, :] = (o * _silu(gate_ref[rows, :])).astype(o_ref.dtype)
            else:
                out_refs[d][rows, :] = outs[slot] + carried

    n_ctx_blocks = n_ctx_chunks // PREP_CHUNKS
    ctx_state = [jnp.zeros((HEAD_DIM, HEAD_DIM), F32), jnp.zeros((HEAD_DIM, HEAD_DIM), F32)]
    for step in range(n_ctx_blocks):
        upds, decs = [], []
        for d, bi in ((0, step), (1, n_ctx_blocks - 1 - step)):
            a_ref, lb, tri_p, end_row = ((caf_ref, lb_f, tri_fwd_p, c - 1), (cab_ref, lb_b, tri_bwd_p, 0))[d]
            rows_c = pl.ds(bi * rows_p, rows_p)
            k, g = _gates(a_ref[rows_c, :], lb)
            b = _split_dot(tri_p, g).reshape(PREP_CHUNKS, c, HEAD_DIM)
            b_end = b[:, end_row:end_row + 1, :]
            k_d = (k.reshape(PREP_CHUNKS, c, HEAD_DIM) * jnp.exp(b_end - b)).reshape(rows_p, HEAD_DIM).astype(BF16)
            upds.append(_tn_dot(cv_ref[rows_c, :].astype(BF16), block_diag(k_d)))
            decs.append(jnp.exp(b_end))
        for d in (0, 1):
            for j in orders[d]:
                ctx_state[d] = ctx_state[d] * decs[d][j] + upds[d][:, j * HEAD_DIM:(j + 1) * HEAD_DIM]
    sf_scr[...] = ctx_state[0]
    sb_scr[...] = ctx_state[1]

    prepare(0, 0)

    def body(finish):
        def pair(i, carry):
            scan(2 * i, 0, finish)
            prepare(2 * i + 1, 1)
            scan(2 * i + 1, 1, finish)
            prepare(2 * i + 2, 0)
            return carry
        return pair

    lax.fori_loop(0, n_steps // 4, body(False), 0)
    lax.fori_loop(n_steps // 4, n_steps // 2, body(True), 0)


def _hgrn(px, pc, lb_logits, gain, *, width):
    bsz, n, _ = px.shape
    n_ctx = pc.shape[1]
    heads = width // HEAD_DIM
    hd = HEAD_DIM

    def seg(s):
        return pl.BlockSpec((None, n, hd), lambda b, h: (b, 0, s * heads + h))

    def cseg(s):
        return pl.BlockSpec((None, n_ctx, hd), lambda b, h: (b, 0, s * heads + h))

    depth1 = lb_logits.shape[1]
    slots = 2 * SCAN_BLOCKS
    assert (n // SCAN_CHUNK) % (4 * SCAN_BLOCKS * PREP_CHUNKS) == 0, "scan steps come in pairs, per half"
    assert (n_ctx // SCAN_CHUNK) % PREP_CHUNKS == 0, "the context prefix is scanned in whole blocks"
    return pl.pallas_call(
        functools.partial(_hgrn_kernel, n_chunks=n // SCAN_CHUNK, n_ctx_chunks=n_ctx // SCAN_CHUNK),
        out_shape=jax.ShapeDtypeStruct((bsz, n, width), BF16),
        grid=(bsz, heads),
        in_specs=[seg(0), seg(1), seg(2), seg(3), seg(4), cseg(0), cseg(1), cseg(2),
                  pl.BlockSpec((2, depth1, hd), lambda b, h: (0, 0, h)),
                  pl.BlockSpec((1, hd), lambda b, h: (0, h))],
        out_specs=pl.BlockSpec((None, n, hd), lambda b, h: (b, 0, h)),
        scratch_shapes=[pltpu.VMEM((n, hd), F32), pltpu.VMEM((n, hd), F32),
                        pltpu.VMEM((hd, hd), F32), pltpu.VMEM((hd, hd), F32),
                        *[pltpu.VMEM((2, slots, PREP_CHUNKS * SCAN_CHUNK, hd), BF16) for _ in range(4)],
                        pltpu.VMEM((2, slots, PREP_CHUNKS, 8, hd), F32)],
        compiler_params=_params("parallel", "parallel"),
        name="hgrn_scan",
    )(px, px, px, px, px, pc, pc, pc, lb_logits, gain.reshape(1, width))


def _dft_constants(n):
    n2 = GRID_W
    n1h = n // n2
    m2r = np.arange(n2)[None, :]
    k2 = np.arange(n2)[:, None]
    f_ang = -2.0 * np.pi * k2 * m2r / n2
    fr, fi = np.cos(f_ang), np.sin(f_ang)
    f_fwd = np.block([[fr, -fi], [fi, fr]])
    f_inv = np.block([[fr, fi], [-fi, fr]])
    k1 = np.arange(n1h)[:, None, None]
    m1 = np.arange(n1h)[None, :, None]
    m2 = np.arange(n2)[None, None, :]
    theta = -2.0 * np.pi * (k1 + 0.5) * (m1 / (2 * n1h) + m2 / (2 * n))
    parts = np.stack([np.cos(theta), np.sin(theta)], axis=0)
    parts = parts.reshape(2, n1h, n1h, n2 // OUTER_ROWS, OUTER_ROWS)
    eye = np.eye(OUTER_ROWS)
    rows = n1h * OUTER_ROWS
    fwd = np.einsum("pknsj,jm->spkjnm", parts, eye).reshape(n2 // OUTER_ROWS, 2 * rows, rows)
    inv = np.einsum("pknsj,jm->snjpkm", parts, eye).reshape(n2 // OUTER_ROWS, rows, 2 * rows)
    return fwd, inv, f_fwd, f_inv


def _filt_mlp_kernel(z_ref, w1_ref, b1_ref, w2_ref, b2_ref, w3_ref, b3_ref, fr_ref, o_ref):
    fr = fr_ref[...]
    hp = lambda a, b: jnp.dot(a, b, precision=HIGHEST, preferred_element_type=F32)
    hid = jnp.sin(fr * (hp(z_ref[...], w1_ref[...]) + b1_ref[...]))
    hid = jnp.sin(fr * (hp(hid, w2_ref[...]) + b2_ref[...]))
    o_ref[...] = jnp.sin(fr * (hp(hid, w3_ref[...]) + b3_ref[...]))


def _filt_taps_kernel(hid_ref, wo_ref, dl_ref, h_ref, s_ref, *, n):
    hid = hid_ref[...]
    hid_hi = hid.astype(BF16)
    h = _dot3(hid_hi, (hid - hid_hi.astype(F32)).astype(BF16), wo_ref[...])
    t = lax.broadcasted_iota(jnp.int32, h.shape, 0).astype(F32) * (1.0 / max(n - 1, 1))
    hw = h * jnp.exp(-t * dl_ref[...])
    h_ref[...] = hw
    s_ref[0:1, :] = jnp.sum(jnp.abs(hw), axis=0, keepdims=True)
    s_ref[1:2, :] = jnp.abs(hw[0:1, :])


def _spectrum_kernel(af_ref, ab_ref, f_ref, fl_ref, s_ref, d_ref, o_ref, *, n):
    n2 = GRID_W
    l1 = s_ref[0:1, :] + s_ref[2:3, :] - s_ref[3:4, :]
    hb0 = s_ref[4:5, :]
    norm = 1.0 / l1
    scale = 2.0 / (2 * n)
    for i in range(af_ref.shape[1]):
        def inner(a_ref):
            z = _dot3(f_ref[...], fl_ref[...], jnp.concatenate([a_ref[0, i], a_ref[1, i]], axis=0))
            return z[:n2], z[n2:]

        zfr, zfi = inner(af_ref)
        zbr, zbi = inner(ab_ref)
        o_ref[0, i] = scale * ((zfr + zbr - hb0) * norm + d_ref[...])
        o_ref[1, i] = scale * ((zfi - zbi) * norm)


def _dot3(a_hi, a_lo, x):
    x_hi = x.astype(BF16)
    x_lo = (x - x_hi.astype(F32)).astype(BF16)
    return _dot(a_hi, x_hi) + _dot(a_hi, x_lo) + _dot(a_lo, x_hi)


def _hi_lo(a):
    hi = np.asarray(a, np.float64).astype(BF16)
    lo = (a - hi.astype(np.float64)).astype(BF16)
    return jnp.asarray(hi), jnp.asarray(lo)


def _slabs(x):
    return [x[..., h * OUTER_ROWS:(h + 1) * OUTER_ROWS, :] for h in range(x.shape[-2] // OUTER_ROWS)]


def _outer_dft_kernel(*refs, high):
    g_refs, x_ref, o_ref = refs[:-2], refs[-2], refs[-1]
    n1h, _, tl = x_ref.shape
    parts = []
    for h, xs in enumerate(_slabs(x_ref[...])):
        xs = xs.reshape(n1h * OUTER_ROWS, tl)
        r = _dot3(g_refs[0][h], g_refs[1][h], xs) if high else _dot(g_refs[0][h], xs.astype(BF16))
        parts.append(r.reshape(2, n1h, OUTER_ROWS, tl))
    o_ref[...] = jnp.concatenate(parts, axis=2).astype(o_ref.dtype)


def _outer_dft(gs, x4, seg, *, c, tl):
    high = len(gs) == 2
    s, n1h, n2, _ = x4.shape
    tl = min(tl, c)
    per = c // tl
    rows = OUTER_ROWS if high else BF16_ROWS
    g_spec = pl.BlockSpec((rows // OUTER_ROWS,) + gs[0].shape[1:], lambda j, b, l: (j, 0, 0))
    return pl.pallas_call(
        functools.partial(_outer_dft_kernel, high=high),
        out_shape=jax.ShapeDtypeStruct((s, 2, n1h, n2, c), F32 if high else BF16),
        grid=(n2 // rows, s, per),
        in_specs=[g_spec] * len(gs) + [pl.BlockSpec((None, n1h, rows, tl),
                                                    lambda j, b, l: (b, 0, j, seg * per + l))],
        out_specs=pl.BlockSpec((None, 2, n1h, rows, tl), lambda j, b, l: (b, 0, 0, j, l)),
        compiler_params=_params("parallel", "parallel", "parallel"),
        name="outer_dft",
    )(*gs, x4)


def _hyena_spectra(n, width, filt, hy_d, consts):
    w1, b1, w2, b2, w3, b3, freq, w_out = filt
    gt_fwd, _, f_fwd, _ = consts
    n1h = n // GRID_W
    order = w1.shape[1]
    pos = np.arange(n, dtype=np.float64)
    t = pos / max(n - 1, 1)
    bands = (FILT_EMB - 1) // 2
    fb = np.linspace(1e-4, bands - 1, bands)
    ang = (2 * math.pi / n) * pos[:, None] * fb[None, :]
    z = np.concatenate([t[:, None], np.cos(ang), -np.sin(ang)], axis=-1)
    emb_pad = 128
    z = jnp.asarray(np.pad(z, ((0, 0), (0, emb_pad - FILT_EMB))), F32)
    w1p = jnp.pad(w1, ((0, emb_pad - FILT_EMB), (0, 0)))
    row = lambda a: a.reshape(1, -1)
    full = lambda shape: pl.BlockSpec(shape, lambda: tuple(0 for _ in shape))
    hid = pl.pallas_call(
        _filt_mlp_kernel,
        out_shape=jax.ShapeDtypeStruct((n, order), F32),
        in_specs=[full((n, emb_pad)), full((emb_pad, order)), full((1, order)), full((order, order)),
                  full((1, order)), full((order, order)), full((1, order)), full((1, order))],
        out_specs=full((n, order)),
        compiler_params=pltpu.CompilerParams(vmem_limit_bytes=VMEM_LIMIT_BYTES),
        name="filter_mlp",
    )(z, w1p, row(b1), w2, row(b2), w3, row(b3), row(freq))

    cols = 4 * width
    deltas = np.abs(np.linspace(math.log(DECAY_TARGET) / DECAY_SLOW_PCT,
                                math.log(DECAY_TARGET) / DECAY_FAST_PCT, width))
    deltas4 = jnp.asarray(np.tile(deltas, 4)[None, :], F32)
    tc = min(512, cols)
    taps, sums = pl.pallas_call(
        functools.partial(_filt_taps_kernel, n=n),
        out_shape=(jax.ShapeDtypeStruct((n, cols), F32), jax.ShapeDtypeStruct((2, cols), F32)),
        grid=(cols // tc,),
        in_specs=[pl.BlockSpec((n, order), lambda j: (0, 0)),
                  pl.BlockSpec((order, tc), lambda j: (0, j)),
                  pl.BlockSpec((1, tc), lambda j: (0, j))],
        out_specs=(pl.BlockSpec((n, tc), lambda j: (0, j)), pl.BlockSpec((2, tc), lambda j: (0, j))),
        compiler_params=_params("parallel"),
        name="filter_taps",
    )(hid, w_out, deltas4)

    a = _outer_dft(_hi_lo(gt_fwd), taps.reshape(1, n1h, GRID_W, cols), 0, c=cols, tl=1024)[0]
    s4 = sums.reshape(2, 2, 2, width)
    bwd0 = taps[0].reshape(2, 2, width)[:, 1]
    stats = jnp.stack([s4[0, :, 0], s4[1, :, 0], s4[0, :, 1], s4[1, :, 1], bwd0], axis=1)

    kb = min(SPECTRUM_BINS, n1h)

    def tap_spec(side):
        return pl.BlockSpec((2, kb, GRID_W, width), lambda k1, f: (0, k1, 0, 2 * f + side))

    f_hi, f_lo = _hi_lo(f_fwd)
    f_spec = pl.BlockSpec((2 * GRID_W, 2 * GRID_W), lambda k1, f: (0, 0))
    return pl.pallas_call(
        functools.partial(_spectrum_kernel, n=n),
        out_shape=jax.ShapeDtypeStruct((2, 2, n1h, GRID_W, width), F32),
        grid=(n1h // kb, 2),
        in_specs=[tap_spec(0), tap_spec(1), f_spec, f_spec,
                  pl.BlockSpec((None, 5, width), lambda k1, f: (f, 0, 0)),
                  pl.BlockSpec((None, 1, width), lambda k1, f: (f, 0, 0))],
        out_specs=pl.BlockSpec((None, 2, kb, GRID_W, width), lambda k1, f: (f, 0, k1, 0, 0)),
        compiler_params=_params("parallel", "parallel"),
        name="filter_spectrum",
    )(a, a, f_hi, f_lo, stats, hy_d.reshape(2, 1, width))


def _inner_conv_kernel(a_ref, f_ref, fi_ref, h_ref, o_ref):
    n2 = GRID_W
    for i in range(a_ref.shape[1]):
        z = _dot(f_ref[...], jnp.concatenate([a_ref[0, i], a_ref[1, i]], axis=0))
        zr, zi = z[:n2], z[n2:]
        hr, hi = h_ref[0, i], h_ref[1, i]
        stacked = jnp.concatenate([zr * hr - zi * hi, zr * hi + zi * hr], axis=0).astype(BF16)
        w = _dot(fi_ref[...], stacked).astype(BF16)
        o_ref[0, i] = w[:n2]
        o_ref[1, i] = w[n2:]


def _inner_conv(a, f_fwd, f_inv, spec, filt):
    bsz, _, n1h, n2, c = a.shape
    kb = min(INNER_BINS, n1h)
    return pl.pallas_call(
        _inner_conv_kernel,
        out_shape=jax.ShapeDtypeStruct(a.shape, BF16),
        grid=(n1h // kb, bsz),
        in_specs=[pl.BlockSpec((None, 2, kb, n2, c), lambda k1, b: (b, 0, k1, 0, 0)),
                  pl.BlockSpec((2 * n2, 2 * n2), lambda k1, b: (0, 0)),
                  pl.BlockSpec((2 * n2, 2 * n2), lambda k1, b: (0, 0)),
                  pl.BlockSpec((None, 2, kb, n2, c), lambda k1, b: (filt, 0, k1, 0, 0))],
        out_specs=pl.BlockSpec((None, 2, kb, n2, c), lambda k1, b: (b, 0, k1, 0, 0)),
        compiler_params=_params("parallel", "arbitrary"),
        name="inner_conv",
    )(a, f_fwd, f_inv, spec)


def _outer_idft_gate_kernel(gi_ref, g_ref, b_ref, x_ref, o_ref, *, again):
    n1h, _, tl = x_ref.shape
    parts = []
    for h, (bs, xs) in enumerate(zip(_slabs(b_ref[...].astype(F32)), _slabs(x_ref[...]))):
        stacked = bs.reshape(2 * n1h * OUTER_ROWS, tl).astype(BF16)
        y = _dot(gi_ref[h], stacked) * xs.reshape(n1h * OUTER_ROWS, tl)
        if again:
            parts.append(_dot(g_ref[h], y.astype(BF16)).reshape(2, n1h, OUTER_ROWS, tl))
        else:
            parts.append(y.reshape(n1h, OUTER_ROWS, tl))
    o_ref[...] = jnp.concatenate(parts, axis=-2).astype(o_ref.dtype)


def _outer_idft_gate(g_inv, g_fwd, b, px4, seg, *, again, tl):
    bsz, _, n1h, n2, c = b.shape
    tl = min(tl, c)
    per = c // tl
    rows = BF16_ROWS
    if again:
        out_shape = jax.ShapeDtypeStruct((bsz, 2, n1h, n2, c), BF16)
        out_spec = pl.BlockSpec((None, 2, n1h, rows, tl), lambda j, bb, l: (bb, 0, 0, j, l))
    else:
        out_shape = jax.ShapeDtypeStruct((bsz, n1h, n2, c), BF16)
        out_spec = pl.BlockSpec((None, n1h, rows, tl), lambda j, bb, l: (bb, 0, j, l))
    return pl.pallas_call(
        functools.partial(_outer_idft_gate_kernel, again=again),
        out_shape=out_shape,
        grid=(n2 // rows, bsz, per),
        in_specs=[pl.BlockSpec((rows // OUTER_ROWS,) + g_inv.shape[1:], lambda j, bb, l: (j, 0, 0)),
                  pl.BlockSpec((rows // OUTER_ROWS,) + g_fwd.shape[1:], lambda j, bb, l: (j, 0, 0)),
                  pl.BlockSpec((None, 2, n1h, rows, tl), lambda j, bb, l: (bb, 0, 0, j, l)),
                  pl.BlockSpec((None, n1h, rows, tl), lambda j, bb, l: (bb, 0, j, seg * per + l))],
        out_specs=out_spec,
        compiler_params=_params("parallel", "parallel", "parallel"),
        name="outer_idft_gate",
    )(g_inv, g_fwd, b, px4)


def _hyena_mix(px, spec, consts, *, bsz, n, width):
    g_fwd, g_inv, f_fwd, f_inv = consts
    g_fwd_b, g_inv_b = jnp.asarray(g_fwd, BF16), jnp.asarray(g_inv, BF16)
    f_fwd_b, f_inv_b = jnp.asarray(f_fwd, BF16), jnp.asarray(f_inv, BF16)
    n1h = n // GRID_W
    px4 = px.reshape(bsz, n1h, GRID_W, -1)
    tl = width
    a = _outer_dft((g_fwd_b,), px4, 5, c=width, tl=tl)
    a = _inner_conv(a, f_fwd_b, f_inv_b, spec, 0)
    a = _outer_idft_gate(g_inv_b, g_fwd_b, a, px4, 6, again=True, tl=tl)
    a = _inner_conv(a, f_fwd_b, f_inv_b, spec, 1)
    hy = _outer_idft_gate(g_inv_b, g_fwd_b, a, px4, 7, again=False, tl=tl)
    return hy.reshape(bsz * n, width)


def _tiles(d_ff):
    assert d_ff % 128 == 0, "a ragged last hidden tile still has to be lane-tile aligned"
    return TOKEN_TILE, 2 * TOKEN_TILE, min(HIDDEN_TILE, d_ff)


def kernel(x, c, ctx, c_ctx, ada_w, ada_b, norm_ffn1, ffn1_w1, ffn1_w3, ffn1_w2, norm_mix, w_in, hg_lb_logits, hg_norm, hy_conv_w, hy_conv_b, filt_w1, filt_b1, filt_w2, filt_b2, filt_w3, filt_b3, filt_freq, filt_w_out, hy_d, w_out, norm_ffn2, ffn2_w1, ffn2_w3, ffn2_w2, final_norm):
    bsz, n, d = x.shape
    n_ctx = ctx.shape[1]
    depth = ada_w.shape[0]
    assert depth == 1, "single-layer block"
    width = d // 2
    n_seg = w_in.shape[2] // width
    assert n_seg == 8 and n % GRID_W == 0 and n_ctx % SCAN_CHUNK == 0 and width % HEAD_DIM == 0
    tm, tm_wide, tf = _tiles(ffn1_w1.shape[2])
    tf_mix = max(3 * tf // 4, 128)
    l = 0

    rows = -(-(bsz + 1) // 8) * 8
    cs = jnp.concatenate([c, c_ctx[None, :], jnp.zeros((rows - bsz - 1, d), F32)], axis=0)
    mods = _ada(cs, ada_w[l], ada_b[l]).reshape(rows, N_MOD, d)
    mx = [mods[:bsz, i][:, None, :] for i in range(N_MOD)]
    mc = [mods[bsz:bsz + 1, i][:, None, :] for i in range(N_MOD)]

    bf = _to_bf16
    w1a, w3a, w2a = bf(ffn1_w1[l]), bf(ffn1_w3[l]), bf(ffn1_w2[l])
    xt = x.reshape(bsz * n, d)
    yt = ctx.reshape(bsz * n_ctx, d)

    xt = _ffn(xt, mx[0], mx[1], mx[2], norm_ffn1[l], w1a, w3a, w2a, final_norm,
              tokens_per_mod=n, final_norm=False, tm=tm, tf=tf)
    yt = _ffn(yt, mc[0], mc[1], mc[2], norm_ffn1[l], w1a, w3a, w2a, final_norm,
              tokens_per_mod=bsz * n_ctx, final_norm=False, tm=tm, tf=tf)

    w_in_b = bf(w_in[l])
    pc = _proj(yt, mc[3], mc[4], norm_mix[l], w_in_b, hy_conv_w[l], hy_conv_b[l], n_seg=3, seg_w=width,
               conv_from=n_seg, tokens_per_mod=bsz * n_ctx, tm=tm)
    px = _proj(xt, mx[3], mx[4], norm_mix[l], w_in_b, hy_conv_w[l], hy_conv_b[l], n_seg=n_seg, seg_w=width,
               conv_from=5, tokens_per_mod=n, tm=tm_wide)
    cols = n_seg * width

    hg = _hgrn(px.reshape(bsz, n, cols), pc.reshape(bsz, n_ctx, 3 * width), hg_lb_logits, hg_norm[l], width=width)

    consts = _dft_constants(n)
    filt = (filt_w1[l], filt_b1[l], filt_w2[l], filt_b2[l], filt_w3[l], filt_b3[l], filt_freq[l], filt_w_out[l])
    spec = _hyena_spectra(n, width, filt, hy_d[l], consts)
    hy = _hyena_mix(px, spec, consts, bsz=bsz, n=n, width=width)

    out = _ffn(xt, mx[6], mx[7], mx[8], norm_ffn2[l], bf(ffn2_w1[l]), bf(ffn2_w3[l]), bf(ffn2_w2[l]), final_norm,
               tokens_per_mod=n, final_norm=True, tm=tm, tf=tf_mix,
               mix=(hg.reshape(bsz * n, width), hy, bf(w_out[l]), mx[5]))
    return out.reshape(bsz, n, d)
```
